```python
import math
import jax, jax.numpy as jnp
from jax import lax
import numpy as np

D_MODEL = 1024
BATCH = 8
SEQ = 2048
DEPTH = 1
DEC_BATCH = 128
DEC_SEQ = 1
PAST_LEN = 16384
PAGE_SIZE = 128

HEAD_A = 64
C_A = D_MODEL // 2
H_A = C_A // HEAD_A
LORA_W = 64
LORA_A = 64
LORA_G = 128
A_COLS = 3 * C_A + LORA_W + LORA_A + LORA_G
DK = 128
DV = 128
H_B = (D_MODEL // 2) // DV
C_BK = H_B * DK
C_BV = H_B * DV
CONV_W = 4
CONV_CH = 2 * C_BK + C_BV
B_COLS = CONV_CH + C_BV + 2 * H_B
GDN_CHUNK = 64
IN_COLS = A_COLS + B_COLS + 2 * D_MODEL
D_FF = -(-8 * D_MODEL // (3 * 256)) * 256
PLE_DIM = 256
NORM_EPS = 1e-6
GN_EPS = 64e-5

kernel_name = "rwkv7_gdn_gated_hybrid_step"


def rmsnorm(x, gain):
    xf = x.astype(jnp.float32)
    y = xf * lax.rsqrt(jnp.mean(xf * xf, axis=-1, keepdims=True) + NORM_EPS)
    return (y * gain.astype(jnp.float32)).astype(x.dtype)


def l2norm(x):
    return x * lax.rsqrt(jnp.sum(x * x, axis=-1, keepdims=True) + 1e-6)


def rwkv7_scan(r, decay, k, v, a_in, b_in, s0):
    def step(S, inp):
        r_t, w_t, k_t, v_t, a_t, b_t = inp
        sa = jnp.einsum('bhvk,bhk->bhv', S, a_t)
        S = S * w_t[:, :, None, :] + sa[..., None] * b_t[:, :, None, :] + v_t[..., None] * k_t[:, :, None, :]
        return S, jnp.einsum('bhvk,bhk->bhv', S, r_t)
    xs = tuple(jnp.moveaxis(t, 1, 0) for t in (r, decay, k, v, a_in, b_in))
    S, ys = lax.scan(step, s0, xs)
    return jnp.moveaxis(ys, 0, 1), S


def gated_delta_chunked(q, k, v, g, beta, s0):
    Bn, T, H, _ = q.shape
    V = v.shape[-1]
    C = min(GDN_CHUNK, T)
    n = -(-T // C)
    pad = n * C - T

    def blocks(t):
        t = jnp.pad(t, [(0, 0), (0, pad)] + [(0, 0)] * (t.ndim - 2))
        t = t.reshape((Bn, n, C) + t.shape[2:])
        return jnp.moveaxis(jnp.moveaxis(t, 1, 0), 3, 2)

    q, k, v, g, beta = (blocks(t) for t in (q, k, v, g, beta))
    gc = jnp.cumsum(g, axis=-1)
    tril = jnp.tril(jnp.ones((C, C), bool))
    strict = jnp.tril(jnp.ones((C, C), bool), -1)
    diff = gc[..., :, None] - gc[..., None, :]
    decay = jnp.where(tril, jnp.exp(jnp.where(tril, diff, 0.0)), 0.0)
    kb = k * beta[..., None]
    vb = v * beta[..., None]
    lmat = jnp.where(strict, jnp.einsum('nbhik,nbhjk->nbhij', kb, k) * decay, 0.0)
    eye = jnp.eye(C, dtype=lmat.dtype)
    tinv = lax.linalg.triangular_solve(eye + lmat, jnp.broadcast_to(eye, lmat.shape),
                                       left_side=True, lower=True, unit_diagonal=True)
    u = tinv @ vb
    w = tinv @ (kb * jnp.exp(gc)[..., None])
    qk = jnp.where(tril, jnp.einsum('nbhik,nbhjk->nbhij', q, k) * decay, 0.0)
    qg = q * jnp.exp(gc)[..., None]
    kd = k * jnp.exp(gc[..., -1:] - gc)[..., None]
    glast = jnp.exp(gc[..., -1])

    def step(S, inp):
        u_c, w_c, qk_c, qg_c, kd_c, gl_c = inp
        v_new = u_c - w_c @ S
        o = qg_c @ S + qk_c @ v_new
        S = S * gl_c[..., None, None] + jnp.einsum('bhck,bhcv->bhkv', kd_c, v_new)
        return S, o

    S, o = lax.scan(step, s0, (u, w, qk, qg, kd, glast))
    o = jnp.swapaxes(jnp.moveaxis(o, 0, 1), 2, 3).reshape(Bn, n * C, H, V)[:, :T]
    return o, S


def hybrid_layer(h, pe, shift0, wkv0, conv0, gdn0,
                 norm_mix, w_in, mu_shift, rw_w0, rw_w2, rw_a0, rw_a2, rw_g2, rw_kk, rw_ka, rw_rk,
                 rw_ln_w, rw_ln_b, gdn_conv, gdn_a_log, gdn_dt_bias, gdn_norm,
                 w_branch_a, w_branch_b, w_out, norm_ffn, w_ffn_gate, w_ffn_up, w_ffn_down,
                 norm_ple, w_ple_gate, w_ple_proj):
    f32 = jnp.float32
    Bn, T, _ = h.shape
    act = h.dtype
    u = rmsnorm(h, norm_mix)
    proj = u @ w_in
    pa = proj[..., :A_COLS]
    pb = proj[..., A_COLS:A_COLS + B_COLS]
    pg = proj[..., A_COLS + B_COLS:]

    prev = jnp.concatenate([shift0.astype(act), pa[:, :-1]], axis=1)
    xa = (pa + (prev - pa) * mu_shift).astype(f32)
    new_shift = pa[:, -1:].astype(shift0.dtype)
    r, k, v, xw, xaa, xg = jnp.split(
        xa, [C_A, 2 * C_A, 3 * C_A, 3 * C_A + LORA_W, 3 * C_A + LORA_W + LORA_A], axis=-1)
    w_log = -jax.nn.softplus(-(rw_w0 + jnp.tanh(xw) @ rw_w2)) - 0.5
    decay = jnp.exp(-jnp.exp(w_log))
    a = jax.nn.sigmoid(rw_a0 + xaa @ rw_a2)
    g = jax.nn.sigmoid(xg) @ rw_g2
    heads = lambda t: t.reshape(Bn, T, H_A, HEAD_A)
    kk = l2norm(heads(k * rw_kk))
    k = k * (1.0 + (a - 1.0) * rw_ka)
    rh, kh, vh = heads(r), heads(k), heads(v)
    y, wkv = rwkv7_scan(rh, heads(decay), kh, vh, -kk, kk * heads(a), wkv0.astype(f32))
    mean = jnp.mean(y, axis=-1, keepdims=True)
    var = jnp.mean(jnp.square(y - mean), axis=-1, keepdims=True)
    yn = ((y - mean) * lax.rsqrt(var + GN_EPS)).reshape(Bn, T, C_A) * rw_ln_w + rw_ln_b
    bonus = (jnp.sum(rh * kh * rw_rk, axis=-1, keepdims=True) * vh).reshape(Bn, T, C_A)
    o_a = (yn + bonus) * g

    qkv = pb[..., :CONV_CH].astype(f32)
    z = pb[..., CONV_CH:CONV_CH + C_BV].astype(f32)
    b_raw = pb[..., CONV_CH + C_BV:CONV_CH + C_BV + H_B].astype(f32)
    a_raw = pb[..., CONV_CH + C_BV + H_B:].astype(f32)
    xp = jnp.concatenate([conv0.astype(f32), qkv], axis=1)
    acc = xp[:, :T] * gdn_conv[0]
    for j in range(1, CONV_W):
        acc = acc + xp[:, j:j + T] * gdn_conv[j]
    c = jax.nn.silu(acc)
    new_conv = xp[:, T:].astype(conv0.dtype)
    qb = l2norm(c[..., :C_BK].reshape(Bn, T, H_B, DK)) * (DK ** -0.5)
    kb_ = l2norm(c[..., C_BK:2 * C_BK].reshape(Bn, T, H_B, DK))
    vb_ = c[..., 2 * C_BK:].reshape(Bn, T, H_B, DV)
    beta = jax.nn.sigmoid(b_raw)
    glog = -jnp.exp(gdn_a_log.astype(f32)) * jax.nn.softplus(a_raw + gdn_dt_bias)
    ob, gdn = gated_delta_chunked(qb, kb_, vb_, glog, beta, gdn0.astype(f32))
    ob = ob * lax.rsqrt(jnp.mean(ob * ob, axis=-1, keepdims=True) + NORM_EPS) * gdn_norm
    o_b = (ob * jax.nn.silu(z.reshape(Bn, T, H_B, DV))).reshape(Bn, T, C_BV)

    gate_a = jax.nn.sigmoid(pg[..., :D_MODEL])
    gate_b = jax.nn.sigmoid(pg[..., D_MODEL:])
    mix = gate_a * (o_a.astype(act) @ w_branch_a) + gate_b * (o_b.astype(act) @ w_branch_b)
    h = h + mix @ w_out

    u2 = rmsnorm(h, norm_ffn)
    h = h + (jax.nn.silu(u2 @ w_ffn_gate) * (u2 @ w_ffn_up)) @ w_ffn_down

    h = h + jax.nn.sigmoid(rmsnorm(h, norm_ple) @ w_ple_gate) * (pe.astype(act) @ w_ple_proj)
    return h, new_shift, wkv.astype(wkv0.dtype), new_conv, gdn.astype(gdn0.dtype)


def setup_inputs(seed: int = 0) -> dict:
    key = jax.random.key(seed)
    ks = iter(jax.random.split(key, 48))
    nrm = lambda shape, s: jax.random.normal(next(ks), shape, jnp.float32) * s
    unif = lambda shape, lo, hi: jax.random.uniform(next(ks), shape, jnp.float32, lo, hi)
    L = DEPTH
    dt_init = jnp.exp(unif((L, H_B), math.log(1e-3), math.log(1e-1)))
    return {
        'x_prompt': nrm((BATCH, SEQ, D_MODEL), 1.0),
        'x_sample': nrm((DEC_BATCH, DEC_SEQ, D_MODEL), 1.0),
        'p_prompt': nrm((L, BATCH, SEQ, PLE_DIM), 1.0),
        'p_sample': nrm((L, DEC_BATCH, DEC_SEQ, PLE_DIM), 1.0),
        'state_shift': nrm((L, DEC_BATCH, 1, A_COLS), 1.0),
        'state_wkv': nrm((L, DEC_BATCH, H_A, HEAD_A, HEAD_A), 0.5),
        'state_conv': nrm((L, DEC_BATCH, CONV_W - 1, CONV_CH), 1.0),
        'state_gdn': nrm((L, DEC_BATCH, H_B, DK, DV), 0.5),
        'norm_mix': 1.0 + nrm((L, D_MODEL), 0.01),
        'w_in': nrm((L, D_MODEL, IN_COLS), D_MODEL ** -0.5),
        'mu_shift': unif((L, A_COLS), 0.0, 1.0),
        'rw_w0': unif((L, C_A), -6.0, -1.0),
        'rw_w2': nrm((L, LORA_W, C_A), 0.1),
        'rw_a0': nrm((L, C_A), 0.1),
        'rw_a2': nrm((L, LORA_A, C_A), 0.5 * LORA_A ** -0.5),
        'rw_g2': nrm((L, LORA_G, C_A), LORA_G ** -0.5),
        'rw_kk': 0.85 + nrm((L, C_A), 0.02),
        'rw_ka': 1.0 + nrm((L, C_A), 0.02),
        'rw_rk': nrm((L, H_A, HEAD_A), 0.1),
        'rw_ln_w': 1.0 + nrm((L, C_A), 0.01),
        'rw_ln_b': nrm((L, C_A), 0.01),
        'gdn_conv': nrm((L, CONV_W, CONV_CH), 0.5),
        'gdn_a_log': jnp.log(unif((L, H_B), 1.0, 16.0)),
        'gdn_dt_bias': dt_init + jnp.log(-jnp.expm1(-dt_init)),
        'gdn_norm': 1.0 + nrm((L, DV), 0.01),
        'w_branch_a': nrm((L, C_A, D_MODEL), C_A ** -0.5),
        'w_branch_b': nrm((L, C_BV, D_MODEL), C_BV ** -0.5),
        'w_out': nrm((L, D_MODEL, D_MODEL), D_MODEL ** -0.5),
        'norm_ffn': 1.0 + nrm((L, D_MODEL), 0.01),
        'w_ffn_gate': nrm((L, D_MODEL, D_FF), D_MODEL ** -0.5),
        'w_ffn_up': nrm((L, D_MODEL, D_FF), D_MODEL ** -0.5),
        'w_ffn_down': nrm((L, D_FF, D_MODEL), D_FF ** -0.5),
        'norm_ple': 1.0 + nrm((L, D_MODEL), 0.01),
        'w_ple_gate': nrm((L, D_MODEL, D_MODEL), D_MODEL ** -0.5),
        'w_ple_proj': nrm((L, PLE_DIM, D_MODEL), PLE_DIM ** -0.5),
        'norm_final': 1.0 + nrm((D_MODEL,), 0.01),
    }


def reference(x_prompt, x_sample, p_prompt, p_sample, state_shift, state_wkv, state_conv, state_gdn,
              norm_mix, w_in, mu_shift, rw_w0, rw_w2, rw_a0, rw_a2, rw_g2, rw_kk, rw_ka, rw_rk,
              rw_ln_w, rw_ln_b, gdn_conv, gdn_a_log, gdn_dt_bias, gdn_norm,
              w_branch_a, w_branch_b, w_out, norm_ffn, w_ffn_gate, w_ffn_up, w_ffn_down,
              norm_ple, w_ple_gate, w_ple_proj, norm_final):
    hp, hs = x_prompt, x_sample
    bp = x_prompt.shape[0]
    st_p, st_s = [], []
    for i in range(DEPTH):
        lw = (norm_mix[i], w_in[i], mu_shift[i], rw_w0[i], rw_w2[i], rw_a0[i], rw_a2[i], rw_g2[i],
              rw_kk[i], rw_ka[i], rw_rk[i], rw_ln_w[i], rw_ln_b[i], gdn_conv[i], gdn_a_log[i],
              gdn_dt_bias[i], gdn_norm[i], w_branch_a[i], w_branch_b[i], w_out[i], norm_ffn[i],
              w_ffn_gate[i], w_ffn_up[i], w_ffn_down[i], norm_ple[i], w_ple_gate[i], w_ple_proj[i])
        z_shift = jnp.zeros((bp, 1, A_COLS), state_shift.dtype)
        z_wkv = jnp.zeros((bp, H_A, HEAD_A, HEAD_A), state_wkv.dtype)
        z_conv = jnp.zeros((bp, CONV_W - 1, CONV_CH), state_conv.dtype)
        z_gdn = jnp.zeros((bp, H_B, DK, DV), state_gdn.dtype)
        hp, *sp = hybrid_layer(hp, p_prompt[i], z_shift, z_wkv, z_conv, z_gdn, *lw)
        hs, *ss = hybrid_layer(hs, p_sample[i], state_shift[i], state_wkv[i], state_conv[i], state_gdn[i], *lw)
        st_p.append(sp)
        st_s.append(ss)
    y_prompt = rmsnorm(hp, norm_final)
    y_sample = rmsnorm(hs, norm_final)
    shift_prompt = jnp.stack([s[0] for s in st_p])
    wkv_prompt = jnp.stack([s[1] for s in st_p])
    conv_prompt = jnp.stack([s[2] for s in st_p])
    gdn_prompt = jnp.stack([s[3] for s in st_p])
    shift_sample = jnp.stack([s[0] for s in st_s])
    wkv_sample = jnp.stack([s[1] for s in st_s])
    conv_sample = jnp.stack([s[2] for s in st_s])
    gdn_sample = jnp.stack([s[3] for s in st_s])
    return (y_prompt, y_sample, shift_prompt, wkv_prompt, conv_prompt, gdn_prompt,
            shift_sample, wkv_sample, conv_sample, gdn_sample)
```

```python
import functools

import jax
import jax.numpy as jnp
from jax import lax
from jax.experimental import pallas as pl
from jax.experimental.pallas import tpu as pltpu

F32 = jnp.float32
BF16 = jnp.bfloat16
HI = lax.Precision.HIGHEST

D_MODEL = 1024
HEAD_A = 64
C_A = 512
H_A = 8
LORA_W = 64
LORA_A = 64
LORA_G = 128
A_COLS = 3 * C_A + LORA_W + LORA_A + LORA_G
DK = 128
DV = 128
H_B = 4
C_BK = 512
C_BV = 512
CONV_W = 4
CONV_CH = 2 * C_BK + C_BV
D_FF = 2816
PLE_DIM = 256
NORM_EPS = 1e-6
GN_EPS = 64e-5
L2_EPS = 1e-6

LANES = 128
CHUNK = 64
PAIR = 2 * CHUNK
T_TILE = 256
ROW_TILE = 256
DEC_TILE = 8
HALO = 8
VMEM_LIMIT = 56 * 1024 * 1024


def _dot(a, b, prec=None):
    return jnp.dot(a, b, preferred_element_type=F32, precision=prec)


def _dot_nt(a, b, prec=None):
    return lax.dot_general(a, b, (((1,), (1,)), ((), ())), preferred_element_type=F32, precision=prec)


def _dot_tn(a, b, prec=None):
    return lax.dot_general(a, b, (((0,), (0,)), ((), ())), preferred_element_type=F32, precision=prec)


def _sigmoid(x):
    return 1.0 / (1.0 + jnp.exp(-x))


def _silu(x):
    return x * _sigmoid(x)


def _softplus(x):
    return jnp.maximum(x, 0.0) + jnp.log(1.0 + jnp.exp(-jnp.abs(x)))


def _rms(x, gain):
    return x * lax.rsqrt(jnp.mean(x * x, axis=-1, keepdims=True) + NORM_EPS) * gain


def _pair_masks():
    ri = lax.broadcasted_iota(jnp.int32, (PAIR, PAIR), 0)
    ci = lax.broadcasted_iota(jnp.int32, (PAIR, PAIR), 1)
    same = (ri < CHUNK) == (ci < CHUNK)
    strict = same & (ci < ri)
    incl = same & (ci <= ri)
    eye = (ri == ci).astype(F32)
    return strict, incl, eye


def _neumann_inverse(lmat, eye, prec):
    t = eye + lmat
    p = lmat
    n = 2
    while n < CHUNK:
        p = _dot(p, p, prec)
        t = t + _dot(t, p, prec)
        n *= 2
    return t


def _inproj_kernel(x_ref, g_ref, wa_ref, wq_ref, wz_ref, wb_ref, pa_ref, qkv_ref, z_ref, ba_ref):
    u = _rms(x_ref[...], g_ref[...]).astype(BF16)
    pa_ref[...] = _dot(u, wa_ref[...])
    qkv_ref[...] = _dot(u, wq_ref[...])
    z_ref[...] = _dot(u, wz_ref[...])
    ba_ref[...] = _dot(u, wb_ref[...])


def _const_spec(shape):
    nd = len(shape)
    return pl.BlockSpec(shape, lambda *_: (0,) * nd, pipeline_mode=pl.Buffered(1))


def _inproj(x, gain, wa, wq, wz, wb):
    n = x.shape[0]
    tm = min(ROW_TILE, n)
    row = lambda w: pl.BlockSpec((tm, w), lambda i: (i, 0))
    return pl.pallas_call(
        _inproj_kernel,
        grid=(n // tm,),
        in_specs=[row(D_MODEL), _const_spec(gain.shape), _const_spec(wa.shape), _const_spec(wq.shape),
                  _const_spec(wz.shape), _const_spec(wb.shape)],
        out_specs=[row(A_COLS), row(CONV_CH), row(C_BV), row(LANES)],
        out_shape=[jax.ShapeDtypeStruct((n, A_COLS), F32), jax.ShapeDtypeStruct((n, CONV_CH), F32),
                   jax.ShapeDtypeStruct((n, C_BV), F32), jax.ShapeDtypeStruct((n, LANES), F32)],
        compiler_params=pltpu.CompilerParams(dimension_semantics=("parallel",), vmem_limit_bytes=VMEM_LIMIT),
        name="inproj",
    )(x, gain, wa, wq, wz, wb)


def _rwkv_prep(pa, prev, mu, w0, w2p, a0, a2p, g2, kkw, ka, rk, hb):
    xa = pa + (prev - pa) * mu
    r = xa[:, :C_A]
    k = xa[:, C_A:2 * C_A]
    v = xa[:, 2 * C_A:3 * C_A]
    xwa = xa[:, 3 * C_A:3 * C_A + LORA_W + LORA_A]
    xg = xa[:, 3 * C_A + LORA_W + LORA_A:]
    w_log = -_softplus(-(w0 + _dot(jnp.tanh(xwa), w2p))) - 0.5
    logw = -jnp.exp(w_log)
    a = _sigmoid(a0 + _dot(xwa, a2p))
    g = _dot(_sigmoid(xg), g2)
    kx = k * kkw
    kk = kx * lax.rsqrt(_dot(kx * kx, hb, HI) + L2_EPS)
    k2 = k * (1.0 + (a - 1.0) * ka)
    bonus = _dot(r * k2 * rk, hb, HI) * v
    return r, k2, v, logw, a, g, kk, bonus


def _group_norm_gate(y, bonus, g, lnw, lnb, hb):
    mean = _dot(y, hb, HI) * (1.0 / HEAD_A)
    d = y - mean
    var = _dot(d * d, hb, HI) * (1.0 / HEAD_A)
    yn = d * lax.rsqrt(var + GN_EPS) * lnw + lnb
    return (yn + bonus) * g


def _rwkv_prompt_kernel(pa_ref, mu_ref, w0_ref, w2_ref, a0_ref, a2_ref, g2_ref, kkw_ref, ka_ref, rk_ref,
                        lnw_ref, lnb_ref, hb_ref, tri_ref, blk_ref,
                        oa_ref, wkv_ref,
                        ext_ref, s_ref, at_ref, bt_ref, kt_ref, rt_ref, v_ref, bh_ref, kh_ref, gl_ref, y_ref):
    t = pl.program_id(1)

    @pl.when(t == 0)
    def _():
        ext_ref[0:HALO, :] = jnp.zeros((HALO, A_COLS), F32)
        s_ref[...] = jnp.zeros_like(s_ref)

    pa = pa_ref[0]
    ext_ref[HALO:, :] = pa
    prev = ext_ref[pl.ds(HALO - 1, T_TILE), :]
    ext_ref[0:HALO, :] = pa[T_TILE - HALO:, :]

    hb = hb_ref[...]
    r, k2, v, logw, a, g, kk, bonus = _rwkv_prep(
        pa, prev, mu_ref[...], w0_ref[...], w2_ref[...], a0_ref[...], a2_ref[...], g2_ref[...],
        kkw_ref[...], ka_ref[...], rk_ref[...], hb)

    cs = _dot(tri_ref[...], logw, HI)
    tot = _dot(blk_ref[...], logw, HI)
    dinv = jnp.exp(-cs)
    dend = jnp.exp(tot - cs)
    b_in = kk * a
    at_ref[...] = -kk * jnp.exp(cs - logw)
    bt_ref[...] = b_in * dinv
    kt_ref[...] = k2 * dinv
    rt_ref[...] = r * jnp.exp(cs)
    v_ref[...] = v
    bh_ref[...] = b_in * dend
    kh_ref[...] = k2 * dend
    gl_ref[...] = jnp.exp(tot)

    strict, incl, eye = _pair_masks()
    low = lax.broadcasted_iota(jnp.int32, (CHUNK, LANES), 1) < HEAD_A

    def stack(x):
        return jnp.concatenate([jnp.where(low, x, 0.0), jnp.where(low, 0.0, x)], axis=0)

    def chunk_body(c, carry):
        r0 = pl.multiple_of(c * CHUNK, CHUNK)
        for p in range(C_A // LANES):
            ls = slice(LANES * p, LANES * (p + 1))
            ld = lambda ref: stack(ref[pl.ds(r0, CHUNK), ls])
            at, bt, kt, rt, vs, bh, kh = (ld(ref) for ref in (at_ref, bt_ref, kt_ref, rt_ref, v_ref, bh_ref, kh_ref))
            s = s_ref[p]
            a_ab = jnp.where(strict, _dot_nt(at, bt, HI), 0.0)
            a_ak = jnp.where(strict, _dot_nt(at, kt, HI), 0.0)
            a_rb = jnp.where(incl, _dot_nt(rt, bt, HI), 0.0)
            a_rk = jnp.where(incl, _dot_nt(rt, kt, HI), 0.0)
            tinv = _neumann_inverse(a_ab, eye, HI)
            u = _dot(tinv, _dot_nt(at, s, HI) + _dot(a_ak, vs, HI), HI)
            y = _dot_nt(rt, s, HI) + _dot(a_rb, u, HI) + _dot(a_rk, vs, HI)
            s_ref[p] = s * gl_ref[pl.ds(r0, 1), ls] + _dot_tn(u, bh, HI) + _dot_tn(vs, kh, HI)
            y_ref[pl.ds(r0, CHUNK), ls] = y[:CHUNK] + y[CHUNK:]
        return carry

    lax.fori_loop(0, T_TILE // CHUNK, chunk_body, 0)

    oa_ref[0] = _group_norm_gate(y_ref[...], bonus, g, lnw_ref[...], lnb_ref[...], hb)
    wkv_ref[0] = s_ref[...]


def _rwkv_prompt(pa, mu, w0, w2p, a0, a2p, g2, kkw, ka, rk, lnw, lnb, hb, tri, blk):
    bsz, seq, _ = pa.shape
    consts = (mu, w0, w2p, a0, a2p, g2, kkw, ka, rk, lnw, lnb, hb, tri, blk)
    npair = C_A // LANES
    tile = lambda: pltpu.VMEM((T_TILE, C_A), F32)
    return pl.pallas_call(
        _rwkv_prompt_kernel,
        grid=(bsz, seq // T_TILE),
        in_specs=[pl.BlockSpec((1, T_TILE, A_COLS), lambda b, t: (b, t, 0))] + [_const_spec(c.shape) for c in consts],
        out_specs=[pl.BlockSpec((1, T_TILE, C_A), lambda b, t: (b, t, 0)),
                   pl.BlockSpec((1, npair, LANES, LANES), lambda b, t: (b, 0, 0, 0))],
        out_shape=[jax.ShapeDtypeStruct((bsz, seq, C_A), F32),
                   jax.ShapeDtypeStruct((bsz, npair, LANES, LANES), F32)],
        scratch_shapes=[pltpu.VMEM((T_TILE + HALO, A_COLS), F32), pltpu.VMEM((npair, LANES, LANES), F32)]
                       + [tile() for _ in range(9)],
        compiler_params=pltpu.CompilerParams(dimension_semantics=("parallel", "arbitrary"),
                                             vmem_limit_bytes=VMEM_LIMIT),
        name="rwkv_prompt",
    )(pa, *consts)


def _gdn_prep(x0, x1, x2, x3, ba, conv, alog, dtb):
    c = _silu(x0 * conv[0:1] + x1 * conv[1:2] + x2 * conv[2:3] + x3 * conv[3:4])
    qs, ks = [], []
    for h in range(H_B):
        qh = c[:, DK * h:DK * (h + 1)]
        kh = c[:, C_BK + DK * h:C_BK + DK * (h + 1)]
        qs.append(qh * lax.rsqrt(jnp.sum(qh * qh, axis=-1, keepdims=True) + L2_EPS) * (DK ** -0.5))
        ks.append(kh * lax.rsqrt(jnp.sum(kh * kh, axis=-1, keepdims=True) + L2_EPS))
    q = jnp.concatenate(qs, axis=1)
    k = jnp.concatenate(ks, axis=1)
    v = c[:, 2 * C_BK:]
    beta = _sigmoid(ba)
    glog = -jnp.exp(alog) * _softplus(ba + dtb)
    return q, k, v, beta, glog


def _head_norm_gate(o, z, gnorm):
    outs = []
    for h in range(H_B):
        oh = o[:, DV * h:DV * (h + 1)]
        zh = z[:, DV * h:DV * (h + 1)]
        oh = oh * lax.rsqrt(jnp.mean(oh * oh, axis=-1, keepdims=True) + NORM_EPS) * gnorm
        outs.append(oh * _silu(zh))
    return jnp.concatenate(outs, axis=1)


def _gdn_prompt_kernel(qkv_ref, z_ref, ba_ref, conv_ref, alog_ref, dtb_ref, gnorm_ref, tri_ref, blk_ref,
                       ob_ref, gdn_ref,
                       ext_ref, s_ref, q_ref, k_ref, v_ref, gc_ref, gt_ref, be_ref, o_ref):
    t = pl.program_id(1)

    @pl.when(t == 0)
    def _():
        ext_ref[0:HALO, :] = jnp.zeros((HALO, CONV_CH), F32)
        s_ref[...] = jnp.zeros_like(s_ref)

    x3 = qkv_ref[0]
    ext_ref[HALO:, :] = x3
    x0 = ext_ref[pl.ds(HALO - 3, T_TILE), :]
    x1 = ext_ref[pl.ds(HALO - 2, T_TILE), :]
    x2 = ext_ref[pl.ds(HALO - 1, T_TILE), :]
    ext_ref[0:HALO, :] = x3[T_TILE - HALO:, :]

    q, k, v, beta, glog = _gdn_prep(x0, x1, x2, x3, ba_ref[0], conv_ref[...], alog_ref[...], dtb_ref[...])
    q_ref[...] = q
    k_ref[...] = k
    v_ref[...] = v
    be_ref[...] = beta
    gc_ref[...] = _dot(tri_ref[...], glog, HI)
    gt_ref[...] = _dot(blk_ref[...], glog, HI)

    strict, incl, eye = _pair_masks()

    def chunk_body(c, carry):
        r0 = pl.multiple_of(c * CHUNK, CHUNK)
        rows = pl.ds(r0, CHUNK)
        for pr in range(H_B // 2):
            heads = (2 * pr, 2 * pr + 1)
            cat = lambda ref: jnp.concatenate([ref[rows, DK * h:DK * (h + 1)] for h in heads], axis=0)
            col = lambda ref, off: jnp.concatenate([ref[rows, off + h:off + h + 1] for h in heads], axis=0)
            qs, ks, vs = cat(q_ref), cat(k_ref), cat(v_ref)
            beta_c = col(be_ref, 0)
            gc = col(gc_ref, H_B)
            gt = col(gt_ref, H_B)
            gc_full = jnp.broadcast_to(gc, (PAIR, PAIR))
            diff = gc_full - gc_full.T
            decay = jnp.where(incl, jnp.exp(jnp.where(incl, diff, 0.0)), 0.0)
            kb = ks * beta_c
            vb = vs * beta_c
            lmat = jnp.where(strict, _dot_nt(kb, ks, HI) * decay, 0.0)
            tinv = _neumann_inverse(-lmat, eye, HI)
            eg = jnp.exp(gc)
            u = _dot(tinv, vb, HI)
            w = _dot(tinv, kb * eg, HI)
            qk = jnp.where(incl, _dot_nt(qs, ks, HI) * decay, 0.0)
            qg = qs * eg
            kd = ks * jnp.exp(gt - gc)
            v_new, o_state = [], []
            for i, h in enumerate(heads):
                hs = slice(CHUNK * i, CHUNK * (i + 1))
                s = s_ref[h]
                v_new.append(u[hs] - _dot(w[hs], s, HI))
                o_state.append(_dot(qg[hs], s, HI))
            v_new = jnp.concatenate(v_new, axis=0)
            o = jnp.concatenate(o_state, axis=0) + _dot(qk, v_new, HI)
            for i, h in enumerate(heads):
                hs = slice(CHUNK * i, CHUNK * (i + 1))
                s_ref[h] = s_ref[h] * jnp.exp(gt[CHUNK * i:CHUNK * i + 1]) + _dot_tn(kd[hs], v_new[hs], HI)
                o_ref[rows, DV * h:DV * (h + 1)] = o[hs]
        return carry

    lax.fori_loop(0, T_TILE // CHUNK, chunk_body, 0)

    ob_ref[0] = _head_norm_gate(o_ref[...], z_ref[0], gnorm_ref[...])
    gdn_ref[0] = s_ref[...]


def _gdn_prompt(qkv, z, ba, conv, alog, dtb, gnorm, tri, blk):
    bsz, seq, _ = qkv.shape
    consts = (conv, alog, dtb, gnorm, tri, blk)
    tok = lambda w: pl.BlockSpec((1, T_TILE, w), lambda b, t: (b, t, 0))
    return pl.pallas_call(
        _gdn_prompt_kernel,
        grid=(bsz, seq // T_TILE),
        in_specs=[tok(CONV_CH), tok(C_BV), tok(LANES)] + [_const_spec(c.shape) for c in consts],
        out_specs=[tok(C_BV), pl.BlockSpec((1, H_B, DK, DV), lambda b, t: (b, 0, 0, 0))],
        out_shape=[jax.ShapeDtypeStruct((bsz, seq, C_BV), F32), jax.ShapeDtypeStruct((bsz, H_B, DK, DV), F32)],
        scratch_shapes=[pltpu.VMEM((T_TILE + HALO, CONV_CH), F32), pltpu.VMEM((H_B, DK, DV), F32),
                        pltpu.VMEM((T_TILE, C_BK), F32), pltpu.VMEM((T_TILE, C_BK), F32),
                        pltpu.VMEM((T_TILE, C_BV), F32), pltpu.VMEM((T_TILE, LANES), F32),
                        pltpu.VMEM((T_TILE, LANES), F32), pltpu.VMEM((T_TILE, LANES), F32),
                        pltpu.VMEM((T_TILE, C_BV), F32)],
        compiler_params=pltpu.CompilerParams(dimension_semantics=("parallel", "arbitrary"),
                                             vmem_limit_bytes=VMEM_LIMIT),
        name="gdn_prompt",
    )(qkv, z, ba, *consts)


def _to_columns(x):
    pad = jnp.zeros((LANES - DEC_TILE, x.shape[1]), F32)
    return jnp.concatenate([x, pad], axis=0).T


def _decode_kernel(pa_ref, shift_ref, qkv_ref, z_ref, ba_ref, cst_ref, wkv_ref, gdn_ref,
                   mu_ref, w0_ref, w2_ref, a0_ref, a2_ref, g2_ref, kkw_ref, ka_ref, rk_ref, lnw_ref, lnb_ref, hb_ref,
                   conv_ref, alog_ref, dtb_ref, gnorm_ref,
                   oa_ref, ob_ref, wkvo_ref, gdno_ref):
    hb = hb_ref[...]
    r, k2, v, logw, a, g, kk, bonus = _rwkv_prep(
        pa_ref[...], shift_ref[...], mu_ref[...], w0_ref[...], w2_ref[...], a0_ref[...], a2_ref[...], g2_ref[...],
        kkw_ref[...], ka_ref[...], rk_ref[...], hb)
    w = jnp.exp(logw)
    a_in = -kk
    b_in = kk * a
    v_cols = _to_columns(v)
    lane = lax.broadcasted_iota(jnp.int32, (C_A, LANES), 1)
    y_cols = jnp.zeros((C_A, LANES), F32)
    for s_i in range(DEC_TILE):
        ys = []
        for h in range(H_A):
            ls = slice(HEAD_A * h, HEAD_A * (h + 1))
            st = wkv_ref[s_i, h]
            row = lambda x: x[s_i:s_i + 1, ls]
            sa = jnp.sum(st * row(a_in), axis=-1, keepdims=True)
            st = st * row(w) + sa * row(b_in) + v_cols[ls, s_i:s_i + 1] * row(k2)
            wkvo_ref[s_i, h] = st
            ys.append(jnp.sum(st * row(r), axis=-1, keepdims=True))
        y_cols = jnp.where(lane == s_i, jnp.concatenate(ys, axis=0), y_cols)
    y = y_cols.T[:DEC_TILE]
    oa_ref[...] = _group_norm_gate(y, bonus, g, lnw_ref[...], lnb_ref[...], hb)

    cst = cst_ref[...]
    q, k, vv, beta, glog = _gdn_prep(cst[:, :CONV_CH], cst[:, CONV_CH:2 * CONV_CH], cst[:, 2 * CONV_CH:],
                                     qkv_ref[...], ba_ref[...], conv_ref[...], alog_ref[...], dtb_ref[...])
    eg = jnp.exp(glog)
    q_cols = _to_columns(q)
    k_cols = _to_columns(k)
    o_rows = []
    for s_i in range(DEC_TILE):
        o_heads = []
        for h in range(H_B):
            ls = slice(DK * h, DK * (h + 1))
            st = gdn_ref[s_i, h]
            be = beta[s_i:s_i + 1, h:h + 1]
            e = eg[s_i:s_i + 1, H_B + h:H_B + h + 1]
            kc = k_cols[ls, s_i:s_i + 1]
            qc = q_cols[ls, s_i:s_i + 1]
            v_new = be * vv[s_i:s_i + 1, ls] - jnp.sum((kc * (be * e)) * st, axis=0, keepdims=True)
            qk = jnp.sum(q[s_i:s_i + 1, ls] * k[s_i:s_i + 1, ls], axis=-1, keepdims=True)
            o_heads.append(jnp.sum((qc * e) * st, axis=0, keepdims=True) + qk * v_new)
            gdno_ref[s_i, h] = st * e + kc * v_new
        o_rows.append(jnp.concatenate(o_heads, axis=1))
    o = jnp.concatenate(o_rows, axis=0)
    ob_ref[...] = _head_norm_gate(o, z_ref[...], gnorm_ref[...])


def _decode(pa, shift, qkv, z, ba, cst, wkv, gdn, rw_consts, gdn_consts):
    n = pa.shape[0]
    row = lambda w: pl.BlockSpec((DEC_TILE, w), lambda i: (i, 0))
    consts = tuple(rw_consts) + tuple(gdn_consts)
    return pl.pallas_call(
        _decode_kernel,
        grid=(n // DEC_TILE,),
        in_specs=[row(A_COLS), row(A_COLS), row(CONV_CH), row(C_BV), row(LANES), row(3 * CONV_CH),
                  pl.BlockSpec((DEC_TILE, H_A, HEAD_A, HEAD_A), lambda i: (i, 0, 0, 0)),
                  pl.BlockSpec((DEC_TILE, H_B, DK, DV), lambda i: (i, 0, 0, 0))]
                 + [_const_spec(c.shape) for c in consts],
        out_specs=[row(C_A), row(C_BV),
                   pl.BlockSpec((DEC_TILE, H_A, HEAD_A, HEAD_A), lambda i: (i, 0, 0, 0)),
                   pl.BlockSpec((DEC_TILE, H_B, DK, DV), lambda i: (i, 0, 0, 0))],
        out_shape=[jax.ShapeDtypeStruct((n, C_A), F32), jax.ShapeDtypeStruct((n, C_BV), F32),
                   jax.ShapeDtypeStruct(wkv.shape, F32), jax.ShapeDtypeStruct(gdn.shape, F32)],
        compiler_params=pltpu.CompilerParams(dimension_semantics=("parallel",), vmem_limit_bytes=VMEM_LIMIT),
        name="decode",
    )(pa, shift, qkv, z, ba, cst, wkv, gdn, *consts)


def _post_kernel(x_ref, oa_ref, ob_ref, pe_ref, nmix_ref, wg_ref, wba_ref, wbb_ref, wout_ref, nffn_ref,
                 wfg_ref, wfu_ref, wfd_ref, nple_ref, wpg_ref, wpp_ref, nfin_ref, y_ref):
    x = x_ref[...]
    u = _rms(x, nmix_ref[...]).astype(BF16)
    gates = _sigmoid(_dot(u, wg_ref[...]))
    mix = (gates[:, :D_MODEL] * _dot(oa_ref[...].astype(BF16), wba_ref[...])
           + gates[:, D_MODEL:] * _dot(ob_ref[...].astype(BF16), wbb_ref[...]))
    h = x + _dot(mix.astype(BF16), wout_ref[...])
    u2 = _rms(h, nffn_ref[...]).astype(BF16)
    ff = _silu(_dot(u2, wfg_ref[...])) * _dot(u2, wfu_ref[...])
    h = h + _dot(ff.astype(BF16), wfd_ref[...])
    u3 = _rms(h, nple_ref[...]).astype(BF16)
    h = h + _sigmoid(_dot(u3, wpg_ref[...])) * _dot(pe_ref[...].astype(BF16), wpp_ref[...])
    y_ref[...] = _rms(h, nfin_ref[...])


def _post(x, oa, ob, pe, consts):
    n = x.shape[0]
    tm = min(ROW_TILE, n)
    row = lambda w: pl.BlockSpec((tm, w), lambda i: (i, 0))
    return pl.pallas_call(
        _post_kernel,
        grid=(n // tm,),
        in_specs=[row(D_MODEL), row(C_A), row(C_BV), row(PLE_DIM)] + [_const_spec(c.shape) for c in consts],
        out_specs=row(D_MODEL),
        out_shape=jax.ShapeDtypeStruct((n, D_MODEL), F32),
        compiler_params=pltpu.CompilerParams(dimension_semantics=("parallel",), vmem_limit_bytes=VMEM_LIMIT),
        name="post",
    )(x, oa, ob, pe, *consts)


def _time_masks():
    i = jnp.arange(T_TILE)
    same = (i[:, None] // CHUNK) == (i[None, :] // CHUNK)
    tri = (same & (i[None, :] <= i[:, None])).astype(F32)
    return tri, same.astype(F32)


def kernel(x_prompt, x_sample, p_prompt, p_sample, state_shift, state_wkv, state_conv, state_gdn, norm_mix, w_in, mu_shift, rw_w0, rw_w2, rw_a0, rw_a2, rw_g2, rw_kk, rw_ka, rw_rk, rw_ln_w, rw_ln_b, gdn_conv, gdn_a_log, gdn_dt_bias, gdn_norm, w_branch_a, w_branch_b, w_out, norm_ffn, w_ffn_gate, w_ffn_up, w_ffn_down, norm_ple, w_ple_gate, w_ple_proj, norm_final):
    bsz, seq, _ = x_prompt.shape
    nd = x_sample.shape[0]
    row = lambda p: p.reshape(1, -1)

    w_in0 = w_in[0]
    b0 = A_COLS
    wa = w_in0[:, :b0].astype(BF16)
    wq = w_in0[:, b0:b0 + CONV_CH].astype(BF16)
    wz = w_in0[:, b0 + CONV_CH:b0 + CONV_CH + C_BV].astype(BF16)
    wb = jnp.pad(w_in0[:, b0 + CONV_CH + C_BV:b0 + CONV_CH + C_BV + 2 * H_B], ((0, 0), (0, LANES - 2 * H_B))).astype(BF16)
    wg = w_in0[:, b0 + CONV_CH + C_BV + 2 * H_B:].astype(BF16)
    w2p = jnp.concatenate([rw_w2[0], jnp.zeros((LORA_A, C_A), F32)], axis=0)
    a2p = jnp.concatenate([jnp.zeros((LORA_W, C_A), F32), rw_a2[0]], axis=0)
    ch = jnp.arange(C_A) // HEAD_A
    hb = (ch[:, None] == ch[None, :]).astype(F32)
    tri, blk = _time_masks()
    alog = jnp.pad(gdn_a_log[0], (H_B, LANES - 2 * H_B)).reshape(1, LANES)
    dtb = jnp.pad(gdn_dt_bias[0], (H_B, LANES - 2 * H_B)).reshape(1, LANES)
    rw_consts = (row(mu_shift[0]), row(rw_w0[0]), w2p, row(rw_a0[0]), a2p, rw_g2[0], row(rw_kk[0]), row(rw_ka[0]),
                 row(rw_rk[0]), row(rw_ln_w[0]), row(rw_ln_b[0]), hb)
    gdn_consts = (gdn_conv[0], alog, dtb, row(gdn_norm[0]))
    post_consts = (row(norm_mix[0]), wg, w_branch_a[0].astype(BF16), w_branch_b[0].astype(BF16),
                   w_out[0].astype(BF16), row(norm_ffn[0]), w_ffn_gate[0].astype(BF16), w_ffn_up[0].astype(BF16),
                   w_ffn_down[0].astype(BF16), row(norm_ple[0]), w_ple_gate[0].astype(BF16),
                   w_ple_proj[0].astype(BF16), row(norm_final))

    xp = x_prompt.reshape(bsz * seq, D_MODEL)
    pa, qkv, z, ba = _inproj(xp, row(norm_mix[0]), wa, wq, wz, wb)
    pa3 = pa.reshape(bsz, seq, A_COLS)
    qkv3 = qkv.reshape(bsz, seq, CONV_CH)
    oa, wkv_pairs = _rwkv_prompt(pa3, *rw_consts, tri, blk)
    ob, gdn_p = _gdn_prompt(qkv3, z.reshape(bsz, seq, C_BV), ba.reshape(bsz, seq, LANES), *gdn_consts, tri, blk)
    y_prompt = _post(xp, oa.reshape(bsz * seq, C_A), ob.reshape(bsz * seq, C_BV),
                     p_prompt[0].reshape(bsz * seq, PLE_DIM), post_consts).reshape(bsz, seq, D_MODEL)
    wkv_p = jnp.stack([wkv_pairs[:, :, :HEAD_A, :HEAD_A], wkv_pairs[:, :, HEAD_A:, HEAD_A:]], axis=2)
    wkv_p = wkv_p.reshape(bsz, H_A, HEAD_A, HEAD_A)
    shift_p = pa3[:, seq - 1:, :]
    conv_p = qkv3[:, seq - (CONV_W - 1):, :]

    xs = x_sample.reshape(nd, D_MODEL)
    pa_s, qkv_s, z_s, ba_s = _inproj(xs, row(norm_mix[0]), wa, wq, wz, wb)
    cst = state_conv[0].reshape(nd, (CONV_W - 1) * CONV_CH)
    oa_s, ob_s, wkv_s, gdn_s = _decode(pa_s, state_shift[0].reshape(nd, A_COLS), qkv_s, z_s, ba_s, cst,
                                       state_wkv[0], state_gdn[0], rw_consts, gdn_consts)
    y_sample = _post(xs, oa_s, ob_s, p_sample[0].reshape(nd, PLE_DIM), post_consts).reshape(nd, 1, D_MODEL)
    conv_s = jnp.concatenate([cst[:, CONV_CH:], qkv_s], axis=1).reshape(nd, CONV_W - 1, CONV_CH)

    return (y_prompt, y_sample, shift_p[None], wkv_p[None], conv_p[None], gdn_p[None],
            pa_s.reshape(1, nd, 1, A_COLS), wkv_s[None], conv_s[None], gdn_s[None])
```

```python
import functools

import jax
import jax.numpy as jnp
from jax import lax
from jax.experimental import pallas as pl
from jax.experimental.pallas import tpu as pltpu

F32 = jnp.float32
BF16 = jnp.bfloat16
HI = lax.Precision.HIGHEST

D_MODEL = 1024
HEAD_A = 64
C_A = 512
H_A = 8
LORA_W = 64
LORA_A = 64
LORA_G = 128
A_COLS = 3 * C_A + LORA_W + LORA_A + LORA_G
DK = 128
DV = 128
H_B = 4
C_BK = 512
C_BV = 512
CONV_W = 4
CONV_CH = 2 * C_BK + C_BV
D_FF = 2816
PLE_DIM = 256
NORM_EPS = 1e-6
GN_EPS = 64e-5
L2_EPS = 1e-6

LANES = 128
CHUNK = 64
PAIR = 2 * CHUNK
T_TILE = 256
ROW_TILE = 256
DEC_TILE = 8
HALO = 8
VMEM_LIMIT = 56 * 1024 * 1024
SUM_TERMS = 2
CUMSUM_TERMS = 3
INV_TERMS = 2
CHAIN_TERMS = 2
PAIR_GROUP = 2


def _dot(a, b, prec=None):
    return jnp.dot(a, b, preferred_element_type=F32, precision=prec)


def _dot_nt(a, b, prec=None):
    return lax.dot_general(a, b, (((1,), (1,)), ((), ())), preferred_element_type=F32, precision=prec)


def _dot_tn(a, b, prec=None):
    return lax.dot_general(a, b, (((0,), (0,)), ((), ())), preferred_element_type=F32, precision=prec)


def _sigmoid(x):
    return 1.0 / (1.0 + jnp.exp(-x))


def _silu(x):
    return x * _sigmoid(x)


def _softplus(x):
    return jnp.maximum(x, 0.0) + jnp.log(1.0 + jnp.exp(-jnp.abs(x)))


def _rms(x, gain):
    return x * lax.rsqrt(jnp.mean(x * x, axis=-1, keepdims=True) + NORM_EPS) * gain


def _pair_masks():
    ri = lax.broadcasted_iota(jnp.int32, (PAIR, PAIR), 0)
    ci = lax.broadcasted_iota(jnp.int32, (PAIR, PAIR), 1)
    same = (ri < CHUNK) == (ci < CHUNK)
    strict = same & (ci < ri)
    incl = same & (ci <= ri)
    eye = (ri == ci).astype(F32)
    return strict, incl, eye


_NN = (((1,), (0,)), ((), ()))
_NT = (((1,), (1,)), ((), ()))
_TN = (((0,), (0,)), ((), ()))


def _split(x, terms):
    if isinstance(x, (list, tuple)):
        return list(x)
    if x.dtype == BF16:
        return [x]
    parts = []
    for i in range(terms):
        h = x.astype(BF16)
        parts.append(h)
        if i + 1 < terms:
            x = x - h.astype(F32)
    return parts


def _mm(a, b, dims=_NN, na=1, nb=1):
    pa, pb = _split(a, na), _split(b, nb)
    acc = None
    for i, ai in enumerate(pa):
        for j, bj in enumerate(pb):
            if i + j < max(len(pa), len(pb)):
                d = lax.dot_general(ai, bj, dims, preferred_element_type=F32)
                acc = d if acc is None else acc + d
    return acc


def _neumann_inverse(lmat, eye, terms):
    t = eye + lmat
    ps = _split(lmat, terms)
    n = 2
    while n < CHUNK:
        ps = _split(_mm(ps, ps), terms)
        t = t + _mm(t, ps, _NN, terms)
        n *= 2
    return t


def _neumann_inverse_many(lmats, eye, terms):
    ts = [eye + l for l in lmats]
    ps = [_split(l, terms) for l in lmats]
    n = 2
    while n < CHUNK:
        ps = [_split(_mm(p, p), terms) for p in ps]
        ts = [t + _mm(t, p, _NN, terms) for t, p in zip(ts, ps)]
        n *= 2
    return ts


def _inproj_kernel(x_ref, g_ref, wa_ref, wq_ref, wz_ref, wb_ref, pa_ref, qkv_ref, z_ref, ba_ref):
    u = _rms(x_ref[...], g_ref[...]).astype(BF16)
    pa_ref[...] = _dot(u, wa_ref[...])
    qkv_ref[...] = _dot(u, wq_ref[...])
    z_ref[...] = _dot(u, wz_ref[...])
    ba_ref[...] = _dot(u, wb_ref[...])


def _const_spec(shape):
    nd = len(shape)
    return pl.BlockSpec(shape, lambda *_: (0,) * nd, pipeline_mode=pl.Buffered(1))


def _inproj(x, gain, wa, wq, wz, wb):
    n = x.shape[0]
    tm = min(ROW_TILE, n)
    row = lambda w: pl.BlockSpec((tm, w), lambda i: (i, 0))
    return pl.pallas_call(
        _inproj_kernel,
        grid=(n // tm,),
        in_specs=[row(D_MODEL), _const_spec(gain.shape), _const_spec(wa.shape), _const_spec(wq.shape),
                  _const_spec(wz.shape), _const_spec(wb.shape)],
        out_specs=[row(A_COLS), row(CONV_CH), row(C_BV), row(LANES)],
        out_shape=[jax.ShapeDtypeStruct((n, A_COLS), F32), jax.ShapeDtypeStruct((n, CONV_CH), F32),
                   jax.ShapeDtypeStruct((n, C_BV), F32), jax.ShapeDtypeStruct((n, LANES), F32)],
        compiler_params=pltpu.CompilerParams(dimension_semantics=("parallel",), vmem_limit_bytes=VMEM_LIMIT),
        name="inproj",
    )(x, gain, wa, wq, wz, wb)


def _rwkv_prep(pa, prev, mu, w0, w2p, a0, a2p, g2, kkw, ka, rk, hb):
    xa = pa + (prev - pa) * mu
    r = xa[:, :C_A]
    k = xa[:, C_A:2 * C_A]
    v = xa[:, 2 * C_A:3 * C_A]
    xwa = xa[:, 3 * C_A:3 * C_A + LORA_W + LORA_A]
    xg = xa[:, 3 * C_A + LORA_W + LORA_A:]
    w_log = -_softplus(-(w0 + _mm(jnp.tanh(xwa), w2p))) - 0.5
    logw = -jnp.exp(w_log)
    a = _sigmoid(a0 + _mm(xwa, a2p))
    g = _mm(_sigmoid(xg), g2)
    kx = k * kkw
    kk = kx * lax.rsqrt(_head_sum(kx * kx, hb) + L2_EPS)
    k2 = k * (1.0 + (a - 1.0) * ka)
    bonus = _head_sum(r * k2 * rk, hb) * v
    return r, k2, v, logw, a, g, kk, bonus


def _head_sum(x, hb):
    return _mm(x, hb, _NN, SUM_TERMS)


def _group_norm_gate(y, bonus, g, lnw, lnb, hb):
    mean = _head_sum(y, hb) * (1.0 / HEAD_A)
    d = y - mean
    var = _head_sum(d * d, hb) * (1.0 / HEAD_A)
    yn = d * lax.rsqrt(var + GN_EPS) * lnw + lnb
    return (yn + bonus) * g


def _rwkv_prompt_kernel(pa_ref, mu_ref, w0_ref, w2_ref, a0_ref, a2_ref, g2_ref, kkw_ref, ka_ref, rk_ref,
                        lnw_ref, lnb_ref, hb_ref, tri_ref,
                        oa_ref, wkv_ref,
                        ext_ref, s_ref, at_ref, bt_ref, kt_ref, rt_ref, v_ref, bh_ref, kh_ref, y_ref):
    t = pl.program_id(1)

    @pl.when(t == 0)
    def _():
        ext_ref[0:HALO, :] = jnp.zeros((HALO, A_COLS), F32)
        s_ref[...] = jnp.zeros_like(s_ref)

    pa = pa_ref[0]
    ext_ref[HALO:, :] = pa
    prev = ext_ref[pl.ds(HALO - 1, T_TILE), :]
    ext_ref[0:HALO, :] = pa[T_TILE - HALO:, :]

    hb = hb_ref[...]
    r, k2, v, logw, a, g, kk, bonus = _rwkv_prep(
        pa, prev, mu_ref[...], w0_ref[...], w2_ref[...], a0_ref[...], a2_ref[...], g2_ref[...],
        kkw_ref[...], ka_ref[...], rk_ref[...], hb)

    nchunk = T_TILE // CHUNK
    cs = _mm(tri_ref[...], logw, _NN, 1, CUMSUM_TERMS)
    ends = [cs[CHUNK * (c + 1) - 1:CHUNK * (c + 1)] for c in range(nchunk)]
    tot = jnp.concatenate([jnp.broadcast_to(e, (CHUNK, C_A)) for e in ends], axis=0)
    dinv = jnp.exp(-cs)
    dend = jnp.exp(tot - cs)
    b_in = kk * a
    at_ref[...] = -kk * jnp.exp(cs - logw)
    bt_ref[...] = b_in * dinv
    kt_ref[...] = k2 * dinv
    rt_ref[...] = r * jnp.exp(cs)
    v_ref[...] = v
    bh_ref[...] = b_in * dend
    kh_ref[...] = k2 * dend
    gl = [jnp.exp(e) for e in ends]

    strict, incl, eye = _pair_masks()
    low = lax.broadcasted_iota(jnp.int32, (CHUNK, LANES), 1) < HEAD_A

    def stack(x):
        return jnp.concatenate([jnp.where(low, x, 0.0), jnp.where(low, 0.0, x)], axis=0)

    npair = C_A // LANES
    for p0 in range(0, npair, PAIR_GROUP):
        idx = [(p, c) for p in range(p0, p0 + PAIR_GROUP) for c in range(nchunk)]
        lanes = lambda p: slice(LANES * p, LANES * (p + 1))
        rows = lambda c: slice(CHUNK * c, CHUNK * (c + 1))
        ld = lambda ref: [stack(ref[rows(c), lanes(p)]) for p, c in idx]
        at, bt, kt, rt, vs, bh, kh = (ld(ref) for ref in (at_ref, bt_ref, kt_ref, rt_ref, v_ref, bh_ref, kh_ref))
        n = range(len(idx))
        aa = [_mm(jnp.concatenate([at[i], rt[i]], axis=0), jnp.concatenate([bt[i], kt[i]], axis=0), _NT) for i in n]
        a_ab = [jnp.where(strict, aa[i][:PAIR, :PAIR], 0.0) for i in n]
        a_ak = [jnp.where(strict, aa[i][:PAIR, PAIR:], 0.0) for i in n]
        a_rb = [jnp.where(incl, aa[i][PAIR:, :PAIR], 0.0) for i in n]
        a_rk = [jnp.where(incl, aa[i][PAIR:, PAIR:], 0.0) for i in n]
        tinv = _neumann_inverse_many(a_ab, eye, INV_TERMS)
        akv = [_mm(a_ak[i], vs[i]) for i in n]
        gu = [_mm(tinv[i], jnp.concatenate([at[i], akv[i]], axis=1)) for i in n]
        ry = [_mm(a_rb[i], gu[i]) for i in n]
        rkv = [_mm(a_rk[i], vs[i]) for i in n]
        pq = [_mm(gu[i], bh[i], _TN) for i in n]
        vk = [_mm(vs[i], kh[i], _TN) for i in n]
        rp = [rt[i] + ry[i][:, :PAIR] for i in n]
        yc = [ry[i][:, PAIR:] + rkv[i] for i in n]
        pm = [eye * gl[idx[i][1]][:, lanes(idx[i][0])] + pq[i][:PAIR] for i in n]
        qm = [pq[i][PAIR:] + vk[i] for i in n]
        s = {p: s_ref[p] for p in range(p0, p0 + PAIR_GROUP)}
        for c in range(nchunk):
            for p in range(p0, p0 + PAIR_GROUP):
                i = idx.index((p, c))
                y = _mm(rp[i], s[p], _NT) + yc[i]
                y_ref[rows(c), lanes(p)] = y[:CHUNK] + y[CHUNK:]
                s[p] = _mm(s[p], pm[i], _NN, CHAIN_TERMS, CHAIN_TERMS) + qm[i]
        for p in range(p0, p0 + PAIR_GROUP):
            s_ref[p] = s[p]

    oa_ref[0] = _group_norm_gate(y_ref[...], bonus, g, lnw_ref[...], lnb_ref[...], hb)
    wkv_ref[0] = s_ref[...]


def _rwkv_prompt(pa, mu, w0, w2p, a0, a2p, g2, kkw, ka, rk, lnw, lnb, hb, tri):
    bsz, seq, _ = pa.shape
    consts = (mu, w0, w2p, a0, a2p, g2, kkw, ka, rk, lnw, lnb, hb, tri)
    npair = C_A // LANES
    tile = lambda: pltpu.VMEM((T_TILE, C_A), F32)
    return pl.pallas_call(
        _rwkv_prompt_kernel,
        grid=(bsz, seq // T_TILE),
        in_specs=[pl.BlockSpec((1, T_TILE, A_COLS), lambda b, t: (b, t, 0))] + [_const_spec(c.shape) for c in consts],
        out_specs=[pl.BlockSpec((1, T_TILE, C_A), lambda b, t: (b, t, 0)),
                   pl.BlockSpec((1, npair, LANES, LANES), lambda b, t: (b, 0, 0, 0))],
        out_shape=[jax.ShapeDtypeStruct((bsz, seq, C_A), F32),
                   jax.ShapeDtypeStruct((bsz, npair, LANES, LANES), F32)],
        scratch_shapes=[pltpu.VMEM((T_TILE + HALO, A_COLS), F32), pltpu.VMEM((npair, LANES, LANES), F32)]
                       + [tile() for _ in range(8)],
        compiler_params=pltpu.CompilerParams(dimension_semantics=("parallel", "arbitrary"),
                                             vmem_limit_bytes=VMEM_LIMIT),
        name="rwkv_prompt",
    )(pa, *consts)


def _gdn_prep(x0, x1, x2, x3, ba, conv, alog, dtb):
    c = _silu(x0 * conv[0:1] + x1 * conv[1:2] + x2 * conv[2:3] + x3 * conv[3:4])
    qs, ks = [], []
    for h in range(H_B):
        qh = c[:, DK * h:DK * (h + 1)]
        kh = c[:, C_BK + DK * h:C_BK + DK * (h + 1)]
        qs.append(qh * lax.rsqrt(jnp.sum(qh * qh, axis=-1, keepdims=True) + L2_EPS) * (DK ** -0.5))
        ks.append(kh * lax.rsqrt(jnp.sum(kh * kh, axis=-1, keepdims=True) + L2_EPS))
    q = jnp.concatenate(qs, axis=1)
    k = jnp.concatenate(ks, axis=1)
    v = c[:, 2 * C_BK:]
    beta = _sigmoid(ba)
    glog = -jnp.exp(alog) * _softplus(ba + dtb)
    return q, k, v, beta, glog


def _head_norm_gate(o, z, gnorm):
    outs = []
    for h in range(H_B):
        oh = o[:, DV * h:DV * (h + 1)]
        zh = z[:, DV * h:DV * (h + 1)]
        oh = oh * lax.rsqrt(jnp.mean(oh * oh, axis=-1, keepdims=True) + NORM_EPS) * gnorm
        outs.append(oh * _silu(zh))
    return jnp.concatenate(outs, axis=1)


def _gdn_prompt_kernel(qkv_ref, z_ref, ba_ref, conv_ref, alog_ref, dtb_ref, gnorm_ref, tri_ref,
                       ob_ref, gdn_ref,
                       ext_ref, s_ref, q_ref, k_ref, v_ref, gc_ref, be_ref, o_ref):
    t = pl.program_id(1)

    @pl.when(t == 0)
    def _():
        ext_ref[0:HALO, :] = jnp.zeros((HALO, CONV_CH), F32)
        s_ref[...] = jnp.zeros_like(s_ref)

    x3 = qkv_ref[0]
    ext_ref[HALO:, :] = x3
    x0 = ext_ref[pl.ds(HALO - 3, T_TILE), :]
    x1 = ext_ref[pl.ds(HALO - 2, T_TILE), :]
    x2 = ext_ref[pl.ds(HALO - 1, T_TILE), :]
    ext_ref[0:HALO, :] = x3[T_TILE - HALO:, :]

    q, k, v, beta, glog = _gdn_prep(x0, x1, x2, x3, ba_ref[0], conv_ref[...], alog_ref[...], dtb_ref[...])
    q_ref[...] = q
    k_ref[...] = k
    v_ref[...] = v
    be_ref[...] = beta
    gc_ref[...] = _mm(tri_ref[...], glog, _NN, 1, CUMSUM_TERMS)

    strict, incl, eye = _pair_masks()

    nchunk = T_TILE // CHUNK
    idx = [(pr, c) for pr in range(H_B // 2) for c in range(nchunk)]
    n = range(len(idx))
    rows = lambda c: slice(CHUNK * c, CHUNK * (c + 1))
    cat = lambda ref: [jnp.concatenate([ref[rows(c), DK * h:DK * (h + 1)] for h in (2 * pr, 2 * pr + 1)], axis=0)
                       for pr, c in idx]
    col = lambda ref, off: [jnp.concatenate([ref[rows(c), off + h:off + h + 1] for h in (2 * pr, 2 * pr + 1)], axis=0)
                            for pr, c in idx]
    qs, ks, vs = cat(q_ref), cat(k_ref), cat(v_ref)
    beta_c = col(be_ref, 0)
    gc = col(gc_ref, H_B)
    gt = [jnp.concatenate([jnp.broadcast_to(g_[CHUNK * (i + 1) - 1:CHUNK * (i + 1)], (CHUNK, 1)) for i in range(2)],
                          axis=0) for g_ in gc]
    decay = []
    for g_ in gc:
        gc_full = jnp.broadcast_to(g_, (PAIR, PAIR))
        diff = gc_full - gc_full.T
        decay.append(jnp.where(incl, jnp.exp(jnp.where(incl, diff, 0.0)), 0.0))
    kb = [ks[i] * beta_c[i] for i in n]
    vb = [vs[i] * beta_c[i] for i in n]
    kq = [_mm(jnp.concatenate([kb[i], qs[i]], axis=0), ks[i], _NT) for i in n]
    lmat = [jnp.where(strict, kq[i][:PAIR] * decay[i], 0.0) for i in n]
    qk = [jnp.where(incl, kq[i][PAIR:] * decay[i], 0.0) for i in n]
    tinv = _neumann_inverse_many([-l for l in lmat], eye, INV_TERMS)
    eg = [jnp.exp(g_) for g_ in gc]
    uw = [_mm(tinv[i], jnp.concatenate([vb[i], kb[i] * eg[i]], axis=1)) for i in n]
    ow = [_mm(qk[i], uw[i]) for i in n]
    oc = [ow[i][:, :DV] for i in n]
    rq = [qs[i] * eg[i] - ow[i][:, DV:] for i in n]
    kd = [ks[i] * jnp.exp(gt[i] - gc[i]) for i in n]
    half = lambda j: slice(CHUNK * j, CHUNK * (j + 1))
    pq = [[_mm(kd[i][half(j)], uw[i][half(j)], _TN) for j in range(2)] for i in n]
    pm = [[eye * jnp.exp(gt[i][CHUNK * j:CHUNK * j + 1]) - pq[i][j][:, DV:] for j in range(2)] for i in n]
    s = [s_ref[h] for h in range(H_B)]
    for c in range(nchunk):
        for h in range(H_B):
            i, j = idx.index((h // 2, c)), h % 2
            o_ref[rows(c), DV * h:DV * (h + 1)] = _mm(rq[i][half(j)], s[h]) + oc[i][half(j)]
            s[h] = _mm(pm[i][j], s[h], _NN, CHAIN_TERMS, CHAIN_TERMS) + pq[i][j][:, :DV]
    for h in range(H_B):
        s_ref[h] = s[h]

    ob_ref[0] = _head_norm_gate(o_ref[...], z_ref[0], gnorm_ref[...])
    gdn_ref[0] = s_ref[...]


def _gdn_prompt(qkv, z, ba, conv, alog, dtb, gnorm, tri):
    bsz, seq, _ = qkv.shape
    consts = (conv, alog, dtb, gnorm, tri)
    tok = lambda w: pl.BlockSpec((1, T_TILE, w), lambda b, t: (b, t, 0))
    return pl.pallas_call(
        _gdn_prompt_kernel,
        grid=(bsz, seq // T_TILE),
        in_specs=[tok(CONV_CH), tok(C_BV), tok(LANES)] + [_const_spec(c.shape) for c in consts],
        out_specs=[tok(C_BV), pl.BlockSpec((1, H_B, DK, DV), lambda b, t: (b, 0, 0, 0))],
        out_shape=[jax.ShapeDtypeStruct((bsz, seq, C_BV), F32), jax.ShapeDtypeStruct((bsz, H_B, DK, DV), F32)],
        scratch_shapes=[pltpu.VMEM((T_TILE + HALO, CONV_CH), F32), pltpu.VMEM((H_B, DK, DV), F32),
                        pltpu.VMEM((T_TILE, C_BK), F32), pltpu.VMEM((T_TILE, C_BK), F32),
                        pltpu.VMEM((T_TILE, C_BV), F32), pltpu.VMEM((T_TILE, LANES), F32),
                        pltpu.VMEM((T_TILE, LANES), F32), pltpu.VMEM((T_TILE, C_BV), F32)],
        compiler_params=pltpu.CompilerParams(dimension_semantics=("parallel", "arbitrary"),
                                             vmem_limit_bytes=VMEM_LIMIT),
        name="gdn_prompt",
    )(qkv, z, ba, *consts)


def _to_columns(x):
    pad = jnp.zeros((LANES - DEC_TILE, x.shape[1]), F32)
    return jnp.concatenate([x, pad], axis=0).T


def _decode_kernel(pa_ref, shift_ref, qkv_ref, z_ref, ba_ref, cst_ref, wkv_ref, gdn_ref,
                   mu_ref, w0_ref, w2_ref, a0_ref, a2_ref, g2_ref, kkw_ref, ka_ref, rk_ref, lnw_ref, lnb_ref, hb_ref,
                   conv_ref, alog_ref, dtb_ref, gnorm_ref,
                   oa_ref, ob_ref, wkvo_ref, gdno_ref):
    hb = hb_ref[...]
    r, k2, v, logw, a, g, kk, bonus = _rwkv_prep(
        pa_ref[...], shift_ref[...], mu_ref[...], w0_ref[...], w2_ref[...], a0_ref[...], a2_ref[...], g2_ref[...],
        kkw_ref[...], ka_ref[...], rk_ref[...], hb)
    w = jnp.exp(logw)
    a_in = -kk
    b_in = kk * a
    v_cols = _to_columns(v)
    lane = lax.broadcasted_iota(jnp.int32, (C_A, LANES), 1)
    y_cols = jnp.zeros((C_A, LANES), F32)
    for s_i in range(DEC_TILE):
        ys = []
        for h in range(H_A):
            ls = slice(HEAD_A * h, HEAD_A * (h + 1))
            st = wkv_ref[s_i, h]
            row = lambda x: x[s_i:s_i + 1, ls]
            sa = jnp.sum(st * row(a_in), axis=-1, keepdims=True)
            st = st * row(w) + sa * row(b_in) + v_cols[ls, s_i:s_i + 1] * row(k2)
            wkvo_ref[s_i, h] = st
            ys.append(jnp.sum(st * row(r), axis=-1, keepdims=True))
        y_cols = jnp.where(lane == s_i, jnp.concatenate(ys, axis=0), y_cols)
    y = y_cols.T[:DEC_TILE]
    oa_ref[...] = _group_norm_gate(y, bonus, g, lnw_ref[...], lnb_ref[...], hb)

    cst = cst_ref[...]
    q, k, vv, beta, glog = _gdn_prep(cst[:, :CONV_CH], cst[:, CONV_CH:2 * CONV_CH], cst[:, 2 * CONV_CH:],
                                     qkv_ref[...], ba_ref[...], conv_ref[...], alog_ref[...], dtb_ref[...])
    eg = jnp.exp(glog)
    q_cols = _to_columns(q)
    k_cols = _to_columns(k)
    o_rows = []
    for s_i in range(DEC_TILE):
        o_heads = []
        for h in range(H_B):
            ls = slice(DK * h, DK * (h + 1))
            st = gdn_ref[s_i, h]
            be = beta[s_i:s_i + 1, h:h + 1]
            e = eg[s_i:s_i + 1, H_B + h:H_B + h + 1]
            kc = k_cols[ls, s_i:s_i + 1]
            qc = q_cols[ls, s_i:s_i + 1]
            v_new = be * vv[s_i:s_i + 1, ls] - jnp.sum((kc * (be * e)) * st, axis=0, keepdims=True)
            qk = jnp.sum(q[s_i:s_i + 1, ls] * k[s_i:s_i + 1, ls], axis=-1, keepdims=True)
            o_heads.append(jnp.sum((qc * e) * st, axis=0, keepdims=True) + qk * v_new)
            gdno_ref[s_i, h] = st * e + kc * v_new
        o_rows.append(jnp.concatenate(o_heads, axis=1))
    o = jnp.concatenate(o_rows, axis=0)
    ob_ref[...] = _head_norm_gate(o, z_ref[...], gnorm_ref[...])


def _decode(pa, shift, qkv, z, ba, cst, wkv, gdn, rw_consts, gdn_consts):
    n = pa.shape[0]
    row = lambda w: pl.BlockSpec((DEC_TILE, w), lambda i: (i, 0))
    consts = tuple(rw_consts) + tuple(gdn_consts)
    return pl.pallas_call(
        _decode_kernel,
        grid=(n // DEC_TILE,),
        in_specs=[row(A_COLS), row(A_COLS), row(CONV_CH), row(C_BV), row(LANES), row(3 * CONV_CH),
                  pl.BlockSpec((DEC_TILE, H_A, HEAD_A, HEAD_A), lambda i: (i, 0, 0, 0)),
                  pl.BlockSpec((DEC_TILE, H_B, DK, DV), lambda i: (i, 0, 0, 0))]
                 + [_const_spec(c.shape) for c in consts],
        out_specs=[row(C_A), row(C_BV),
                   pl.BlockSpec((DEC_TILE, H_A, HEAD_A, HEAD_A), lambda i: (i, 0, 0, 0)),
                   pl.BlockSpec((DEC_TILE, H_B, DK, DV), lambda i: (i, 0, 0, 0))],
        out_shape=[jax.ShapeDtypeStruct((n, C_A), F32), jax.ShapeDtypeStruct((n, C_BV), F32),
                   jax.ShapeDtypeStruct(wkv.shape, F32), jax.ShapeDtypeStruct(gdn.shape, F32)],
        compiler_params=pltpu.CompilerParams(dimension_semantics=("parallel",), vmem_limit_bytes=VMEM_LIMIT),
        name="decode",
    )(pa, shift, qkv, z, ba, cst, wkv, gdn, *consts)


def _post_kernel(x_ref, oa_ref, ob_ref, pe_ref, nmix_ref, wg_ref, wba_ref, wbb_ref, wout_ref, nffn_ref,
                 wfg_ref, wfu_ref, wfd_ref, nple_ref, wpg_ref, wpp_ref, nfin_ref, y_ref):
    x = x_ref[...]
    u = _rms(x, nmix_ref[...]).astype(BF16)
    gates = _sigmoid(_dot(u, wg_ref[...]))
    mix = (gates[:, :D_MODEL] * _dot(oa_ref[...].astype(BF16), wba_ref[...])
           + gates[:, D_MODEL:] * _dot(ob_ref[...].astype(BF16), wbb_ref[...]))
    h = x + _dot(mix.astype(BF16), wout_ref[...])
    u2 = _rms(h, nffn_ref[...]).astype(BF16)
    ff = _silu(_dot(u2, wfg_ref[...])) * _dot(u2, wfu_ref[...])
    h = h + _dot(ff.astype(BF16), wfd_ref[...])
    u3 = _rms(h, nple_ref[...]).astype(BF16)
    h = h + _sigmoid(_dot(u3, wpg_ref[...])) * _dot(pe_ref[...].astype(BF16), wpp_ref[...])
    y_ref[...] = _rms(h, nfin_ref[...])


def _post(x, oa, ob, pe, consts):
    n = x.shape[0]
    tm = min(ROW_TILE, n)
    row = lambda w: pl.BlockSpec((tm, w), lambda i: (i, 0))
    return pl.pallas_call(
        _post_kernel,
        grid=(n // tm,),
        in_specs=[row(D_MODEL), row(C_A), row(C_BV), row(PLE_DIM)] + [_const_spec(c.shape) for c in consts],
        out_specs=row(D_MODEL),
        out_shape=jax.ShapeDtypeStruct((n, D_MODEL), F32),
        compiler_params=pltpu.CompilerParams(dimension_semantics=("parallel",), vmem_limit_bytes=VMEM_LIMIT),
        name="post",
    )(x, oa, ob, pe, *consts)


def _chunk_cumsum_matrix():
    i = jnp.arange(T_TILE)
    same = (i[:, None] // CHUNK) == (i[None, :] // CHUNK)
    return (same & (i[None, :] <= i[:, None])).astype(BF16)


def kernel(x_prompt, x_sample, p_prompt, p_sample, state_shift, state_wkv, state_conv, state_gdn, norm_mix, w_in, mu_shift, rw_w0, rw_w2, rw_a0, rw_a2, rw_g2, rw_kk, rw_ka, rw_rk, rw_ln_w, rw_ln_b, gdn_conv, gdn_a_log, gdn_dt_bias, gdn_norm, w_branch_a, w_branch_b, w_out, norm_ffn, w_ffn_gate, w_ffn_up, w_ffn_down, norm_ple, w_ple_gate, w_ple_proj, norm_final):
    bsz, seq, _ = x_prompt.shape
    nd = x_sample.shape[0]
    row = lambda p: p.reshape(1, -1)

    w_in0 = w_in[0]
    b0 = A_COLS
    wa = w_in0[:, :b0].astype(BF16)
    wq = w_in0[:, b0:b0 + CONV_CH].astype(BF16)
    wz = w_in0[:, b0 + CONV_CH:b0 + CONV_CH + C_BV].astype(BF16)
    wb = jnp.pad(w_in0[:, b0 + CONV_CH + C_BV:b0 + CONV_CH + C_BV + 2 * H_B], ((0, 0), (0, LANES - 2 * H_B))).astype(BF16)
    wg = w_in0[:, b0 + CONV_CH + C_BV + 2 * H_B:].astype(BF16)
    w2p = jnp.concatenate([rw_w2[0], jnp.zeros((LORA_A, C_A), F32)], axis=0).astype(BF16)
    a2p = jnp.concatenate([jnp.zeros((LORA_W, C_A), F32), rw_a2[0]], axis=0).astype(BF16)
    ch = jnp.arange(C_A) // HEAD_A
    hb = (ch[:, None] == ch[None, :]).astype(BF16)
    tri = _chunk_cumsum_matrix()
    alog = jnp.pad(gdn_a_log[0], (H_B, LANES - 2 * H_B)).reshape(1, LANES)
    dtb = jnp.pad(gdn_dt_bias[0], (H_B, LANES - 2 * H_B)).reshape(1, LANES)
    rw_consts = (row(mu_shift[0]), row(rw_w0[0]), w2p, row(rw_a0[0]), a2p, rw_g2[0].astype(BF16), row(rw_kk[0]), row(rw_ka[0]),
                 row(rw_rk[0]), row(rw_ln_w[0]), row(rw_ln_b[0]), hb)
    gdn_consts = (gdn_conv[0], alog, dtb, row(gdn_norm[0]))
    post_consts = (row(norm_mix[0]), wg, w_branch_a[0].astype(BF16), w_branch_b[0].astype(BF16),
                   w_out[0].astype(BF16), row(norm_ffn[0]), w_ffn_gate[0].astype(BF16), w_ffn_up[0].astype(BF16),
                   w_ffn_down[0].astype(BF16), row(norm_ple[0]), w_ple_gate[0].astype(BF16),
                   w_ple_proj[0].astype(BF16), row(norm_final))

    xp = x_prompt.reshape(bsz * seq, D_MODEL)
    pa, qkv, z, ba = _inproj(xp, row(norm_mix[0]), wa, wq, wz, wb)
    pa3 = pa.reshape(bsz, seq, A_COLS)
    qkv3 = qkv.reshape(bsz, seq, CONV_CH)
    oa, wkv_pairs = _rwkv_prompt(pa3, *rw_consts, tri)
    ob, gdn_p = _gdn_prompt(qkv3, z.reshape(bsz, seq, C_BV), ba.reshape(bsz, seq, LANES), *gdn_consts, tri)
    y_prompt = _post(xp, oa.reshape(bsz * seq, C_A), ob.reshape(bsz * seq, C_BV),
                     p_prompt[0].reshape(bsz * seq, PLE_DIM), post_consts).reshape(bsz, seq, D_MODEL)
    wkv_p = jnp.stack([wkv_pairs[:, :, :HEAD_A, :HEAD_A], wkv_pairs[:, :, HEAD_A:, HEAD_A:]], axis=2)
    wkv_p = wkv_p.reshape(bsz, H_A, HEAD_A, HEAD_A)
    shift_p = pa3[:, seq - 1:, :]
    conv_p = qkv3[:, seq - (CONV_W - 1):, :]

    xs = x_sample.reshape(nd, D_MODEL)
    pa_s, qkv_s, z_s, ba_s = _inproj(xs, row(norm_mix[0]), wa, wq, wz, wb)
    cst = state_conv[0].reshape(nd, (CONV_W - 1) * CONV_CH)
    oa_s, ob_s, wkv_s, gdn_s = _decode(pa_s, state_shift[0].reshape(nd, A_COLS), qkv_s, z_s, ba_s, cst,
                                       state_wkv[0], state_gdn[0], rw_consts, gdn_consts)
    y_sample = _post(xs, oa_s, ob_s, p_sample[0].reshape(nd, PLE_DIM), post_consts).reshape(nd, 1, D_MODEL)
    conv_s = jnp.concatenate([cst[:, CONV_CH:], qkv_s], axis=1).reshape(nd, CONV_W - 1, CONV_CH)

    return (y_prompt, y_sample, shift_p[None], wkv_p[None], conv_p[None], gdn_p[None],
            pa_s.reshape(1, nd, 1, A_COLS), wkv_s[None], conv_s[None], gdn_s[None])
```

```python
import functools

import jax
import jax.numpy as jnp
from jax import lax
from jax.experimental import pallas as pl
from jax.experimental.pallas import tpu as pltpu

F32 = jnp.float32
BF16 = jnp.bfloat16
HI = lax.Precision.HIGHEST

D_MODEL = 1024
HEAD_A = 64
C_A = 512
H_A = 8
LORA_W = 64
LORA_A = 64
LORA_G = 128
A_COLS = 3 * C_A + LORA_W + LORA_A + LORA_G
DK = 128
DV = 128
H_B = 4
C_BK = 512
C_BV = 512
CONV_W = 4
CONV_CH = 2 * C_BK + C_BV
D_FF = 2816
PLE_DIM = 256
NORM_EPS = 1e-6
GN_EPS = 64e-5
L2_EPS = 1e-6

LANES = 128
CHUNK = 64
PAIR = 2 * CHUNK
T_TILE = 256
ROW_TILE = 256
DEC_TILE = 8
HALO = 8
VMEM_LIMIT = 56 * 1024 * 1024
SUM_TERMS = 2
CUMSUM_TERMS = 3
INV_TERMS = 1
CHAIN_TERMS = 1
PAIR_GROUP = 2


def _dot(a, b, prec=None):
    return jnp.dot(a, b, preferred_element_type=F32, precision=prec)


def _dot_nt(a, b, prec=None):
    return lax.dot_general(a, b, (((1,), (1,)), ((), ())), preferred_element_type=F32, precision=prec)


def _dot_tn(a, b, prec=None):
    return lax.dot_general(a, b, (((0,), (0,)), ((), ())), preferred_element_type=F32, precision=prec)


def _sigmoid(x):
    return 1.0 / (1.0 + jnp.exp(-x))


def _silu(x):
    return x * _sigmoid(x)


def _softplus(x):
    return jnp.maximum(x, 0.0) + jnp.log(1.0 + jnp.exp(-jnp.abs(x)))


def _rms(x, gain):
    return x * lax.rsqrt(jnp.mean(x * x, axis=-1, keepdims=True) + NORM_EPS) * gain


def _pair_masks():
    ri = lax.broadcasted_iota(jnp.int32, (PAIR, PAIR), 0)
    ci = lax.broadcasted_iota(jnp.int32, (PAIR, PAIR), 1)
    same = (ri < CHUNK) == (ci < CHUNK)
    strict = same & (ci < ri)
    incl = same & (ci <= ri)
    eye = (ri == ci).astype(F32)
    return strict, incl, eye


_NN = (((1,), (0,)), ((), ()))
_NT = (((1,), (1,)), ((), ()))
_TN = (((0,), (0,)), ((), ()))


def _split(x, terms):
    if isinstance(x, (list, tuple)):
        return list(x)
    if x.dtype == BF16:
        return [x]
    parts = []
    for i in range(terms):
        h = x.astype(BF16)
        parts.append(h)
        if i + 1 < terms:
            x = x - h.astype(F32)
    return parts


def _mm(a, b, dims=_NN, na=1, nb=1):
    pa, pb = _split(a, na), _split(b, nb)
    acc = None
    for i, ai in enumerate(pa):
        for j, bj in enumerate(pb):
            if i + j < max(len(pa), len(pb)):
                d = lax.dot_general(ai, bj, dims, preferred_element_type=F32)
                acc = d if acc is None else acc + d
    return acc


def _neumann_inverse(lmat, eye, terms):
    t = eye + lmat
    ps = _split(lmat, terms)
    n = 2
    while n < CHUNK:
        ps = _split(_mm(ps, ps), terms)
        t = t + _mm(t, ps, _NN, terms)
        n *= 2
    return t


def _neumann_inverse_many(lmats, eye, terms):
    ts = [eye + l for l in lmats]
    ps = [_split(l, terms) for l in lmats]
    n = 2
    while n < CHUNK:
        ps = [_split(_mm(p, p), terms) for p in ps]
        ts = [t + _mm(t, p, _NN, terms) for t, p in zip(ts, ps)]
        n *= 2
    return ts


def _inproj_kernel(x_ref, g_ref, wa_ref, wq_ref, wz_ref, wb_ref, pa_ref, qkv_ref, z_ref, ba_ref):
    u = _rms(x_ref[...], g_ref[...]).astype(BF16)
    pa_ref[...] = _dot(u, wa_ref[...])
    qkv_ref[...] = _dot(u, wq_ref[...])
    z_ref[...] = _dot(u, wz_ref[...])
    ba_ref[...] = _dot(u, wb_ref[...])


def _const_spec(shape):
    nd = len(shape)
    return pl.BlockSpec(shape, lambda *_: (0,) * nd, pipeline_mode=pl.Buffered(1))


def _inproj(x, gain, wa, wq, wz, wb):
    n = x.shape[0]
    tm = min(ROW_TILE, n)
    row = lambda w: pl.BlockSpec((tm, w), lambda i: (i, 0))
    return pl.pallas_call(
        _inproj_kernel,
        grid=(n // tm,),
        in_specs=[row(D_MODEL), _const_spec(gain.shape), _const_spec(wa.shape), _const_spec(wq.shape),
                  _const_spec(wz.shape), _const_spec(wb.shape)],
        out_specs=[row(A_COLS), row(CONV_CH), row(C_BV), row(LANES)],
        out_shape=[jax.ShapeDtypeStruct((n, A_COLS), F32), jax.ShapeDtypeStruct((n, CONV_CH), F32),
                   jax.ShapeDtypeStruct((n, C_BV), F32), jax.ShapeDtypeStruct((n, LANES), F32)],
        compiler_params=pltpu.CompilerParams(dimension_semantics=("parallel",), vmem_limit_bytes=VMEM_LIMIT),
        name="inproj",
    )(x, gain, wa, wq, wz, wb)


def _rwkv_prep(pa, prev, mu, w0, w2p, a0, a2p, g2, kkw, ka, rk, hb):
    xa = pa + (prev - pa) * mu
    r = xa[:, :C_A]
    k = xa[:, C_A:2 * C_A]
    v = xa[:, 2 * C_A:3 * C_A]
    xwa = xa[:, 3 * C_A:3 * C_A + LORA_W + LORA_A]
    xg = xa[:, 3 * C_A + LORA_W + LORA_A:]
    w_log = -_softplus(-(w0 + _mm(jnp.tanh(xwa), w2p))) - 0.5
    logw = -jnp.exp(w_log)
    a = _sigmoid(a0 + _mm(xwa, a2p))
    g = _mm(_sigmoid(xg), g2)
    kx = k * kkw
    kk = kx * lax.rsqrt(_head_sum(kx * kx, hb) + L2_EPS)
    k2 = k * (1.0 + (a - 1.0) * ka)
    bonus = _head_sum(r * k2 * rk, hb) * v
    return r, k2, v, logw, a, g, kk, bonus


def _head_sum(x, hb):
    return _mm(x, hb, _NN, SUM_TERMS)


def _group_norm_gate(y, bonus, g, lnw, lnb, hb):
    mean = _head_sum(y, hb) * (1.0 / HEAD_A)
    d = y - mean
    var = _head_sum(d * d, hb) * (1.0 / HEAD_A)
    yn = d * lax.rsqrt(var + GN_EPS) * lnw + lnb
    return (yn + bonus) * g


def _rwkv_prompt_kernel(pa_ref, mu_ref, w0_ref, w2_ref, a0_ref, a2_ref, g2_ref, kkw_ref, ka_ref, rk_ref,
                        lnw_ref, lnb_ref, hb_ref, tri_ref,
                        oa_ref, wkv_ref,
                        ext_ref, s_ref, at_ref, bt_ref, kt_ref, rt_ref, v_ref, bh_ref, kh_ref, y_ref):
    t = pl.program_id(1)

    @pl.when(t == 0)
    def _():
        ext_ref[0:HALO, :] = jnp.zeros((HALO, A_COLS), F32)
        s_ref[...] = jnp.zeros_like(s_ref)

    pa = pa_ref[0]
    ext_ref[HALO:, :] = pa
    prev = ext_ref[pl.ds(HALO - 1, T_TILE), :]
    ext_ref[0:HALO, :] = pa[T_TILE - HALO:, :]

    hb = hb_ref[...]
    r, k2, v, logw, a, g, kk, bonus = _rwkv_prep(
        pa, prev, mu_ref[...], w0_ref[...], w2_ref[...], a0_ref[...], a2_ref[...], g2_ref[...],
        kkw_ref[...], ka_ref[...], rk_ref[...], hb)

    nchunk = T_TILE // CHUNK
    cs = _mm(tri_ref[...], logw, _NN, 1, CUMSUM_TERMS)
    ends = [cs[CHUNK * (c + 1) - 1:CHUNK * (c + 1)] for c in range(nchunk)]
    tot = jnp.concatenate([jnp.broadcast_to(e, (CHUNK, C_A)) for e in ends], axis=0)
    dinv = jnp.exp(-cs)
    dend = jnp.exp(tot - cs)
    b_in = kk * a
    at_ref[...] = -kk * jnp.exp(cs - logw)
    bt_ref[...] = b_in * dinv
    kt_ref[...] = k2 * dinv
    rt_ref[...] = r * jnp.exp(cs)
    v_ref[...] = v
    bh_ref[...] = b_in * dend
    kh_ref[...] = k2 * dend
    gl = [jnp.exp(e) for e in ends]

    strict, incl, eye = _pair_masks()
    low = lax.broadcasted_iota(jnp.int32, (CHUNK, LANES), 1) < HEAD_A

    def stack(x):
        return jnp.concatenate([jnp.where(low, x, 0.0), jnp.where(low, 0.0, x)], axis=0)

    npair = C_A // LANES
    for p0 in range(0, npair, PAIR_GROUP):
        idx = [(p, c) for p in range(p0, p0 + PAIR_GROUP) for c in range(nchunk)]
        lanes = lambda p: slice(LANES * p, LANES * (p + 1))
        rows = lambda c: slice(CHUNK * c, CHUNK * (c + 1))
        ld = lambda ref: [stack(ref[rows(c), lanes(p)]) for p, c in idx]
        at, bt, kt, rt, vs, bh, kh = (ld(ref) for ref in (at_ref, bt_ref, kt_ref, rt_ref, v_ref, bh_ref, kh_ref))
        n = range(len(idx))
        aa = [_mm(jnp.concatenate([at[i], rt[i]], axis=0), jnp.concatenate([bt[i], kt[i]], axis=0), _NT) for i in n]
        a_ab = [jnp.where(strict, aa[i][:PAIR, :PAIR], 0.0) for i in n]
        a_ak = [jnp.where(strict, aa[i][:PAIR, PAIR:], 0.0) for i in n]
        a_rb = [jnp.where(incl, aa[i][PAIR:, :PAIR], 0.0) for i in n]
        a_rk = [jnp.where(incl, aa[i][PAIR:, PAIR:], 0.0) for i in n]
        tinv = _neumann_inverse_many(a_ab, eye, INV_TERMS)
        akv = [_mm(a_ak[i], vs[i]) for i in n]
        gu = [_mm(tinv[i], jnp.concatenate([at[i], akv[i]], axis=1)) for i in n]
        ry = [_mm(a_rb[i], gu[i]) for i in n]
        rkv = [_mm(a_rk[i], vs[i]) for i in n]
        pq = [_mm(gu[i], bh[i], _TN) for i in n]
        vk = [_mm(vs[i], kh[i], _TN) for i in n]
        rp = [rt[i] + ry[i][:, :PAIR] for i in n]
        yc = [ry[i][:, PAIR:] + rkv[i] for i in n]
        pm = [eye * gl[idx[i][1]][:, lanes(idx[i][0])] + pq[i][:PAIR] for i in n]
        qm = [pq[i][PAIR:] + vk[i] for i in n]
        s = {p: s_ref[p] for p in range(p0, p0 + PAIR_GROUP)}
        for c in range(nchunk):
            for p in range(p0, p0 + PAIR_GROUP):
                i = idx.index((p, c))
                y = _mm(rp[i], s[p], _NT) + yc[i]
                y_ref[rows(c), lanes(p)] = y[:CHUNK] + y[CHUNK:]
                s[p] = _mm(s[p], pm[i], _NN, CHAIN_TERMS, CHAIN_TERMS) + qm[i]
        for p in range(p0, p0 + PAIR_GROUP):
            s_ref[p] = s[p]

    oa_ref[0] = _group_norm_gate(y_ref[...], bonus, g, lnw_ref[...], lnb_ref[...], hb)
    wkv_ref[0] = s_ref[...]


def _rwkv_prompt(pa, mu, w0, w2p, a0, a2p, g2, kkw, ka, rk, lnw, lnb, hb, tri):
    bsz, seq, _ = pa.shape
    consts = (mu, w0, w2p, a0, a2p, g2, kkw, ka, rk, lnw, lnb, hb, tri)
    npair = C_A // LANES
    tile = lambda: pltpu.VMEM((T_TILE, C_A), F32)
    return pl.pallas_call(
        _rwkv_prompt_kernel,
        grid=(bsz, seq // T_TILE),
        in_specs=[pl.BlockSpec((1, T_TILE, A_COLS), lambda b, t: (b, t, 0))] + [_const_spec(c.shape) for c in consts],
        out_specs=[pl.BlockSpec((1, T_TILE, C_A), lambda b, t: (b, t, 0)),
                   pl.BlockSpec((1, npair, LANES, LANES), lambda b, t: (b, 0, 0, 0))],
        out_shape=[jax.ShapeDtypeStruct((bsz, seq, C_A), F32),
                   jax.ShapeDtypeStruct((bsz, npair, LANES, LANES), F32)],
        scratch_shapes=[pltpu.VMEM((T_TILE + HALO, A_COLS), F32), pltpu.VMEM((npair, LANES, LANES), F32)]
                       + [tile() for _ in range(8)],
        compiler_params=pltpu.CompilerParams(dimension_semantics=("parallel", "arbitrary"),
                                             vmem_limit_bytes=VMEM_LIMIT),
        name="rwkv_prompt",
    )(pa, *consts)


def _gdn_prep(x0, x1, x2, x3, ba, conv, alog, dtb):
    c = _silu(x0 * conv[0:1] + x1 * conv[1:2] + x2 * conv[2:3] + x3 * conv[3:4])
    qs, ks = [], []
    for h in range(H_B):
        qh = c[:, DK * h:DK * (h + 1)]
        kh = c[:, C_BK + DK * h:C_BK + DK * (h + 1)]
        qs.append(qh * lax.rsqrt(jnp.sum(qh * qh, axis=-1, keepdims=True) + L2_EPS) * (DK ** -0.5))
        ks.append(kh * lax.rsqrt(jnp.sum(kh * kh, axis=-1, keepdims=True) + L2_EPS))
    q = jnp.concatenate(qs, axis=1)
    k = jnp.concatenate(ks, axis=1)
    v = c[:, 2 * C_BK:]
    beta = _sigmoid(ba)
    glog = -jnp.exp(alog) * _softplus(ba + dtb)
    return q, k, v, beta, glog


def _head_norm_gate(o, z, gnorm):
    outs = []
    for h in range(H_B):
        oh = o[:, DV * h:DV * (h + 1)]
        zh = z[:, DV * h:DV * (h + 1)]
        oh = oh * lax.rsqrt(jnp.mean(oh * oh, axis=-1, keepdims=True) + NORM_EPS) * gnorm
        outs.append(oh * _silu(zh))
    return jnp.concatenate(outs, axis=1)


def _gdn_prompt_kernel(qkv_ref, z_ref, ba_ref, conv_ref, alog_ref, dtb_ref, gnorm_ref, tri_ref,
                       ob_ref, gdn_ref,
                       ext_ref, s_ref, q_ref, k_ref, v_ref, gc_ref, be_ref, o_ref):
    t = pl.program_id(1)

    @pl.when(t == 0)
    def _():
        ext_ref[0:HALO, :] = jnp.zeros((HALO, CONV_CH), F32)
        s_ref[...] = jnp.zeros_like(s_ref)

    x3 = qkv_ref[0]
    ext_ref[HALO:, :] = x3
    x0 = ext_ref[pl.ds(HALO - 3, T_TILE), :]
    x1 = ext_ref[pl.ds(HALO - 2, T_TILE), :]
    x2 = ext_ref[pl.ds(HALO - 1, T_TILE), :]
    ext_ref[0:HALO, :] = x3[T_TILE - HALO:, :]

    q, k, v, beta, glog = _gdn_prep(x0, x1, x2, x3, ba_ref[0], conv_ref[...], alog_ref[...], dtb_ref[...])
    q_ref[...] = q
    k_ref[...] = k
    v_ref[...] = v
    be_ref[...] = beta
    gc_ref[...] = _mm(tri_ref[...], glog, _NN, 1, CUMSUM_TERMS)

    strict, incl, eye = _pair_masks()

    nchunk = T_TILE // CHUNK
    idx = [(pr, c) for pr in range(H_B // 2) for c in range(nchunk)]
    n = range(len(idx))
    rows = lambda c: slice(CHUNK * c, CHUNK * (c + 1))
    cat = lambda ref: [jnp.concatenate([ref[rows(c), DK * h:DK * (h + 1)] for h in (2 * pr, 2 * pr + 1)], axis=0)
                       for pr, c in idx]
    col = lambda ref, off: [jnp.concatenate([ref[rows(c), off + h:off + h + 1] for h in (2 * pr, 2 * pr + 1)], axis=0)
                            for pr, c in idx]
    qs, ks, vs = cat(q_ref), cat(k_ref), cat(v_ref)
    beta_c = col(be_ref, 0)
    gc = col(gc_ref, H_B)
    gt = [jnp.concatenate([jnp.broadcast_to(g_[CHUNK * (i + 1) - 1:CHUNK * (i + 1)], (CHUNK, 1)) for i in range(2)],
                          axis=0) for g_ in gc]
    decay = []
    for g_ in gc:
        gc_full = jnp.broadcast_to(g_, (PAIR, PAIR))
        diff = gc_full - gc_full.T
        decay.append(jnp.where(incl, jnp.exp(jnp.where(incl, diff, 0.0)), 0.0))
    kb = [ks[i] * beta_c[i] for i in n]
    vb = [vs[i] * beta_c[i] for i in n]
    kq = [_mm(jnp.concatenate([kb[i], qs[i]], axis=0), ks[i], _NT) for i in n]
    lmat = [jnp.where(strict, kq[i][:PAIR] * decay[i], 0.0) for i in n]
    qk = [jnp.where(incl, kq[i][PAIR:] * decay[i], 0.0) for i in n]
    tinv = _neumann_inverse_many([-l for l in lmat], eye, INV_TERMS)
    eg = [jnp.exp(g_) for g_ in gc]
    uw = [_mm(tinv[i], jnp.concatenate([vb[i], kb[i] * eg[i]], axis=1)) for i in n]
    ow = [_mm(qk[i], uw[i]) for i in n]
    oc = [ow[i][:, :DV] for i in n]
    rq = [qs[i] * eg[i] - ow[i][:, DV:] for i in n]
    kd = [ks[i] * jnp.exp(gt[i] - gc[i]) for i in n]
    half = lambda j: slice(CHUNK * j, CHUNK * (j + 1))
    pq = [[_mm(kd[i][half(j)], uw[i][half(j)], _TN) for j in range(2)] for i in n]
    pm = [[eye * jnp.exp(gt[i][CHUNK * j:CHUNK * j + 1]) - pq[i][j][:, DV:] for j in range(2)] for i in n]
    s = [s_ref[h] for h in range(H_B)]
    for c in range(nchunk):
        for h in range(H_B):
            i, j = idx.index((h // 2, c)), h % 2
            o_ref[rows(c), DV * h:DV * (h + 1)] = _mm(rq[i][half(j)], s[h]) + oc[i][half(j)]
            s[h] = _mm(pm[i][j], s[h], _NN, CHAIN_TERMS, CHAIN_TERMS) + pq[i][j][:, :DV]
    for h in range(H_B):
        s_ref[h] = s[h]

    ob_ref[0] = _head_norm_gate(o_ref[...], z_ref[0], gnorm_ref[...])
    gdn_ref[0] = s_ref[...]


def _gdn_prompt(qkv, z, ba, conv, alog, dtb, gnorm, tri):
    bsz, seq, _ = qkv.shape
    consts = (conv, alog, dtb, gnorm, tri)
    tok = lambda w: pl.BlockSpec((1, T_TILE, w), lambda b, t: (b, t, 0))
    return pl.pallas_call(
        _gdn_prompt_kernel,
        grid=(bsz, seq // T_TILE),
        in_specs=[tok(CONV_CH), tok(C_BV), tok(LANES)] + [_const_spec(c.shape) for c in consts],
        out_specs=[tok(C_BV), pl.BlockSpec((1, H_B, DK, DV), lambda b, t: (b, 0, 0, 0))],
        out_shape=[jax.ShapeDtypeStruct((bsz, seq, C_BV), F32), jax.ShapeDtypeStruct((bsz, H_B, DK, DV), F32)],
        scratch_shapes=[pltpu.VMEM((T_TILE + HALO, CONV_CH), F32), pltpu.VMEM((H_B, DK, DV), F32),
                        pltpu.VMEM((T_TILE, C_BK), F32), pltpu.VMEM((T_TILE, C_BK), F32),
                        pltpu.VMEM((T_TILE, C_BV), F32), pltpu.VMEM((T_TILE, LANES), F32),
                        pltpu.VMEM((T_TILE, LANES), F32), pltpu.VMEM((T_TILE, C_BV), F32)],
        compiler_params=pltpu.CompilerParams(dimension_semantics=("parallel", "arbitrary"),
                                             vmem_limit_bytes=VMEM_LIMIT),
        name="gdn_prompt",
    )(qkv, z, ba, *consts)


def _to_columns(x):
    pad = jnp.zeros((LANES - DEC_TILE, x.shape[1]), F32)
    return jnp.concatenate([x, pad], axis=0).T


def _decode_rwkv_kernel(pa_ref, shift_ref, wkv_ref,
                        mu_ref, w0_ref, w2_ref, a0_ref, a2_ref, g2_ref, kkw_ref, ka_ref, rk_ref, lnw_ref, lnb_ref,
                        hb_ref, oa_ref, wkvo_ref, tr_ref, yt_ref, g_ref, bonus_ref):
    h = pl.program_id(0)

    @pl.when(h == 0)
    def _():
        r, k2, v, logw, a, g, kk, bonus = _rwkv_prep(
            pa_ref[...], shift_ref[...], mu_ref[...], w0_ref[...], w2_ref[...], a0_ref[...], a2_ref[...],
            g2_ref[...], kkw_ref[...], ka_ref[...], rk_ref[...], hb_ref[...])
        for i, x in enumerate((-kk, jnp.exp(logw), kk * a, k2, r, v)):
            tr_ref[i] = x.T
        g_ref[...] = g
        bonus_ref[...] = bonus

    base = pl.multiple_of(h * HEAD_A, HEAD_A)
    hs = pl.ds(base, HEAD_A)
    a_t, w_t, b_t, k_t, r_t = (tr_ref[i, hs, :] for i in range(5))

    def value_row(vi, carry):
        st = wkv_ref[0, vi]
        sa = jnp.sum(st * a_t, axis=0, keepdims=True)
        st = st * w_t + sa * b_t + tr_ref[5, pl.ds(base + vi, 1), :] * k_t
        wkvo_ref[0, vi] = st
        yt_ref[pl.ds(base + vi, 1), :] = jnp.sum(st * r_t, axis=0, keepdims=True)
        return carry

    lax.fori_loop(0, HEAD_A, value_row, 0, unroll=8)

    @pl.when(h == H_A - 1)
    def _():
        oa_ref[...] = _group_norm_gate(yt_ref[...].T, bonus_ref[...], g_ref[...], lnw_ref[...], lnb_ref[...],
                                       hb_ref[...])


def _decode_rwkv(pa, shift, wkv_t, rw_consts):
    n = pa.shape[0]
    full = lambda w: pl.BlockSpec((n, w), lambda h: (0, 0))
    state = pl.BlockSpec((1, HEAD_A, HEAD_A, n), lambda h: (h, 0, 0, 0))
    return pl.pallas_call(
        _decode_rwkv_kernel,
        grid=(H_A,),
        in_specs=[full(A_COLS), full(A_COLS), state] + [_const_spec(c.shape) for c in rw_consts],
        out_specs=[full(C_A), state],
        out_shape=[jax.ShapeDtypeStruct((n, C_A), F32), jax.ShapeDtypeStruct(wkv_t.shape, F32)],
        scratch_shapes=[pltpu.VMEM((6, C_A, n), F32), pltpu.VMEM((C_A, n), F32),
                        pltpu.VMEM((n, C_A), F32), pltpu.VMEM((n, C_A), F32)],
        compiler_params=pltpu.CompilerParams(dimension_semantics=("arbitrary",), vmem_limit_bytes=VMEM_LIMIT),
        name="decode_rwkv",
    )(pa, shift, wkv_t, *rw_consts)


def _decode_gdn_kernel(qkv_ref, z_ref, ba_ref, cst_ref, gdn_ref, conv_ref, alog_ref, dtb_ref, gnorm_ref,
                       ob_ref, gdno_ref):
    cst = cst_ref[...]
    q, k, vv, beta, glog = _gdn_prep(cst[:, :CONV_CH], cst[:, CONV_CH:2 * CONV_CH], cst[:, 2 * CONV_CH:],
                                     qkv_ref[...], ba_ref[...], conv_ref[...], alog_ref[...], dtb_ref[...])
    eg = jnp.exp(glog)
    q_cols = _to_columns(q)
    k_cols = _to_columns(k)
    gunits = [(s_i, h) for s_i in range(DEC_TILE) for h in range(H_B)]
    gn = range(len(gunits))
    ls = lambda h: slice(DK * h, DK * (h + 1))
    be = [beta[s_i:s_i + 1, h:h + 1] for s_i, h in gunits]
    e = [eg[s_i:s_i + 1, H_B + h:H_B + h + 1] for s_i, h in gunits]
    kc = [k_cols[ls(h), s_i:s_i + 1] for s_i, h in gunits]
    qc = [q_cols[ls(h), s_i:s_i + 1] for s_i, h in gunits]
    gst = [gdn_ref[s_i, h] for s_i, h in gunits]
    ws = [jnp.sum((kc[i] * (be[i] * e[i])) * gst[i], axis=0, keepdims=True) for i in gn]
    qs = [jnp.sum((qc[i] * e[i]) * gst[i], axis=0, keepdims=True) for i in gn]
    v_new = [be[i] * vv[s_i:s_i + 1, ls(h)] - ws[i] for i, (s_i, h) in enumerate(gunits)]
    qk = [jnp.sum(q[s_i:s_i + 1, ls(h)] * k[s_i:s_i + 1, ls(h)], axis=-1, keepdims=True) for s_i, h in gunits]
    for i, (s_i, h) in enumerate(gunits):
        gdno_ref[s_i, h] = gst[i] * e[i] + kc[i] * v_new[i]
    o_units = [qs[i] + qk[i] * v_new[i] for i in gn]
    o = jnp.concatenate([jnp.concatenate(o_units[H_B * s_i:H_B * (s_i + 1)], axis=1) for s_i in range(DEC_TILE)],
                        axis=0)
    ob_ref[...] = _head_norm_gate(o, z_ref[...], gnorm_ref[...])


def _decode_gdn(qkv, z, ba, cst, gdn, gdn_consts):
    n = qkv.shape[0]
    row = lambda w: pl.BlockSpec((DEC_TILE, w), lambda i: (i, 0))
    state = pl.BlockSpec((DEC_TILE, H_B, DK, DV), lambda i: (i, 0, 0, 0))
    return pl.pallas_call(
        _decode_gdn_kernel,
        grid=(n // DEC_TILE,),
        in_specs=[row(CONV_CH), row(C_BV), row(LANES), row(3 * CONV_CH), state]
                 + [_const_spec(c.shape) for c in gdn_consts],
        out_specs=[row(C_BV), state],
        out_shape=[jax.ShapeDtypeStruct((n, C_BV), F32), jax.ShapeDtypeStruct(gdn.shape, F32)],
        compiler_params=pltpu.CompilerParams(dimension_semantics=("parallel",), vmem_limit_bytes=VMEM_LIMIT),
        name="decode_gdn",
    )(qkv, z, ba, cst, gdn, *gdn_consts)


def _post_kernel(x_ref, oa_ref, ob_ref, pe_ref, nmix_ref, wg_ref, wba_ref, wbb_ref, wout_ref, nffn_ref,
                 wfg_ref, wfu_ref, wfd_ref, nple_ref, wpg_ref, wpp_ref, nfin_ref, y_ref):
    x = x_ref[...]
    u = _rms(x, nmix_ref[...]).astype(BF16)
    gates = _sigmoid(_dot(u, wg_ref[...]))
    mix = (gates[:, :D_MODEL] * _dot(oa_ref[...].astype(BF16), wba_ref[...])
           + gates[:, D_MODEL:] * _dot(ob_ref[...].astype(BF16), wbb_ref[...]))
    h = x + _dot(mix.astype(BF16), wout_ref[...])
    u2 = _rms(h, nffn_ref[...]).astype(BF16)
    ff = _silu(_dot(u2, wfg_ref[...])) * _dot(u2, wfu_ref[...])
    h = h + _dot(ff.astype(BF16), wfd_ref[...])
    u3 = _rms(h, nple_ref[...]).astype(BF16)
    h = h + _sigmoid(_dot(u3, wpg_ref[...])) * _dot(pe_ref[...].astype(BF16), wpp_ref[...])
    y_ref[...] = _rms(h, nfin_ref[...])


def _post(x, oa, ob, pe, consts):
    n = x.shape[0]
    tm = min(ROW_TILE, n)
    row = lambda w: pl.BlockSpec((tm, w), lambda i: (i, 0))
    return pl.pallas_call(
        _post_kernel,
        grid=(n // tm,),
        in_specs=[row(D_MODEL), row(C_A), row(C_BV), row(PLE_DIM)] + [_const_spec(c.shape) for c in consts],
        out_specs=row(D_MODEL),
        out_shape=jax.ShapeDtypeStruct((n, D_MODEL), F32),
        compiler_params=pltpu.CompilerParams(dimension_semantics=("parallel",), vmem_limit_bytes=VMEM_LIMIT),
        name="post",
    )(x, oa, ob, pe, *consts)


def _chunk_cumsum_matrix():
    i = jnp.arange(T_TILE)
    same = (i[:, None] // CHUNK) == (i[None, :] // CHUNK)
    return (same & (i[None, :] <= i[:, None])).astype(BF16)


def kernel(x_prompt, x_sample, p_prompt, p_sample, state_shift, state_wkv, state_conv, state_gdn, norm_mix, w_in, mu_shift, rw_w0, rw_w2, rw_a0, rw_a2, rw_g2, rw_kk, rw_ka, rw_rk, rw_ln_w, rw_ln_b, gdn_conv, gdn_a_log, gdn_dt_bias, gdn_norm, w_branch_a, w_branch_b, w_out, norm_ffn, w_ffn_gate, w_ffn_up, w_ffn_down, norm_ple, w_ple_gate, w_ple_proj, norm_final):
    bsz, seq, _ = x_prompt.shape
    nd = x_sample.shape[0]
    row = lambda p: p.reshape(1, -1)

    w_in0 = w_in[0]
    b0 = A_COLS
    wa = w_in0[:, :b0].astype(BF16)
    wq = w_in0[:, b0:b0 + CONV_CH].astype(BF16)
    wz = w_in0[:, b0 + CONV_CH:b0 + CONV_CH + C_BV].astype(BF16)
    wb = jnp.pad(w_in0[:, b0 + CONV_CH + C_BV:b0 + CONV_CH + C_BV + 2 * H_B], ((0, 0), (0, LANES - 2 * H_B))).astype(BF16)
    wg = w_in0[:, b0 + CONV_CH + C_BV + 2 * H_B:].astype(BF16)
    w2p = jnp.concatenate([rw_w2[0], jnp.zeros((LORA_A, C_A), F32)], axis=0).astype(BF16)
    a2p = jnp.concatenate([jnp.zeros((LORA_W, C_A), F32), rw_a2[0]], axis=0).astype(BF16)
    ch = jnp.arange(C_A) // HEAD_A
    hb = (ch[:, None] == ch[None, :]).astype(BF16)
    tri = _chunk_cumsum_matrix()
    alog = jnp.pad(gdn_a_log[0], (H_B, LANES - 2 * H_B)).reshape(1, LANES)
    dtb = jnp.pad(gdn_dt_bias[0], (H_B, LANES - 2 * H_B)).reshape(1, LANES)
    rw_consts = (row(mu_shift[0]), row(rw_w0[0]), w2p, row(rw_a0[0]), a2p, rw_g2[0].astype(BF16), row(rw_kk[0]), row(rw_ka[0]),
                 row(rw_rk[0]), row(rw_ln_w[0]), row(rw_ln_b[0]), hb)
    gdn_consts = (gdn_conv[0], alog, dtb, row(gdn_norm[0]))
    post_consts = (row(norm_mix[0]), wg, w_branch_a[0].astype(BF16), w_branch_b[0].astype(BF16),
                   w_out[0].astype(BF16), row(norm_ffn[0]), w_ffn_gate[0].astype(BF16), w_ffn_up[0].astype(BF16),
                   w_ffn_down[0].astype(BF16), row(norm_ple[0]), w_ple_gate[0].astype(BF16),
                   w_ple_proj[0].astype(BF16), row(norm_final))

    xp = x_prompt.reshape(bsz * seq, D_MODEL)
    pa, qkv, z, ba = _inproj(xp, row(norm_mix[0]), wa, wq, wz, wb)
    pa3 = pa.reshape(bsz, seq, A_COLS)
    qkv3 = qkv.reshape(bsz, seq, CONV_CH)
    oa, wkv_pairs = _rwkv_prompt(pa3, *rw_consts, tri)
    ob, gdn_p = _gdn_prompt(qkv3, z.reshape(bsz, seq, C_BV), ba.reshape(bsz, seq, LANES), *gdn_consts, tri)
    y_prompt = _post(xp, oa.reshape(bsz * seq, C_A), ob.reshape(bsz * seq, C_BV),
                     p_prompt[0].reshape(bsz * seq, PLE_DIM), post_consts).reshape(bsz, seq, D_MODEL)
    wkv_p = jnp.stack([wkv_pairs[:, :, :HEAD_A, :HEAD_A], wkv_pairs[:, :, HEAD_A:, HEAD_A:]], axis=2)
    wkv_p = wkv_p.reshape(bsz, H_A, HEAD_A, HEAD_A)
    shift_p = pa3[:, seq - 1:, :]
    conv_p = qkv3[:, seq - (CONV_W - 1):, :]

    xs = x_sample.reshape(nd, D_MODEL)
    pa_s, qkv_s, z_s, ba_s = _inproj(xs, row(norm_mix[0]), wa, wq, wz, wb)
    cst = state_conv[0].reshape(nd, (CONV_W - 1) * CONV_CH)
    oa_s, wkv_t = _decode_rwkv(pa_s, state_shift[0].reshape(nd, A_COLS), jnp.transpose(state_wkv[0], (1, 2, 3, 0)),
                               rw_consts)
    wkv_s = jnp.transpose(wkv_t, (3, 0, 1, 2))
    ob_s, gdn_s = _decode_gdn(qkv_s, z_s, ba_s, cst, state_gdn[0], gdn_consts)
    y_sample = _post(xs, oa_s, ob_s, p_sample[0].reshape(nd, PLE_DIM), post_consts).reshape(nd, 1, D_MODEL)
    conv_s = jnp.concatenate([cst[:, CONV_CH:], qkv_s], axis=1).reshape(nd, CONV_W - 1, CONV_CH)

    return (y_prompt, y_sample, shift_p[None], wkv_p[None], conv_p[None], gdn_p[None],
            pa_s.reshape(1, nd, 1, A_COLS), wkv_s[None], conv_s[None], gdn_s[None])
```

```python
import functools

import jax
import jax.numpy as jnp
from jax import lax
from jax.experimental import pallas as pl
from jax.experimental.pallas import tpu as pltpu

F32 = jnp.float32
BF16 = jnp.bfloat16
HI = lax.Precision.HIGHEST

D_MODEL = 1024
HEAD_A = 64
C_A = 512
H_A = 8
LORA_W = 64
LORA_A = 64
LORA_G = 128
A_COLS = 3 * C_A + LORA_W + LORA_A + LORA_G
DK = 128
DV = 128
H_B = 4
C_BK = 512
C_BV = 512
CONV_W = 4
CONV_CH = 2 * C_BK + C_BV
D_FF = 2816
PLE_DIM = 256
NORM_EPS = 1e-6
GN_EPS = 64e-5
L2_EPS = 1e-6

LANES = 128
CHUNK = 64
PAIR = 2 * CHUNK
T_TILE = 256
ROW_TILE = 256
DEC_TILE = 8
HALO = 8
VMEM_LIMIT = 56 * 1024 * 1024
SUM_TERMS = 2
CUMSUM_TERMS = 3
INV_TERMS = 1
CHAIN_TERMS = 1
PAIR_GROUP = 4


def _dot(a, b, prec=None):
    return jnp.dot(a, b, preferred_element_type=F32, precision=prec)


def _dot_nt(a, b, prec=None):
    return lax.dot_general(a, b, (((1,), (1,)), ((), ())), preferred_element_type=F32, precision=prec)


def _dot_tn(a, b, prec=None):
    return lax.dot_general(a, b, (((0,), (0,)), ((), ())), preferred_element_type=F32, precision=prec)


def _sigmoid(x):
    return 1.0 / (1.0 + jnp.exp(-x))


def _silu(x):
    return x * _sigmoid(x)


def _softplus(x):
    return jnp.maximum(x, 0.0) + jnp.log(1.0 + jnp.exp(-jnp.abs(x)))


def _rms(x, gain):
    return x * lax.rsqrt(jnp.mean(x * x, axis=-1, keepdims=True) + NORM_EPS) * gain


def _pair_masks():
    ri = lax.broadcasted_iota(jnp.int32, (PAIR, PAIR), 0)
    ci = lax.broadcasted_iota(jnp.int32, (PAIR, PAIR), 1)
    same = (ri < CHUNK) == (ci < CHUNK)
    strict = same & (ci < ri)
    incl = same & (ci <= ri)
    eye = (ri == ci).astype(F32)
    return strict, incl, eye


_NN = (((1,), (0,)), ((), ()))
_NT = (((1,), (1,)), ((), ()))
_TN = (((0,), (0,)), ((), ()))


def _split(x, terms):
    if isinstance(x, (list, tuple)):
        return list(x)
    if x.dtype == BF16:
        return [x]
    parts = []
    for i in range(terms):
        h = x.astype(BF16)
        parts.append(h)
        if i + 1 < terms:
            x = x - h.astype(F32)
    return parts


def _mm(a, b, dims=_NN, na=1, nb=1):
    pa, pb = _split(a, na), _split(b, nb)
    acc = None
    for i, ai in enumerate(pa):
        for j, bj in enumerate(pb):
            if i + j < max(len(pa), len(pb)):
                d = lax.dot_general(ai, bj, dims, preferred_element_type=F32)
                acc = d if acc is None else acc + d
    return acc


def _neumann_inverse(lmat, eye, terms):
    t = eye + lmat
    ps = _split(lmat, terms)
    n = 2
    while n < CHUNK:
        ps = _split(_mm(ps, ps), terms)
        t = t + _mm(t, ps, _NN, terms)
        n *= 2
    return t


def _neumann_inverse_many(lmats, eye, terms):
    ts = [eye + l for l in lmats]
    ps = [_split(l, terms) for l in lmats]
    n = 2
    while n < CHUNK:
        ps = [_split(_mm(p, p), terms) for p in ps]
        ts = [t + _mm(t, p, _NN, terms) for t, p in zip(ts, ps)]
        n *= 2
    return ts


def _inproj_kernel(x_ref, g_ref, wa_ref, wq_ref, wz_ref, wb_ref, pa_ref, qkv_ref, z_ref, ba_ref):
    u = _rms(x_ref[...], g_ref[...]).astype(BF16)
    pa_ref[...] = _dot(u, wa_ref[...])
    qkv_ref[...] = _dot(u, wq_ref[...])
    z_ref[...] = _dot(u, wz_ref[...])
    ba_ref[...] = _dot(u, wb_ref[...])


def _const_spec(shape):
    nd = len(shape)
    return pl.BlockSpec(shape, lambda *_: (0,) * nd, pipeline_mode=pl.Buffered(1))


def _inproj(x, gain, wa, wq, wz, wb):
    n = x.shape[0]
    tm = min(ROW_TILE, n)
    row = lambda w: pl.BlockSpec((tm, w), lambda i: (i, 0))
    return pl.pallas_call(
        _inproj_kernel,
        grid=(n // tm,),
        in_specs=[row(D_MODEL), _const_spec(gain.shape), _const_spec(wa.shape), _const_spec(wq.shape),
                  _const_spec(wz.shape), _const_spec(wb.shape)],
        out_specs=[row(A_COLS), row(CONV_CH), row(C_BV), row(LANES)],
        out_shape=[jax.ShapeDtypeStruct((n, A_COLS), F32), jax.ShapeDtypeStruct((n, CONV_CH), F32),
                   jax.ShapeDtypeStruct((n, C_BV), F32), jax.ShapeDtypeStruct((n, LANES), F32)],
        compiler_params=pltpu.CompilerParams(dimension_semantics=("parallel",), vmem_limit_bytes=VMEM_LIMIT),
        name="inproj",
    )(x, gain, wa, wq, wz, wb)


def _rwkv_prep(pa, prev, mu, w0, w2p, a0, a2p, g2, kkw, ka, rk, hb):
    xa = pa + (prev - pa) * mu
    r = xa[:, :C_A]
    k = xa[:, C_A:2 * C_A]
    v = xa[:, 2 * C_A:3 * C_A]
    xwa = xa[:, 3 * C_A:3 * C_A + LORA_W + LORA_A]
    xg = xa[:, 3 * C_A + LORA_W + LORA_A:]
    w_log = -_softplus(-(w0 + _mm(jnp.tanh(xwa), w2p))) - 0.5
    logw = -jnp.exp(w_log)
    a = _sigmoid(a0 + _mm(xwa, a2p))
    g = _mm(_sigmoid(xg), g2)
    kx = k * kkw
    kk = kx * lax.rsqrt(_head_sum(kx * kx, hb) + L2_EPS)
    k2 = k * (1.0 + (a - 1.0) * ka)
    bonus = _head_sum(r * k2 * rk, hb) * v
    return r, k2, v, logw, a, g, kk, bonus


def _head_sum(x, hb):
    return _mm(x, hb, _NN, SUM_TERMS)


def _group_norm_gate(y, bonus, g, lnw, lnb, hb):
    mean = _head_sum(y, hb) * (1.0 / HEAD_A)
    d = y - mean
    var = _head_sum(d * d, hb) * (1.0 / HEAD_A)
    yn = d * lax.rsqrt(var + GN_EPS) * lnw + lnb
    return (yn + bonus) * g


def _rwkv_prompt_kernel(pa_ref, mu_ref, w0_ref, w2_ref, a0_ref, a2_ref, g2_ref, kkw_ref, ka_ref, rk_ref,
                        lnw_ref, lnb_ref, hb_ref, tri_ref,
                        oa_ref, wkv_ref,
                        ext_ref, s_ref, at_ref, bt_ref, kt_ref, rt_ref, v_ref, bh_ref, kh_ref, y_ref):
    t = pl.program_id(1)

    @pl.when(t == 0)
    def _():
        ext_ref[0:HALO, :] = jnp.zeros((HALO, A_COLS), F32)
        s_ref[...] = jnp.zeros_like(s_ref)

    pa = pa_ref[0]
    ext_ref[HALO:, :] = pa
    prev = ext_ref[pl.ds(HALO - 1, T_TILE), :]
    ext_ref[0:HALO, :] = pa[T_TILE - HALO:, :]

    hb = hb_ref[...]
    r, k2, v, logw, a, g, kk, bonus = _rwkv_prep(
        pa, prev, mu_ref[...], w0_ref[...], w2_ref[...], a0_ref[...], a2_ref[...], g2_ref[...],
        kkw_ref[...], ka_ref[...], rk_ref[...], hb)

    nchunk = T_TILE // CHUNK
    cs = _mm(tri_ref[...], logw, _NN, 1, CUMSUM_TERMS)
    ends = [cs[CHUNK * (c + 1) - 1:CHUNK * (c + 1)] for c in range(nchunk)]
    tot = jnp.concatenate([jnp.broadcast_to(e, (CHUNK, C_A)) for e in ends], axis=0)
    dinv = jnp.exp(-cs)
    dend = jnp.exp(tot - cs)
    b_in = kk * a
    at_ref[...] = -kk * jnp.exp(cs - logw)
    bt_ref[...] = b_in * dinv
    kt_ref[...] = k2 * dinv
    rt_ref[...] = r * jnp.exp(cs)
    v_ref[...] = v
    bh_ref[...] = b_in * dend
    kh_ref[...] = k2 * dend
    gl = [jnp.exp(e) for e in ends]

    strict, incl, eye = _pair_masks()
    low = lax.broadcasted_iota(jnp.int32, (CHUNK, LANES), 1) < HEAD_A

    def stack(x):
        return jnp.concatenate([jnp.where(low, x, 0.0), jnp.where(low, 0.0, x)], axis=0)

    npair = C_A // LANES
    for p0 in range(0, npair, PAIR_GROUP):
        idx = [(p, c) for p in range(p0, p0 + PAIR_GROUP) for c in range(nchunk)]
        lanes = lambda p: slice(LANES * p, LANES * (p + 1))
        rows = lambda c: slice(CHUNK * c, CHUNK * (c + 1))
        ld = lambda ref: [stack(ref[rows(c), lanes(p)]) for p, c in idx]
        at, bt, kt, rt, vs, bh, kh = (ld(ref) for ref in (at_ref, bt_ref, kt_ref, rt_ref, v_ref, bh_ref, kh_ref))
        n = range(len(idx))
        aa = [_mm(jnp.concatenate([at[i], rt[i]], axis=0), jnp.concatenate([bt[i], kt[i]], axis=0), _NT) for i in n]
        a_ab = [jnp.where(strict, aa[i][:PAIR, :PAIR], 0.0) for i in n]
        a_ak = [jnp.where(strict, aa[i][:PAIR, PAIR:], 0.0) for i in n]
        a_rb = [jnp.where(incl, aa[i][PAIR:, :PAIR], 0.0) for i in n]
        a_rk = [jnp.where(incl, aa[i][PAIR:, PAIR:], 0.0) for i in n]
        tinv = _neumann_inverse_many(a_ab, eye, INV_TERMS)
        akv = [_mm(a_ak[i], vs[i]) for i in n]
        gu = [_mm(tinv[i], jnp.concatenate([at[i], akv[i]], axis=1)) for i in n]
        ry = [_mm(a_rb[i], gu[i]) for i in n]
        rkv = [_mm(a_rk[i], vs[i]) for i in n]
        pq = [_mm(gu[i], bh[i], _TN) for i in n]
        vk = [_mm(vs[i], kh[i], _TN) for i in n]
        rp = [rt[i] + ry[i][:, :PAIR] for i in n]
        yc = [ry[i][:, PAIR:] + rkv[i] for i in n]
        pm = [eye * gl[idx[i][1]][:, lanes(idx[i][0])] + pq[i][:PAIR] for i in n]
        qm = [pq[i][PAIR:] + vk[i] for i in n]
        s = {p: s_ref[p] for p in range(p0, p0 + PAIR_GROUP)}
        for c in range(nchunk):
            for p in range(p0, p0 + PAIR_GROUP):
                i = idx.index((p, c))
                y = _mm(rp[i], s[p], _NT) + yc[i]
                y_ref[rows(c), lanes(p)] = y[:CHUNK] + y[CHUNK:]
                s[p] = _mm(s[p], pm[i], _NN, CHAIN_TERMS, CHAIN_TERMS) + qm[i]
        for p in range(p0, p0 + PAIR_GROUP):
            s_ref[p] = s[p]

    oa_ref[0] = _group_norm_gate(y_ref[...], bonus, g, lnw_ref[...], lnb_ref[...], hb)
    wkv_ref[0] = s_ref[...]


def _rwkv_prompt(pa, mu, w0, w2p, a0, a2p, g2, kkw, ka, rk, lnw, lnb, hb, tri):
    bsz, seq, _ = pa.shape
    consts = (mu, w0, w2p, a0, a2p, g2, kkw, ka, rk, lnw, lnb, hb, tri)
    npair = C_A // LANES
    tile = lambda: pltpu.VMEM((T_TILE, C_A), F32)
    return pl.pallas_call(
        _rwkv_prompt_kernel,
        grid=(bsz, seq // T_TILE),
        in_specs=[pl.BlockSpec((1, T_TILE, A_COLS), lambda b, t: (b, t, 0))] + [_const_spec(c.shape) for c in consts],
        out_specs=[pl.BlockSpec((1, T_TILE, C_A), lambda b, t: (b, t, 0)),
                   pl.BlockSpec((1, npair, LANES, LANES), lambda b, t: (b, 0, 0, 0))],
        out_shape=[jax.ShapeDtypeStruct((bsz, seq, C_A), F32),
                   jax.ShapeDtypeStruct((bsz, npair, LANES, LANES), F32)],
        scratch_shapes=[pltpu.VMEM((T_TILE + HALO, A_COLS), F32), pltpu.VMEM((npair, LANES, LANES), F32)]
                       + [tile() for _ in range(8)],
        compiler_params=pltpu.CompilerParams(dimension_semantics=("parallel", "arbitrary"),
                                             vmem_limit_bytes=VMEM_LIMIT),
        name="rwkv_prompt",
    )(pa, *consts)


def _gdn_prep(x0, x1, x2, x3, ba, conv, alog, dtb):
    c = _silu(x0 * conv[0:1] + x1 * conv[1:2] + x2 * conv[2:3] + x3 * conv[3:4])
    qs, ks = [], []
    for h in range(H_B):
        qh = c[:, DK * h:DK * (h + 1)]
        kh = c[:, C_BK + DK * h:C_BK + DK * (h + 1)]
        qs.append(qh * lax.rsqrt(jnp.sum(qh * qh, axis=-1, keepdims=True) + L2_EPS) * (DK ** -0.5))
        ks.append(kh * lax.rsqrt(jnp.sum(kh * kh, axis=-1, keepdims=True) + L2_EPS))
    q = jnp.concatenate(qs, axis=1)
    k = jnp.concatenate(ks, axis=1)
    v = c[:, 2 * C_BK:]
    beta = _sigmoid(ba)
    glog = -jnp.exp(alog) * _softplus(ba + dtb)
    return q, k, v, beta, glog


def _head_norm_gate(o, z, gnorm):
    outs = []
    for h in range(H_B):
        oh = o[:, DV * h:DV * (h + 1)]
        zh = z[:, DV * h:DV * (h + 1)]
        oh = oh * lax.rsqrt(jnp.mean(oh * oh, axis=-1, keepdims=True) + NORM_EPS) * gnorm
        outs.append(oh * _silu(zh))
    return jnp.concatenate(outs, axis=1)


def _gdn_prompt_kernel(qkv_ref, z_ref, ba_ref, conv_ref, alog_ref, dtb_ref, gnorm_ref, tri_ref,
                       ob_ref, gdn_ref,
                       ext_ref, s_ref, q_ref, k_ref, v_ref, gc_ref, be_ref, o_ref):
    t = pl.program_id(1)

    @pl.when(t == 0)
    def _():
        ext_ref[0:HALO, :] = jnp.zeros((HALO, CONV_CH), F32)
        s_ref[...] = jnp.zeros_like(s_ref)

    x3 = qkv_ref[0]
    ext_ref[HALO:, :] = x3
    x0 = ext_ref[pl.ds(HALO - 3, T_TILE), :]
    x1 = ext_ref[pl.ds(HALO - 2, T_TILE), :]
    x2 = ext_ref[pl.ds(HALO - 1, T_TILE), :]
    ext_ref[0:HALO, :] = x3[T_TILE - HALO:, :]

    q, k, v, beta, glog = _gdn_prep(x0, x1, x2, x3, ba_ref[0], conv_ref[...], alog_ref[...], dtb_ref[...])
    q_ref[...] = q
    k_ref[...] = k
    v_ref[...] = v
    be_ref[...] = beta
    gc_ref[...] = _mm(tri_ref[...], glog, _NN, 1, CUMSUM_TERMS)

    strict, incl, eye = _pair_masks()

    nchunk = T_TILE // CHUNK
    idx = [(pr, c) for pr in range(H_B // 2) for c in range(nchunk)]
    n = range(len(idx))
    rows = lambda c: slice(CHUNK * c, CHUNK * (c + 1))
    cat = lambda ref: [jnp.concatenate([ref[rows(c), DK * h:DK * (h + 1)] for h in (2 * pr, 2 * pr + 1)], axis=0)
                       for pr, c in idx]
    col = lambda ref, off: [jnp.concatenate([ref[rows(c), off + h:off + h + 1] for h in (2 * pr, 2 * pr + 1)], axis=0)
                            for pr, c in idx]
    qs, ks, vs = cat(q_ref), cat(k_ref), cat(v_ref)
    beta_c = col(be_ref, 0)
    gc = col(gc_ref, H_B)
    gt = [jnp.concatenate([jnp.broadcast_to(g_[CHUNK * (i + 1) - 1:CHUNK * (i + 1)], (CHUNK, 1)) for i in range(2)],
                          axis=0) for g_ in gc]
    decay = []
    for g_ in gc:
        gc_full = jnp.broadcast_to(g_, (PAIR, PAIR))
        diff = gc_full - gc_full.T
        decay.append(jnp.where(incl, jnp.exp(jnp.where(incl, diff, 0.0)), 0.0))
    kb = [ks[i] * beta_c[i] for i in n]
    vb = [vs[i] * beta_c[i] for i in n]
    kq = [_mm(jnp.concatenate([kb[i], qs[i]], axis=0), ks[i], _NT) for i in n]
    lmat = [jnp.where(strict, kq[i][:PAIR] * decay[i], 0.0) for i in n]
    qk = [jnp.where(incl, kq[i][PAIR:] * decay[i], 0.0) for i in n]
    tinv = _neumann_inverse_many([-l for l in lmat], eye, INV_TERMS)
    eg = [jnp.exp(g_) for g_ in gc]
    uw = [_mm(tinv[i], jnp.concatenate([vb[i], kb[i] * eg[i]], axis=1)) for i in n]
    ow = [_mm(qk[i], uw[i]) for i in n]
    oc = [ow[i][:, :DV] for i in n]
    rq = [qs[i] * eg[i] - ow[i][:, DV:] for i in n]
    kd = [ks[i] * jnp.exp(gt[i] - gc[i]) for i in n]
    half = lambda j: slice(CHUNK * j, CHUNK * (j + 1))
    pq = [[_mm(kd[i][half(j)], uw[i][half(j)], _TN) for j in range(2)] for i in n]
    pm = [[eye * jnp.exp(gt[i][CHUNK * j:CHUNK * j + 1]) - pq[i][j][:, DV:] for j in range(2)] for i in n]
    s = [s_ref[h] for h in range(H_B)]
    for c in range(nchunk):
        for h in range(H_B):
            i, j = idx.index((h // 2, c)), h % 2
            o_ref[rows(c), DV * h:DV * (h + 1)] = _mm(rq[i][half(j)], s[h]) + oc[i][half(j)]
            s[h] = _mm(pm[i][j], s[h], _NN, CHAIN_TERMS, CHAIN_TERMS) + pq[i][j][:, :DV]
    for h in range(H_B):
        s_ref[h] = s[h]

    ob_ref[0] = _head_norm_gate(o_ref[...], z_ref[0], gnorm_ref[...])
    gdn_ref[0] = s_ref[...]


def _gdn_prompt(qkv, z, ba, conv, alog, dtb, gnorm, tri):
    bsz, seq, _ = qkv.shape
    consts = (conv, alog, dtb, gnorm, tri)
    tok = lambda w: pl.BlockSpec((1, T_TILE, w), lambda b, t: (b, t, 0))
    return pl.pallas_call(
        _gdn_prompt_kernel,
        grid=(bsz, seq // T_TILE),
        in_specs=[tok(CONV_CH), tok(C_BV), tok(LANES)] + [_const_spec(c.shape) for c in consts],
        out_specs=[tok(C_BV), pl.BlockSpec((1, H_B, DK, DV), lambda b, t: (b, 0, 0, 0))],
        out_shape=[jax.ShapeDtypeStruct((bsz, seq, C_BV), F32), jax.ShapeDtypeStruct((bsz, H_B, DK, DV), F32)],
        scratch_shapes=[pltpu.VMEM((T_TILE + HALO, CONV_CH), F32), pltpu.VMEM((H_B, DK, DV), F32),
                        pltpu.VMEM((T_TILE, C_BK), F32), pltpu.VMEM((T_TILE, C_BK), F32),
                        pltpu.VMEM((T_TILE, C_BV), F32), pltpu.VMEM((T_TILE, LANES), F32),
                        pltpu.VMEM((T_TILE, LANES), F32), pltpu.VMEM((T_TILE, C_BV), F32)],
        compiler_params=pltpu.CompilerParams(dimension_semantics=("parallel", "arbitrary"),
                                             vmem_limit_bytes=VMEM_LIMIT),
        name="gdn_prompt",
    )(qkv, z, ba, *consts)


def _to_columns(x):
    pad = jnp.zeros((LANES - DEC_TILE, x.shape[1]), F32)
    return jnp.concatenate([x, pad], axis=0).T


def _decode_rwkv_kernel(pa_ref, shift_ref, wkv_ref,
                        mu_ref, w0_ref, w2_ref, a0_ref, a2_ref, g2_ref, kkw_ref, ka_ref, rk_ref, lnw_ref, lnb_ref,
                        hb_ref, oa_ref, wkvo_ref, tr_ref, yt_ref, g_ref, bonus_ref):
    h = pl.program_id(0)

    @pl.when(h == 0)
    def _():
        r, k2, v, logw, a, g, kk, bonus = _rwkv_prep(
            pa_ref[...], shift_ref[...], mu_ref[...], w0_ref[...], w2_ref[...], a0_ref[...], a2_ref[...],
            g2_ref[...], kkw_ref[...], ka_ref[...], rk_ref[...], hb_ref[...])
        for i, x in enumerate((-kk, jnp.exp(logw), kk * a, k2, r, v)):
            tr_ref[i] = x.T
        g_ref[...] = g
        bonus_ref[...] = bonus

    base = pl.multiple_of(h * HEAD_A, HEAD_A)
    hs = pl.ds(base, HEAD_A)
    a_t, w_t, b_t, k_t, r_t = (tr_ref[i, hs, :] for i in range(5))

    def value_row(vi, carry):
        st = wkv_ref[0, vi]
        sa = jnp.sum(st * a_t, axis=0, keepdims=True)
        st = st * w_t + sa * b_t + tr_ref[5, pl.ds(base + vi, 1), :] * k_t
        wkvo_ref[0, vi] = st
        yt_ref[pl.ds(base + vi, 1), :] = jnp.sum(st * r_t, axis=0, keepdims=True)
        return carry

    lax.fori_loop(0, HEAD_A, value_row, 0, unroll=8)

    @pl.when(h == H_A - 1)
    def _():
        oa_ref[...] = _group_norm_gate(yt_ref[...].T, bonus_ref[...], g_ref[...], lnw_ref[...], lnb_ref[...],
                                       hb_ref[...])


def _decode_rwkv(pa, shift, wkv_t, rw_consts):
    n = pa.shape[0]
    full = lambda w: pl.BlockSpec((n, w), lambda h: (0, 0))
    state = pl.BlockSpec((1, HEAD_A, HEAD_A, n), lambda h: (h, 0, 0, 0))
    return pl.pallas_call(
        _decode_rwkv_kernel,
        grid=(H_A,),
        in_specs=[full(A_COLS), full(A_COLS), state] + [_const_spec(c.shape) for c in rw_consts],
        out_specs=[full(C_A), state],
        out_shape=[jax.ShapeDtypeStruct((n, C_A), F32), jax.ShapeDtypeStruct(wkv_t.shape, F32)],
        scratch_shapes=[pltpu.VMEM((6, C_A, n), F32), pltpu.VMEM((C_A, n), F32),
                        pltpu.VMEM((n, C_A), F32), pltpu.VMEM((n, C_A), F32)],
        compiler_params=pltpu.CompilerParams(dimension_semantics=("arbitrary",), vmem_limit_bytes=VMEM_LIMIT),
        name="decode_rwkv",
    )(pa, shift, wkv_t, *rw_consts)


def _decode_gdn_kernel(qkv_ref, z_ref, ba_ref, cst_ref, gdn_ref, conv_ref, alog_ref, dtb_ref, gnorm_ref,
                       ob_ref, gdno_ref):
    cst = cst_ref[...]
    q, k, vv, beta, glog = _gdn_prep(cst[:, :CONV_CH], cst[:, CONV_CH:2 * CONV_CH], cst[:, 2 * CONV_CH:],
                                     qkv_ref[...], ba_ref[...], conv_ref[...], alog_ref[...], dtb_ref[...])
    eg = jnp.exp(glog)
    q_cols = _to_columns(q)
    k_cols = _to_columns(k)
    gunits = [(s_i, h) for s_i in range(DEC_TILE) for h in range(H_B)]
    gn = range(len(gunits))
    ls = lambda h: slice(DK * h, DK * (h + 1))
    be = [beta[s_i:s_i + 1, h:h + 1] for s_i, h in gunits]
    e = [eg[s_i:s_i + 1, H_B + h:H_B + h + 1] for s_i, h in gunits]
    kc = [k_cols[ls(h), s_i:s_i + 1] for s_i, h in gunits]
    qc = [q_cols[ls(h), s_i:s_i + 1] for s_i, h in gunits]
    gst = [gdn_ref[s_i, h] for s_i, h in gunits]
    ws = [jnp.sum((kc[i] * (be[i] * e[i])) * gst[i], axis=0, keepdims=True) for i in gn]
    qs = [jnp.sum((qc[i] * e[i]) * gst[i], axis=0, keepdims=True) for i in gn]
    v_new = [be[i] * vv[s_i:s_i + 1, ls(h)] - ws[i] for i, (s_i, h) in enumerate(gunits)]
    qk = [jnp.sum(q[s_i:s_i + 1, ls(h)] * k[s_i:s_i + 1, ls(h)], axis=-1, keepdims=True) for s_i, h in gunits]
    for i, (s_i, h) in enumerate(gunits):
        gdno_ref[s_i, h] = gst[i] * e[i] + kc[i] * v_new[i]
    o_units = [qs[i] + qk[i] * v_new[i] for i in gn]
    o = jnp.concatenate([jnp.concatenate(o_units[H_B * s_i:H_B * (s_i + 1)], axis=1) for s_i in range(DEC_TILE)],
                        axis=0)
    ob_ref[...] = _head_norm_gate(o, z_ref[...], gnorm_ref[...])


def _decode_gdn(qkv, z, ba, cst, gdn, gdn_consts):
    n = qkv.shape[0]
    row = lambda w: pl.BlockSpec((DEC_TILE, w), lambda i: (i, 0))
    state = pl.BlockSpec((DEC_TILE, H_B, DK, DV), lambda i: (i, 0, 0, 0))
    return pl.pallas_call(
        _decode_gdn_kernel,
        grid=(n // DEC_TILE,),
        in_specs=[row(CONV_CH), row(C_BV), row(LANES), row(3 * CONV_CH), state]
                 + [_const_spec(c.shape) for c in gdn_consts],
        out_specs=[row(C_BV), state],
        out_shape=[jax.ShapeDtypeStruct((n, C_BV), F32), jax.ShapeDtypeStruct(gdn.shape, F32)],
        compiler_params=pltpu.CompilerParams(dimension_semantics=("parallel",), vmem_limit_bytes=VMEM_LIMIT),
        name="decode_gdn",
    )(qkv, z, ba, cst, gdn, *gdn_consts)


def _post_kernel(x_ref, oa_ref, ob_ref, pe_ref, nmix_ref, wg_ref, wba_ref, wbb_ref, wout_ref, nffn_ref,
                 wfg_ref, wfu_ref, wfd_ref, nple_ref, wpg_ref, wpp_ref, nfin_ref, y_ref):
    x = x_ref[...]
    u = _rms(x, nmix_ref[...]).astype(BF16)
    gates = _sigmoid(_dot(u, wg_ref[...]))
    mix = (gates[:, :D_MODEL] * _dot(oa_ref[...].astype(BF16), wba_ref[...])
           + gates[:, D_MODEL:] * _dot(ob_ref[...].astype(BF16), wbb_ref[...]))
    h = x + _dot(mix.astype(BF16), wout_ref[...])
    u2 = _rms(h, nffn_ref[...]).astype(BF16)
    ff = _silu(_dot(u2, wfg_ref[...])) * _dot(u2, wfu_ref[...])
    h = h + _dot(ff.astype(BF16), wfd_ref[...])
    u3 = _rms(h, nple_ref[...]).astype(BF16)
    h = h + _sigmoid(_dot(u3, wpg_ref[...])) * _dot(pe_ref[...].astype(BF16), wpp_ref[...])
    y_ref[...] = _rms(h, nfin_ref[...])


def _post(x, oa, ob, pe, consts):
    n = x.shape[0]
    tm = min(ROW_TILE, n)
    row = lambda w: pl.BlockSpec((tm, w), lambda i: (i, 0))
    return pl.pallas_call(
        _post_kernel,
        grid=(n // tm,),
        in_specs=[row(D_MODEL), row(C_A), row(C_BV), row(PLE_DIM)] + [_const_spec(c.shape) for c in consts],
        out_specs=row(D_MODEL),
        out_shape=jax.ShapeDtypeStruct((n, D_MODEL), F32),
        compiler_params=pltpu.CompilerParams(dimension_semantics=("parallel",), vmem_limit_bytes=VMEM_LIMIT),
        name="post",
    )(x, oa, ob, pe, *consts)


def _chunk_cumsum_matrix():
    i = jnp.arange(T_TILE)
    same = (i[:, None] // CHUNK) == (i[None, :] // CHUNK)
    return (same & (i[None, :] <= i[:, None])).astype(BF16)


def kernel(x_prompt, x_sample, p_prompt, p_sample, state_shift, state_wkv, state_conv, state_gdn, norm_mix, w_in, mu_shift, rw_w0, rw_w2, rw_a0, rw_a2, rw_g2, rw_kk, rw_ka, rw_rk, rw_ln_w, rw_ln_b, gdn_conv, gdn_a_log, gdn_dt_bias, gdn_norm, w_branch_a, w_branch_b, w_out, norm_ffn, w_ffn_gate, w_ffn_up, w_ffn_down, norm_ple, w_ple_gate, w_ple_proj, norm_final):
    bsz, seq, _ = x_prompt.shape
    nd = x_sample.shape[0]
    row = lambda p: p.reshape(1, -1)

    w_in0 = w_in[0]
    b0 = A_COLS
    wa = w_in0[:, :b0].astype(BF16)
    wq = w_in0[:, b0:b0 + CONV_CH].astype(BF16)
    wz = w_in0[:, b0 + CONV_CH:b0 + CONV_CH + C_BV].astype(BF16)
    wb = jnp.pad(w_in0[:, b0 + CONV_CH + C_BV:b0 + CONV_CH + C_BV + 2 * H_B], ((0, 0), (0, LANES - 2 * H_B))).astype(BF16)
    wg = w_in0[:, b0 + CONV_CH + C_BV + 2 * H_B:].astype(BF16)
    w2p = jnp.concatenate([rw_w2[0], jnp.zeros((LORA_A, C_A), F32)], axis=0).astype(BF16)
    a2p = jnp.concatenate([jnp.zeros((LORA_W, C_A), F32), rw_a2[0]], axis=0).astype(BF16)
    ch = jnp.arange(C_A) // HEAD_A
    hb = (ch[:, None] == ch[None, :]).astype(BF16)
    tri = _chunk_cumsum_matrix()
    alog = jnp.pad(gdn_a_log[0], (H_B, LANES - 2 * H_B)).reshape(1, LANES)
    dtb = jnp.pad(gdn_dt_bias[0], (H_B, LANES - 2 * H_B)).reshape(1, LANES)
    rw_consts = (row(mu_shift[0]), row(rw_w0[0]), w2p, row(rw_a0[0]), a2p, rw_g2[0].astype(BF16), row(rw_kk[0]), row(rw_ka[0]),
                 row(rw_rk[0]), row(rw_ln_w[0]), row(rw_ln_b[0]), hb)
    gdn_consts = (gdn_conv[0], alog, dtb, row(gdn_norm[0]))
    post_consts = (row(norm_mix[0]), wg, w_branch_a[0].astype(BF16), w_branch_b[0].astype(BF16),
                   w_out[0].astype(BF16), row(norm_ffn[0]), w_ffn_gate[0].astype(BF16), w_ffn_up[0].astype(BF16),
                   w_ffn_down[0].astype(BF16), row(norm_ple[0]), w_ple_gate[0].astype(BF16),
                   w_ple_proj[0].astype(BF16), row(norm_final))

    xp = x_prompt.reshape(bsz * seq, D_MODEL)
    pa, qkv, z, ba = _inproj(xp, row(norm_mix[0]), wa, wq, wz, wb)
    pa3 = pa.reshape(bsz, seq, A_COLS)
    qkv3 = qkv.reshape(bsz, seq, CONV_CH)
    oa, wkv_pairs = _rwkv_prompt(pa3, *rw_consts, tri)
    ob, gdn_p = _gdn_prompt(qkv3, z.reshape(bsz, seq, C_BV), ba.reshape(bsz, seq, LANES), *gdn_consts, tri)
    y_prompt = _post(xp, oa.reshape(bsz * seq, C_A), ob.reshape(bsz * seq, C_BV),
                     p_prompt[0].reshape(bsz * seq, PLE_DIM), post_consts).reshape(bsz, seq, D_MODEL)
    wkv_p = jnp.stack([wkv_pairs[:, :, :HEAD_A, :HEAD_A], wkv_pairs[:, :, HEAD_A:, HEAD_A:]], axis=2)
    wkv_p = wkv_p.reshape(bsz, H_A, HEAD_A, HEAD_A)
    shift_p = pa3[:, seq - 1:, :]
    conv_p = qkv3[:, seq - (CONV_W - 1):, :]

    xs = x_sample.reshape(nd, D_MODEL)
    pa_s, qkv_s, z_s, ba_s = _inproj(xs, row(norm_mix[0]), wa, wq, wz, wb)
    cst = state_conv[0].reshape(nd, (CONV_W - 1) * CONV_CH)
    oa_s, wkv_t = _decode_rwkv(pa_s, state_shift[0].reshape(nd, A_COLS), jnp.transpose(state_wkv[0], (1, 2, 3, 0)),
                               rw_consts)
    wkv_s = jnp.transpose(wkv_t, (3, 0, 1, 2))
    ob_s, gdn_s = _decode_gdn(qkv_s, z_s, ba_s, cst, state_gdn[0], gdn_consts)
    y_sample = _post(xs, oa_s, ob_s, p_sample[0].reshape(nd, PLE_DIM), post_consts).reshape(nd, 1, D_MODEL)
    conv_s = jnp.concatenate([cst[:, CONV_CH:], qkv_s], axis=1).reshape(nd, CONV_W - 1, CONV_CH)

    return (y_prompt, y_sample, shift_p[None], wkv_p[None], conv_p[None], gdn_p[None],
            pa_s.reshape(1, nd, 1, A_COLS), wkv_s[None], conv_s[None], gdn_s[None])
```

```python
import functools

import jax
import jax.numpy as jnp
from jax import lax
from jax.experimental import pallas as pl
from jax.experimental.pallas import tpu as pltpu

F32 = jnp.float32
BF16 = jnp.bfloat16
HI = lax.Precision.HIGHEST

D_MODEL = 1024
HEAD_A = 64
C_A = 512
H_A = 8
LORA_W = 64
LORA_A = 64
LORA_G = 128
A_COLS = 3 * C_A + LORA_W + LORA_A + LORA_G
DK = 128
DV = 128
H_B = 4
C_BK = 512
C_BV = 512
CONV_W = 4
CONV_CH = 2 * C_BK + C_BV
D_FF = 2816
PLE_DIM = 256
NORM_EPS = 1e-6
GN_EPS = 64e-5
L2_EPS = 1e-6

LANES = 128
CHUNK = 64
PAIR = 2 * CHUNK
T_TILE = 256
ROW_TILE = 256
DEC_TILE = 8
HALO = 8
VMEM_LIMIT = 56 * 1024 * 1024
SUM_TERMS = 2
CUMSUM_TERMS = 3
INV_TERMS = 1
CHAIN_TERMS = 1
PAIR_GROUP = 4


def _dot(a, b, prec=None):
    return jnp.dot(a, b, preferred_element_type=F32, precision=prec)


def _dot_nt(a, b, prec=None):
    return lax.dot_general(a, b, (((1,), (1,)), ((), ())), preferred_element_type=F32, precision=prec)


def _dot_tn(a, b, prec=None):
    return lax.dot_general(a, b, (((0,), (0,)), ((), ())), preferred_element_type=F32, precision=prec)


def _sigmoid(x):
    return 1.0 / (1.0 + jnp.exp(-x))


def _silu(x):
    return x * _sigmoid(x)


def _softplus(x):
    return jnp.maximum(x, 0.0) + jnp.log(1.0 + jnp.exp(-jnp.abs(x)))


def _rms(x, gain):
    return x * lax.rsqrt(jnp.mean(x * x, axis=-1, keepdims=True) + NORM_EPS) * gain


def _pair_masks():
    ri = lax.broadcasted_iota(jnp.int32, (PAIR, PAIR), 0)
    ci = lax.broadcasted_iota(jnp.int32, (PAIR, PAIR), 1)
    same = (ri < CHUNK) == (ci < CHUNK)
    strict = same & (ci < ri)
    incl = same & (ci <= ri)
    eye = (ri == ci).astype(F32)
    return strict, incl, eye


_NN = (((1,), (0,)), ((), ()))
_NT = (((1,), (1,)), ((), ()))
_TN = (((0,), (0,)), ((), ()))


def _split(x, terms):
    if isinstance(x, (list, tuple)):
        return list(x)
    if x.dtype == BF16:
        return [x]
    parts = []
    for i in range(terms):
        h = x.astype(BF16)
        parts.append(h)
        if i + 1 < terms:
            x = x - h.astype(F32)
    return parts


def _mm(a, b, dims=_NN, na=1, nb=1):
    pa, pb = _split(a, na), _split(b, nb)
    acc = None
    for i, ai in enumerate(pa):
        for j, bj in enumerate(pb):
            if i + j < max(len(pa), len(pb)):
                d = lax.dot_general(ai, bj, dims, preferred_element_type=F32)
                acc = d if acc is None else acc + d
    return acc


def _neumann_inverse(lmat, eye, terms):
    t = eye + lmat
    ps = _split(lmat, terms)
    n = 2
    while n < CHUNK:
        ps = _split(_mm(ps, ps), terms)
        t = t + _mm(t, ps, _NN, terms)
        n *= 2
    return t


def _neumann_inverse_many(lmats, eye, terms):
    ts = [eye + l for l in lmats]
    ps = [_split(l, terms) for l in lmats]
    n = 2
    while n < CHUNK:
        ps = [_split(_mm(p, p), terms) for p in ps]
        ts = [t + _mm(t, p, _NN, terms) for t, p in zip(ts, ps)]
        n *= 2
    return ts


def _inproj_kernel(x_ref, g_ref, wa_ref, wq_ref, wz_ref, wb_ref, pa_ref, qkv_ref, z_ref, ba_ref):
    u = _rms(x_ref[...], g_ref[...]).astype(BF16)
    pa_ref[...] = _dot(u, wa_ref[...])
    qkv_ref[...] = _dot(u, wq_ref[...])
    z_ref[...] = _dot(u, wz_ref[...])
    ba_ref[...] = _dot(u, wb_ref[...])


def _const_spec(shape):
    nd = len(shape)
    return pl.BlockSpec(shape, lambda *_: (0,) * nd, pipeline_mode=pl.Buffered(1))


def _inproj(x, gain, wa, wq, wz, wb):
    n = x.shape[0]
    tm = min(ROW_TILE, n)
    row = lambda w: pl.BlockSpec((tm, w), lambda i: (i, 0))
    return pl.pallas_call(
        _inproj_kernel,
        grid=(n // tm,),
        in_specs=[row(D_MODEL), _const_spec(gain.shape), _const_spec(wa.shape), _const_spec(wq.shape),
                  _const_spec(wz.shape), _const_spec(wb.shape)],
        out_specs=[row(A_COLS), row(CONV_CH), row(C_BV), row(LANES)],
        out_shape=[jax.ShapeDtypeStruct((n, A_COLS), F32), jax.ShapeDtypeStruct((n, CONV_CH), F32),
                   jax.ShapeDtypeStruct((n, C_BV), F32), jax.ShapeDtypeStruct((n, LANES), F32)],
        compiler_params=pltpu.CompilerParams(dimension_semantics=("parallel",), vmem_limit_bytes=VMEM_LIMIT),
        name="inproj",
    )(x, gain, wa, wq, wz, wb)


def _rwkv_prep(pa, prev, mu, w0, w2p, a0, a2p, g2, kkw, ka, rk, hb):
    xa = pa + (prev - pa) * mu
    r = xa[:, :C_A]
    k = xa[:, C_A:2 * C_A]
    v = xa[:, 2 * C_A:3 * C_A]
    xwa = xa[:, 3 * C_A:3 * C_A + LORA_W + LORA_A]
    xg = xa[:, 3 * C_A + LORA_W + LORA_A:]
    w_log = -_softplus(-(w0 + _mm(jnp.tanh(xwa), w2p))) - 0.5
    logw = -jnp.exp(w_log)
    a = _sigmoid(a0 + _mm(xwa, a2p))
    g = _mm(_sigmoid(xg), g2)
    kx = k * kkw
    kk = kx * lax.rsqrt(_head_sum(kx * kx, hb) + L2_EPS)
    k2 = k * (1.0 + (a - 1.0) * ka)
    bonus = _head_sum(r * k2 * rk, hb) * v
    return r, k2, v, logw, a, g, kk, bonus


def _head_sum(x, hb):
    return _mm(x, hb, _NN, SUM_TERMS)


def _group_norm_gate(y, bonus, g, lnw, lnb, hb):
    mean = _head_sum(y, hb) * (1.0 / HEAD_A)
    d = y - mean
    var = _head_sum(d * d, hb) * (1.0 / HEAD_A)
    yn = d * lax.rsqrt(var + GN_EPS) * lnw + lnb
    return (yn + bonus) * g


def _rwkv_prompt_kernel(tiles_per_seq, x_ref, nmix_ref, wa_ref,
                        mu_ref, w0_ref, w2_ref, a0_ref, a2_ref, g2_ref, kkw_ref, ka_ref, rk_ref,
                        lnw_ref, lnb_ref, hb_ref, tri_ref,
                        oa_ref, wkv_ref, shift_ref,
                        pa_ref, ext_ref, s_ref, at_ref, bt_ref, kt_ref, rt_ref, v_ref, bh_ref, kh_ref, y_ref):
    s_id = pl.program_id(0)

    @pl.when(s_id == 0)
    def _():
        pa_ref[...] = jnp.zeros_like(pa_ref)

    @pl.when(jnp.maximum(s_id - 1, 0) % tiles_per_seq == 0)
    def _():
        ext_ref[0:HALO, :] = jnp.zeros((HALO, A_COLS), F32)
        s_ref[...] = jnp.zeros_like(s_ref)

    pa = pa_ref[...]
    ext_ref[HALO:, :] = pa
    prev = ext_ref[pl.ds(HALO - 1, T_TILE), :]
    ext_ref[0:HALO, :] = pa[T_TILE - HALO:, :]
    shift_ref[0] = pa[T_TILE - HALO:, :]

    pa_ref[...] = _dot(_rms(x_ref[0], nmix_ref[...]).astype(BF16), wa_ref[...])

    hb = hb_ref[...]
    r, k2, v, logw, a, g, kk, bonus = _rwkv_prep(
        pa, prev, mu_ref[...], w0_ref[...], w2_ref[...], a0_ref[...], a2_ref[...], g2_ref[...],
        kkw_ref[...], ka_ref[...], rk_ref[...], hb)

    nchunk = T_TILE // CHUNK
    cs = _mm(tri_ref[...], logw, _NN, 1, CUMSUM_TERMS)
    ends = [cs[CHUNK * (c + 1) - 1:CHUNK * (c + 1)] for c in range(nchunk)]
    tot = jnp.concatenate([jnp.broadcast_to(e, (CHUNK, C_A)) for e in ends], axis=0)
    dinv = jnp.exp(-cs)
    dend = jnp.exp(tot - cs)
    b_in = kk * a
    at_ref[...] = -kk * jnp.exp(cs - logw)
    bt_ref[...] = b_in * dinv
    kt_ref[...] = k2 * dinv
    rt_ref[...] = r * jnp.exp(cs)
    v_ref[...] = v
    bh_ref[...] = b_in * dend
    kh_ref[...] = k2 * dend
    gl =[jnp.exp(e) for e in ends]

    strict, incl, eye = _pair_masks()
    low = lax.broadcasted_iota(jnp.int32, (CHUNK, LANES), 1) < HEAD_A

    def stack(x):
        return jnp.concatenate([jnp.where(low, x, 0.0), jnp.where(low, 0.0, x)], axis=0)

    npair = C_A // LANES
    for p0 in range(0, npair, PAIR_GROUP):
        idx = [(p, c) for p in range(p0, p0 + PAIR_GROUP) for c in range(nchunk)]
        lanes = lambda p: slice(LANES * p, LANES * (p + 1))
        rows = lambda c: slice(CHUNK * c, CHUNK * (c + 1))
        ld = lambda ref: [stack(ref[rows(c), lanes(p)]) for p, c in idx]
        at, bt, kt, rt, vs, bh, kh = (ld(ref) for ref in (at_ref, bt_ref, kt_ref, rt_ref, v_ref, bh_ref, kh_ref))
        n = range(len(idx))
        aa = [_mm(jnp.concatenate([at[i], rt[i]], axis=0), jnp.concatenate([bt[i], kt[i]], axis=0), _NT) for i in n]
        a_ab = [jnp.where(strict, aa[i][:PAIR, :PAIR], 0.0) for i in n]
        a_ak = [jnp.where(strict, aa[i][:PAIR, PAIR:], 0.0) for i in n]
        a_rb = [jnp.where(incl, aa[i][PAIR:, :PAIR], 0.0) for i in n]
        a_rk = [jnp.where(incl, aa[i][PAIR:, PAIR:], 0.0) for i in n]
        tinv = _neumann_inverse_many(a_ab, eye, INV_TERMS)
        akv = [_mm(a_ak[i], vs[i]) for i in n]
        gu = [_mm(tinv[i], jnp.concatenate([at[i], akv[i]], axis=1)) for i in n]
        ry = [_mm(a_rb[i], gu[i]) for i in n]
        rkv = [_mm(a_rk[i], vs[i]) for i in n]
        pq = [_mm(gu[i], bh[i], _TN) for i in n]
        vk = [_mm(vs[i], kh[i], _TN) for i in n]
        rp = [rt[i] + ry[i][:, :PAIR] for i in n]
        yc = [ry[i][:, PAIR:] + rkv[i] for i in n]
        pm = [eye * gl[idx[i][1]][:, lanes(idx[i][0])] + pq[i][:PAIR] for i in n]
        qm = [pq[i][PAIR:] + vk[i] for i in n]
        s = {p: s_ref[p] for p in range(p0, p0 + PAIR_GROUP)}
        for c in range(nchunk):
            for p in range(p0, p0 + PAIR_GROUP):
                i = idx.index((p, c))
                y = _mm(rp[i], s[p], _NT) + yc[i]
                y_ref[rows(c), lanes(p)] = y[:CHUNK] + y[CHUNK:]
                s[p] = _mm(s[p], pm[i], _NN, CHAIN_TERMS, CHAIN_TERMS) + qm[i]
        for p in range(p0, p0 + PAIR_GROUP):
            s_ref[p] = s[p]

    oa_ref[0] = _group_norm_gate(y_ref[...], bonus, g, lnw_ref[...], lnb_ref[...], hb)
    wkv_ref[0] = s_ref[...]


def _tile_maps(bsz, seq):
    tps = seq // T_TILE
    last = bsz * tps - 1
    cur = lambda s: jnp.minimum(s, last)
    prv = lambda s: jnp.maximum(s - 1, 0)
    load = lambda s: (cur(s) // tps, cur(s) % tps, 0)
    emit = lambda s: (prv(s) // tps, prv(s) % tps, 0)
    per_seq3 = lambda s: (prv(s) // tps, 0, 0)
    per_seq4 = lambda s: (prv(s) // tps, 0, 0, 0)
    return tps, last + 2, load, emit, per_seq3, per_seq4


def _rwkv_prompt(x, nmix, wa, mu, w0, w2p, a0, a2p, g2, kkw, ka, rk, lnw, lnb, hb, tri):
    bsz, seq, _ = x.shape
    consts = (nmix, wa, mu, w0, w2p, a0, a2p, g2, kkw, ka, rk, lnw, lnb, hb, tri)
    npair = C_A // LANES
    tps, steps, load, emit, per_seq3, per_seq4 = _tile_maps(bsz, seq)
    tile = lambda: pltpu.VMEM((T_TILE, C_A), F32)
    return pl.pallas_call(
        functools.partial(_rwkv_prompt_kernel, tps),
        grid=(steps,),
        in_specs=[pl.BlockSpec((1, T_TILE, D_MODEL), load)] + [_const_spec(c.shape) for c in consts],
        out_specs=[pl.BlockSpec((1, T_TILE, C_A), emit),
                   pl.BlockSpec((1, npair, LANES, LANES), per_seq4),
                   pl.BlockSpec((1, HALO, A_COLS), per_seq3)],
        out_shape=[jax.ShapeDtypeStruct((bsz, seq, C_A), F32),
                   jax.ShapeDtypeStruct((bsz, npair, LANES, LANES), F32),
                   jax.ShapeDtypeStruct((bsz, HALO, A_COLS), F32)],
        scratch_shapes=[pltpu.VMEM((T_TILE, A_COLS), F32),
                        pltpu.VMEM((T_TILE + HALO, A_COLS), F32), pltpu.VMEM((npair, LANES, LANES), F32)]
                       + [tile() for _ in range(8)],
        compiler_params=pltpu.CompilerParams(dimension_semantics=("arbitrary",), vmem_limit_bytes=VMEM_LIMIT),
        name="rwkv_prompt",
    )(x, *consts)


def _gdn_prep(x0, x1, x2, x3, ba, conv, alog, dtb):
    c = _silu(x0 * conv[0:1] + x1 * conv[1:2] + x2 * conv[2:3] + x3 * conv[3:4])
    qs, ks = [], []
    for h in range(H_B):
        qh = c[:, DK * h:DK * (h + 1)]
        kh = c[:, C_BK + DK * h:C_BK + DK * (h + 1)]
        qs.append(qh * lax.rsqrt(jnp.sum(qh * qh, axis=-1, keepdims=True) + L2_EPS) * (DK ** -0.5))
        ks.append(kh * lax.rsqrt(jnp.sum(kh * kh, axis=-1, keepdims=True) + L2_EPS))
    q = jnp.concatenate(qs, axis=1)
    k = jnp.concatenate(ks, axis=1)
    v = c[:, 2 * C_BK:]
    beta = _sigmoid(ba)
    glog = -jnp.exp(alog) * _softplus(ba + dtb)
    return q, k, v, beta, glog


def _head_norm_gate(o, z, gnorm):
    outs = []
    for h in range(H_B):
        oh = o[:, DV * h:DV * (h + 1)]
        zh = z[:, DV * h:DV * (h + 1)]
        oh = oh * lax.rsqrt(jnp.mean(oh * oh, axis=-1, keepdims=True) + NORM_EPS) * gnorm
        outs.append(oh * _silu(zh))
    return jnp.concatenate(outs, axis=1)


def _gdn_prompt_kernel(tiles_per_seq, x_ref, nmix_ref, wq_ref, wz_ref, wb_ref,
                       conv_ref, alog_ref, dtb_ref, gnorm_ref, tri_ref,
                       ob_ref, gdn_ref, convo_ref,
                       qkvn_ref, zn_ref, ban_ref, ext_ref, s_ref, q_ref, k_ref, v_ref, gc_ref, be_ref, o_ref):
    s_id = pl.program_id(0)

    @pl.when(s_id == 0)
    def _():
        qkvn_ref[...] = jnp.zeros_like(qkvn_ref)
        zn_ref[...] = jnp.zeros_like(zn_ref)
        ban_ref[...] = jnp.zeros_like(ban_ref)

    @pl.when(jnp.maximum(s_id - 1, 0) % tiles_per_seq == 0)
    def _():
        ext_ref[0:HALO, :] = jnp.zeros((HALO, CONV_CH), F32)
        s_ref[...] = jnp.zeros_like(s_ref)

    x3 = qkvn_ref[...]
    z = zn_ref[...]
    ba = ban_ref[...]
    ext_ref[HALO:, :] = x3
    x0 = ext_ref[pl.ds(HALO - 3, T_TILE), :]
    x1 = ext_ref[pl.ds(HALO - 2, T_TILE), :]
    x2 = ext_ref[pl.ds(HALO - 1, T_TILE), :]
    ext_ref[0:HALO, :] = x3[T_TILE - HALO:, :]
    convo_ref[0] = x3[T_TILE - HALO:, :]

    u = _rms(x_ref[0], nmix_ref[...]).astype(BF16)
    qkvn_ref[...] = _dot(u, wq_ref[...])
    zn_ref[...] = _dot(u, wz_ref[...])
    ban_ref[...] = _dot(u, wb_ref[...])

    q, k, v, beta, glog = _gdn_prep(x0, x1, x2, x3, ba, conv_ref[...], alog_ref[...], dtb_ref[...])
    q_ref[...] = q
    k_ref[...] = k
    v_ref[...] = v
    be_ref[...] = beta
    gc_ref[...] = _mm(tri_ref[...], glog, _NN, 1, CUMSUM_TERMS)

    strict, incl, eye = _pair_masks()

    nchunk = T_TILE // CHUNK
    idx = [(pr, c) for pr in range(H_B // 2) for c in range(nchunk)]
    n = range(len(idx))
    rows = lambda c: slice(CHUNK * c, CHUNK * (c + 1))
    cat = lambda ref: [jnp.concatenate([ref[rows(c), DK * h:DK * (h + 1)] for h in (2 * pr, 2 * pr + 1)], axis=0)
                       for pr, c in idx]
    col = lambda ref, off: [jnp.concatenate([ref[rows(c), off + h:off + h + 1] for h in (2 * pr, 2 * pr + 1)], axis=0)
                            for pr, c in idx]
    qs, ks, vs = cat(q_ref), cat(k_ref), cat(v_ref)
    beta_c = col(be_ref, 0)
    gc = col(gc_ref, H_B)
    gt = [jnp.concatenate([jnp.broadcast_to(g_[CHUNK * (i + 1) - 1:CHUNK * (i + 1)], (CHUNK, 1)) for i in range(2)],
                          axis=0) for g_ in gc]
    decay = []
    for g_ in gc:
        gc_full = jnp.broadcast_to(g_, (PAIR, PAIR))
        diff = gc_full - gc_full.T
        decay.append(jnp.where(incl, jnp.exp(jnp.where(incl, diff, 0.0)), 0.0))
    kb = [ks[i] * beta_c[i] for i in n]
    vb = [vs[i] * beta_c[i] for i in n]
    kq = [_mm(jnp.concatenate([kb[i], qs[i]], axis=0), ks[i], _NT) for i in n]
    lmat = [jnp.where(strict, kq[i][:PAIR] * decay[i], 0.0) for i in n]
    qk = [jnp.where(incl, kq[i][PAIR:] * decay[i], 0.0) for i in n]
    tinv = _neumann_inverse_many([-l for l in lmat], eye, INV_TERMS)
    eg = [jnp.exp(g_) for g_ in gc]
    uw = [_mm(tinv[i], jnp.concatenate([vb[i], kb[i] * eg[i]], axis=1)) for i in n]
    ow = [_mm(qk[i], uw[i]) for i in n]
    oc = [ow[i][:, :DV] for i in n]
    rq = [qs[i] * eg[i] - ow[i][:, DV:] for i in n]
    kd = [ks[i] * jnp.exp(gt[i] - gc[i]) for i in n]
    half = lambda j: slice(CHUNK * j, CHUNK * (j + 1))
    pq = [[_mm(kd[i][half(j)], uw[i][half(j)], _TN) for j in range(2)] for i in n]
    pm = [[eye * jnp.exp(gt[i][CHUNK * j:CHUNK * j + 1]) - pq[i][j][:, DV:] for j in range(2)] for i in n]
    s = [s_ref[h] for h in range(H_B)]
    for c in range(nchunk):
        for h in range(H_B):
            i, j = idx.index((h // 2, c)), h % 2
            o_ref[rows(c), DV * h:DV * (h + 1)] = _mm(rq[i][half(j)], s[h]) + oc[i][half(j)]
            s[h] = _mm(pm[i][j], s[h], _NN, CHAIN_TERMS, CHAIN_TERMS) + pq[i][j][:, :DV]
    for h in range(H_B):
        s_ref[h] = s[h]

    ob_ref[0] = _head_norm_gate(o_ref[...], z, gnorm_ref[...])
    gdn_ref[0] = s_ref[...]


def _gdn_prompt(x, nmix, wq, wz, wb, conv, alog, dtb, gnorm, tri):
    bsz, seq, _ = x.shape
    consts = (nmix, wq, wz, wb, conv, alog, dtb, gnorm, tri)
    tps, steps, load, emit, per_seq3, per_seq4 = _tile_maps(bsz, seq)
    return pl.pallas_call(
        functools.partial(_gdn_prompt_kernel, tps),
        grid=(steps,),
        in_specs=[pl.BlockSpec((1, T_TILE, D_MODEL), load)] + [_const_spec(c.shape) for c in consts],
        out_specs=[pl.BlockSpec((1, T_TILE, C_BV), emit), pl.BlockSpec((1, H_B, DK, DV), per_seq4),
                   pl.BlockSpec((1, HALO, CONV_CH), per_seq3)],
        out_shape=[jax.ShapeDtypeStruct((bsz, seq, C_BV), F32), jax.ShapeDtypeStruct((bsz, H_B, DK, DV), F32),
                   jax.ShapeDtypeStruct((bsz, HALO, CONV_CH), F32)],
        scratch_shapes=[pltpu.VMEM((T_TILE, CONV_CH), F32), pltpu.VMEM((T_TILE, C_BV), F32),
                        pltpu.VMEM((T_TILE, LANES), F32),
                        pltpu.VMEM((T_TILE + HALO, CONV_CH), F32), pltpu.VMEM((H_B, DK, DV), F32),
                        pltpu.VMEM((T_TILE, C_BK), F32), pltpu.VMEM((T_TILE, C_BK), F32),
                        pltpu.VMEM((T_TILE, C_BV), F32), pltpu.VMEM((T_TILE, LANES), F32),
                        pltpu.VMEM((T_TILE, LANES), F32), pltpu.VMEM((T_TILE, C_BV), F32)],
        compiler_params=pltpu.CompilerParams(dimension_semantics=("arbitrary",), vmem_limit_bytes=VMEM_LIMIT),
        name="gdn_prompt",
    )(x, *consts)


def _to_columns(x):
    pad = jnp.zeros((LANES - DEC_TILE, x.shape[1]), F32)
    return jnp.concatenate([x, pad], axis=0).T


def _decode_rwkv_kernel(pa_ref, shift_ref, wkv_ref,
                        mu_ref, w0_ref, w2_ref, a0_ref, a2_ref, g2_ref, kkw_ref, ka_ref, rk_ref, lnw_ref, lnb_ref,
                        hb_ref, oa_ref, wkvo_ref, tr_ref, yt_ref, g_ref, bonus_ref):
    h = pl.program_id(0)

    @pl.when(h == 0)
    def _():
        r, k2, v, logw, a, g, kk, bonus = _rwkv_prep(
            pa_ref[...], shift_ref[...], mu_ref[...], w0_ref[...], w2_ref[...], a0_ref[...], a2_ref[...],
            g2_ref[...], kkw_ref[...], ka_ref[...], rk_ref[...], hb_ref[...])
        for i, x in enumerate((-kk, jnp.exp(logw), kk * a, k2, r, v)):
            tr_ref[i] = x.T
        g_ref[...] = g
        bonus_ref[...] = bonus

    base = pl.multiple_of(h * HEAD_A, HEAD_A)
    hs = pl.ds(base, HEAD_A)
    a_t, w_t, b_t, k_t, r_t = (tr_ref[i, hs, :] for i in range(5))

    def value_row(vi, carry):
        st = wkv_ref[0, vi]
        sa = jnp.sum(st * a_t, axis=0, keepdims=True)
        st = st * w_t + sa * b_t + tr_ref[5, pl.ds(base + vi, 1), :] * k_t
        wkvo_ref[0, vi] = st
        yt_ref[pl.ds(base + vi, 1), :] = jnp.sum(st * r_t, axis=0, keepdims=True)
        return carry

    lax.fori_loop(0, HEAD_A, value_row, 0, unroll=8)

    @pl.when(h == H_A - 1)
    def _():
        oa_ref[...] = _group_norm_gate(yt_ref[...].T, bonus_ref[...], g_ref[...], lnw_ref[...], lnb_ref[...],
                                       hb_ref[...])


def _decode_rwkv(pa, shift, wkv_t, rw_consts):
    n = pa.shape[0]
    full = lambda w: pl.BlockSpec((n, w), lambda h: (0, 0))
    state = pl.BlockSpec((1, HEAD_A, HEAD_A, n), lambda h: (h, 0, 0, 0))
    return pl.pallas_call(
        _decode_rwkv_kernel,
        grid=(H_A,),
        in_specs=[full(A_COLS), full(A_COLS), state] + [_const_spec(c.shape) for c in rw_consts],
        out_specs=[full(C_A), state],
        out_shape=[jax.ShapeDtypeStruct((n, C_A), F32), jax.ShapeDtypeStruct(wkv_t.shape, F32)],
        scratch_shapes=[pltpu.VMEM((6, C_A, n), F32), pltpu.VMEM((C_A, n), F32),
                        pltpu.VMEM((n, C_A), F32), pltpu.VMEM((n, C_A), F32)],
        compiler_params=pltpu.CompilerParams(dimension_semantics=("arbitrary",), vmem_limit_bytes=VMEM_LIMIT),
        name="decode_rwkv",
    )(pa, shift, wkv_t, *rw_consts)


def _decode_gdn_kernel(qkv_ref, z_ref, ba_ref, cst_ref, gdn_ref, conv_ref, alog_ref, dtb_ref, gnorm_ref,
                       ob_ref, gdno_ref):
    cst = cst_ref[...]
    q, k, vv, beta, glog = _gdn_prep(cst[:, :CONV_CH], cst[:, CONV_CH:2 * CONV_CH], cst[:, 2 * CONV_CH:],
                                     qkv_ref[...], ba_ref[...], conv_ref[...], alog_ref[...], dtb_ref[...])
    eg = jnp.exp(glog)
    q_cols = _to_columns(q)
    k_cols = _to_columns(k)
    gunits = [(s_i, h) for s_i in range(DEC_TILE) for h in range(H_B)]
    gn = range(len(gunits))
    ls = lambda h: slice(DK * h, DK * (h + 1))
    be = [beta[s_i:s_i + 1, h:h + 1] for s_i, h in gunits]
    e = [eg[s_i:s_i + 1, H_B + h:H_B + h + 1] for s_i, h in gunits]
    kc = [k_cols[ls(h), s_i:s_i + 1] for s_i, h in gunits]
    qc = [q_cols[ls(h), s_i:s_i + 1] for s_i, h in gunits]
    gst = [gdn_ref[s_i, h] for s_i, h in gunits]
    ws = [jnp.sum((kc[i] * (be[i] * e[i])) * gst[i], axis=0, keepdims=True) for i in gn]
    qs = [jnp.sum((qc[i] * e[i]) * gst[i], axis=0, keepdims=True) for i in gn]
    v_new = [be[i] * vv[s_i:s_i + 1, ls(h)] - ws[i] for i, (s_i, h) in enumerate(gunits)]
    qk = [jnp.sum(q[s_i:s_i + 1, ls(h)] * k[s_i:s_i + 1, ls(h)], axis=-1, keepdims=True) for s_i, h in gunits]
    for i, (s_i, h) in enumerate(gunits):
        gdno_ref[s_i, h] = gst[i] * e[i] + kc[i] * v_new[i]
    o_units = [qs[i] + qk[i] * v_new[i] for i in gn]
    o = jnp.concatenate([jnp.concatenate(o_units[H_B * s_i:H_B * (s_i + 1)], axis=1) for s_i in range(DEC_TILE)],
                        axis=0)
    ob_ref[...] = _head_norm_gate(o, z_ref[...], gnorm_ref[...])


def _decode_gdn(qkv, z, ba, cst, gdn, gdn_consts):
    n = qkv.shape[0]
    row = lambda w: pl.BlockSpec((DEC_TILE, w), lambda i: (i, 0))
    state = pl.BlockSpec((DEC_TILE, H_B, DK, DV), lambda i: (i, 0, 0, 0))
    return pl.pallas_call(
        _decode_gdn_kernel,
        grid=(n // DEC_TILE,),
        in_specs=[row(CONV_CH), row(C_BV), row(LANES), row(3 * CONV_CH), state]
                 + [_const_spec(c.shape) for c in gdn_consts],
        out_specs=[row(C_BV), state],
        out_shape=[jax.ShapeDtypeStruct((n, C_BV), F32), jax.ShapeDtypeStruct(gdn.shape, F32)],
        compiler_params=pltpu.CompilerParams(dimension_semantics=("parallel",), vmem_limit_bytes=VMEM_LIMIT),
        name="decode_gdn",
    )(qkv, z, ba, cst, gdn, *gdn_consts)


def _post_kernel(x_ref, oa_ref, ob_ref, pe_ref, nmix_ref, wg_ref, wba_ref, wbb_ref, wout_ref, nffn_ref,
                 wfg_ref, wfu_ref, wfd_ref, nple_ref, wpg_ref, wpp_ref, nfin_ref, y_ref):
    x = x_ref[...]
    u = _rms(x, nmix_ref[...]).astype(BF16)
    gates = _sigmoid(_dot(u, wg_ref[...]))
    mix = (gates[:, :D_MODEL] * _dot(oa_ref[...].astype(BF16), wba_ref[...])
           + gates[:, D_MODEL:] * _dot(ob_ref[...].astype(BF16), wbb_ref[...]))
    h = x + _dot(mix.astype(BF16), wout_ref[...])
    u2 = _rms(h, nffn_ref[...]).astype(BF16)
    ff = _silu(_dot(u2, wfg_ref[...])) * _dot(u2, wfu_ref[...])
    h = h + _dot(ff.astype(BF16), wfd_ref[...])
    u3 = _rms(h, nple_ref[...]).astype(BF16)
    h = h + _sigmoid(_dot(u3, wpg_ref[...])) * _dot(pe_ref[...].astype(BF16), wpp_ref[...])
    y_ref[...] = _rms(h, nfin_ref[...])


def _post(x, oa, ob, pe, consts):
    n = x.shape[0]
    tm = min(ROW_TILE, n)
    row = lambda w: pl.BlockSpec((tm, w), lambda i: (i, 0))
    return pl.pallas_call(
        _post_kernel,
        grid=(n // tm,),
        in_specs=[row(D_MODEL), row(C_A), row(C_BV), row(PLE_DIM)] + [_const_spec(c.shape) for c in consts],
        out_specs=row(D_MODEL),
        out_shape=jax.ShapeDtypeStruct((n, D_MODEL), F32),
        compiler_params=pltpu.CompilerParams(dimension_semantics=("parallel",), vmem_limit_bytes=VMEM_LIMIT),
        name="post",
    )(x, oa, ob, pe, *consts)


def _chunk_cumsum_matrix():
    i = jnp.arange(T_TILE)
    same = (i[:, None] // CHUNK) == (i[None, :] // CHUNK)
    return (same & (i[None, :] <= i[:, None])).astype(BF16)


def kernel(x_prompt, x_sample, p_prompt, p_sample, state_shift, state_wkv, state_conv, state_gdn, norm_mix, w_in, mu_shift, rw_w0, rw_w2, rw_a0, rw_a2, rw_g2, rw_kk, rw_ka, rw_rk, rw_ln_w, rw_ln_b, gdn_conv, gdn_a_log, gdn_dt_bias, gdn_norm, w_branch_a, w_branch_b, w_out, norm_ffn, w_ffn_gate, w_ffn_up, w_ffn_down, norm_ple, w_ple_gate, w_ple_proj, norm_final):
    bsz, seq, _ = x_prompt.shape
    nd = x_sample.shape[0]
    row = lambda p: p.reshape(1, -1)

    w_in0 = w_in[0]
    b0 = A_COLS
    wa = w_in0[:, :b0].astype(BF16)
    wq = w_in0[:, b0:b0 + CONV_CH].astype(BF16)
    wz = w_in0[:, b0 + CONV_CH:b0 + CONV_CH + C_BV].astype(BF16)
    wb = jnp.pad(w_in0[:, b0 + CONV_CH + C_BV:b0 + CONV_CH + C_BV + 2 * H_B], ((0, 0), (0, LANES - 2 * H_B))).astype(BF16)
    wg = w_in0[:, b0 + CONV_CH + C_BV + 2 * H_B:].astype(BF16)
    w2p = jnp.concatenate([rw_w2[0], jnp.zeros((LORA_A, C_A), F32)], axis=0).astype(BF16)
    a2p = jnp.concatenate([jnp.zeros((LORA_W, C_A), F32), rw_a2[0]], axis=0).astype(BF16)
    ch = jnp.arange(C_A) // HEAD_A
    hb = (ch[:, None] == ch[None, :]).astype(BF16)
    tri = _chunk_cumsum_matrix()
    alog = jnp.pad(gdn_a_log[0], (H_B, LANES - 2 * H_B)).reshape(1, LANES)
    dtb = jnp.pad(gdn_dt_bias[0], (H_B, LANES - 2 * H_B)).reshape(1, LANES)
    rw_consts = (row(mu_shift[0]), row(rw_w0[0]), w2p, row(rw_a0[0]), a2p, rw_g2[0].astype(BF16), row(rw_kk[0]), row(rw_ka[0]),
                 row(rw_rk[0]), row(rw_ln_w[0]), row(rw_ln_b[0]), hb)
    gdn_consts = (gdn_conv[0], alog, dtb, row(gdn_norm[0]))
    post_consts = (row(norm_mix[0]), wg, w_branch_a[0].astype(BF16), w_branch_b[0].astype(BF16),
                   w_out[0].astype(BF16), row(norm_ffn[0]), w_ffn_gate[0].astype(BF16), w_ffn_up[0].astype(BF16),
                   w_ffn_down[0].astype(BF16), row(norm_ple[0]), w_ple_gate[0].astype(BF16),
                   w_ple_proj[0].astype(BF16), row(norm_final))

    xp = x_prompt.reshape(bsz * seq, D_MODEL)
    oa, wkv_pairs, shift_tail = _rwkv_prompt(x_prompt, row(norm_mix[0]), wa, *rw_consts, tri)
    ob, gdn_p, conv_tail = _gdn_prompt(x_prompt, row(norm_mix[0]), wq, wz, wb, *gdn_consts, tri)
    y_prompt = _post(xp, oa.reshape(bsz * seq, C_A), ob.reshape(bsz * seq, C_BV),
                     p_prompt[0].reshape(bsz * seq, PLE_DIM), post_consts).reshape(bsz, seq, D_MODEL)
    wkv_p = jnp.stack([wkv_pairs[:, :, :HEAD_A, :HEAD_A], wkv_pairs[:, :, HEAD_A:, HEAD_A:]], axis=2)
    wkv_p = wkv_p.reshape(bsz, H_A, HEAD_A, HEAD_A)
    shift_p = shift_tail[:, HALO - 1:, :]
    conv_p = conv_tail[:, HALO - (CONV_W - 1):, :]

    xs = x_sample.reshape(nd, D_MODEL)
    pa_s, qkv_s, z_s, ba_s = _inproj(xs, row(norm_mix[0]), wa, wq, wz, wb)
    cst = state_conv[0].reshape(nd, (CONV_W - 1) * CONV_CH)
    oa_s, wkv_t = _decode_rwkv(pa_s, state_shift[0].reshape(nd, A_COLS), jnp.transpose(state_wkv[0], (1, 2, 3, 0)),
                               rw_consts)
    wkv_s = jnp.transpose(wkv_t, (3, 0, 1, 2))
    ob_s, gdn_s = _decode_gdn(qkv_s, z_s, ba_s, cst, state_gdn[0], gdn_consts)
    y_sample = _post(xs, oa_s, ob_s, p_sample[0].reshape(nd, PLE_DIM), post_consts).reshape(nd, 1, D_MODEL)
    conv_s = jnp.concatenate([cst[:, CONV_CH:], qkv_s], axis=1).reshape(nd, CONV_W - 1, CONV_CH)

    return (y_prompt, y_sample, shift_p[None], wkv_p[None], conv_p[None], gdn_p[None],
            pa_s.reshape(1, nd, 1, A_COLS), wkv_s[None], conv_s[None], gdn_s[None])
```

```python
import functools

import jax
import jax.numpy as jnp
from jax import lax
from jax.experimental import pallas as pl
from jax.experimental.pallas import tpu as pltpu

F32 = jnp.float32
BF16 = jnp.bfloat16

D_MODEL = 1024
HEAD_A = 64
C_A = 512
H_A = 8
LORA_W = 64
LORA_A = 64
LORA_G = 128
A_COLS = 3 * C_A + LORA_W + LORA_A + LORA_G
DK = 128
DV = 128
H_B = 4
C_BK = 512
C_BV = 512
CONV_W = 4
CONV_CH = 2 * C_BK + C_BV
D_FF = 2816
PLE_DIM = 256
NORM_EPS = 1e-6
GN_EPS = 64e-5
L2_EPS = 1e-6

LANES = 128
CHUNK = 64
PAIR = 2 * CHUNK
T_TILE = 256
G_TILE = 512
ROW_TILE = 256
DEC_TILE = 8
HALO = 8
VMEM_LIMIT = 56 * 1024 * 1024
SUM_TERMS = 2
CUMSUM_TERMS = 3
INV_TERMS = 1
CHAIN_TERMS = 1
PAIR_GROUP = 4


def _dot(a, b):
    return jnp.dot(a, b, preferred_element_type=F32)


def _sigmoid(x):
    return 1.0 / (1.0 + jnp.exp(-x))


def _silu(x):
    return x * _sigmoid(x)


def _softplus(x):
    return jnp.maximum(x, 0.0) + jnp.log(1.0 + jnp.exp(-jnp.abs(x)))


def _rms(x, gain):
    return x * lax.rsqrt(jnp.mean(x * x, axis=-1, keepdims=True) + NORM_EPS) * gain


def _pair_masks():
    ri = lax.broadcasted_iota(jnp.int32, (PAIR, PAIR), 0)
    ci = lax.broadcasted_iota(jnp.int32, (PAIR, PAIR), 1)
    same = (ri < CHUNK) == (ci < CHUNK)
    strict = same & (ci < ri)
    incl = same & (ci <= ri)
    eye = (ri == ci).astype(F32)
    return strict, incl, eye


_NN = (((1,), (0,)), ((), ()))
_NT = (((1,), (1,)), ((), ()))
_TN = (((0,), (0,)), ((), ()))


def _split(x, terms):
    if isinstance(x, (list, tuple)):
        return list(x)
    if x.dtype == BF16:
        return [x]
    parts = []
    for i in range(terms):
        h = x.astype(BF16)
        parts.append(h)
        if i + 1 < terms:
            x = x - h.astype(F32)
    return parts


def _mm(a, b, dims=_NN, na=1, nb=1):
    pa, pb = _split(a, na), _split(b, nb)
    acc = None
    for i, ai in enumerate(pa):
        for j, bj in enumerate(pb):
            if i + j < max(len(pa), len(pb)):
                d = lax.dot_general(ai, bj, dims, preferred_element_type=F32)
                acc = d if acc is None else acc + d
    return acc


def _neumann_inverse_many(lmats, eye, terms):
    ts = [eye + l for l in lmats]
    ps = [_split(l, terms) for l in lmats]
    n = 2
    while n < CHUNK:
        ps = [_split(_mm(p, p), terms) for p in ps]
        ts = [t + _mm(t, p, _NN, terms) for t, p in zip(ts, ps)]
        n *= 2
    return ts


def _inproj_kernel(x_ref, g_ref, wa_ref, wq_ref, wz_ref, wb_ref, pa_ref, qkv_ref, z_ref, ba_ref):
    u = _rms(x_ref[...], g_ref[...]).astype(BF16)
    pa_ref[...] = _dot(u, wa_ref[...])
    qkv_ref[...] = _dot(u, wq_ref[...])
    z_ref[...] = _dot(u, wz_ref[...])
    ba_ref[...] = _dot(u, wb_ref[...])


def _const_spec(shape):
    nd = len(shape)
    return pl.BlockSpec(shape, lambda *_: (0,) * nd, pipeline_mode=pl.Buffered(1))


def _inproj(x, gain, wa, wq, wz, wb):
    n = x.shape[0]
    tm = min(ROW_TILE, n)
    row = lambda w: pl.BlockSpec((tm, w), lambda i: (i, 0))
    return pl.pallas_call(
        _inproj_kernel,
        grid=(n // tm,),
        in_specs=[row(D_MODEL), _const_spec(gain.shape), _const_spec(wa.shape), _const_spec(wq.shape),
                  _const_spec(wz.shape), _const_spec(wb.shape)],
        out_specs=[row(A_COLS), row(CONV_CH), row(C_BV), row(LANES)],
        out_shape=[jax.ShapeDtypeStruct((n, A_COLS), F32), jax.ShapeDtypeStruct((n, CONV_CH), F32),
                   jax.ShapeDtypeStruct((n, C_BV), F32), jax.ShapeDtypeStruct((n, LANES), F32)],
        compiler_params=pltpu.CompilerParams(dimension_semantics=("parallel",), vmem_limit_bytes=VMEM_LIMIT),
        name="inproj",
    )(x, gain, wa, wq, wz, wb)


def _rwkv_prep(pa, prev, mu, w0, w2p, a0, a2p, g2, kkw, ka, rk, hb):
    xa = pa + (prev - pa) * mu
    r = xa[:, :C_A]
    k = xa[:, C_A:2 * C_A]
    v = xa[:, 2 * C_A:3 * C_A]
    xwa = xa[:, 3 * C_A:3 * C_A + LORA_W + LORA_A]
    xg = xa[:, 3 * C_A + LORA_W + LORA_A:]
    w_log = -_softplus(-(w0 + _mm(jnp.tanh(xwa), w2p))) - 0.5
    logw = -jnp.exp(w_log)
    a = _sigmoid(a0 + _mm(xwa, a2p))
    g = _mm(_sigmoid(xg), g2)
    kx = k * kkw
    kk = kx * lax.rsqrt(_head_sum(kx * kx, hb) + L2_EPS)
    k2 = k * (1.0 + (a - 1.0) * ka)
    bonus = _head_sum(r * k2 * rk, hb) * v
    return r, k2, v, logw, a, g, kk, bonus


def _head_sum(x, hb):
    groups = [_mm(x[:, LANES * j:LANES * (j + 1)], hb, _NN, SUM_TERMS) for j in range(x.shape[1] // LANES)]
    return jnp.concatenate(groups, axis=1)


def _group_norm_gate(y, bonus, g, lnw, lnb, hb):
    mean = _head_sum(y, hb) * (1.0 / HEAD_A)
    d = y - mean
    var = _head_sum(d * d, hb) * (1.0 / HEAD_A)
    yn = d * lax.rsqrt(var + GN_EPS) * lnw + lnb
    return (yn + bonus) * g


def _rwkv_prompt_kernel(tiles_per_seq, x_ref, nmix_ref, wa_ref,
                        mu_ref, w0_ref, w2_ref, a0_ref, a2_ref, g2_ref, kkw_ref, ka_ref, rk_ref,
                        lnw_ref, lnb_ref, hb_ref, tri_ref,
                        oa_ref, wkv_ref, shift_ref,
                        pa_ref, ext_ref, s_ref, at_ref, bt_ref, kt_ref, rt_ref, v_ref, bh_ref, kh_ref, y_ref):
    s_id = pl.program_id(0)

    @pl.when(s_id == 0)
    def _():
        pa_ref[...] = jnp.zeros_like(pa_ref)

    @pl.when(jnp.maximum(s_id - 1, 0) % tiles_per_seq == 0)
    def _():
        ext_ref[0:HALO, :] = jnp.zeros((HALO, A_COLS), F32)
        s_ref[...] = jnp.zeros_like(s_ref)

    pa = pa_ref[...]
    ext_ref[HALO:, :] = pa
    prev = ext_ref[pl.ds(HALO - 1, T_TILE), :]
    ext_ref[0:HALO, :] = pa[T_TILE - HALO:, :]
    shift_ref[0] = pa[T_TILE - HALO:, :]

    pa_ref[...] = _dot(_rms(x_ref[0], nmix_ref[...]).astype(BF16), wa_ref[...])

    hb = hb_ref[...]
    r, k2, v, logw, a, g, kk, bonus = _rwkv_prep(
        pa, prev, mu_ref[...], w0_ref[...], w2_ref[...], a0_ref[...], a2_ref[...], g2_ref[...],
        kkw_ref[...], ka_ref[...], rk_ref[...], hb)

    nchunk = T_TILE // CHUNK
    cs = _mm(tri_ref[...], logw, _NN, 1, CUMSUM_TERMS)
    ends = [cs[CHUNK * (c + 1) - 1:CHUNK * (c + 1)] for c in range(nchunk)]
    tot = jnp.concatenate([jnp.broadcast_to(e, (CHUNK, C_A)) for e in ends], axis=0)
    dinv = jnp.exp(-cs)
    dend = jnp.exp(tot - cs)
    b_in = kk * a
    at_ref[...] = -kk * jnp.exp(cs - logw)
    bt_ref[...] = b_in * dinv
    kt_ref[...] = k2 * dinv
    rt_ref[...] = r * jnp.exp(cs)
    v_ref[...] = v
    bh_ref[...] = b_in * dend
    kh_ref[...] = k2 * dend
    gl = [jnp.exp(e) for e in ends]

    strict, incl, eye = _pair_masks()
    low = lax.broadcasted_iota(jnp.int32, (CHUNK, LANES), 1) < HEAD_A

    def stack(x):
        return jnp.concatenate([jnp.where(low, x, 0.0), jnp.where(low, 0.0, x)], axis=0)

    npair = C_A // LANES
    for p0 in range(0, npair, PAIR_GROUP):
        idx = [(p, c) for p in range(p0, p0 + PAIR_GROUP) for c in range(nchunk)]
        lanes = lambda p: slice(LANES * p, LANES * (p + 1))
        rows = lambda c: slice(CHUNK * c, CHUNK * (c + 1))
        ld = lambda ref: [stack(ref[rows(c), lanes(p)]) for p, c in idx]
        at, bt, kt, rt, vs, bh, kh = (ld(ref) for ref in (at_ref, bt_ref, kt_ref, rt_ref, v_ref, bh_ref, kh_ref))
        n = range(len(idx))
        aa = [_mm(jnp.concatenate([at[i], rt[i]], axis=0), jnp.concatenate([bt[i], kt[i]], axis=0), _NT) for i in n]
        a_ab = [jnp.where(strict, aa[i][:PAIR, :PAIR], 0.0) for i in n]
        a_ak = [jnp.where(strict, aa[i][:PAIR, PAIR:], 0.0) for i in n]
        a_rb = [jnp.where(incl, aa[i][PAIR:, :PAIR], 0.0) for i in n]
        a_rk = [jnp.where(incl, aa[i][PAIR:, PAIR:], 0.0) for i in n]
        tinv = _neumann_inverse_many(a_ab, eye, INV_TERMS)
        akv = [_mm(a_ak[i], vs[i]) for i in n]
        gu = [_mm(tinv[i], jnp.concatenate([at[i], akv[i]], axis=1)) for i in n]
        ry = [_mm(a_rb[i], gu[i]) for i in n]
        rkv = [_mm(a_rk[i], vs[i]) for i in n]
        pq = [_mm(gu[i], bh[i], _TN) for i in n]
        vk = [_mm(vs[i], kh[i], _TN) for i in n]
        rp = [rt[i] + ry[i][:, :PAIR] for i in n]
        yc = [ry[i][:, PAIR:] + rkv[i] for i in n]
        pm = [eye * gl[idx[i][1]][:, lanes(idx[i][0])] + pq[i][:PAIR] for i in n]
        qm = [pq[i][PAIR:] + vk[i] for i in n]
        s = {p: s_ref[p] for p in range(p0, p0 + PAIR_GROUP)}
        for c in range(nchunk):
            for p in range(p0, p0 + PAIR_GROUP):
                i = idx.index((p, c))
                y = _mm(rp[i], s[p], _NT) + yc[i]
                y_ref[rows(c), lanes(p)] = y[:CHUNK] + y[CHUNK:]
                s[p] = _mm(s[p], pm[i], _NN, CHAIN_TERMS, CHAIN_TERMS) + qm[i]
        for p in range(p0, p0 + PAIR_GROUP):
            s_ref[p] = s[p]

    oa_ref[0] = _group_norm_gate(y_ref[...], bonus, g, lnw_ref[...], lnb_ref[...], hb)
    wkv_ref[0] = s_ref[...]


def _tile_maps(bsz, seq, tile):
    tps = seq // tile
    last = bsz * tps - 1
    cur = lambda s: jnp.minimum(s, last)
    prv = lambda s: jnp.maximum(s - 1, 0)
    load = lambda s: (cur(s) // tps, cur(s) % tps, 0)
    emit = lambda s: (prv(s) // tps, prv(s) % tps, 0)
    per_seq3 = lambda s: (prv(s) // tps, 0, 0)
    per_seq4 = lambda s: (prv(s) // tps, 0, 0, 0)
    return tps, last + 2, load, emit, per_seq3, per_seq4


def _rwkv_prompt(x, nmix, wa, mu, w0, w2p, a0, a2p, g2, kkw, ka, rk, lnw, lnb, hb, tri):
    bsz, seq, _ = x.shape
    consts = (nmix, wa, mu, w0, w2p, a0, a2p, g2, kkw, ka, rk, lnw, lnb, hb, tri)
    npair = C_A // LANES
    tps, steps, load, emit, per_seq3, per_seq4 = _tile_maps(bsz, seq, T_TILE)
    tile = lambda: pltpu.VMEM((T_TILE, C_A), F32)
    return pl.pallas_call(
        functools.partial(_rwkv_prompt_kernel, tps),
        grid=(steps,),
        in_specs=[pl.BlockSpec((1, T_TILE, D_MODEL), load)] + [_const_spec(c.shape) for c in consts],
        out_specs=[pl.BlockSpec((1, T_TILE, C_A), emit),
                   pl.BlockSpec((1, npair, LANES, LANES), per_seq4),
                   pl.BlockSpec((1, HALO, A_COLS), per_seq3)],
        out_shape=[jax.ShapeDtypeStruct((bsz, seq, C_A), F32),
                   jax.ShapeDtypeStruct((bsz, npair, LANES, LANES), F32),
                   jax.ShapeDtypeStruct((bsz, HALO, A_COLS), F32)],
        scratch_shapes=[pltpu.VMEM((T_TILE, A_COLS), F32),
                        pltpu.VMEM((T_TILE + HALO, A_COLS), F32), pltpu.VMEM((npair, LANES, LANES), F32)]
                       + [tile() for _ in range(8)],
        compiler_params=pltpu.CompilerParams(dimension_semantics=("arbitrary",), vmem_limit_bytes=VMEM_LIMIT),
        name="rwkv_prompt",
    )(x, *consts)


def _gdn_prep(x0, x1, x2, x3, ba, conv, alog, dtb):
    c = _silu(x0 * conv[0:1] + x1 * conv[1:2] + x2 * conv[2:3] + x3 * conv[3:4])
    qs, ks = [], []
    for h in range(H_B):
        qh = c[:, DK * h:DK * (h + 1)]
        kh = c[:, C_BK + DK * h:C_BK + DK * (h + 1)]
        qs.append(qh * lax.rsqrt(jnp.sum(qh * qh, axis=-1, keepdims=True) + L2_EPS) * (DK ** -0.5))
        ks.append(kh * lax.rsqrt(jnp.sum(kh * kh, axis=-1, keepdims=True) + L2_EPS))
    q = jnp.concatenate(qs, axis=1)
    k = jnp.concatenate(ks, axis=1)
    v = c[:, 2 * C_BK:]
    beta = _sigmoid(ba)
    glog = -jnp.exp(alog) * _softplus(ba + dtb)
    return q, k, v, beta, glog


def _head_norm_gate(o, z, gnorm):
    outs = []
    for h in range(H_B):
        oh = o[:, DV * h:DV * (h + 1)]
        zh = z[:, DV * h:DV * (h + 1)]
        oh = oh * lax.rsqrt(jnp.mean(oh * oh, axis=-1, keepdims=True) + NORM_EPS) * gnorm
        outs.append(oh * _silu(zh))
    return jnp.concatenate(outs, axis=1)


def _gdn_prompt_kernel(tiles_per_seq, x_ref, nmix_ref, wq_ref, wz_ref, wb_ref,
                       conv_ref, alog_ref, dtb_ref, gnorm_ref, tri_ref,
                       ob_ref, gdn_ref, convo_ref,
                       qkvn_ref, zn_ref, ban_ref, ext_ref, s_ref, q_ref, k_ref, v_ref, gc_ref, be_ref, o_ref):
    s_id = pl.program_id(0)

    @pl.when(s_id == 0)
    def _():
        qkvn_ref[...] = jnp.zeros_like(qkvn_ref)
        zn_ref[...] = jnp.zeros_like(zn_ref)
        ban_ref[...] = jnp.zeros_like(ban_ref)

    @pl.when(jnp.maximum(s_id - 1, 0) % tiles_per_seq == 0)
    def _():
        ext_ref[0:HALO, :] = jnp.zeros((HALO, CONV_CH), F32)
        s_ref[...] = jnp.zeros_like(s_ref)

    x3 = qkvn_ref[...]
    z = zn_ref[...]
    ba = ban_ref[...]
    ext_ref[HALO:, :] = x3
    x0 = ext_ref[pl.ds(HALO - 3, G_TILE), :]
    x1 = ext_ref[pl.ds(HALO - 2, G_TILE), :]
    x2 = ext_ref[pl.ds(HALO - 1, G_TILE), :]
    ext_ref[0:HALO, :] = x3[G_TILE - HALO:, :]
    convo_ref[0] = x3[G_TILE - HALO:, :]

    u = _rms(x_ref[0], nmix_ref[...]).astype(BF16)
    qkvn_ref[...] = _dot(u, wq_ref[...])
    zn_ref[...] = _dot(u, wz_ref[...])
    ban_ref[...] = _dot(u, wb_ref[...])

    q, k, v, beta, glog = _gdn_prep(x0, x1, x2, x3, ba, conv_ref[...], alog_ref[...], dtb_ref[...])
    q_ref[...] = q
    k_ref[...] = k
    v_ref[...] = v
    be_ref[...] = beta
    gc_ref[...] = _mm(tri_ref[...], glog, _NN, 1, CUMSUM_TERMS)

    strict, incl, eye = _pair_masks()

    nchunk = G_TILE // CHUNK
    idx = [(pr, c) for pr in range(H_B // 2) for c in range(nchunk)]
    n = range(len(idx))
    rows = lambda c: slice(CHUNK * c, CHUNK * (c + 1))
    cat = lambda ref: [jnp.concatenate([ref[rows(c), DK * h:DK * (h + 1)] for h in (2 * pr, 2 * pr + 1)], axis=0)
                       for pr, c in idx]
    col = lambda ref, off: [jnp.concatenate([ref[rows(c), off + h:off + h + 1] for h in (2 * pr, 2 * pr + 1)], axis=0)
                            for pr, c in idx]
    qs, ks, vs = cat(q_ref), cat(k_ref), cat(v_ref)
    beta_c = col(be_ref, 0)
    gc = col(gc_ref, H_B)
    gt = [jnp.concatenate([jnp.broadcast_to(g_[CHUNK * (i + 1) - 1:CHUNK * (i + 1)], (CHUNK, 1)) for i in range(2)],
                          axis=0) for g_ in gc]
    decay = []
    for g_ in gc:
        gc_full = jnp.broadcast_to(g_, (PAIR, PAIR))
        diff = gc_full - gc_full.T
        decay.append(jnp.exp(diff))
    kb = [ks[i] * beta_c[i] for i in n]
    vb = [vs[i] * beta_c[i] for i in n]
    kq = [_mm(jnp.concatenate([kb[i], qs[i]], axis=0), ks[i], _NT) for i in n]
    lmat = [jnp.where(strict, kq[i][:PAIR] * decay[i], 0.0) for i in n]
    qk = [jnp.where(incl, kq[i][PAIR:] * decay[i], 0.0) for i in n]
    tinv = _neumann_inverse_many([-l for l in lmat], eye, INV_TERMS)
    eg = [jnp.exp(g_) for g_ in gc]
    uw = [_mm(tinv[i], jnp.concatenate([vb[i], kb[i] * eg[i]], axis=1)) for i in n]
    ow = [_mm(qk[i], uw[i]) for i in n]
    oc = [ow[i][:, :DV] for i in n]
    rq = [qs[i] * eg[i] - ow[i][:, DV:] for i in n]
    kd = [ks[i] * jnp.exp(gt[i] - gc[i]) for i in n]
    half = lambda j: slice(CHUNK * j, CHUNK * (j + 1))
    pq = [[_mm(kd[i][half(j)], uw[i][half(j)], _TN) for j in range(2)] for i in n]
    pm = [[eye * jnp.exp(gt[i][CHUNK * j:CHUNK * j + 1]) - pq[i][j][:, DV:] for j in range(2)] for i in n]
    s = [s_ref[h] for h in range(H_B)]
    for c in range(nchunk):
        for h in range(H_B):
            i, j = idx.index((h // 2, c)), h % 2
            o_ref[rows(c), DV * h:DV * (h + 1)] = _mm(rq[i][half(j)], s[h]) + oc[i][half(j)]
            s[h] = _mm(pm[i][j], s[h], _NN, CHAIN_TERMS, CHAIN_TERMS) + pq[i][j][:, :DV]
    for h in range(H_B):
        s_ref[h] = s[h]

    ob_ref[0] = _head_norm_gate(o_ref[...], z, gnorm_ref[...])
    gdn_ref[0] = s_ref[...]


def _gdn_prompt(x, nmix, wq, wz, wb, conv, alog, dtb, gnorm, tri):
    bsz, seq, _ = x.shape
    consts = (nmix, wq, wz, wb, conv, alog, dtb, gnorm, tri)
    tps, steps, load, emit, per_seq3, per_seq4 = _tile_maps(bsz, seq, G_TILE)
    return pl.pallas_call(
        functools.partial(_gdn_prompt_kernel, tps),
        grid=(steps,),
        in_specs=[pl.BlockSpec((1, G_TILE, D_MODEL), load)] + [_const_spec(c.shape) for c in consts],
        out_specs=[pl.BlockSpec((1, G_TILE, C_BV), emit), pl.BlockSpec((1, H_B, DK, DV), per_seq4),
                   pl.BlockSpec((1, HALO, CONV_CH), per_seq3)],
        out_shape=[jax.ShapeDtypeStruct((bsz, seq, C_BV), F32), jax.ShapeDtypeStruct((bsz, H_B, DK, DV), F32),
                   jax.ShapeDtypeStruct((bsz, HALO, CONV_CH), F32)],
        scratch_shapes=[pltpu.VMEM((G_TILE, CONV_CH), F32), pltpu.VMEM((G_TILE, C_BV), F32),
                        pltpu.VMEM((G_TILE, LANES), F32),
                        pltpu.VMEM((G_TILE + HALO, CONV_CH), F32), pltpu.VMEM((H_B, DK, DV), F32),
                        pltpu.VMEM((G_TILE, C_BK), F32), pltpu.VMEM((G_TILE, C_BK), F32),
                        pltpu.VMEM((G_TILE, C_BV), F32), pltpu.VMEM((G_TILE, LANES), F32),
                        pltpu.VMEM((G_TILE, LANES), F32), pltpu.VMEM((G_TILE, C_BV), F32)],
        compiler_params=pltpu.CompilerParams(dimension_semantics=("arbitrary",), vmem_limit_bytes=VMEM_LIMIT),
        name="gdn_prompt",
    )(x, *consts)


def _to_columns(x):
    pad = jnp.zeros((LANES - DEC_TILE, x.shape[1]), F32)
    return jnp.concatenate([x, pad], axis=0).T


def _decode_rwkv_kernel(pa_ref, shift_ref, wkv_ref,
                        mu_ref, w0_ref, w2_ref, a0_ref, a2_ref, g2_ref, kkw_ref, ka_ref, rk_ref, lnw_ref, lnb_ref,
                        hb_ref, oa_ref, wkvo_ref, tr_ref, yt_ref, g_ref, bonus_ref):
    h = pl.program_id(0)

    @pl.when(h == 0)
    def _():
        r, k2, v, logw, a, g, kk, bonus = _rwkv_prep(
            pa_ref[...], shift_ref[...], mu_ref[...], w0_ref[...], w2_ref[...], a0_ref[...], a2_ref[...],
            g2_ref[...], kkw_ref[...], ka_ref[...], rk_ref[...], hb_ref[...])
        for i, x in enumerate((-kk, jnp.exp(logw), kk * a, k2, r, v)):
            tr_ref[i] = x.T
        g_ref[...] = g
        bonus_ref[...] = bonus

    base = pl.multiple_of(h * HEAD_A, HEAD_A)
    hs = pl.ds(base, HEAD_A)
    a_t, w_t, b_t, k_t, r_t = (tr_ref[i, hs, :] for i in range(5))

    def value_row(vi, carry):
        st = wkv_ref[0, vi]
        sa = jnp.sum(st * a_t, axis=0, keepdims=True)
        st = st * w_t + sa * b_t + tr_ref[5, pl.ds(base + vi, 1), :] * k_t
        wkvo_ref[0, vi] = st
        yt_ref[pl.ds(base + vi, 1), :] = jnp.sum(st * r_t, axis=0, keepdims=True)
        return carry

    lax.fori_loop(0, HEAD_A, value_row, 0, unroll=8)

    @pl.when(h == H_A - 1)
    def _():
        oa_ref[...] = _group_norm_gate(yt_ref[...].T, bonus_ref[...], g_ref[...], lnw_ref[...], lnb_ref[...],
                                       hb_ref[...])


def _decode_rwkv(pa, shift, wkv_t, rw_consts):
    n = pa.shape[0]
    full = lambda w: pl.BlockSpec((n, w), lambda h: (0, 0))
    state = pl.BlockSpec((1, HEAD_A, HEAD_A, n), lambda h: (h, 0, 0, 0))
    return pl.pallas_call(
        _decode_rwkv_kernel,
        grid=(H_A,),
        in_specs=[full(A_COLS), full(A_COLS), state] + [_const_spec(c.shape) for c in rw_consts],
        out_specs=[full(C_A), state],
        out_shape=[jax.ShapeDtypeStruct((n, C_A), F32), jax.ShapeDtypeStruct(wkv_t.shape, F32)],
        scratch_shapes=[pltpu.VMEM((6, C_A, n), F32), pltpu.VMEM((C_A, n), F32),
                        pltpu.VMEM((n, C_A), F32), pltpu.VMEM((n, C_A), F32)],
        compiler_params=pltpu.CompilerParams(dimension_semantics=("arbitrary",), vmem_limit_bytes=VMEM_LIMIT),
        name="decode_rwkv",
    )(pa, shift, wkv_t, *rw_consts)


def _decode_gdn_kernel(qkv_ref, z_ref, ba_ref, cst_ref, gdn_ref, conv_ref, alog_ref, dtb_ref, gnorm_ref,
                       ob_ref, gdno_ref):
    cst = cst_ref[...]
    q, k, vv, beta, glog = _gdn_prep(cst[:, :CONV_CH], cst[:, CONV_CH:2 * CONV_CH], cst[:, 2 * CONV_CH:],
                                     qkv_ref[...], ba_ref[...], conv_ref[...], alog_ref[...], dtb_ref[...])
    eg = jnp.exp(glog)
    k_cols = _to_columns(k)
    ls = lambda h: slice(DK * h, DK * (h + 1))

    def split2(x):
        hi = x.astype(BF16)
        return hi, x - hi.astype(F32)

    rows_a, rows_b = [], []
    for h in range(H_B):
        e_h = eg[:, H_B + h:H_B + h + 1]
        (wh, wl), (gh, gl_) = split2(k[:, ls(h)] * (beta[:, h:h + 1] * e_h)), split2(q[:, ls(h)] * e_h)
        rows_a.append(jnp.concatenate([wh.astype(F32), gh.astype(F32), wl, gl_], axis=0).astype(BF16))
        rows_b.append(jnp.concatenate([wh.astype(F32), gh.astype(F32)], axis=0).astype(BF16))
    gunits = [(s_i, h) for s_i in range(DEC_TILE) for h in range(H_B)]
    gn = range(len(gunits))
    be = [beta[s_i:s_i + 1, h:h + 1] for s_i, h in gunits]
    e = [eg[s_i:s_i + 1, H_B + h:H_B + h + 1] for s_i, h in gunits]
    kc = [k_cols[ls(h), s_i:s_i + 1] for s_i, h in gunits]
    gst = [gdn_ref[s_i, h] for s_i, h in gunits]
    parts = [split2(st) for st in gst]
    ra = [_dot(rows_a[h], parts[i][0].astype(BF16)) + jnp.concatenate(
        [_dot(rows_b[h], parts[i][1].astype(BF16)), jnp.zeros((2 * DEC_TILE, DV), F32)], axis=0)
        for i, (s_i, h) in enumerate(gunits)]
    pick = lambda r, j, s_i: r[DEC_TILE * j + s_i:DEC_TILE * j + s_i + 1]
    ws = [pick(ra[i], 0, s_i) + pick(ra[i], 2, s_i) for i, (s_i, h) in enumerate(gunits)]
    qs = [pick(ra[i], 1, s_i) + pick(ra[i], 3, s_i) for i, (s_i, h) in enumerate(gunits)]
    v_new = [be[i] * vv[s_i:s_i + 1, ls(h)] - ws[i] for i, (s_i, h) in enumerate(gunits)]
    qk = [jnp.sum(q[s_i:s_i + 1, ls(h)] * k[s_i:s_i + 1, ls(h)], axis=-1, keepdims=True) for s_i, h in gunits]
    for i, (s_i, h) in enumerate(gunits):
        gdno_ref[s_i, h] = gst[i] * e[i] + kc[i] * v_new[i]
    o_units = [qs[i] + qk[i] * v_new[i] for i in gn]
    o = jnp.concatenate([jnp.concatenate(o_units[H_B * s_i:H_B * (s_i + 1)], axis=1) for s_i in range(DEC_TILE)],
                        axis=0)
    ob_ref[...] = _head_norm_gate(o, z_ref[...], gnorm_ref[...])


def _decode_gdn(qkv, z, ba, cst, gdn, gdn_consts):
    n = qkv.shape[0]
    row = lambda w: pl.BlockSpec((DEC_TILE, w), lambda i: (i, 0))
    state = pl.BlockSpec((DEC_TILE, H_B, DK, DV), lambda i: (i, 0, 0, 0))
    return pl.pallas_call(
        _decode_gdn_kernel,
        grid=(n // DEC_TILE,),
        in_specs=[row(CONV_CH), row(C_BV), row(LANES), row(3 * CONV_CH), state]
                 + [_const_spec(c.shape) for c in gdn_consts],
        out_specs=[row(C_BV), state],
        out_shape=[jax.ShapeDtypeStruct((n, C_BV), F32), jax.ShapeDtypeStruct(gdn.shape, F32)],
        compiler_params=pltpu.CompilerParams(dimension_semantics=("parallel",), vmem_limit_bytes=VMEM_LIMIT),
        name="decode_gdn",
    )(qkv, z, ba, cst, gdn, *gdn_consts)


def _post_kernel(x_ref, oa_ref, ob_ref, pe_ref, nmix_ref, wg_ref, wba_ref, wbb_ref, wout_ref, nffn_ref,
                 wfg_ref, wfu_ref, wfd_ref, nple_ref, wpg_ref, wpp_ref, nfin_ref, y_ref):
    x = x_ref[...]
    u = _rms(x, nmix_ref[...]).astype(BF16)
    gates = _sigmoid(_dot(u, wg_ref[...]))
    mix = (gates[:, :D_MODEL] * _dot(oa_ref[...].astype(BF16), wba_ref[...])
           + gates[:, D_MODEL:] * _dot(ob_ref[...].astype(BF16), wbb_ref[...]))
    h = x + _dot(mix.astype(BF16), wout_ref[...])
    u2 = _rms(h, nffn_ref[...]).astype(BF16)
    ff = _silu(_dot(u2, wfg_ref[...])) * _dot(u2, wfu_ref[...])
    h = h + _dot(ff.astype(BF16), wfd_ref[...])
    u3 = _rms(h, nple_ref[...]).astype(BF16)
    h = h + _sigmoid(_dot(u3, wpg_ref[...])) * _dot(pe_ref[...].astype(BF16), wpp_ref[...])
    y_ref[...] = _rms(h, nfin_ref[...])


def _post(x, oa, ob, pe, consts):
    n = x.shape[0]
    tm = min(ROW_TILE, n)
    row = lambda w: pl.BlockSpec((tm, w), lambda i: (i, 0))
    return pl.pallas_call(
        _post_kernel,
        grid=(n // tm,),
        in_specs=[row(D_MODEL), row(C_A), row(C_BV), row(PLE_DIM)] + [_const_spec(c.shape) for c in consts],
        out_specs=row(D_MODEL),
        out_shape=jax.ShapeDtypeStruct((n, D_MODEL), F32),
        compiler_params=pltpu.CompilerParams(dimension_semantics=("parallel",), vmem_limit_bytes=VMEM_LIMIT),
        name="post",
    )(x, oa, ob, pe, *consts)


def _chunk_cumsum_matrix(tile):
    i = jnp.arange(tile)
    same = (i[:, None] // CHUNK) == (i[None, :] // CHUNK)
    return (same & (i[None, :] <= i[:, None])).astype(BF16)


def kernel(x_prompt, x_sample, p_prompt, p_sample, state_shift, state_wkv, state_conv, state_gdn, norm_mix, w_in, mu_shift, rw_w0, rw_w2, rw_a0, rw_a2, rw_g2, rw_kk, rw_ka, rw_rk, rw_ln_w, rw_ln_b, gdn_conv, gdn_a_log, gdn_dt_bias, gdn_norm, w_branch_a, w_branch_b, w_out, norm_ffn, w_ffn_gate, w_ffn_up, w_ffn_down, norm_ple, w_ple_gate, w_ple_proj, norm_final):
    bsz, seq, _ = x_prompt.shape
    nd = x_sample.shape[0]
    row = lambda p: p.reshape(1, -1)

    w_in0 = w_in[0]
    b0 = A_COLS
    wa = w_in0[:, :b0].astype(BF16)
    wq = w_in0[:, b0:b0 + CONV_CH].astype(BF16)
    wz = w_in0[:, b0 + CONV_CH:b0 + CONV_CH + C_BV].astype(BF16)
    wb = jnp.pad(w_in0[:, b0 + CONV_CH + C_BV:b0 + CONV_CH + C_BV + 2 * H_B], ((0, 0), (0, LANES - 2 * H_B))).astype(BF16)
    wg = w_in0[:, b0 + CONV_CH + C_BV + 2 * H_B:].astype(BF16)
    w2p = jnp.concatenate([rw_w2[0], jnp.zeros((LORA_A, C_A), F32)], axis=0).astype(BF16)
    a2p = jnp.concatenate([jnp.zeros((LORA_W, C_A), F32), rw_a2[0]], axis=0).astype(BF16)
    ch = jnp.arange(LANES) // HEAD_A
    hb = (ch[:, None] == ch[None, :]).astype(BF16)
    tri = _chunk_cumsum_matrix(T_TILE)
    tri_g = _chunk_cumsum_matrix(G_TILE)
    alog = jnp.pad(gdn_a_log[0], (H_B, LANES - 2 * H_B)).reshape(1, LANES)
    dtb = jnp.pad(gdn_dt_bias[0], (H_B, LANES - 2 * H_B)).reshape(1, LANES)
    rw_consts = (row(mu_shift[0]), row(rw_w0[0]), w2p, row(rw_a0[0]), a2p, rw_g2[0].astype(BF16), row(rw_kk[0]), row(rw_ka[0]),
                 row(rw_rk[0]), row(rw_ln_w[0]), row(rw_ln_b[0]), hb)
    gdn_consts = (gdn_conv[0], alog, dtb, row(gdn_norm[0]))
    post_consts = (row(norm_mix[0]), wg, w_branch_a[0].astype(BF16), w_branch_b[0].astype(BF16),
                   w_out[0].astype(BF16), row(norm_ffn[0]), w_ffn_gate[0].astype(BF16), w_ffn_up[0].astype(BF16),
                   w_ffn_down[0].astype(BF16), row(norm_ple[0]), w_ple_gate[0].astype(BF16),
                   w_ple_proj[0].astype(BF16), row(norm_final))

    xp = x_prompt.reshape(bsz * seq, D_MODEL)
    oa, wkv_pairs, shift_tail = _rwkv_prompt(x_prompt, row(norm_mix[0]), wa, *rw_consts, tri)
    ob, gdn_p, conv_tail = _gdn_prompt(x_prompt, row(norm_mix[0]), wq, wz, wb, *gdn_consts, tri_g)
    y_prompt = _post(xp, oa.reshape(bsz * seq, C_A), ob.reshape(bsz * seq, C_BV),
                     p_prompt[0].reshape(bsz * seq, PLE_DIM), post_consts).reshape(bsz, seq, D_MODEL)
    wkv_p = jnp.stack([wkv_pairs[:, :, :HEAD_A, :HEAD_A], wkv_pairs[:, :, HEAD_A:, HEAD_A:]], axis=2)
    wkv_p = wkv_p.reshape(bsz, H_A, HEAD_A, HEAD_A)
    shift_p = shift_tail[:, HALO - 1:, :]
    conv_p = conv_tail[:, HALO - (CONV_W - 1):, :]

    xs = x_sample.reshape(nd, D_MODEL)
    pa_s, qkv_s, z_s, ba_s = _inproj(xs, row(norm_mix[0]), wa, wq, wz, wb)
    cst = state_conv[0].reshape(nd, (CONV_W - 1) * CONV_CH)
    oa_s, wkv_t = _decode_rwkv(pa_s, state_shift[0].reshape(nd, A_COLS), jnp.transpose(state_wkv[0], (1, 2, 3, 0)),
                               rw_consts)
    wkv_s = jnp.transpose(wkv_t, (3, 0, 1, 2))
    ob_s, gdn_s = _decode_gdn(qkv_s, z_s, ba_s, cst, state_gdn[0], gdn_consts)
    y_sample = _post(xs, oa_s, ob_s, p_sample[0].reshape(nd, PLE_DIM), post_consts).reshape(nd, 1, D_MODEL)
    conv_s = jnp.concatenate([cst[:, CONV_CH:], qkv_s], axis=1).reshape(nd, CONV_W - 1, CONV_CH)

    return (y_prompt, y_sample, shift_p[None], wkv_p[None], conv_p[None], gdn_p[None],
            pa_s.reshape(1, nd, 1, A_COLS), wkv_s[None], conv_s[None], gdn_s[None])
```

```python
import functools

import jax
import jax.numpy as jnp
from jax import lax
from jax.experimental import pallas as pl
from jax.experimental.pallas import tpu as pltpu

F32 = jnp.float32
BF16 = jnp.bfloat16

D_MODEL = 1024
HEAD_A = 64
C_A = 512
H_A = 8
LORA_W = 64
LORA_A = 64
LORA_G = 128
A_COLS = 3 * C_A + LORA_W + LORA_A + LORA_G
DK = 128
DV = 128
H_B = 4
C_BK = 512
C_BV = 512
CONV_W = 4
CONV_CH = 2 * C_BK + C_BV
D_FF = 2816
PLE_DIM = 256
NORM_EPS = 1e-6
GN_EPS = 64e-5
L2_EPS = 1e-6

LANES = 128
CHUNK = 64
PAIR = 2 * CHUNK
T_TILE = 256
G_TILE = 512
ROW_TILE = 256
DEC_TILE = 8
HALO = 8
VMEM_LIMIT = 56 * 1024 * 1024
SUM_TERMS = 2
CUMSUM_TERMS = 3
INV_TERMS = 1
CHAIN_TERMS = 1
PAIR_GROUP = 4


def _dot(a, b):
    return jnp.dot(a, b, preferred_element_type=F32)


def _sigmoid(x):
    return 1.0 / (1.0 + jnp.exp(-x))


def _silu(x):
    return x * _sigmoid(x)


def _softplus(x):
    return jnp.maximum(x, 0.0) + jnp.log(1.0 + jnp.exp(-jnp.abs(x)))


def _rms(x, gain):
    return x * lax.rsqrt(jnp.mean(x * x, axis=-1, keepdims=True) + NORM_EPS) * gain


def _pair_masks():
    ri = lax.broadcasted_iota(jnp.int32, (PAIR, PAIR), 0)
    ci = lax.broadcasted_iota(jnp.int32, (PAIR, PAIR), 1)
    same = (ri < CHUNK) == (ci < CHUNK)
    strict = same & (ci < ri)
    incl = same & (ci <= ri)
    eye = (ri == ci).astype(F32)
    return strict, incl, eye


_NN = (((1,), (0,)), ((), ()))
_NT = (((1,), (1,)), ((), ()))
_TN = (((0,), (0,)), ((), ()))


def _split(x, terms):
    if isinstance(x, (list, tuple)):
        return list(x)
    if x.dtype == BF16:
        return [x]
    parts = []
    for i in range(terms):
        h = x.astype(BF16)
        parts.append(h)
        if i + 1 < terms:
            x = x - h.astype(F32)
    return parts


def _mm(a, b, dims=_NN, na=1, nb=1):
    pa, pb = _split(a, na), _split(b, nb)
    acc = None
    for i, ai in enumerate(pa):
        for j, bj in enumerate(pb):
            if i + j < max(len(pa), len(pb)):
                d = lax.dot_general(ai, bj, dims, preferred_element_type=F32)
                acc = d if acc is None else acc + d
    return acc


def _neumann_inverse_many(lmats, eye, terms):
    operand = (lambda x: x.astype(BF16)) if terms == 1 else (lambda x: x.astype(F32))
    ts = [eye + l for l in lmats]
    ps = [operand(_mm(l, l, _NN, terms, terms)) for l in lmats]
    n = 4
    while n < CHUNK:
        both = [_mm(p, jnp.concatenate([p, operand(t)], axis=1), _NN, terms, terms) for p, t in zip(ps, ts)]
        ps = [operand(b[:, :PAIR]) for b in both]
        ts = [t + b[:, PAIR:] for t, b in zip(ts, both)]
        n *= 2
    return [t + _mm(p, t, _NN, terms, terms) for p, t in zip(ps, ts)]


def _inproj_kernel(x_ref, g_ref, wa_ref, wq_ref, wz_ref, wb_ref, pa_ref, qkv_ref, z_ref, ba_ref):
    u = _rms(x_ref[...], g_ref[...]).astype(BF16)
    pa_ref[...] = _dot(u, wa_ref[...])
    qkv_ref[...] = _dot(u, wq_ref[...])
    z_ref[...] = _dot(u, wz_ref[...])
    ba_ref[...] = _dot(u, wb_ref[...])


def _const_spec(shape):
    nd = len(shape)
    return pl.BlockSpec(shape, lambda *_: (0,) * nd, pipeline_mode=pl.Buffered(1))


def _inproj(x, gain, wa, wq, wz, wb):
    n = x.shape[0]
    tm = min(ROW_TILE, n)
    row = lambda w: pl.BlockSpec((tm, w), lambda i: (i, 0))
    return pl.pallas_call(
        _inproj_kernel,
        grid=(n // tm,),
        in_specs=[row(D_MODEL), _const_spec(gain.shape), _const_spec(wa.shape), _const_spec(wq.shape),
                  _const_spec(wz.shape), _const_spec(wb.shape)],
        out_specs=[row(A_COLS), row(CONV_CH), row(C_BV), row(LANES)],
        out_shape=[jax.ShapeDtypeStruct((n, A_COLS), F32), jax.ShapeDtypeStruct((n, CONV_CH), F32),
                   jax.ShapeDtypeStruct((n, C_BV), F32), jax.ShapeDtypeStruct((n, LANES), F32)],
        compiler_params=pltpu.CompilerParams(dimension_semantics=("parallel",), vmem_limit_bytes=VMEM_LIMIT),
        name="inproj",
    )(x, gain, wa, wq, wz, wb)


def _rwkv_prep(pa, prev, mu, w0, w2p, a0, a2p, g2, kkw, ka, rk, hb):
    xa = pa + (prev - pa) * mu
    r = xa[:, :C_A]
    k = xa[:, C_A:2 * C_A]
    v = xa[:, 2 * C_A:3 * C_A]
    xwa = xa[:, 3 * C_A:3 * C_A + LORA_W + LORA_A]
    xg = xa[:, 3 * C_A + LORA_W + LORA_A:]
    w_log = -_softplus(-(w0 + _mm(jnp.tanh(xwa), w2p))) - 0.5
    logw = -jnp.exp(w_log)
    a = _sigmoid(a0 + _mm(xwa, a2p))
    g = _mm(_sigmoid(xg), g2)
    kx = k * kkw
    kk = kx * lax.rsqrt(_head_sum(kx * kx, hb) + L2_EPS)
    k2 = k * (1.0 + (a - 1.0) * ka)
    bonus = _head_sum(r * k2 * rk, hb) * v
    return r, k2, v, logw, a, g, kk, bonus


def _head_sum(x, hb):
    groups = [_mm(x[:, LANES * j:LANES * (j + 1)], hb, _NN, SUM_TERMS) for j in range(x.shape[1] // LANES)]
    return jnp.concatenate(groups, axis=1)


def _group_norm_gate(y, bonus, g, lnw, lnb, hb):
    mean = _head_sum(y, hb) * (1.0 / HEAD_A)
    d = y - mean
    var = _head_sum(d * d, hb) * (1.0 / HEAD_A)
    yn = d * lax.rsqrt(var + GN_EPS) * lnw + lnb
    return (yn + bonus) * g


def _rwkv_prompt_kernel(tiles_per_seq, x_ref, nmix_ref, wa_ref,
                        mu_ref, w0_ref, w2_ref, a0_ref, a2_ref, g2_ref, kkw_ref, ka_ref, rk_ref,
                        lnw_ref, lnb_ref, hb_ref, tri_ref,
                        oa_ref, wkv_ref, shift_ref,
                        pa_ref, ext_ref, s_ref, at_ref, bt_ref, kt_ref, rt_ref, v_ref, bh_ref, kh_ref, y_ref):
    s_id = pl.program_id(0)

    @pl.when(s_id == 0)
    def _():
        pa_ref[...] = jnp.zeros_like(pa_ref)

    @pl.when(jnp.maximum(s_id - 1, 0) % tiles_per_seq == 0)
    def _():
        ext_ref[0:HALO, :] = jnp.zeros((HALO, A_COLS), F32)
        s_ref[...] = jnp.zeros_like(s_ref)

    pa = pa_ref[...]
    ext_ref[HALO:, :] = pa
    prev = ext_ref[pl.ds(HALO - 1, T_TILE), :]
    ext_ref[0:HALO, :] = pa[T_TILE - HALO:, :]
    shift_ref[0] = pa[T_TILE - HALO:, :]

    pa_ref[...] = _dot(_rms(x_ref[0], nmix_ref[...]).astype(BF16), wa_ref[...])

    hb = hb_ref[...]
    r, k2, v, logw, a, g, kk, bonus = _rwkv_prep(
        pa, prev, mu_ref[...], w0_ref[...], w2_ref[...], a0_ref[...], a2_ref[...], g2_ref[...],
        kkw_ref[...], ka_ref[...], rk_ref[...], hb)

    nchunk = T_TILE // CHUNK
    cs = _mm(tri_ref[...], logw, _NN, 1, CUMSUM_TERMS)
    ends = [cs[CHUNK * (c + 1) - 1:CHUNK * (c + 1)] for c in range(nchunk)]
    tot = jnp.concatenate([jnp.broadcast_to(e, (CHUNK, C_A)) for e in ends], axis=0)
    dinv = jnp.exp(-cs)
    dend = jnp.exp(tot - cs)
    b_in = kk * a
    at_ref[...] = (-kk * jnp.exp(cs - logw)).astype(BF16)
    bt_ref[...] = (b_in * dinv).astype(BF16)
    kt_ref[...] = (k2 * dinv).astype(BF16)
    rt_ref[...] = (r * jnp.exp(cs)).astype(BF16)
    v_ref[...] = v.astype(BF16)
    bh_ref[...] = (b_in * dend).astype(BF16)
    kh_ref[...] = (k2 * dend).astype(BF16)
    gl = [jnp.exp(e) for e in ends]

    strict, incl, eye = _pair_masks()
    low = lax.broadcasted_iota(jnp.int32, (CHUNK, LANES), 1) < HEAD_A
    zero = jnp.zeros((CHUNK, LANES), BF16)

    def stack(x):
        return jnp.concatenate([jnp.where(low, x, zero), jnp.where(low, zero, x)], axis=0)

    npair = C_A // LANES
    for p0 in range(0, npair, PAIR_GROUP):
        idx = [(p, c) for p in range(p0, p0 + PAIR_GROUP) for c in range(nchunk)]
        lanes = lambda p: slice(LANES * p, LANES * (p + 1))
        rows = lambda c: slice(CHUNK * c, CHUNK * (c + 1))
        ld = lambda ref: [stack(ref[rows(c), lanes(p)]) for p, c in idx]
        at, bt, kt, rt, vs, bh, kh = (ld(ref) for ref in (at_ref, bt_ref, kt_ref, rt_ref, v_ref, bh_ref, kh_ref))
        n = range(len(idx))
        aa = [_mm(jnp.concatenate([at[i], rt[i]], axis=0), jnp.concatenate([bt[i], kt[i]], axis=0), _NT) for i in n]
        a_ab = [jnp.where(strict, aa[i][:PAIR, :PAIR], 0.0).astype(BF16) for i in n]
        a_ak = [jnp.where(strict, aa[i][:PAIR, PAIR:], 0.0).astype(BF16) for i in n]
        a_rb = [jnp.where(incl, aa[i][PAIR:, :PAIR], 0.0).astype(BF16) for i in n]
        a_rk = [jnp.where(incl, aa[i][PAIR:, PAIR:], 0.0).astype(BF16) for i in n]
        tinv = [t.astype(BF16) for t in _neumann_inverse_many(a_ab, eye, INV_TERMS)]
        akv = [_mm(a_ak[i], vs[i]).astype(BF16) for i in n]
        gu = [_mm(tinv[i], jnp.concatenate([at[i], akv[i]], axis=1)).astype(BF16) for i in n]
        ry = [_mm(a_rb[i], gu[i]) for i in n]
        rkv = [_mm(a_rk[i], vs[i]) for i in n]
        pq = [_mm(gu[i], bh[i], _TN) for i in n]
        vk = [_mm(vs[i], kh[i], _TN) for i in n]
        rp = [(rt[i].astype(F32) + ry[i][:, :PAIR]).astype(BF16) for i in n]
        yc = [ry[i][:, PAIR:] + rkv[i] for i in n]
        pm = [(eye * gl[idx[i][1]][:, lanes(idx[i][0])] + pq[i][:PAIR]).astype(BF16) for i in n]
        qm = [pq[i][PAIR:] + vk[i] for i in n]
        s = {p: s_ref[p] for p in range(p0, p0 + PAIR_GROUP)}
        for c in range(nchunk):
            for p in range(p0, p0 + PAIR_GROUP):
                i = idx.index((p, c))
                y = _mm(rp[i], s[p], _NT) + yc[i]
                y_ref[rows(c), lanes(p)] = y[:CHUNK] + y[CHUNK:]
                s[p] = _mm(s[p], pm[i], _NN, CHAIN_TERMS, CHAIN_TERMS) + qm[i]
        for p in range(p0, p0 + PAIR_GROUP):
            s_ref[p] = s[p]

    oa_ref[0] = _group_norm_gate(y_ref[...], bonus, g, lnw_ref[...], lnb_ref[...], hb)
    wkv_ref[0] = s_ref[...]


def _tile_maps(bsz, seq, tile):
    tps = seq // tile
    last = bsz * tps - 1
    cur = lambda s: jnp.minimum(s, last)
    prv = lambda s: jnp.maximum(s - 1, 0)
    load = lambda s: (cur(s) // tps, cur(s) % tps, 0)
    emit = lambda s: (prv(s) // tps, prv(s) % tps, 0)
    per_seq3 = lambda s: (prv(s) // tps, 0, 0)
    per_seq4 = lambda s: (prv(s) // tps, 0, 0, 0)
    return tps, last + 2, load, emit, per_seq3, per_seq4


def _rwkv_prompt(x, nmix, wa, mu, w0, w2p, a0, a2p, g2, kkw, ka, rk, lnw, lnb, hb, tri):
    bsz, seq, _ = x.shape
    consts = (nmix, wa, mu, w0, w2p, a0, a2p, g2, kkw, ka, rk, lnw, lnb, hb, tri)
    npair = C_A // LANES
    tps, steps, load, emit, per_seq3, per_seq4 = _tile_maps(bsz, seq, T_TILE)
    tile = lambda dtype: pltpu.VMEM((T_TILE, C_A), dtype)
    return pl.pallas_call(
        functools.partial(_rwkv_prompt_kernel, tps),
        grid=(steps,),
        in_specs=[pl.BlockSpec((1, T_TILE, D_MODEL), load)] + [_const_spec(c.shape) for c in consts],
        out_specs=[pl.BlockSpec((1, T_TILE, C_A), emit),
                   pl.BlockSpec((1, npair, LANES, LANES), per_seq4),
                   pl.BlockSpec((1, HALO, A_COLS), per_seq3)],
        out_shape=[jax.ShapeDtypeStruct((bsz, seq, C_A), F32),
                   jax.ShapeDtypeStruct((bsz, npair, LANES, LANES), F32),
                   jax.ShapeDtypeStruct((bsz, HALO, A_COLS), F32)],
        scratch_shapes=[pltpu.VMEM((T_TILE, A_COLS), F32),
                        pltpu.VMEM((T_TILE + HALO, A_COLS), F32), pltpu.VMEM((npair, LANES, LANES), F32)]
                       + [tile(BF16) for _ in range(7)] + [tile(F32)],
        compiler_params=pltpu.CompilerParams(dimension_semantics=("arbitrary",), vmem_limit_bytes=VMEM_LIMIT),
        name="rwkv_prompt",
    )(x, *consts)


def _gdn_prep(x0, x1, x2, x3, ba, conv, alog, dtb):
    c = _silu(x0 * conv[0:1] + x1 * conv[1:2] + x2 * conv[2:3] + x3 * conv[3:4])
    qs, ks = [], []
    for h in range(H_B):
        qh = c[:, DK * h:DK * (h + 1)]
        kh = c[:, C_BK + DK * h:C_BK + DK * (h + 1)]
        qs.append(qh * lax.rsqrt(jnp.sum(qh * qh, axis=-1, keepdims=True) + L2_EPS) * (DK ** -0.5))
        ks.append(kh * lax.rsqrt(jnp.sum(kh * kh, axis=-1, keepdims=True) + L2_EPS))
    q = jnp.concatenate(qs, axis=1)
    k = jnp.concatenate(ks, axis=1)
    v = c[:, 2 * C_BK:]
    beta = _sigmoid(ba)
    glog = -jnp.exp(alog) * _softplus(ba + dtb)
    return q, k, v, beta, glog


def _head_norm_gate(o, z, gnorm):
    outs = []
    for h in range(H_B):
        oh = o[:, DV * h:DV * (h + 1)]
        zh = z[:, DV * h:DV * (h + 1)]
        oh = oh * lax.rsqrt(jnp.mean(oh * oh, axis=-1, keepdims=True) + NORM_EPS) * gnorm
        outs.append(oh * _silu(zh))
    return jnp.concatenate(outs, axis=1)


def _gdn_prompt_kernel(tiles_per_seq, x_ref, nmix_ref, wq_ref, wz_ref, wb_ref,
                       conv_ref, alog_ref, dtb_ref, gnorm_ref, tri_ref,
                       ob_ref, gdn_ref, convo_ref,
                       qkvn_ref, zn_ref, ban_ref, ext_ref, s_ref, q_ref, k_ref, v_ref, gc_ref, be_ref, o_ref):
    s_id = pl.program_id(0)

    @pl.when(s_id == 0)
    def _():
        qkvn_ref[...] = jnp.zeros_like(qkvn_ref)
        zn_ref[...] = jnp.zeros_like(zn_ref)
        ban_ref[...] = jnp.zeros_like(ban_ref)

    @pl.when(jnp.maximum(s_id - 1, 0) % tiles_per_seq == 0)
    def _():
        ext_ref[0:HALO, :] = jnp.zeros((HALO, CONV_CH), F32)
        s_ref[...] = jnp.zeros_like(s_ref)

    x3 = qkvn_ref[...]
    z = zn_ref[...]
    ba = ban_ref[...]
    ext_ref[HALO:, :] = x3
    x0 = ext_ref[pl.ds(HALO - 3, G_TILE), :]
    x1 = ext_ref[pl.ds(HALO - 2, G_TILE), :]
    x2 = ext_ref[pl.ds(HALO - 1, G_TILE), :]
    ext_ref[0:HALO, :] = x3[G_TILE - HALO:, :]
    convo_ref[0] = x3[G_TILE - HALO:, :]

    u = _rms(x_ref[0], nmix_ref[...]).astype(BF16)
    qkvn_ref[...] = _dot(u, wq_ref[...])
    zn_ref[...] = _dot(u, wz_ref[...])
    ban_ref[...] = _dot(u, wb_ref[...])

    q, k, v, beta, glog = _gdn_prep(x0, x1, x2, x3, ba, conv_ref[...], alog_ref[...], dtb_ref[...])
    q_ref[...] = q
    k_ref[...] = k
    v_ref[...] = v
    be_ref[...] = beta
    gc_ref[...] = _mm(tri_ref[...], glog, _NN, 1, CUMSUM_TERMS)

    strict, incl, eye = _pair_masks()

    nchunk = G_TILE // CHUNK
    idx = [(pr, c) for pr in range(H_B // 2) for c in range(nchunk)]
    n = range(len(idx))
    rows = lambda c: slice(CHUNK * c, CHUNK * (c + 1))
    cat = lambda ref: [jnp.concatenate([ref[rows(c), DK * h:DK * (h + 1)] for h in (2 * pr, 2 * pr + 1)], axis=0)
                       for pr, c in idx]
    col = lambda ref, off: [jnp.concatenate([ref[rows(c), off + h:off + h + 1] for h in (2 * pr, 2 * pr + 1)], axis=0)
                            for pr, c in idx]
    qs, ks, vs = cat(q_ref), cat(k_ref), cat(v_ref)
    beta_c = col(be_ref, 0)
    gc = col(gc_ref, H_B)
    gt = [jnp.concatenate([jnp.broadcast_to(g_[CHUNK * (i + 1) - 1:CHUNK * (i + 1)], (CHUNK, 1)) for i in range(2)],
                          axis=0) for g_ in gc]
    decay = []
    for g_ in gc:
        gc_full = jnp.broadcast_to(g_, (PAIR, PAIR))
        diff = gc_full - gc_full.T
        decay.append(jnp.exp(diff))
    kb = [ks[i] * beta_c[i] for i in n]
    vb = [vs[i] * beta_c[i] for i in n]
    kq = [_mm(jnp.concatenate([kb[i], qs[i]], axis=0), ks[i], _NT) for i in n]
    lneg = [jnp.where(strict, -(kq[i][:PAIR] * decay[i]), 0.0).astype(BF16) for i in n]
    qk = [jnp.where(incl, kq[i][PAIR:] * decay[i], 0.0).astype(BF16) for i in n]
    tinv = [t.astype(BF16) for t in _neumann_inverse_many(lneg, eye, INV_TERMS)]
    eg = [jnp.exp(g_) for g_ in gc]
    uw = [_mm(tinv[i], jnp.concatenate([vb[i], kb[i] * eg[i]], axis=1)).astype(BF16) for i in n]
    ow = [_mm(qk[i], uw[i]) for i in n]
    oc = [ow[i][:, :DV] for i in n]
    rq = [(qs[i] * eg[i] - ow[i][:, DV:]).astype(BF16) for i in n]
    kd = [(ks[i] * jnp.exp(gt[i] - gc[i])).astype(BF16) for i in n]
    half = lambda j: slice(CHUNK * j, CHUNK * (j + 1))
    pq = [[_mm(kd[i][half(j)], uw[i][half(j)], _TN) for j in range(2)] for i in n]
    pm = [[(eye * jnp.exp(gt[i][CHUNK * j:CHUNK * j + 1]) - pq[i][j][:, DV:]).astype(BF16) for j in range(2)]
          for i in n]
    s = [s_ref[h] for h in range(H_B)]
    for c in range(nchunk):
        for h in range(H_B):
            i, j = idx.index((h // 2, c)), h % 2
            o_ref[rows(c), DV * h:DV * (h + 1)] = _mm(rq[i][half(j)], s[h]) + oc[i][half(j)]
            s[h] = _mm(pm[i][j], s[h], _NN, CHAIN_TERMS, CHAIN_TERMS) + pq[i][j][:, :DV]
    for h in range(H_B):
        s_ref[h] = s[h]

    ob_ref[0] = _head_norm_gate(o_ref[...], z, gnorm_ref[...])
    gdn_ref[0] = s_ref[...]


def _gdn_prompt(x, nmix, wq, wz, wb, conv, alog, dtb, gnorm, tri):
    bsz, seq, _ = x.shape
    consts = (nmix, wq, wz, wb, conv, alog, dtb, gnorm, tri)
    tps, steps, load, emit, per_seq3, per_seq4 = _tile_maps(bsz, seq, G_TILE)
    return pl.pallas_call(
        functools.partial(_gdn_prompt_kernel, tps),
        grid=(steps,),
        in_specs=[pl.BlockSpec((1, G_TILE, D_MODEL), load)] + [_const_spec(c.shape) for c in consts],
        out_specs=[pl.BlockSpec((1, G_TILE, C_BV), emit), pl.BlockSpec((1, H_B, DK, DV), per_seq4),
                   pl.BlockSpec((1, HALO, CONV_CH), per_seq3)],
        out_shape=[jax.ShapeDtypeStruct((bsz, seq, C_BV), F32), jax.ShapeDtypeStruct((bsz, H_B, DK, DV), F32),
                   jax.ShapeDtypeStruct((bsz, HALO, CONV_CH), F32)],
        scratch_shapes=[pltpu.VMEM((G_TILE, CONV_CH), F32), pltpu.VMEM((G_TILE, C_BV), F32),
                        pltpu.VMEM((G_TILE, LANES), F32),
                        pltpu.VMEM((G_TILE + HALO, CONV_CH), F32), pltpu.VMEM((H_B, DK, DV), F32),
                        pltpu.VMEM((G_TILE, C_BK), F32), pltpu.VMEM((G_TILE, C_BK), F32),
                        pltpu.VMEM((G_TILE, C_BV), F32), pltpu.VMEM((G_TILE, LANES), F32),
                        pltpu.VMEM((G_TILE, LANES), F32), pltpu.VMEM((G_TILE, C_BV), F32)],
        compiler_params=pltpu.CompilerParams(dimension_semantics=("arbitrary",), vmem_limit_bytes=VMEM_LIMIT),
        name="gdn_prompt",
    )(x, *consts)


def _to_columns(x):
    pad = jnp.zeros((LANES - DEC_TILE, x.shape[1]), F32)
    return jnp.concatenate([x, pad], axis=0).T


def _decode_rwkv_kernel(pa_ref, shift_ref, wkv_ref,
                        mu_ref, w0_ref, w2_ref, a0_ref, a2_ref, g2_ref, kkw_ref, ka_ref, rk_ref, lnw_ref, lnb_ref,
                        hb_ref, oa_ref, wkvo_ref, tr_ref, yt_ref, g_ref, bonus_ref):
    h = pl.program_id(0)

    @pl.when(h == 0)
    def _():
        r, k2, v, logw, a, g, kk, bonus = _rwkv_prep(
            pa_ref[...], shift_ref[...], mu_ref[...], w0_ref[...], w2_ref[...], a0_ref[...], a2_ref[...],
            g2_ref[...], kkw_ref[...], ka_ref[...], rk_ref[...], hb_ref[...])
        for i, x in enumerate((-kk, jnp.exp(logw), kk * a, k2, r, v)):
            tr_ref[i] = x.T
        g_ref[...] = g
        bonus_ref[...] = bonus

    base = pl.multiple_of(h * HEAD_A, HEAD_A)
    hs = pl.ds(base, HEAD_A)
    a_t, w_t, b_t, k_t, r_t = (tr_ref[i, hs, :] for i in range(5))

    def value_row(vi, carry):
        st = wkv_ref[0, vi]
        sa = jnp.sum(st * a_t, axis=0, keepdims=True)
        st = st * w_t + sa * b_t + tr_ref[5, pl.ds(base + vi, 1), :] * k_t
        wkvo_ref[0, vi] = st
        yt_ref[pl.ds(base + vi, 1), :] = jnp.sum(st * r_t, axis=0, keepdims=True)
        return carry

    lax.fori_loop(0, HEAD_A, value_row, 0, unroll=8)

    @pl.when(h == H_A - 1)
    def _():
        oa_ref[...] = _group_norm_gate(yt_ref[...].T, bonus_ref[...], g_ref[...], lnw_ref[...], lnb_ref[...],
                                       hb_ref[...])


def _decode_rwkv(pa, shift, wkv_t, rw_consts):
    n = pa.shape[0]
    full = lambda w: pl.BlockSpec((n, w), lambda h: (0, 0))
    state = pl.BlockSpec((1, HEAD_A, HEAD_A, n), lambda h: (h, 0, 0, 0))
    return pl.pallas_call(
        _decode_rwkv_kernel,
        grid=(H_A,),
        in_specs=[full(A_COLS), full(A_COLS), state] + [_const_spec(c.shape) for c in rw_consts],
        out_specs=[full(C_A), state],
        out_shape=[jax.ShapeDtypeStruct((n, C_A), F32), jax.ShapeDtypeStruct(wkv_t.shape, F32)],
        scratch_shapes=[pltpu.VMEM((6, C_A, n), F32), pltpu.VMEM((C_A, n), F32),
                        pltpu.VMEM((n, C_A), F32), pltpu.VMEM((n, C_A), F32)],
        compiler_params=pltpu.CompilerParams(dimension_semantics=("arbitrary",), vmem_limit_bytes=VMEM_LIMIT),
        name="decode_rwkv",
    )(pa, shift, wkv_t, *rw_consts)


def _decode_gdn_kernel(qkv_ref, z_ref, ba_ref, cst_ref, gdn_ref, conv_ref, alog_ref, dtb_ref, gnorm_ref,
                       ob_ref, gdno_ref):
    cst = cst_ref[...]
    q, k, vv, beta, glog = _gdn_prep(cst[:, :CONV_CH], cst[:, CONV_CH:2 * CONV_CH], cst[:, 2 * CONV_CH:],
                                     qkv_ref[...], ba_ref[...], conv_ref[...], alog_ref[...], dtb_ref[...])
    eg = jnp.exp(glog)
    k_cols = _to_columns(k)
    ls = lambda h: slice(DK * h, DK * (h + 1))

    def split2(x):
        hi = x.astype(BF16)
        return hi, x - hi.astype(F32)

    rows_a, rows_b = [], []
    for h in range(H_B):
        e_h = eg[:, H_B + h:H_B + h + 1]
        (wh, wl), (gh, gl_) = split2(k[:, ls(h)] * (beta[:, h:h + 1] * e_h)), split2(q[:, ls(h)] * e_h)
        rows_a.append(jnp.concatenate([wh.astype(F32), gh.astype(F32), wl, gl_], axis=0).astype(BF16))
        rows_b.append(jnp.concatenate([wh.astype(F32), gh.astype(F32)], axis=0).astype(BF16))
    gunits = [(s_i, h) for s_i in range(DEC_TILE) for h in range(H_B)]
    gn = range(len(gunits))
    be = [beta[s_i:s_i + 1, h:h + 1] for s_i, h in gunits]
    e = [eg[s_i:s_i + 1, H_B + h:H_B + h + 1] for s_i, h in gunits]
    kc = [k_cols[ls(h), s_i:s_i + 1] for s_i, h in gunits]
    gst = [gdn_ref[s_i, h] for s_i, h in gunits]
    parts = [split2(st) for st in gst]
    ra = [_dot(rows_a[h], parts[i][0].astype(BF16)) + jnp.concatenate(
        [_dot(rows_b[h], parts[i][1].astype(BF16)), jnp.zeros((2 * DEC_TILE, DV), F32)], axis=0)
        for i, (s_i, h) in enumerate(gunits)]
    pick = lambda r, j, s_i: r[DEC_TILE * j + s_i:DEC_TILE * j + s_i + 1]
    ws = [pick(ra[i], 0, s_i) + pick(ra[i], 2, s_i) for i, (s_i, h) in enumerate(gunits)]
    qs = [pick(ra[i], 1, s_i) + pick(ra[i], 3, s_i) for i, (s_i, h) in enumerate(gunits)]
    v_new = [be[i] * vv[s_i:s_i + 1, ls(h)] - ws[i] for i, (s_i, h) in enumerate(gunits)]
    qk = [jnp.sum(q[s_i:s_i + 1, ls(h)] * k[s_i:s_i + 1, ls(h)], axis=-1, keepdims=True) for s_i, h in gunits]
    for i, (s_i, h) in enumerate(gunits):
        gdno_ref[s_i, h] = gst[i] * e[i] + kc[i] * v_new[i]
    o_units = [qs[i] + qk[i] * v_new[i] for i in gn]
    o = jnp.concatenate([jnp.concatenate(o_units[H_B * s_i:H_B * (s_i + 1)], axis=1) for s_i in range(DEC_TILE)],
                        axis=0)
    ob_ref[...] = _head_norm_gate(o, z_ref[...], gnorm_ref[...])


def _decode_gdn(qkv, z, ba, cst, gdn, gdn_consts):
    n = qkv.shape[0]
    row = lambda w: pl.BlockSpec((DEC_TILE, w), lambda i: (i, 0))
    state = pl.BlockSpec((DEC_TILE, H_B, DK, DV), lambda i: (i, 0, 0, 0))
    return pl.pallas_call(
        _decode_gdn_kernel,
        grid=(n // DEC_TILE,),
        in_specs=[row(CONV_CH), row(C_BV), row(LANES), row(3 * CONV_CH), state]
                 + [_const_spec(c.shape) for c in gdn_consts],
        out_specs=[row(C_BV), state],
        out_shape=[jax.ShapeDtypeStruct((n, C_BV), F32), jax.ShapeDtypeStruct(gdn.shape, F32)],
        compiler_params=pltpu.CompilerParams(dimension_semantics=("parallel",), vmem_limit_bytes=VMEM_LIMIT),
        name="decode_gdn",
    )(qkv, z, ba, cst, gdn, *gdn_consts)


def _post_kernel(x_ref, oa_ref, ob_ref, pe_ref, nmix_ref, wg_ref, wba_ref, wbb_ref, wout_ref, nffn_ref,
                 wfg_ref, wfu_ref, wfd_ref, nple_ref, wpg_ref, wpp_ref, nfin_ref, y_ref):
    x = x_ref[...]
    u = _rms(x, nmix_ref[...]).astype(BF16)
    gates = _sigmoid(_dot(u, wg_ref[...]))
    mix = (gates[:, :D_MODEL] * _dot(oa_ref[...].astype(BF16), wba_ref[...])
           + gates[:, D_MODEL:] * _dot(ob_ref[...].astype(BF16), wbb_ref[...]))
    h = x + _dot(mix.astype(BF16), wout_ref[...])
    u2 = _rms(h, nffn_ref[...]).astype(BF16)
    ff = _silu(_dot(u2, wfg_ref[...])) * _dot(u2, wfu_ref[...])
    h = h + _dot(ff.astype(BF16), wfd_ref[...])
    u3 = _rms(h, nple_ref[...]).astype(BF16)
    h = h + _sigmoid(_dot(u3, wpg_ref[...])) * _dot(pe_ref[...].astype(BF16), wpp_ref[...])
    y_ref[...] = _rms(h, nfin_ref[...])


def _post(x, oa, ob, pe, consts):
    n = x.shape[0]
    tm = min(ROW_TILE, n)
    row = lambda w: pl.BlockSpec((tm, w), lambda i: (i, 0))
    return pl.pallas_call(
        _post_kernel,
        grid=(n // tm,),
        in_specs=[row(D_MODEL), row(C_A), row(C_BV), row(PLE_DIM)] + [_const_spec(c.shape) for c in consts],
        out_specs=row(D_MODEL),
        out_shape=jax.ShapeDtypeStruct((n, D_MODEL), F32),
        compiler_params=pltpu.CompilerParams(dimension_semantics=("parallel",), vmem_limit_bytes=VMEM_LIMIT),
        name="post",
    )(x, oa, ob, pe, *consts)


B_COLS0 = A_COLS
Z_COLS0 = B_COLS0 + CONV_CH
BA_COLS0 = Z_COLS0 + C_BV
G_COLS0 = BA_COLS0 + 2 * H_B
W_ROWS = 128


def _split_w_in_kernel(w_ref, wa_ref, wq_ref, wz_ref, wb_ref, wg_ref):
    wa_ref[...] = w_ref[:, :B_COLS0].astype(BF16)
    wq_ref[...] = w_ref[:, B_COLS0:Z_COLS0].astype(BF16)
    wz_ref[...] = w_ref[:, Z_COLS0:BA_COLS0].astype(BF16)
    wb_ref[...] = w_ref[:, BA_COLS0:BA_COLS0 + LANES].astype(BF16)
    wg_ref[...] = w_ref[:, G_COLS0:].astype(BF16)


def _split_w_in(w):
    rows, cols = w.shape
    out = lambda width: pl.BlockSpec((W_ROWS, width), lambda i: (i, 0))
    widths = (A_COLS, CONV_CH, C_BV, LANES, cols - G_COLS0)
    return pl.pallas_call(
        _split_w_in_kernel,
        grid=(rows // W_ROWS,),
        in_specs=[pl.BlockSpec((W_ROWS, cols), lambda i: (i, 0))],
        out_specs=[out(wd) for wd in widths],
        out_shape=[jax.ShapeDtypeStruct((rows, wd), BF16) for wd in widths],
        compiler_params=pltpu.CompilerParams(dimension_semantics=("parallel",), vmem_limit_bytes=VMEM_LIMIT),
        name="split_w_in",
    )(w)


def _chunk_cumsum_matrix(tile):
    i = jnp.arange(tile)
    same = (i[:, None] // CHUNK) == (i[None, :] // CHUNK)
    return (same & (i[None, :] <= i[:, None])).astype(BF16)


def kernel(x_prompt, x_sample, p_prompt, p_sample, state_shift, state_wkv, state_conv, state_gdn, norm_mix, w_in, mu_shift, rw_w0, rw_w2, rw_a0, rw_a2, rw_g2, rw_kk, rw_ka, rw_rk, rw_ln_w, rw_ln_b, gdn_conv, gdn_a_log, gdn_dt_bias, gdn_norm, w_branch_a, w_branch_b, w_out, norm_ffn, w_ffn_gate, w_ffn_up, w_ffn_down, norm_ple, w_ple_gate, w_ple_proj, norm_final):
    bsz, seq, _ = x_prompt.shape
    nd = x_sample.shape[0]
    row = lambda p: p.reshape(1, -1)

    wa, wq, wz, wb, wg = _split_w_in(w_in[0])
    w2p = jnp.concatenate([rw_w2[0], jnp.zeros((LORA_A, C_A), F32)], axis=0).astype(BF16)
    a2p = jnp.concatenate([jnp.zeros((LORA_W, C_A), F32), rw_a2[0]], axis=0).astype(BF16)
    ch = jnp.arange(LANES) // HEAD_A
    hb = (ch[:, None] == ch[None, :]).astype(BF16)
    tri = _chunk_cumsum_matrix(T_TILE)
    tri_g = _chunk_cumsum_matrix(G_TILE)
    alog = jnp.pad(gdn_a_log[0], (H_B, LANES - 2 * H_B)).reshape(1, LANES)
    dtb = jnp.pad(gdn_dt_bias[0], (H_B, LANES - 2 * H_B)).reshape(1, LANES)
    rw_consts = (row(mu_shift[0]), row(rw_w0[0]), w2p, row(rw_a0[0]), a2p, rw_g2[0].astype(BF16), row(rw_kk[0]), row(rw_ka[0]),
                 row(rw_rk[0]), row(rw_ln_w[0]), row(rw_ln_b[0]), hb)
    gdn_consts = (gdn_conv[0], alog, dtb, row(gdn_norm[0]))
    post_consts = (row(norm_mix[0]), wg, w_branch_a[0].astype(BF16), w_branch_b[0].astype(BF16),
                   w_out[0].astype(BF16), row(norm_ffn[0]), w_ffn_gate[0].astype(BF16), w_ffn_up[0].astype(BF16),
                   w_ffn_down[0].astype(BF16), row(norm_ple[0]), w_ple_gate[0].astype(BF16),
                   w_ple_proj[0].astype(BF16), row(norm_final))

    xp = x_prompt.reshape(bsz * seq, D_MODEL)
    oa, wkv_pairs, shift_tail = _rwkv_prompt(x_prompt, row(norm_mix[0]), wa, *rw_consts, tri)
    ob, gdn_p, conv_tail = _gdn_prompt(x_prompt, row(norm_mix[0]), wq, wz, wb, *gdn_consts, tri_g)
    y_prompt = _post(xp, oa.reshape(bsz * seq, C_A), ob.reshape(bsz * seq, C_BV),
                     p_prompt[0].reshape(bsz * seq, PLE_DIM), post_consts).reshape(bsz, seq, D_MODEL)
    wkv_p = jnp.stack([wkv_pairs[:, :, :HEAD_A, :HEAD_A], wkv_pairs[:, :, HEAD_A:, HEAD_A:]], axis=2)
    wkv_p = wkv_p.reshape(bsz, H_A, HEAD_A, HEAD_A)
    shift_p = shift_tail[:, HALO - 1:, :]
    conv_p = conv_tail[:, HALO - (CONV_W - 1):, :]

    xs = x_sample.reshape(nd, D_MODEL)
    pa_s, qkv_s, z_s, ba_s = _inproj(xs, row(norm_mix[0]), wa, wq, wz, wb)
    cst = state_conv[0].reshape(nd, (CONV_W - 1) * CONV_CH)
    oa_s, wkv_t = _decode_rwkv(pa_s, state_shift[0].reshape(nd, A_COLS), jnp.transpose(state_wkv[0], (1, 2, 3, 0)),
                               rw_consts)
    wkv_s = jnp.transpose(wkv_t, (3, 0, 1, 2))
    ob_s, gdn_s = _decode_gdn(qkv_s, z_s, ba_s, cst, state_gdn[0], gdn_consts)
    y_sample = _post(xs, oa_s, ob_s, p_sample[0].reshape(nd, PLE_DIM), post_consts).reshape(nd, 1, D_MODEL)
    conv_s = jnp.concatenate([cst[:, CONV_CH:], qkv_s], axis=1).reshape(nd, CONV_W - 1, CONV_CH)

    return (y_prompt, y_sample, shift_p[None], wkv_p[None], conv_p[None], gdn_p[None],
            pa_s.reshape(1, nd, 1, A_COLS), wkv_s[None], conv_s[None], gdn_s[None])
```

```python
import functools

import jax
import jax.numpy as jnp
from jax import lax
from jax.experimental import pallas as pl
from jax.experimental.pallas import tpu as pltpu

F32 = jnp.float32
BF16 = jnp.bfloat16

D_MODEL = 1024
HEAD_A = 64
C_A = 512
H_A = 8
LORA_W = 64
LORA_A = 64
LORA_G = 128
A_COLS = 3 * C_A + LORA_W + LORA_A + LORA_G
DK = 128
DV = 128
H_B = 4
C_BK = 512
C_BV = 512
CONV_W = 4
CONV_CH = 2 * C_BK + C_BV
D_FF = 2816
PLE_DIM = 256
NORM_EPS = 1e-6
GN_EPS = 64e-5
L2_EPS = 1e-6

LANES = 128
CHUNK = 64
PAIR = 2 * CHUNK
T_TILE = 256
G_TILE = 512
ROW_TILE = 256
DEC_TILE = 8
HALO = 8
VMEM_LIMIT = 56 * 1024 * 1024
SUM_TERMS = 2
CUMSUM_TERMS = 3
INV_TERMS = 1
CHAIN_TERMS = 1
PAIR_GROUP = 4


def _dot(a, b):
    return jnp.dot(a, b, preferred_element_type=F32)


def _dot_t(a, bt):
    return lax.dot_general(a, bt, (((1,), (1,)), ((), ())), preferred_element_type=F32)


def _sigmoid(x):
    return 1.0 / (1.0 + jnp.exp(-x))


def _silu(x):
    return x * _sigmoid(x)


def _softplus(x):
    return jnp.maximum(x, 0.0) + jnp.log(1.0 + jnp.exp(-jnp.abs(x)))


def _rms(x, gain):
    return x * lax.rsqrt(jnp.mean(x * x, axis=-1, keepdims=True) + NORM_EPS) * gain


def _pair_masks():
    ri = lax.broadcasted_iota(jnp.int32, (PAIR, PAIR), 0)
    ci = lax.broadcasted_iota(jnp.int32, (PAIR, PAIR), 1)
    same = (ri < CHUNK) == (ci < CHUNK)
    strict = same & (ci < ri)
    incl = same & (ci <= ri)
    eye = (ri == ci).astype(F32)
    return strict, incl, eye


_NN = (((1,), (0,)), ((), ()))
_NT = (((1,), (1,)), ((), ()))
_TN = (((0,), (0,)), ((), ()))


def _split(x, terms):
    if isinstance(x, (list, tuple)):
        return list(x)
    if x.dtype == BF16:
        return [x]
    parts = []
    for i in range(terms):
        h = x.astype(BF16)
        parts.append(h)
        if i + 1 < terms:
            x = x - h.astype(F32)
    return parts


def _mm(a, b, dims=_NN, na=1, nb=1):
    pa, pb = _split(a, na), _split(b, nb)
    acc = None
    for i, ai in enumerate(pa):
        for j, bj in enumerate(pb):
            if i + j < max(len(pa), len(pb)):
                d = lax.dot_general(ai, bj, dims, preferred_element_type=F32)
                acc = d if acc is None else acc + d
    return acc


def _neumann_inverse_many(lmats, eye, terms):
    ts = [eye + l for l in lmats]
    ps = [_split(l, terms) for l in lmats]
    n = 2
    while n < CHUNK:
        ps = [_split(_mm(p, p), terms) for p in ps]
        ts = [t + _mm(t, p, _NN, terms) for t, p in zip(ts, ps)]
        n *= 2
    return ts


def _inproj_kernel(x_ref, g_ref, wa_ref, wq_ref, wz_ref, wb_ref, pa_ref, qkv_ref, z_ref, ba_ref):
    u = _rms(x_ref[...], g_ref[...]).astype(BF16)
    pa_ref[...] = _dot_t(u, wa_ref[...])
    qkv_ref[...] = _dot_t(u, wq_ref[...])
    z_ref[...] = _dot_t(u, wz_ref[...])
    ba_ref[...] = _dot_t(u, wb_ref[...])


def _const_spec(shape):
    nd = len(shape)
    return pl.BlockSpec(shape, lambda *_: (0,) * nd, pipeline_mode=pl.Buffered(1))


def _inproj(x, gain, wa, wq, wz, wb):
    n = x.shape[0]
    tm = min(ROW_TILE, n)
    row = lambda w: pl.BlockSpec((tm, w), lambda i: (i, 0))
    return pl.pallas_call(
        _inproj_kernel,
        grid=(n // tm,),
        in_specs=[row(D_MODEL), _const_spec(gain.shape), _const_spec(wa.shape), _const_spec(wq.shape),
                  _const_spec(wz.shape), _const_spec(wb.shape)],
        out_specs=[row(A_COLS), row(CONV_CH), row(C_BV), row(LANES)],
        out_shape=[jax.ShapeDtypeStruct((n, A_COLS), F32), jax.ShapeDtypeStruct((n, CONV_CH), F32),
                   jax.ShapeDtypeStruct((n, C_BV), F32), jax.ShapeDtypeStruct((n, LANES), F32)],
        compiler_params=pltpu.CompilerParams(dimension_semantics=("parallel",), vmem_limit_bytes=VMEM_LIMIT),
        name="inproj",
    )(x, gain, wa, wq, wz, wb)


def _rwkv_prep(pa, prev, mu, w0, w2p, a0, a2p, g2, kkw, ka, rk, hb):
    xa = pa + (prev - pa) * mu
    r = xa[:, :C_A]
    k = xa[:, C_A:2 * C_A]
    v = xa[:, 2 * C_A:3 * C_A]
    xwa = xa[:, 3 * C_A:3 * C_A + LORA_W + LORA_A]
    xg = xa[:, 3 * C_A + LORA_W + LORA_A:]
    w_log = -_softplus(-(w0 + _mm(jnp.tanh(xwa), w2p))) - 0.5
    logw = -jnp.exp(w_log)
    a = _sigmoid(a0 + _mm(xwa, a2p))
    g = _mm(_sigmoid(xg), g2)
    kx = k * kkw
    kk = kx * lax.rsqrt(_head_sum(kx * kx, hb) + L2_EPS)
    k2 = k * (1.0 + (a - 1.0) * ka)
    bonus = _head_sum(r * k2 * rk, hb) * v
    return r, k2, v, logw, a, g, kk, bonus


def _head_sum(x, hb):
    groups = [_mm(x[:, LANES * j:LANES * (j + 1)], hb, _NN, SUM_TERMS) for j in range(x.shape[1] // LANES)]
    return jnp.concatenate(groups, axis=1)


def _group_norm_gate(y, bonus, g, lnw, lnb, hb):
    mean = _head_sum(y, hb) * (1.0 / HEAD_A)
    d = y - mean
    var = _head_sum(d * d, hb) * (1.0 / HEAD_A)
    yn = d * lax.rsqrt(var + GN_EPS) * lnw + lnb
    return (yn + bonus) * g


def _rwkv_prompt_kernel(tiles_per_seq, x_ref, nmix_ref, wa_ref,
                        mu_ref, w0_ref, w2_ref, a0_ref, a2_ref, g2_ref, kkw_ref, ka_ref, rk_ref,
                        lnw_ref, lnb_ref, hb_ref, tri_ref,
                        oa_ref, wkv_ref, shift_ref,
                        pa_ref, ext_ref, s_ref, at_ref, bt_ref, kt_ref, rt_ref, v_ref, bh_ref, kh_ref, y_ref):
    s_id = pl.program_id(0)

    @pl.when(s_id == 0)
    def _():
        pa_ref[...] = jnp.zeros_like(pa_ref)

    @pl.when(jnp.maximum(s_id - 1, 0) % tiles_per_seq == 0)
    def _():
        ext_ref[0:HALO, :] = jnp.zeros((HALO, A_COLS), F32)
        s_ref[...] = jnp.zeros_like(s_ref)

    pa = pa_ref[...]
    ext_ref[HALO:, :] = pa
    prev = ext_ref[pl.ds(HALO - 1, T_TILE), :]
    ext_ref[0:HALO, :] = pa[T_TILE - HALO:, :]
    shift_ref[0] = pa[T_TILE - HALO:, :]

    pa_ref[...] = _dot_t(_rms(x_ref[0], nmix_ref[...]).astype(BF16), wa_ref[...])

    hb = hb_ref[...]
    r, k2, v, logw, a, g, kk, bonus = _rwkv_prep(
        pa, prev, mu_ref[...], w0_ref[...], w2_ref[...], a0_ref[...], a2_ref[...], g2_ref[...],
        kkw_ref[...], ka_ref[...], rk_ref[...], hb)

    nchunk = T_TILE // CHUNK
    cs = _mm(tri_ref[...], logw, _NN, 1, CUMSUM_TERMS)
    ends = [cs[CHUNK * (c + 1) - 1:CHUNK * (c + 1)] for c in range(nchunk)]
    tot = jnp.concatenate([jnp.broadcast_to(e, (CHUNK, C_A)) for e in ends], axis=0)
    dinv = jnp.exp(-cs)
    dend = jnp.exp(tot - cs)
    b_in = kk * a
    at_ref[...] = -kk * jnp.exp(cs - logw)
    bt_ref[...] = b_in * dinv
    kt_ref[...] = k2 * dinv
    rt_ref[...] = r * jnp.exp(cs)
    v_ref[...] = v
    bh_ref[...] = b_in * dend
    kh_ref[...] = k2 * dend
    gl = [jnp.exp(e) for e in ends]

    strict, incl, eye = _pair_masks()
    low = lax.broadcasted_iota(jnp.int32, (CHUNK, LANES), 1) < HEAD_A

    def stack(x):
        return jnp.concatenate([jnp.where(low, x, 0.0), jnp.where(low, 0.0, x)], axis=0)

    npair = C_A // LANES
    for p0 in range(0, npair, PAIR_GROUP):
        idx = [(p, c) for p in range(p0, p0 + PAIR_GROUP) for c in range(nchunk)]
        lanes = lambda p: slice(LANES * p, LANES * (p + 1))
        rows = lambda c: slice(CHUNK * c, CHUNK * (c + 1))
        ld = lambda ref: [stack(ref[rows(c), lanes(p)]) for p, c in idx]
        at, bt, kt, rt, vs, bh, kh = (ld(ref) for ref in (at_ref, bt_ref, kt_ref, rt_ref, v_ref, bh_ref, kh_ref))
        n = range(len(idx))
        aa = [_mm(jnp.concatenate([at[i], rt[i]], axis=0), jnp.concatenate([bt[i], kt[i]], axis=0), _NT) for i in n]
        a_ab = [jnp.where(strict, aa[i][:PAIR, :PAIR], 0.0) for i in n]
        a_ak = [jnp.where(strict, aa[i][:PAIR, PAIR:], 0.0) for i in n]
        a_rb = [jnp.where(incl, aa[i][PAIR:, :PAIR], 0.0) for i in n]
        a_rk = [jnp.where(incl, aa[i][PAIR:, PAIR:], 0.0) for i in n]
        tinv = _neumann_inverse_many(a_ab, eye, INV_TERMS)
        akv = [_mm(a_ak[i], vs[i]) for i in n]
        gu = [_mm(tinv[i], jnp.concatenate([at[i], akv[i]], axis=1)) for i in n]
        ry = [_mm(a_rb[i], gu[i]) for i in n]
        rkv = [_mm(a_rk[i], vs[i]) for i in n]
        pq = [_mm(gu[i], bh[i], _TN) for i in n]
        vk = [_mm(vs[i], kh[i], _TN) for i in n]
        rp = [rt[i] + ry[i][:, :PAIR] for i in n]
        yc = [ry[i][:, PAIR:] + rkv[i] for i in n]
        pm = [eye * gl[idx[i][1]][:, lanes(idx[i][0])] + pq[i][:PAIR] for i in n]
        qm = [pq[i][PAIR:] + vk[i] for i in n]
        s = {p: s_ref[p] for p in range(p0, p0 + PAIR_GROUP)}
        for c in range(nchunk):
            for p in range(p0, p0 + PAIR_GROUP):
                i = idx.index((p, c))
                y = _mm(rp[i], s[p], _NT) + yc[i]
                y_ref[rows(c), lanes(p)] = y[:CHUNK] + y[CHUNK:]
                s[p] = _mm(s[p], pm[i], _NN, CHAIN_TERMS, CHAIN_TERMS) + qm[i]
        for p in range(p0, p0 + PAIR_GROUP):
            s_ref[p] = s[p]

    oa_ref[0] = _group_norm_gate(y_ref[...], bonus, g, lnw_ref[...], lnb_ref[...], hb)
    wkv_ref[0] = s_ref[...]


def _tile_maps(bsz, seq, tile):
    tps = seq // tile
    last = bsz * tps - 1
    cur = lambda s: jnp.minimum(s, last)
    prv = lambda s: jnp.maximum(s - 1, 0)
    load = lambda s: (cur(s) // tps, cur(s) % tps, 0)
    emit = lambda s: (prv(s) // tps, prv(s) % tps, 0)
    per_seq3 = lambda s: (prv(s) // tps, 0, 0)
    per_seq4 = lambda s: (prv(s) // tps, 0, 0, 0)
    return tps, last + 2, load, emit, per_seq3, per_seq4


def _rwkv_prompt(x, nmix, wa, mu, w0, w2p, a0, a2p, g2, kkw, ka, rk, lnw, lnb, hb, tri):
    bsz, seq, _ = x.shape
    consts = (nmix, wa, mu, w0, w2p, a0, a2p, g2, kkw, ka, rk, lnw, lnb, hb, tri)
    npair = C_A // LANES
    tps, steps, load, emit, per_seq3, per_seq4 = _tile_maps(bsz, seq, T_TILE)
    tile = lambda: pltpu.VMEM((T_TILE, C_A), F32)
    return pl.pallas_call(
        functools.partial(_rwkv_prompt_kernel, tps),
        grid=(steps,),
        in_specs=[pl.BlockSpec((1, T_TILE, D_MODEL), load)] + [_const_spec(c.shape) for c in consts],
        out_specs=[pl.BlockSpec((1, T_TILE, C_A), emit),
                   pl.BlockSpec((1, npair, LANES, LANES), per_seq4),
                   pl.BlockSpec((1, HALO, A_COLS), per_seq3)],
        out_shape=[jax.ShapeDtypeStruct((bsz, seq, C_A), F32),
                   jax.ShapeDtypeStruct((bsz, npair, LANES, LANES), F32),
                   jax.ShapeDtypeStruct((bsz, HALO, A_COLS), F32)],
        scratch_shapes=[pltpu.VMEM((T_TILE, A_COLS), F32),
                        pltpu.VMEM((T_TILE + HALO, A_COLS), F32), pltpu.VMEM((npair, LANES, LANES), F32)]
                       + [tile() for _ in range(8)],
        compiler_params=pltpu.CompilerParams(dimension_semantics=("arbitrary",), vmem_limit_bytes=VMEM_LIMIT),
        name="rwkv_prompt",
    )(x, *consts)


def _gdn_prep(x0, x1, x2, x3, ba, conv, alog, dtb):
    c = _silu(x0 * conv[0:1] + x1 * conv[1:2] + x2 * conv[2:3] + x3 * conv[3:4])
    qs, ks = [], []
    for h in range(H_B):
        qh = c[:, DK * h:DK * (h + 1)]
        kh = c[:, C_BK + DK * h:C_BK + DK * (h + 1)]
        qs.append(qh * lax.rsqrt(jnp.sum(qh * qh, axis=-1, keepdims=True) + L2_EPS) * (DK ** -0.5))
        ks.append(kh * lax.rsqrt(jnp.sum(kh * kh, axis=-1, keepdims=True) + L2_EPS))
    q = jnp.concatenate(qs, axis=1)
    k = jnp.concatenate(ks, axis=1)
    v = c[:, 2 * C_BK:]
    beta = _sigmoid(ba)
    glog = -jnp.exp(alog) * _softplus(ba + dtb)
    return q, k, v, beta, glog


def _head_norm_gate(o, z, gnorm):
    outs = []
    for h in range(H_B):
        oh = o[:, DV * h:DV * (h + 1)]
        zh = z[:, DV * h:DV * (h + 1)]
        oh = oh * lax.rsqrt(jnp.mean(oh * oh, axis=-1, keepdims=True) + NORM_EPS) * gnorm
        outs.append(oh * _silu(zh))
    return jnp.concatenate(outs, axis=1)


def _gdn_prompt_kernel(tiles_per_seq, x_ref, nmix_ref, wq_ref, wz_ref, wb_ref,
                       conv_ref, alog_ref, dtb_ref, gnorm_ref, tri_ref,
                       ob_ref, gdn_ref, convo_ref,
                       qkvn_ref, zn_ref, ban_ref, ext_ref, s_ref, q_ref, k_ref, v_ref, gc_ref, be_ref, o_ref):
    s_id = pl.program_id(0)

    @pl.when(s_id == 0)
    def _():
        qkvn_ref[...] = jnp.zeros_like(qkvn_ref)
        zn_ref[...] = jnp.zeros_like(zn_ref)
        ban_ref[...] = jnp.zeros_like(ban_ref)

    @pl.when(jnp.maximum(s_id - 1, 0) % tiles_per_seq == 0)
    def _():
        ext_ref[0:HALO, :] = jnp.zeros((HALO, CONV_CH), F32)
        s_ref[...] = jnp.zeros_like(s_ref)

    x3 = qkvn_ref[...]
    z = zn_ref[...]
    ba = ban_ref[...]
    ext_ref[HALO:, :] = x3
    x0 = ext_ref[pl.ds(HALO - 3, G_TILE), :]
    x1 = ext_ref[pl.ds(HALO - 2, G_TILE), :]
    x2 = ext_ref[pl.ds(HALO - 1, G_TILE), :]
    ext_ref[0:HALO, :] = x3[G_TILE - HALO:, :]
    convo_ref[0] = x3[G_TILE - HALO:, :]

    u = _rms(x_ref[0], nmix_ref[...]).astype(BF16)
    qkvn_ref[...] = _dot_t(u, wq_ref[...])
    zn_ref[...] = _dot_t(u, wz_ref[...])
    ban_ref[...] = _dot_t(u, wb_ref[...])

    q, k, v, beta, glog = _gdn_prep(x0, x1, x2, x3, ba, conv_ref[...], alog_ref[...], dtb_ref[...])
    q_ref[...] = q
    k_ref[...] = k
    v_ref[...] = v
    be_ref[...] = beta
    gc_ref[...] = _mm(tri_ref[...], glog, _NN, 1, CUMSUM_TERMS)

    strict, incl, eye = _pair_masks()

    nchunk = G_TILE // CHUNK
    idx = [(pr, c) for pr in range(H_B // 2) for c in range(nchunk)]
    n = range(len(idx))
    rows = lambda c: slice(CHUNK * c, CHUNK * (c + 1))
    cat = lambda ref: [jnp.concatenate([ref[rows(c), DK * h:DK * (h + 1)] for h in (2 * pr, 2 * pr + 1)], axis=0)
                       for pr, c in idx]
    col = lambda ref, off: [jnp.concatenate([ref[rows(c), off + h:off + h + 1] for h in (2 * pr, 2 * pr + 1)], axis=0)
                            for pr, c in idx]
    qs, ks, vs = cat(q_ref), cat(k_ref), cat(v_ref)
    beta_c = col(be_ref, 0)
    gc = col(gc_ref, H_B)
    gt = [jnp.concatenate([jnp.broadcast_to(g_[CHUNK * (i + 1) - 1:CHUNK * (i + 1)], (CHUNK, 1)) for i in range(2)],
                          axis=0) for g_ in gc]
    decay = []
    for g_ in gc:
        gc_full = jnp.broadcast_to(g_, (PAIR, PAIR))
        diff = gc_full - gc_full.T
        decay.append(jnp.exp(diff))
    kb = [ks[i] * beta_c[i] for i in n]
    vb = [vs[i] * beta_c[i] for i in n]
    kq = [_mm(jnp.concatenate([kb[i], qs[i]], axis=0), ks[i], _NT) for i in n]
    lmat = [jnp.where(strict, kq[i][:PAIR] * decay[i], 0.0) for i in n]
    qk = [jnp.where(incl, kq[i][PAIR:] * decay[i], 0.0) for i in n]
    tinv = _neumann_inverse_many([-l for l in lmat], eye, INV_TERMS)
    eg = [jnp.exp(g_) for g_ in gc]
    uw = [_mm(tinv[i], jnp.concatenate([vb[i], kb[i] * eg[i]], axis=1)) for i in n]
    ow = [_mm(qk[i], uw[i]) for i in n]
    oc = [ow[i][:, :DV] for i in n]
    rq = [qs[i] * eg[i] - ow[i][:, DV:] for i in n]
    kd = [ks[i] * jnp.exp(gt[i] - gc[i]) for i in n]
    half = lambda j: slice(CHUNK * j, CHUNK * (j + 1))
    pq = [[_mm(kd[i][half(j)], uw[i][half(j)], _TN) for j in range(2)] for i in n]
    pm = [[eye * jnp.exp(gt[i][CHUNK * j:CHUNK * j + 1]) - pq[i][j][:, DV:] for j in range(2)] for i in n]
    s = [s_ref[h] for h in range(H_B)]
    for c in range(nchunk):
        for h in range(H_B):
            i, j = idx.index((h // 2, c)), h % 2
            o_ref[rows(c), DV * h:DV * (h + 1)] = _mm(rq[i][half(j)], s[h]) + oc[i][half(j)]
            s[h] = _mm(pm[i][j], s[h], _NN, CHAIN_TERMS, CHAIN_TERMS) + pq[i][j][:, :DV]
    for h in range(H_B):
        s_ref[h] = s[h]

    ob_ref[0] = _head_norm_gate(o_ref[...], z, gnorm_ref[...])
    gdn_ref[0] = s_ref[...]


def _gdn_prompt(x, nmix, wq, wz, wb, conv, alog, dtb, gnorm, tri):
    bsz, seq, _ = x.shape
    consts = (nmix, wq, wz, wb, conv, alog, dtb, gnorm, tri)
    tps, steps, load, emit, per_seq3, per_seq4 = _tile_maps(bsz, seq, G_TILE)
    return pl.pallas_call(
        functools.partial(_gdn_prompt_kernel, tps),
        grid=(steps,),
        in_specs=[pl.BlockSpec((1, G_TILE, D_MODEL), load)] + [_const_spec(c.shape) for c in consts],
        out_specs=[pl.BlockSpec((1, G_TILE, C_BV), emit), pl.BlockSpec((1, H_B, DK, DV), per_seq4),
                   pl.BlockSpec((1, HALO, CONV_CH), per_seq3)],
        out_shape=[jax.ShapeDtypeStruct((bsz, seq, C_BV), F32), jax.ShapeDtypeStruct((bsz, H_B, DK, DV), F32),
                   jax.ShapeDtypeStruct((bsz, HALO, CONV_CH), F32)],
        scratch_shapes=[pltpu.VMEM((G_TILE, CONV_CH), F32), pltpu.VMEM((G_TILE, C_BV), F32),
                        pltpu.VMEM((G_TILE, LANES), F32),
                        pltpu.VMEM((G_TILE + HALO, CONV_CH), F32), pltpu.VMEM((H_B, DK, DV), F32),
                        pltpu.VMEM((G_TILE, C_BK), F32), pltpu.VMEM((G_TILE, C_BK), F32),
                        pltpu.VMEM((G_TILE, C_BV), F32), pltpu.VMEM((G_TILE, LANES), F32),
                        pltpu.VMEM((G_TILE, LANES), F32), pltpu.VMEM((G_TILE, C_BV), F32)],
        compiler_params=pltpu.CompilerParams(dimension_semantics=("arbitrary",), vmem_limit_bytes=VMEM_LIMIT),
        name="gdn_prompt",
    )(x, *consts)


def _to_columns(x):
    pad = jnp.zeros((LANES - DEC_TILE, x.shape[1]), F32)
    return jnp.concatenate([x, pad], axis=0).T


def _decode_rwkv_kernel(pa_ref, shift_ref, wkv_ref,
                        mu_ref, w0_ref, w2_ref, a0_ref, a2_ref, g2_ref, kkw_ref, ka_ref, rk_ref, lnw_ref, lnb_ref,
                        hb_ref, oa_ref, wkvo_ref, tr_ref, yt_ref, g_ref, bonus_ref):
    h = pl.program_id(0)

    @pl.when(h == 0)
    def _():
        r, k2, v, logw, a, g, kk, bonus = _rwkv_prep(
            pa_ref[...], shift_ref[...], mu_ref[...], w0_ref[...], w2_ref[...], a0_ref[...], a2_ref[...],
            g2_ref[...], kkw_ref[...], ka_ref[...], rk_ref[...], hb_ref[...])
        for i, x in enumerate((-kk, jnp.exp(logw), kk * a, k2, r, v)):
            tr_ref[i] = x.T
        g_ref[...] = g
        bonus_ref[...] = bonus

    base = pl.multiple_of(h * HEAD_A, HEAD_A)
    hs = pl.ds(base, HEAD_A)
    a_t, w_t, b_t, k_t, r_t = (tr_ref[i, hs, :] for i in range(5))

    def value_row(vi, carry):
        st = wkv_ref[0, vi]
        sa = jnp.sum(st * a_t, axis=0, keepdims=True)
        st = st * w_t + sa * b_t + tr_ref[5, pl.ds(base + vi, 1), :] * k_t
        wkvo_ref[0, vi] = st
        yt_ref[pl.ds(base + vi, 1), :] = jnp.sum(st * r_t, axis=0, keepdims=True)
        return carry

    lax.fori_loop(0, HEAD_A, value_row, 0, unroll=8)

    @pl.when(h == H_A - 1)
    def _():
        oa_ref[...] = _group_norm_gate(yt_ref[...].T, bonus_ref[...], g_ref[...], lnw_ref[...], lnb_ref[...],
                                       hb_ref[...])


def _decode_rwkv(pa, shift, wkv_t, rw_consts):
    n = pa.shape[0]
    full = lambda w: pl.BlockSpec((n, w), lambda h: (0, 0))
    state = pl.BlockSpec((1, HEAD_A, HEAD_A, n), lambda h: (h, 0, 0, 0))
    return pl.pallas_call(
        _decode_rwkv_kernel,
        grid=(H_A,),
        in_specs=[full(A_COLS), full(A_COLS), state] + [_const_spec(c.shape) for c in rw_consts],
        out_specs=[full(C_A), state],
        out_shape=[jax.ShapeDtypeStruct((n, C_A), F32), jax.ShapeDtypeStruct(wkv_t.shape, F32)],
        scratch_shapes=[pltpu.VMEM((6, C_A, n), F32), pltpu.VMEM((C_A, n), F32),
                        pltpu.VMEM((n, C_A), F32), pltpu.VMEM((n, C_A), F32)],
        compiler_params=pltpu.CompilerParams(dimension_semantics=("arbitrary",), vmem_limit_bytes=VMEM_LIMIT),
        name="decode_rwkv",
    )(pa, shift, wkv_t, *rw_consts)


def _decode_gdn_kernel(qkv_ref, z_ref, ba_ref, cst_ref, gdn_ref, conv_ref, alog_ref, dtb_ref, gnorm_ref,
                       ob_ref, gdno_ref):
    cst = cst_ref[...]
    q, k, vv, beta, glog = _gdn_prep(cst[:, :CONV_CH], cst[:, CONV_CH:2 * CONV_CH], cst[:, 2 * CONV_CH:],
                                     qkv_ref[...], ba_ref[...], conv_ref[...], alog_ref[...], dtb_ref[...])
    eg = jnp.exp(glog)
    k_cols = _to_columns(k)
    ls = lambda h: slice(DK * h, DK * (h + 1))

    def split2(x):
        hi = x.astype(BF16)
        return hi, x - hi.astype(F32)

    rows_a, rows_b = [], []
    for h in range(H_B):
        e_h = eg[:, H_B + h:H_B + h + 1]
        (wh, wl), (gh, gl_) = split2(k[:, ls(h)] * (beta[:, h:h + 1] * e_h)), split2(q[:, ls(h)] * e_h)
        rows_a.append(jnp.concatenate([wh.astype(F32), gh.astype(F32), wl, gl_], axis=0).astype(BF16))
        rows_b.append(jnp.concatenate([wh.astype(F32), gh.astype(F32)], axis=0).astype(BF16))
    gunits = [(s_i, h) for s_i in range(DEC_TILE) for h in range(H_B)]
    gn = range(len(gunits))
    be = [beta[s_i:s_i + 1, h:h + 1] for s_i, h in gunits]
    e = [eg[s_i:s_i + 1, H_B + h:H_B + h + 1] for s_i, h in gunits]
    kc = [k_cols[ls(h), s_i:s_i + 1] for s_i, h in gunits]
    gst = [gdn_ref[s_i, h] for s_i, h in gunits]
    parts = [split2(st) for st in gst]
    ra = [_dot(rows_a[h], parts[i][0].astype(BF16)) + jnp.concatenate(
        [_dot(rows_b[h], parts[i][1].astype(BF16)), jnp.zeros((2 * DEC_TILE, DV), F32)], axis=0)
        for i, (s_i, h) in enumerate(gunits)]
    pick = lambda r, j, s_i: r[DEC_TILE * j + s_i:DEC_TILE * j + s_i + 1]
    ws = [pick(ra[i], 0, s_i) + pick(ra[i], 2, s_i) for i, (s_i, h) in enumerate(gunits)]
    qs = [pick(ra[i], 1, s_i) + pick(ra[i], 3, s_i) for i, (s_i, h) in enumerate(gunits)]
    v_new = [be[i] * vv[s_i:s_i + 1, ls(h)] - ws[i] for i, (s_i, h) in enumerate(gunits)]
    qk = [jnp.sum(q[s_i:s_i + 1, ls(h)] * k[s_i:s_i + 1, ls(h)], axis=-1, keepdims=True) for s_i, h in gunits]
    for i, (s_i, h) in enumerate(gunits):
        gdno_ref[s_i, h] = gst[i] * e[i] + kc[i] * v_new[i]
    o_units = [qs[i] + qk[i] * v_new[i] for i in gn]
    o = jnp.concatenate([jnp.concatenate(o_units[H_B * s_i:H_B * (s_i + 1)], axis=1) for s_i in range(DEC_TILE)],
                        axis=0)
    ob_ref[...] = _head_norm_gate(o, z_ref[...], gnorm_ref[...])


def _decode_gdn(qkv, z, ba, cst, gdn, gdn_consts):
    n = qkv.shape[0]
    row = lambda w: pl.BlockSpec((DEC_TILE, w), lambda i: (i, 0))
    state = pl.BlockSpec((DEC_TILE, H_B, DK, DV), lambda i: (i, 0, 0, 0))
    return pl.pallas_call(
        _decode_gdn_kernel,
        grid=(n // DEC_TILE,),
        in_specs=[row(CONV_CH), row(C_BV), row(LANES), row(3 * CONV_CH), state]
                 + [_const_spec(c.shape) for c in gdn_consts],
        out_specs=[row(C_BV), state],
        out_shape=[jax.ShapeDtypeStruct((n, C_BV), F32), jax.ShapeDtypeStruct(gdn.shape, F32)],
        compiler_params=pltpu.CompilerParams(dimension_semantics=("parallel",), vmem_limit_bytes=VMEM_LIMIT),
        name="decode_gdn",
    )(qkv, z, ba, cst, gdn, *gdn_consts)


def _post_kernel(x_ref, oa_ref, ob_ref, pe_ref, nmix_ref, wg_ref, wba_ref, wbb_ref, wout_ref, nffn_ref,
                 wfg_ref, wfu_ref, wfd_ref, nple_ref, wpg_ref, wpp_ref, nfin_ref, y_ref):
    x = x_ref[...]
    u = _rms(x, nmix_ref[...]).astype(BF16)
    gates = _sigmoid(_dot_t(u, wg_ref[...]))
    mix = (gates[:, :D_MODEL] * _dot(oa_ref[...].astype(BF16), wba_ref[...])
           + gates[:, D_MODEL:] * _dot(ob_ref[...].astype(BF16), wbb_ref[...]))
    h = x + _dot(mix.astype(BF16), wout_ref[...])
    u2 = _rms(h, nffn_ref[...]).astype(BF16)
    ff = _silu(_dot(u2, wfg_ref[...])) * _dot(u2, wfu_ref[...])
    h = h + _dot(ff.astype(BF16), wfd_ref[...])
    u3 = _rms(h, nple_ref[...]).astype(BF16)
    h = h + _sigmoid(_dot(u3, wpg_ref[...])) * _dot(pe_ref[...].astype(BF16), wpp_ref[...])
    y_ref[...] = _rms(h, nfin_ref[...])


def _post(x, oa, ob, pe, consts):
    n = x.shape[0]
    tm = min(ROW_TILE, n)
    row = lambda w: pl.BlockSpec((tm, w), lambda i: (i, 0))
    return pl.pallas_call(
        _post_kernel,
        grid=(n // tm,),
        in_specs=[row(D_MODEL), row(C_A), row(C_BV), row(PLE_DIM)] + [_const_spec(c.shape) for c in consts],
        out_specs=row(D_MODEL),
        out_shape=jax.ShapeDtypeStruct((n, D_MODEL), F32),
        compiler_params=pltpu.CompilerParams(dimension_semantics=("parallel",), vmem_limit_bytes=VMEM_LIMIT),
        name="post",
    )(x, oa, ob, pe, *consts)


def _chunk_cumsum_matrix(tile):
    i = jnp.arange(tile)
    same = (i[:, None] // CHUNK) == (i[None, :] // CHUNK)
    return (same & (i[None, :] <= i[:, None])).astype(BF16)


def kernel(x_prompt, x_sample, p_prompt, p_sample, state_shift, state_wkv, state_conv, state_gdn, norm_mix, w_in, mu_shift, rw_w0, rw_w2, rw_a0, rw_a2, rw_g2, rw_kk, rw_ka, rw_rk, rw_ln_w, rw_ln_b, gdn_conv, gdn_a_log, gdn_dt_bias, gdn_norm, w_branch_a, w_branch_b, w_out, norm_ffn, w_ffn_gate, w_ffn_up, w_ffn_down, norm_ple, w_ple_gate, w_ple_proj, norm_final):
    bsz, seq, _ = x_prompt.shape
    nd = x_sample.shape[0]
    row = lambda p: p.reshape(1, -1)

    w_t = jnp.transpose(w_in[0])
    b0 = A_COLS
    wa = w_t[:b0].astype(BF16)
    wq = w_t[b0:b0 + CONV_CH].astype(BF16)
    wz = w_t[b0 + CONV_CH:b0 + CONV_CH + C_BV].astype(BF16)
    wb = jnp.pad(w_t[b0 + CONV_CH + C_BV:b0 + CONV_CH + C_BV + 2 * H_B], ((0, LANES - 2 * H_B), (0, 0))).astype(BF16)
    wg = w_t[b0 + CONV_CH + C_BV + 2 * H_B:].astype(BF16)
    w2p = jnp.concatenate([rw_w2[0], jnp.zeros((LORA_A, C_A), F32)], axis=0).astype(BF16)
    a2p = jnp.concatenate([jnp.zeros((LORA_W, C_A), F32), rw_a2[0]], axis=0).astype(BF16)
    ch = jnp.arange(LANES) // HEAD_A
    hb = (ch[:, None] == ch[None, :]).astype(BF16)
    tri = _chunk_cumsum_matrix(T_TILE)
    tri_g = _chunk_cumsum_matrix(G_TILE)
    alog = jnp.pad(gdn_a_log[0], (H_B, LANES - 2 * H_B)).reshape(1, LANES)
    dtb = jnp.pad(gdn_dt_bias[0], (H_B, LANES - 2 * H_B)).reshape(1, LANES)
    rw_consts = (row(mu_shift[0]), row(rw_w0[0]), w2p, row(rw_a0[0]), a2p, rw_g2[0].astype(BF16), row(rw_kk[0]), row(rw_ka[0]),
                 row(rw_rk[0]), row(rw_ln_w[0]), row(rw_ln_b[0]), hb)
    gdn_consts = (gdn_conv[0], alog, dtb, row(gdn_norm[0]))
    post_consts = (row(norm_mix[0]), wg, w_branch_a[0].astype(BF16), w_branch_b[0].astype(BF16),
                   w_out[0].astype(BF16), row(norm_ffn[0]), w_ffn_gate[0].astype(BF16), w_ffn_up[0].astype(BF16),
                   w_ffn_down[0].astype(BF16), row(norm_ple[0]), w_ple_gate[0].astype(BF16),
                   w_ple_proj[0].astype(BF16), row(norm_final))

    xp = x_prompt.reshape(bsz * seq, D_MODEL)
    oa, wkv_pairs, shift_tail = _rwkv_prompt(x_prompt, row(norm_mix[0]), wa, *rw_consts, tri)
    ob, gdn_p, conv_tail = _gdn_prompt(x_prompt, row(norm_mix[0]), wq, wz, wb, *gdn_consts, tri_g)
    y_prompt = _post(xp, oa.reshape(bsz * seq, C_A), ob.reshape(bsz * seq, C_BV),
                     p_prompt[0].reshape(bsz * seq, PLE_DIM), post_consts).reshape(bsz, seq, D_MODEL)
    wkv_p = jnp.stack([wkv_pairs[:, :, :HEAD_A, :HEAD_A], wkv_pairs[:, :, HEAD_A:, HEAD_A:]], axis=2)
    wkv_p = wkv_p.reshape(bsz, H_A, HEAD_A, HEAD_A)
    shift_p = shift_tail[:, HALO - 1:, :]
    conv_p = conv_tail[:, HALO - (CONV_W - 1):, :]

    xs = x_sample.reshape(nd, D_MODEL)
    pa_s, qkv_s, z_s, ba_s = _inproj(xs, row(norm_mix[0]), wa, wq, wz, wb)
    cst = state_conv[0].reshape(nd, (CONV_W - 1) * CONV_CH)
    oa_s, wkv_t = _decode_rwkv(pa_s, state_shift[0].reshape(nd, A_COLS), jnp.transpose(state_wkv[0], (1, 2, 3, 0)),
                               rw_consts)
    wkv_s = jnp.transpose(wkv_t, (3, 0, 1, 2))
    ob_s, gdn_s = _decode_gdn(qkv_s, z_s, ba_s, cst, state_gdn[0], gdn_consts)
    y_sample = _post(xs, oa_s, ob_s, p_sample[0].reshape(nd, PLE_DIM), post_consts).reshape(nd, 1, D_MODEL)
    conv_s = jnp.concatenate([cst[:, CONV_CH:], qkv_s], axis=1).reshape(nd, CONV_W - 1, CONV_CH)

    return (y_prompt, y_sample, shift_p[None], wkv_p[None], conv_p[None], gdn_p[None],
            pa_s.reshape(1, nd, 1, A_COLS), wkv_s[None], conv_s[None], gdn_s[None])
```

```python
import functools

import jax
import jax.numpy as jnp
from jax import lax
from jax.experimental import pallas as pl
from jax.experimental.pallas import tpu as pltpu

F32 = jnp.float32
BF16 = jnp.bfloat16

D_MODEL = 1024
HEAD_A = 64
C_A = 512
H_A = 8
LORA_W = 64
LORA_A = 64
LORA_G = 128
A_COLS = 3 * C_A + LORA_W + LORA_A + LORA_G
DK = 128
DV = 128
H_B = 4
C_BK = 512
C_BV = 512
CONV_W = 4
CONV_CH = 2 * C_BK + C_BV
D_FF = 2816
PLE_DIM = 256
NORM_EPS = 1e-6
GN_EPS = 64e-5
L2_EPS = 1e-6

LANES = 128
CHUNK = 64
PAIR = 2 * CHUNK
T_TILE = 256
G_TILE = 512
ROW_TILE = 512
POST_GROUP_ROWS = 256
DEC_TILE = 8
HALO = 8
VMEM_LIMIT = 56 * 1024 * 1024
SUM_TERMS = 2
CUMSUM_TERMS = 3
INV_TERMS = 1
CHAIN_TERMS = 1
PAIR_GROUP = 4


def _dot(a, b):
    return jnp.dot(a, b, preferred_element_type=F32)


def _sigmoid(x):
    return 1.0 / (1.0 + jnp.exp(-x))


def _silu(x):
    return x * _sigmoid(x)


def _softplus(x):
    return jnp.maximum(x, 0.0) + jnp.log(1.0 + jnp.exp(-jnp.abs(x)))


def _rms(x, gain):
    return x * lax.rsqrt(jnp.mean(x * x, axis=-1, keepdims=True) + NORM_EPS) * gain


def _pair_masks():
    ri = lax.broadcasted_iota(jnp.int32, (PAIR, PAIR), 0)
    ci = lax.broadcasted_iota(jnp.int32, (PAIR, PAIR), 1)
    same = (ri < CHUNK) == (ci < CHUNK)
    strict = same & (ci < ri)
    incl = same & (ci <= ri)
    eye = (ri == ci).astype(F32)
    return strict, incl, eye


_NN = (((1,), (0,)), ((), ()))
_NT = (((1,), (1,)), ((), ()))
_TN = (((0,), (0,)), ((), ()))


def _split(x, terms):
    if isinstance(x, (list, tuple)):
        return list(x)
    if x.dtype == BF16:
        return [x]
    parts = []
    for i in range(terms):
        h = x.astype(BF16)
        parts.append(h)
        if i + 1 < terms:
            x = x - h.astype(F32)
    return parts


def _mm(a, b, dims=_NN, na=1, nb=1):
    pa, pb = _split(a, na), _split(b, nb)
    acc = None
    for i, ai in enumerate(pa):
        for j, bj in enumerate(pb):
            if i + j < max(len(pa), len(pb)):
                d = lax.dot_general(ai, bj, dims, preferred_element_type=F32)
                acc = d if acc is None else acc + d
    return acc


def _neumann_inverse_many(lmats, eye, terms):
    ts = [eye + l for l in lmats]
    ps = [_split(l, terms) for l in lmats]
    n = 2
    while n < CHUNK:
        ps = [_split(_mm(p, p), terms) for p in ps]
        ts = [t + _mm(t, p, _NN, terms) for t, p in zip(ts, ps)]
        n *= 2
    return ts


def _inproj_kernel(x_ref, g_ref, wa_ref, wq_ref, wz_ref, wb_ref, pa_ref, qkv_ref, z_ref, ba_ref):
    u = _rms(x_ref[...], g_ref[...]).astype(BF16)
    pa_ref[...] = _dot(u, wa_ref[...])
    qkv_ref[...] = _dot(u, wq_ref[...])
    z_ref[...] = _dot(u, wz_ref[...])
    ba_ref[...] = _dot(u, wb_ref[...])


def _const_spec(shape):
    nd = len(shape)
    return pl.BlockSpec(shape, lambda *_: (0,) * nd, pipeline_mode=pl.Buffered(1))


def _inproj(x, gain, wa, wq, wz, wb):
    n = x.shape[0]
    tm = min(ROW_TILE, n)
    row = lambda w: pl.BlockSpec((tm, w), lambda i: (i, 0))
    return pl.pallas_call(
        _inproj_kernel,
        grid=(n // tm,),
        in_specs=[row(D_MODEL), _const_spec(gain.shape), _const_spec(wa.shape), _const_spec(wq.shape),
                  _const_spec(wz.shape), _const_spec(wb.shape)],
        out_specs=[row(A_COLS), row(CONV_CH), row(C_BV), row(LANES)],
        out_shape=[jax.ShapeDtypeStruct((n, A_COLS), F32), jax.ShapeDtypeStruct((n, CONV_CH), F32),
                   jax.ShapeDtypeStruct((n, C_BV), F32), jax.ShapeDtypeStruct((n, LANES), F32)],
        compiler_params=pltpu.CompilerParams(dimension_semantics=("parallel",), vmem_limit_bytes=VMEM_LIMIT),
        name="inproj",
    )(x, gain, wa, wq, wz, wb)


def _rwkv_prep(pa, prev, mu, w0, w2p, a0, a2p, g2, kkw, ka, rk, hb):
    xa = pa + (prev - pa) * mu
    r = xa[:, :C_A]
    k = xa[:, C_A:2 * C_A]
    v = xa[:, 2 * C_A:3 * C_A]
    xwa = xa[:, 3 * C_A:3 * C_A + LORA_W + LORA_A]
    xg = xa[:, 3 * C_A + LORA_W + LORA_A:]
    w_log = -_softplus(-(w0 + _mm(jnp.tanh(xwa), w2p))) - 0.5
    logw = -jnp.exp(w_log)
    a = _sigmoid(a0 + _mm(xwa, a2p))
    g = _mm(_sigmoid(xg), g2)
    kx = k * kkw
    kk = kx * lax.rsqrt(_head_sum(kx * kx, hb) + L2_EPS)
    k2 = k * (1.0 + (a - 1.0) * ka)
    bonus = _head_sum(r * k2 * rk, hb) * v
    return r, k2, v, logw, a, g, kk, bonus


def _head_sum(x, hb):
    groups = [_mm(x[:, LANES * j:LANES * (j + 1)], hb, _NN, SUM_TERMS) for j in range(x.shape[1] // LANES)]
    return jnp.concatenate(groups, axis=1)


def _group_norm_gate(y, bonus, g, lnw, lnb, hb):
    mean = _head_sum(y, hb) * (1.0 / HEAD_A)
    d = y - mean
    var = _head_sum(d * d, hb) * (1.0 / HEAD_A)
    yn = d * lax.rsqrt(var + GN_EPS) * lnw + lnb
    return (yn + bonus) * g


def _rwkv_prompt_kernel(tiles_per_seq, x_ref, nmix_ref, wa_ref,
                        mu_ref, w0_ref, w2_ref, a0_ref, a2_ref, g2_ref, kkw_ref, ka_ref, rk_ref,
                        lnw_ref, lnb_ref, hb_ref, tri_ref,
                        oa_ref, wkv_ref, shift_ref,
                        pa_ref, ext_ref, s_ref, at_ref, bt_ref, kt_ref, rt_ref, v_ref, bh_ref, kh_ref, y_ref):
    s_id = pl.program_id(0)

    @pl.when(s_id == 0)
    def _():
        pa_ref[...] = jnp.zeros_like(pa_ref)

    @pl.when(jnp.maximum(s_id - 1, 0) % tiles_per_seq == 0)
    def _():
        ext_ref[0:HALO, :] = jnp.zeros((HALO, A_COLS), F32)
        s_ref[...] = jnp.zeros_like(s_ref)

    pa = pa_ref[...]
    ext_ref[HALO:, :] = pa
    prev = ext_ref[pl.ds(HALO - 1, T_TILE), :]
    ext_ref[0:HALO, :] = pa[T_TILE - HALO:, :]
    shift_ref[0] = pa[T_TILE - HALO:, :]

    pa_ref[...] = _dot(_rms(x_ref[0], nmix_ref[...]).astype(BF16), wa_ref[...])

    hb = hb_ref[...]
    r, k2, v, logw, a, g, kk, bonus = _rwkv_prep(
        pa, prev, mu_ref[...], w0_ref[...], w2_ref[...], a0_ref[...], a2_ref[...], g2_ref[...],
        kkw_ref[...], ka_ref[...], rk_ref[...], hb)

    nchunk = T_TILE // CHUNK
    cs = _mm(tri_ref[...], logw, _NN, 1, CUMSUM_TERMS)
    ends = [cs[CHUNK * (c + 1) - 1:CHUNK * (c + 1)] for c in range(nchunk)]
    tot = jnp.concatenate([jnp.broadcast_to(e, (CHUNK, C_A)) for e in ends], axis=0)
    dinv = jnp.exp(-cs)
    dend = jnp.exp(tot - cs)
    b_in = kk * a
    at_ref[...] = -kk * jnp.exp(cs - logw)
    bt_ref[...] = b_in * dinv
    kt_ref[...] = k2 * dinv
    rt_ref[...] = r * jnp.exp(cs)
    v_ref[...] = v
    bh_ref[...] = b_in * dend
    kh_ref[...] = k2 * dend
    gl = [jnp.exp(e) for e in ends]

    strict, incl, eye = _pair_masks()
    low = lax.broadcasted_iota(jnp.int32, (CHUNK, LANES), 1) < HEAD_A

    def stack(x):
        return jnp.concatenate([jnp.where(low, x, 0.0), jnp.where(low, 0.0, x)], axis=0)

    npair = C_A // LANES
    for p0 in range(0, npair, PAIR_GROUP):
        idx = [(p, c) for p in range(p0, p0 + PAIR_GROUP) for c in range(nchunk)]
        lanes = lambda p: slice(LANES * p, LANES * (p + 1))
        rows = lambda c: slice(CHUNK * c, CHUNK * (c + 1))
        ld = lambda ref: [stack(ref[rows(c), lanes(p)]) for p, c in idx]
        at, bt, kt, rt, vs, bh, kh = (ld(ref) for ref in (at_ref, bt_ref, kt_ref, rt_ref, v_ref, bh_ref, kh_ref))
        n = range(len(idx))
        aa = [_mm(jnp.concatenate([at[i], rt[i]], axis=0), jnp.concatenate([bt[i], kt[i]], axis=0), _NT) for i in n]
        a_ab = [jnp.where(strict, aa[i][:PAIR, :PAIR], 0.0) for i in n]
        a_ak = [jnp.where(strict, aa[i][:PAIR, PAIR:], 0.0) for i in n]
        a_rb = [jnp.where(incl, aa[i][PAIR:, :PAIR], 0.0) for i in n]
        a_rk = [jnp.where(incl, aa[i][PAIR:, PAIR:], 0.0) for i in n]
        tinv = _neumann_inverse_many(a_ab, eye, INV_TERMS)
        akv = [_mm(a_ak[i], vs[i]) for i in n]
        gu = [_mm(tinv[i], jnp.concatenate([at[i], akv[i]], axis=1)) for i in n]
        ry = [_mm(a_rb[i], gu[i]) for i in n]
        rkv = [_mm(a_rk[i], vs[i]) for i in n]
        pq = [_mm(gu[i], bh[i], _TN) for i in n]
        vk = [_mm(vs[i], kh[i], _TN) for i in n]
        rp = [rt[i] + ry[i][:, :PAIR] for i in n]
        yc = [ry[i][:, PAIR:] + rkv[i] for i in n]
        pm = [eye * gl[idx[i][1]][:, lanes(idx[i][0])] + pq[i][:PAIR] for i in n]
        qm = [pq[i][PAIR:] + vk[i] for i in n]
        s = {p: s_ref[p] for p in range(p0, p0 + PAIR_GROUP)}
        for c in range(nchunk):
            for p in range(p0, p0 + PAIR_GROUP):
                i = idx.index((p, c))
                y = _mm(rp[i], s[p], _NT) + yc[i]
                y_ref[rows(c), lanes(p)] = y[:CHUNK] + y[CHUNK:]
                s[p] = _mm(s[p], pm[i], _NN, CHAIN_TERMS, CHAIN_TERMS) + qm[i]
        for p in range(p0, p0 + PAIR_GROUP):
            s_ref[p] = s[p]

    oa_ref[0] = _group_norm_gate(y_ref[...], bonus, g, lnw_ref[...], lnb_ref[...], hb)
    wkv_ref[0] = s_ref[...]


def _tile_maps(bsz, seq, tile):
    tps = seq // tile
    last = bsz * tps - 1
    cur = lambda s: jnp.minimum(s, last)
    prv = lambda s: jnp.maximum(s - 1, 0)
    load = lambda s: (cur(s) // tps, cur(s) % tps, 0)
    emit = lambda s: (prv(s) // tps, prv(s) % tps, 0)
    per_seq3 = lambda s: (prv(s) // tps, 0, 0)
    per_seq4 = lambda s: (prv(s) // tps, 0, 0, 0)
    return tps, last + 2, load, emit, per_seq3, per_seq4


def _rwkv_prompt(x, nmix, wa, mu, w0, w2p, a0, a2p, g2, kkw, ka, rk, lnw, lnb, hb, tri):
    bsz, seq, _ = x.shape
    consts = (nmix, wa, mu, w0, w2p, a0, a2p, g2, kkw, ka, rk, lnw, lnb, hb, tri)
    npair = C_A // LANES
    tps, steps, load, emit, per_seq3, per_seq4 = _tile_maps(bsz, seq, T_TILE)
    tile = lambda: pltpu.VMEM((T_TILE, C_A), F32)
    return pl.pallas_call(
        functools.partial(_rwkv_prompt_kernel, tps),
        grid=(steps,),
        in_specs=[pl.BlockSpec((1, T_TILE, D_MODEL), load)] + [_const_spec(c.shape) for c in consts],
        out_specs=[pl.BlockSpec((1, T_TILE, C_A), emit),
                   pl.BlockSpec((1, npair, LANES, LANES), per_seq4),
                   pl.BlockSpec((1, HALO, A_COLS), per_seq3)],
        out_shape=[jax.ShapeDtypeStruct((bsz, seq, C_A), F32),
                   jax.ShapeDtypeStruct((bsz, npair, LANES, LANES), F32),
                   jax.ShapeDtypeStruct((bsz, HALO, A_COLS), F32)],
        scratch_shapes=[pltpu.VMEM((T_TILE, A_COLS), F32),
                        pltpu.VMEM((T_TILE + HALO, A_COLS), F32), pltpu.VMEM((npair, LANES, LANES), F32)]
                       + [tile() for _ in range(8)],
        compiler_params=pltpu.CompilerParams(dimension_semantics=("arbitrary",), vmem_limit_bytes=VMEM_LIMIT),
        name="rwkv_prompt",
    )(x, *consts)


def _gdn_prep(x0, x1, x2, x3, ba, conv, alog, dtb):
    c = _silu(x0 * conv[0:1] + x1 * conv[1:2] + x2 * conv[2:3] + x3 * conv[3:4])
    qs, ks = [], []
    for h in range(H_B):
        qh = c[:, DK * h:DK * (h + 1)]
        kh = c[:, C_BK + DK * h:C_BK + DK * (h + 1)]
        qs.append(qh * lax.rsqrt(jnp.sum(qh * qh, axis=-1, keepdims=True) + L2_EPS) * (DK ** -0.5))
        ks.append(kh * lax.rsqrt(jnp.sum(kh * kh, axis=-1, keepdims=True) + L2_EPS))
    q = jnp.concatenate(qs, axis=1)
    k = jnp.concatenate(ks, axis=1)
    v = c[:, 2 * C_BK:]
    beta = _sigmoid(ba)
    glog = -jnp.exp(alog) * _softplus(ba + dtb)
    return q, k, v, beta, glog


def _head_norm_gate(o, z, gnorm):
    outs = []
    for h in range(H_B):
        oh = o[:, DV * h:DV * (h + 1)]
        zh = z[:, DV * h:DV * (h + 1)]
        oh = oh * lax.rsqrt(jnp.mean(oh * oh, axis=-1, keepdims=True) + NORM_EPS) * gnorm
        outs.append(oh * _silu(zh))
    return jnp.concatenate(outs, axis=1)


def _gdn_prompt_kernel(tiles_per_seq, x_ref, nmix_ref, wq_ref, wz_ref, wb_ref,
                       conv_ref, alog_ref, dtb_ref, gnorm_ref, tri_ref,
                       ob_ref, gdn_ref, convo_ref,
                       qkvn_ref, zn_ref, ban_ref, ext_ref, s_ref, q_ref, k_ref, v_ref, gc_ref, be_ref, o_ref):
    s_id = pl.program_id(0)

    @pl.when(s_id == 0)
    def _():
        qkvn_ref[...] = jnp.zeros_like(qkvn_ref)
        zn_ref[...] = jnp.zeros_like(zn_ref)
        ban_ref[...] = jnp.zeros_like(ban_ref)

    @pl.when(jnp.maximum(s_id - 1, 0) % tiles_per_seq == 0)
    def _():
        ext_ref[0:HALO, :] = jnp.zeros((HALO, CONV_CH), F32)
        s_ref[...] = jnp.zeros_like(s_ref)

    x3 = qkvn_ref[...]
    z = zn_ref[...]
    ba = ban_ref[...]
    ext_ref[HALO:, :] = x3
    x0 = ext_ref[pl.ds(HALO - 3, G_TILE), :]
    x1 = ext_ref[pl.ds(HALO - 2, G_TILE), :]
    x2 = ext_ref[pl.ds(HALO - 1, G_TILE), :]
    ext_ref[0:HALO, :] = x3[G_TILE - HALO:, :]
    convo_ref[0] = x3[G_TILE - HALO:, :]

    u = _rms(x_ref[0], nmix_ref[...]).astype(BF16)
    qkvn_ref[...] = _dot(u, wq_ref[...])
    zn_ref[...] = _dot(u, wz_ref[...])
    ban_ref[...] = _dot(u, wb_ref[...])

    q, k, v, beta, glog = _gdn_prep(x0, x1, x2, x3, ba, conv_ref[...], alog_ref[...], dtb_ref[...])
    q_ref[...] = q
    k_ref[...] = k
    v_ref[...] = v
    be_ref[...] = beta
    gc_ref[...] = _mm(tri_ref[...], glog, _NN, 1, CUMSUM_TERMS)

    strict, incl, eye = _pair_masks()

    nchunk = G_TILE // CHUNK
    idx = [(pr, c) for pr in range(H_B // 2) for c in range(nchunk)]
    n = range(len(idx))
    rows = lambda c: slice(CHUNK * c, CHUNK * (c + 1))
    cat = lambda ref: [jnp.concatenate([ref[rows(c), DK * h:DK * (h + 1)] for h in (2 * pr, 2 * pr + 1)], axis=0)
                       for pr, c in idx]
    col = lambda ref, off: [jnp.concatenate([ref[rows(c), off + h:off + h + 1] for h in (2 * pr, 2 * pr + 1)], axis=0)
                            for pr, c in idx]
    qs, ks, vs = cat(q_ref), cat(k_ref), cat(v_ref)
    beta_c = col(be_ref, 0)
    gc = col(gc_ref, H_B)
    gt = [jnp.concatenate([jnp.broadcast_to(g_[CHUNK * (i + 1) - 1:CHUNK * (i + 1)], (CHUNK, 1)) for i in range(2)],
                          axis=0) for g_ in gc]
    decay = []
    for g_ in gc:
        gc_full = jnp.broadcast_to(g_, (PAIR, PAIR))
        diff = gc_full - gc_full.T
        decay.append(jnp.exp(diff))
    kb = [ks[i] * beta_c[i] for i in n]
    vb = [vs[i] * beta_c[i] for i in n]
    kq = [_mm(jnp.concatenate([kb[i], qs[i]], axis=0), ks[i], _NT) for i in n]
    lmat = [jnp.where(strict, kq[i][:PAIR] * decay[i], 0.0) for i in n]
    qk = [jnp.where(incl, kq[i][PAIR:] * decay[i], 0.0) for i in n]
    tinv = _neumann_inverse_many([-l for l in lmat], eye, INV_TERMS)
    eg = [jnp.exp(g_) for g_ in gc]
    uw = [_mm(tinv[i], jnp.concatenate([vb[i], kb[i] * eg[i]], axis=1)) for i in n]
    ow = [_mm(qk[i], uw[i]) for i in n]
    oc = [ow[i][:, :DV] for i in n]
    rq = [qs[i] * eg[i] - ow[i][:, DV:] for i in n]
    kd = [ks[i] * jnp.exp(gt[i] - gc[i]) for i in n]
    half = lambda j: slice(CHUNK * j, CHUNK * (j + 1))
    pq = [[_mm(kd[i][half(j)], uw[i][half(j)], _TN) for j in range(2)] for i in n]
    pm = [[eye * jnp.exp(gt[i][CHUNK * j:CHUNK * j + 1]) - pq[i][j][:, DV:] for j in range(2)] for i in n]
    s = [s_ref[h] for h in range(H_B)]
    for c in range(nchunk):
        for h in range(H_B):
            i, j = idx.index((h // 2, c)), h % 2
            o_ref[rows(c), DV * h:DV * (h + 1)] = _mm(rq[i][half(j)], s[h]) + oc[i][half(j)]
            s[h] = _mm(pm[i][j], s[h], _NN, CHAIN_TERMS, CHAIN_TERMS) + pq[i][j][:, :DV]
    for h in range(H_B):
        s_ref[h] = s[h]

    ob_ref[0] = _head_norm_gate(o_ref[...], z, gnorm_ref[...])
    gdn_ref[0] = s_ref[...]


def _gdn_prompt(x, nmix, wq, wz, wb, conv, alog, dtb, gnorm, tri):
    bsz, seq, _ = x.shape
    consts = (nmix, wq, wz, wb, conv, alog, dtb, gnorm, tri)
    tps, steps, load, emit, per_seq3, per_seq4 = _tile_maps(bsz, seq, G_TILE)
    return pl.pallas_call(
        functools.partial(_gdn_prompt_kernel, tps),
        grid=(steps,),
        in_specs=[pl.BlockSpec((1, G_TILE, D_MODEL), load)] + [_const_spec(c.shape) for c in consts],
        out_specs=[pl.BlockSpec((1, G_TILE, C_BV), emit), pl.BlockSpec((1, H_B, DK, DV), per_seq4),
                   pl.BlockSpec((1, HALO, CONV_CH), per_seq3)],
        out_shape=[jax.ShapeDtypeStruct((bsz, seq, C_BV), F32), jax.ShapeDtypeStruct((bsz, H_B, DK, DV), F32),
                   jax.ShapeDtypeStruct((bsz, HALO, CONV_CH), F32)],
        scratch_shapes=[pltpu.VMEM((G_TILE, CONV_CH), F32), pltpu.VMEM((G_TILE, C_BV), F32),
                        pltpu.VMEM((G_TILE, LANES), F32),
                        pltpu.VMEM((G_TILE + HALO, CONV_CH), F32), pltpu.VMEM((H_B, DK, DV), F32),
                        pltpu.VMEM((G_TILE, C_BK), F32), pltpu.VMEM((G_TILE, C_BK), F32),
                        pltpu.VMEM((G_TILE, C_BV), F32), pltpu.VMEM((G_TILE, LANES), F32),
                        pltpu.VMEM((G_TILE, LANES), F32), pltpu.VMEM((G_TILE, C_BV), F32)],
        compiler_params=pltpu.CompilerParams(dimension_semantics=("arbitrary",), vmem_limit_bytes=VMEM_LIMIT),
        name="gdn_prompt",
    )(x, *consts)


def _to_columns(x):
    pad = jnp.zeros((LANES - DEC_TILE, x.shape[1]), F32)
    return jnp.concatenate([x, pad], axis=0).T


def _decode_rwkv_kernel(pa_ref, shift_ref, wkv_ref,
                        mu_ref, w0_ref, w2_ref, a0_ref, a2_ref, g2_ref, kkw_ref, ka_ref, rk_ref, lnw_ref, lnb_ref,
                        hb_ref, oa_ref, wkvo_ref, tr_ref, yt_ref, g_ref, bonus_ref):
    h = pl.program_id(0)

    @pl.when(h == 0)
    def _():
        r, k2, v, logw, a, g, kk, bonus = _rwkv_prep(
            pa_ref[...], shift_ref[...], mu_ref[...], w0_ref[...], w2_ref[...], a0_ref[...], a2_ref[...],
            g2_ref[...], kkw_ref[...], ka_ref[...], rk_ref[...], hb_ref[...])
        for i, x in enumerate((-kk, jnp.exp(logw), kk * a, k2, r, v)):
            tr_ref[i] = x.T
        g_ref[...] = g
        bonus_ref[...] = bonus

    base = pl.multiple_of(h * HEAD_A, HEAD_A)
    hs = pl.ds(base, HEAD_A)
    a_t, w_t, b_t, k_t, r_t = (tr_ref[i, hs, :] for i in range(5))

    def value_row(vi, carry):
        st = wkv_ref[0, vi]
        sa = jnp.sum(st * a_t, axis=0, keepdims=True)
        st = st * w_t + sa * b_t + tr_ref[5, pl.ds(base + vi, 1), :] * k_t
        wkvo_ref[0, vi] = st
        yt_ref[pl.ds(base + vi, 1), :] = jnp.sum(st * r_t, axis=0, keepdims=True)
        return carry

    lax.fori_loop(0, HEAD_A, value_row, 0, unroll=8)

    @pl.when(h == H_A - 1)
    def _():
        oa_ref[...] = _group_norm_gate(yt_ref[...].T, bonus_ref[...], g_ref[...], lnw_ref[...], lnb_ref[...],
                                       hb_ref[...])


def _decode_rwkv(pa, shift, wkv_t, rw_consts):
    n = pa.shape[0]
    full = lambda w: pl.BlockSpec((n, w), lambda h: (0, 0))
    state = pl.BlockSpec((1, HEAD_A, HEAD_A, n), lambda h: (h, 0, 0, 0))
    return pl.pallas_call(
        _decode_rwkv_kernel,
        grid=(H_A,),
        in_specs=[full(A_COLS), full(A_COLS), state] + [_const_spec(c.shape) for c in rw_consts],
        out_specs=[full(C_A), state],
        out_shape=[jax.ShapeDtypeStruct((n, C_A), F32), jax.ShapeDtypeStruct(wkv_t.shape, F32)],
        scratch_shapes=[pltpu.VMEM((6, C_A, n), F32), pltpu.VMEM((C_A, n), F32),
                        pltpu.VMEM((n, C_A), F32), pltpu.VMEM((n, C_A), F32)],
        compiler_params=pltpu.CompilerParams(dimension_semantics=("arbitrary",), vmem_limit_bytes=VMEM_LIMIT),
        name="decode_rwkv",
    )(pa, shift, wkv_t, *rw_consts)


def _decode_gdn_kernel(qkv_ref, z_ref, ba_ref, cst_ref, gdn_ref, conv_ref, alog_ref, dtb_ref, gnorm_ref,
                       ob_ref, gdno_ref):
    cst = cst_ref[...]
    q, k, vv, beta, glog = _gdn_prep(cst[:, :CONV_CH], cst[:, CONV_CH:2 * CONV_CH], cst[:, 2 * CONV_CH:],
                                     qkv_ref[...], ba_ref[...], conv_ref[...], alog_ref[...], dtb_ref[...])
    eg = jnp.exp(glog)
    k_cols = _to_columns(k)
    ls = lambda h: slice(DK * h, DK * (h + 1))

    def split2(x):
        hi = x.astype(BF16)
        return hi, x - hi.astype(F32)

    rows_a, rows_b = [], []
    for h in range(H_B):
        e_h = eg[:, H_B + h:H_B + h + 1]
        (wh, wl), (gh, gl_) = split2(k[:, ls(h)] * (beta[:, h:h + 1] * e_h)), split2(q[:, ls(h)] * e_h)
        rows_a.append(jnp.concatenate([wh.astype(F32), gh.astype(F32), wl, gl_], axis=0).astype(BF16))
        rows_b.append(jnp.concatenate([wh.astype(F32), gh.astype(F32)], axis=0).astype(BF16))
    gunits = [(s_i, h) for s_i in range(DEC_TILE) for h in range(H_B)]
    gn = range(len(gunits))
    be = [beta[s_i:s_i + 1, h:h + 1] for s_i, h in gunits]
    e = [eg[s_i:s_i + 1, H_B + h:H_B + h + 1] for s_i, h in gunits]
    kc = [k_cols[ls(h), s_i:s_i + 1] for s_i, h in gunits]
    gst = [gdn_ref[s_i, h] for s_i, h in gunits]
    parts = [split2(st) for st in gst]
    ra = [_dot(rows_a[h], parts[i][0].astype(BF16)) + jnp.concatenate(
        [_dot(rows_b[h], parts[i][1].astype(BF16)), jnp.zeros((2 * DEC_TILE, DV), F32)], axis=0)
        for i, (s_i, h) in enumerate(gunits)]
    pick = lambda r, j, s_i: r[DEC_TILE * j + s_i:DEC_TILE * j + s_i + 1]
    ws = [pick(ra[i], 0, s_i) + pick(ra[i], 2, s_i) for i, (s_i, h) in enumerate(gunits)]
    qs = [pick(ra[i], 1, s_i) + pick(ra[i], 3, s_i) for i, (s_i, h) in enumerate(gunits)]
    v_new = [be[i] * vv[s_i:s_i + 1, ls(h)] - ws[i] for i, (s_i, h) in enumerate(gunits)]
    qk = [jnp.sum(q[s_i:s_i + 1, ls(h)] * k[s_i:s_i + 1, ls(h)], axis=-1, keepdims=True) for s_i, h in gunits]
    for i, (s_i, h) in enumerate(gunits):
        gdno_ref[s_i, h] = gst[i] * e[i] + kc[i] * v_new[i]
    o_units = [qs[i] + qk[i] * v_new[i] for i in gn]
    o = jnp.concatenate([jnp.concatenate(o_units[H_B * s_i:H_B * (s_i + 1)], axis=1) for s_i in range(DEC_TILE)],
                        axis=0)
    ob_ref[...] = _head_norm_gate(o, z_ref[...], gnorm_ref[...])


def _decode_gdn(qkv, z, ba, cst, gdn, gdn_consts):
    n = qkv.shape[0]
    row = lambda w: pl.BlockSpec((DEC_TILE, w), lambda i: (i, 0))
    state = pl.BlockSpec((DEC_TILE, H_B, DK, DV), lambda i: (i, 0, 0, 0))
    return pl.pallas_call(
        _decode_gdn_kernel,
        grid=(n // DEC_TILE,),
        in_specs=[row(CONV_CH), row(C_BV), row(LANES), row(3 * CONV_CH), state]
                 + [_const_spec(c.shape) for c in gdn_consts],
        out_specs=[row(C_BV), state],
        out_shape=[jax.ShapeDtypeStruct((n, C_BV), F32), jax.ShapeDtypeStruct(gdn.shape, F32)],
        compiler_params=pltpu.CompilerParams(dimension_semantics=("parallel",), vmem_limit_bytes=VMEM_LIMIT),
        name="decode_gdn",
    )(qkv, z, ba, cst, gdn, *gdn_consts)


def _post_kernel(x_ref, oa_ref, ob_ref, pe_ref, nmix_ref, wg_ref, wba_ref, wbb_ref, wout_ref, nffn_ref,
                 wfg_ref, wfu_ref, wfd_ref, nple_ref, wpg_ref, wpp_ref, nfin_ref, y_ref):
    nrow = x_ref.shape[0]
    ngroup = max(1, nrow // POST_GROUP_ROWS)
    groups = [slice(r0, r0 + nrow // ngroup) for r0 in range(0, nrow, nrow // ngroup)]
    x = [x_ref[g, :] for g in groups]
    ma = [_dot(oa_ref[g, :].astype(BF16), wba_ref[...]) for g in groups]
    mb = [_dot(ob_ref[g, :].astype(BF16), wbb_ref[...]) for g in groups]
    u = [_rms(xi, nmix_ref[...]).astype(BF16) for xi in x]
    gates = [_sigmoid(_dot(ui, wg_ref[...])) for ui in u]
    mix = [gi[:, :D_MODEL] * mai + gi[:, D_MODEL:] * mbi for gi, mai, mbi in zip(gates, ma, mb)]
    h = [xi + _dot(mi.astype(BF16), wout_ref[...]) for xi, mi in zip(x, mix)]
    pp = [_dot(pe_ref[g, :].astype(BF16), wpp_ref[...]) for g in groups]
    u2 = [_rms(hi, nffn_ref[...]).astype(BF16) for hi in h]
    fg = [_silu(_dot(ui, wfg_ref[...])) for ui in u2]
    ff = [fgi * _dot(ui, wfu_ref[...]) for fgi, ui in zip(fg, u2)]
    h = [hi + _dot(fi.astype(BF16), wfd_ref[...]) for hi, fi in zip(h, ff)]
    u3 = [_rms(hi, nple_ref[...]).astype(BF16) for hi in h]
    h = [hi + _sigmoid(_dot(ui, wpg_ref[...])) * ppi for hi, ui, ppi in zip(h, u3, pp)]
    for g, hi in zip(groups, h):
        y_ref[g, :] = _rms(hi, nfin_ref[...])


def _post(x, oa, ob, pe, consts):
    n = x.shape[0]
    tm = min(ROW_TILE, n)
    row = lambda w: pl.BlockSpec((tm, w), lambda i: (i, 0))
    return pl.pallas_call(
        _post_kernel,
        grid=(n // tm,),
        in_specs=[row(D_MODEL), row(C_A), row(C_BV), row(PLE_DIM)] + [_const_spec(c.shape) for c in consts],
        out_specs=row(D_MODEL),
        out_shape=jax.ShapeDtypeStruct((n, D_MODEL), F32),
        compiler_params=pltpu.CompilerParams(dimension_semantics=("parallel",), vmem_limit_bytes=VMEM_LIMIT),
        name="post",
    )(x, oa, ob, pe, *consts)


def _chunk_cumsum_matrix(tile):
    i = jnp.arange(tile)
    same = (i[:, None] // CHUNK) == (i[None, :] // CHUNK)
    return (same & (i[None, :] <= i[:, None])).astype(BF16)


def kernel(x_prompt, x_sample, p_prompt, p_sample, state_shift, state_wkv, state_conv, state_gdn, norm_mix, w_in, mu_shift, rw_w0, rw_w2, rw_a0, rw_a2, rw_g2, rw_kk, rw_ka, rw_rk, rw_ln_w, rw_ln_b, gdn_conv, gdn_a_log, gdn_dt_bias, gdn_norm, w_branch_a, w_branch_b, w_out, norm_ffn, w_ffn_gate, w_ffn_up, w_ffn_down, norm_ple, w_ple_gate, w_ple_proj, norm_final):
    bsz, seq, _ = x_prompt.shape
    nd = x_sample.shape[0]
    row = lambda p: p.reshape(1, -1)

    w_in0 = w_in[0]
    b0 = A_COLS
    wa = w_in0[:, :b0].astype(BF16)
    wq = w_in0[:, b0:b0 + CONV_CH].astype(BF16)
    wz = w_in0[:, b0 + CONV_CH:b0 + CONV_CH + C_BV].astype(BF16)
    wb = jnp.pad(w_in0[:, b0 + CONV_CH + C_BV:b0 + CONV_CH + C_BV + 2 * H_B], ((0, 0), (0, LANES - 2 * H_B))).astype(BF16)
    wg = w_in0[:, b0 + CONV_CH + C_BV + 2 * H_B:].astype(BF16)
    w2p = jnp.concatenate([rw_w2[0], jnp.zeros((LORA_A, C_A), F32)], axis=0).astype(BF16)
    a2p = jnp.concatenate([jnp.zeros((LORA_W, C_A), F32), rw_a2[0]], axis=0).astype(BF16)
    ch = jnp.arange(LANES) // HEAD_A
    hb = (ch[:, None] == ch[None, :]).astype(BF16)
    tri = _chunk_cumsum_matrix(T_TILE)
    tri_g = _chunk_cumsum_matrix(G_TILE)
    alog = jnp.pad(gdn_a_log[0], (H_B, LANES - 2 * H_B)).reshape(1, LANES)
    dtb = jnp.pad(gdn_dt_bias[0], (H_B, LANES - 2 * H_B)).reshape(1, LANES)
    rw_consts = (row(mu_shift[0]), row(rw_w0[0]), w2p, row(rw_a0[0]), a2p, rw_g2[0].astype(BF16), row(rw_kk[0]), row(rw_ka[0]),
                 row(rw_rk[0]), row(rw_ln_w[0]), row(rw_ln_b[0]), hb)
    gdn_consts = (gdn_conv[0], alog, dtb, row(gdn_norm[0]))
    post_consts = (row(norm_mix[0]), wg, w_branch_a[0].astype(BF16), w_branch_b[0].astype(BF16),
                   w_out[0].astype(BF16), row(norm_ffn[0]), w_ffn_gate[0].astype(BF16), w_ffn_up[0].astype(BF16),
                   w_ffn_down[0].astype(BF16), row(norm_ple[0]), w_ple_gate[0].astype(BF16),
                   w_ple_proj[0].astype(BF16), row(norm_final))

    xp = x_prompt.reshape(bsz * seq, D_MODEL)
    oa, wkv_pairs, shift_tail = _rwkv_prompt(x_prompt, row(norm_mix[0]), wa, *rw_consts, tri)
    ob, gdn_p, conv_tail = _gdn_prompt(x_prompt, row(norm_mix[0]), wq, wz, wb, *gdn_consts, tri_g)
    y_prompt = _post(xp, oa.reshape(bsz * seq, C_A), ob.reshape(bsz * seq, C_BV),
                     p_prompt[0].reshape(bsz * seq, PLE_DIM), post_consts).reshape(bsz, seq, D_MODEL)
    wkv_p = jnp.stack([wkv_pairs[:, :, :HEAD_A, :HEAD_A], wkv_pairs[:, :, HEAD_A:, HEAD_A:]], axis=2)
    wkv_p = wkv_p.reshape(bsz, H_A, HEAD_A, HEAD_A)
    shift_p = shift_tail[:, HALO - 1:, :]
    conv_p = conv_tail[:, HALO - (CONV_W - 1):, :]

    xs = x_sample.reshape(nd, D_MODEL)
    pa_s, qkv_s, z_s, ba_s = _inproj(xs, row(norm_mix[0]), wa, wq, wz, wb)
    cst = state_conv[0].reshape(nd, (CONV_W - 1) * CONV_CH)
    oa_s, wkv_t = _decode_rwkv(pa_s, state_shift[0].reshape(nd, A_COLS), jnp.transpose(state_wkv[0], (1, 2, 3, 0)),
                               rw_consts)
    wkv_s = jnp.transpose(wkv_t, (3, 0, 1, 2))
    ob_s, gdn_s = _decode_gdn(qkv_s, z_s, ba_s, cst, state_gdn[0], gdn_consts)
    y_sample = _post(xs, oa_s, ob_s, p_sample[0].reshape(nd, PLE_DIM), post_consts).reshape(nd, 1, D_MODEL)
    conv_s = jnp.concatenate([cst[:, CONV_CH:], qkv_s], axis=1).reshape(nd, CONV_W - 1, CONV_CH)

    return (y_prompt, y_sample, shift_p[None], wkv_p[None], conv_p[None], gdn_p[None],
            pa_s.reshape(1, nd, 1, A_COLS), wkv_s[None], conv_s[None], gdn_s[None])
```

```python
import functools

import jax
import jax.numpy as jnp
from jax import lax
from jax.experimental import pallas as pl
from jax.experimental.pallas import tpu as pltpu

F32 = jnp.float32
BF16 = jnp.bfloat16

D_MODEL = 1024
HEAD_A = 64
C_A = 512
H_A = 8
LORA_W = 64
LORA_A = 64
LORA_G = 128
A_COLS = 3 * C_A + LORA_W + LORA_A + LORA_G
DK = 128
DV = 128
H_B = 4
C_BK = 512
C_BV = 512
CONV_W = 4
CONV_CH = 2 * C_BK + C_BV
D_FF = 2816
PLE_DIM = 256
NORM_EPS = 1e-6
GN_EPS = 64e-5
L2_EPS = 1e-6

LANES = 128
CHUNK = 64
PAIR = 2 * CHUNK
T_TILE = 512
PROJ_COLS = 256
CHUNK_GROUP = 4
G_TILE = 512
ROW_TILE = 512
POST_GROUP_ROWS = 256
DEC_TILE = 8
HALO = 8
VMEM_LIMIT = 56 * 1024 * 1024
SUM_TERMS = 2
CUMSUM_TERMS = 3
INV_TERMS = 1
CHAIN_TERMS = 1


def _dot(a, b):
    return jnp.dot(a, b, preferred_element_type=F32)


def _sigmoid(x):
    return 1.0 / (1.0 + jnp.exp(-x))


def _silu(x):
    return x * _sigmoid(x)


def _softplus(x):
    return jnp.maximum(x, 0.0) + jnp.log(1.0 + jnp.exp(-jnp.abs(x)))


def _rms(x, gain):
    return x * lax.rsqrt(jnp.mean(x * x, axis=-1, keepdims=True) + NORM_EPS) * gain


def _pair_masks():
    ri = lax.broadcasted_iota(jnp.int32, (PAIR, PAIR), 0)
    ci = lax.broadcasted_iota(jnp.int32, (PAIR, PAIR), 1)
    same = (ri < CHUNK) == (ci < CHUNK)
    strict = same & (ci < ri)
    incl = same & (ci <= ri)
    eye = (ri == ci).astype(F32)
    return strict, incl, eye


_NN = (((1,), (0,)), ((), ()))
_NT = (((1,), (1,)), ((), ()))
_TN = (((0,), (0,)), ((), ()))


def _split(x, terms):
    if isinstance(x, (list, tuple)):
        return list(x)
    if x.dtype == BF16:
        return [x]
    parts = []
    for i in range(terms):
        h = x.astype(BF16)
        parts.append(h)
        if i + 1 < terms:
            x = x - h.astype(F32)
    return parts


def _mm(a, b, dims=_NN, na=1, nb=1):
    pa, pb = _split(a, na), _split(b, nb)
    acc = None
    for i, ai in enumerate(pa):
        for j, bj in enumerate(pb):
            if i + j < max(len(pa), len(pb)):
                d = lax.dot_general(ai, bj, dims, preferred_element_type=F32)
                acc = d if acc is None else acc + d
    return acc


def _neumann_inverse_many(lmats, eye, terms, between=lambda: None):
    ts = [eye + l for l in lmats]
    ps = [_split(l, terms) for l in lmats]
    n = 2
    while n < CHUNK:
        ps = [_split(_mm(p, p), terms) for p in ps]
        ts = [t + _mm(t, p, _NN, terms) for t, p in zip(ts, ps)]
        between()
        n *= 2
    return ts


def _inproj_kernel(x_ref, g_ref, wa_ref, wq_ref, wz_ref, wb_ref, pa_ref, qkv_ref, z_ref, ba_ref):
    u = _rms(x_ref[...], g_ref[...]).astype(BF16)
    pa_ref[...] = _dot(u, wa_ref[...])
    qkv_ref[...] = _dot(u, wq_ref[...])
    z_ref[...] = _dot(u, wz_ref[...])
    ba_ref[...] = _dot(u, wb_ref[...])


def _const_spec(shape):
    nd = len(shape)
    return pl.BlockSpec(shape, lambda *_: (0,) * nd, pipeline_mode=pl.Buffered(1))


def _inproj(x, gain, wa, wq, wz, wb):
    n = x.shape[0]
    tm = min(ROW_TILE, n)
    row = lambda w: pl.BlockSpec((tm, w), lambda i: (i, 0))
    return pl.pallas_call(
        _inproj_kernel,
        grid=(n // tm,),
        in_specs=[row(D_MODEL), _const_spec(gain.shape), _const_spec(wa.shape), _const_spec(wq.shape),
                  _const_spec(wz.shape), _const_spec(wb.shape)],
        out_specs=[row(A_COLS), row(CONV_CH), row(C_BV), row(LANES)],
        out_shape=[jax.ShapeDtypeStruct((n, A_COLS), F32), jax.ShapeDtypeStruct((n, CONV_CH), F32),
                   jax.ShapeDtypeStruct((n, C_BV), F32), jax.ShapeDtypeStruct((n, LANES), F32)],
        compiler_params=pltpu.CompilerParams(dimension_semantics=("parallel",), vmem_limit_bytes=VMEM_LIMIT),
        name="inproj",
    )(x, gain, wa, wq, wz, wb)


def _rwkv_prep(pa, prev, mu, w0, w2p, a0, a2p, g2, kkw, ka, rk, hb, between=lambda: None):
    xa = pa + (prev - pa) * mu
    r = xa[:, :C_A]
    k = xa[:, C_A:2 * C_A]
    v = xa[:, 2 * C_A:3 * C_A]
    xwa = xa[:, 3 * C_A:3 * C_A + LORA_W + LORA_A]
    xg = xa[:, 3 * C_A + LORA_W + LORA_A:]
    w_pre = _mm(jnp.tanh(xwa), w2p)
    a_pre = _mm(xwa, a2p)
    g = _mm(_sigmoid(xg), g2)
    kx = k * kkw
    kk_den = _head_sum(kx * kx, hb)
    between()
    w_log = -_softplus(-(w0 + w_pre)) - 0.5
    logw = -jnp.exp(w_log)
    a = _sigmoid(a0 + a_pre)
    kk = kx * lax.rsqrt(kk_den + L2_EPS)
    k2 = k * (1.0 + (a - 1.0) * ka)
    bonus_sum = _head_sum(r * k2 * rk, hb)
    between()
    bonus = bonus_sum * v
    return r, k2, v, logw, a, g, kk, bonus


def _head_sum(x, hb):
    groups = [_mm(x[:, LANES * j:LANES * (j + 1)], hb, _NN, SUM_TERMS) for j in range(x.shape[1] // LANES)]
    return jnp.concatenate(groups, axis=1)


def _group_norm_gate(y, bonus, g, lnw, lnb, hb):
    mean = _head_sum(y, hb) * (1.0 / HEAD_A)
    d = y - mean
    var = _head_sum(d * d, hb) * (1.0 / HEAD_A)
    yn = d * lax.rsqrt(var + GN_EPS) * lnw + lnb
    return (yn + bonus) * g


def _rwkv_prompt_kernel(tiles_per_seq, x_ref, nmix_ref, wa_ref,
                        mu_ref, w0_ref, w2_ref, a0_ref, a2_ref, g2_ref, kkw_ref, ka_ref, rk_ref,
                        lnw_ref, lnb_ref, hb_ref, tri_ref,
                        oa_ref, wkv_ref, shift_ref,
                        pa_ref, ext_ref, s_ref, at_ref, bt_ref, kt_ref, rt_ref, v_ref, bh_ref, kh_ref, y_ref):
    s_id = pl.program_id(0)

    @pl.when(s_id == 0)
    def _():
        pa_ref[...] = jnp.zeros_like(pa_ref)

    @pl.when(jnp.maximum(s_id - 1, 0) % tiles_per_seq == 0)
    def _():
        ext_ref[0:HALO, :] = jnp.zeros((HALO, A_COLS), F32)
        s_ref[...] = jnp.zeros_like(s_ref)

    pa = pa_ref[...]
    ext_ref[HALO:, :] = pa
    prev = ext_ref[pl.ds(HALO - 1, T_TILE), :]
    ext_ref[0:HALO, :] = pa[T_TILE - HALO:, :]
    shift_ref[0] = pa[T_TILE - HALO:, :]

    u_next = _rms(x_ref[0], nmix_ref[...]).astype(BF16)
    todo = list(range(0, A_COLS, PROJ_COLS))

    def project(count):
        for c0 in todo[:count]:
            pa_ref[:, c0:c0 + PROJ_COLS] = _dot(u_next, wa_ref[:, c0:c0 + PROJ_COLS])
        del todo[:count]

    hb = hb_ref[...]
    r, k2, v, logw, a, g, kk, bonus = _rwkv_prep(
        pa, prev, mu_ref[...], w0_ref[...], w2_ref[...], a0_ref[...], a2_ref[...], g2_ref[...],
        kkw_ref[...], ka_ref[...], rk_ref[...], hb, lambda: project(2))

    nchunk = T_TILE // CHUNK
    cs = _mm(tri_ref[...], logw, _NN, 1, CUMSUM_TERMS)
    project(len(todo))
    ends = [cs[CHUNK * (c + 1) - 1:CHUNK * (c + 1)] for c in range(nchunk)]
    tot = jnp.concatenate([jnp.broadcast_to(e, (CHUNK, C_A)) for e in ends], axis=0)
    dinv = jnp.exp(-cs)
    dend = jnp.exp(tot - cs)
    b_in = kk * a
    at_ref[...] = -kk * jnp.exp(cs - logw)
    bt_ref[...] = b_in * dinv
    kt_ref[...] = k2 * dinv
    rt_ref[...] = r * jnp.exp(cs)
    v_ref[...] = v
    bh_ref[...] = b_in * dend
    kh_ref[...] = k2 * dend
    gl = [jnp.exp(e) for e in ends]

    strict, incl, eye = _pair_masks()
    low = lax.broadcasted_iota(jnp.int32, (CHUNK, LANES), 1) < HEAD_A

    def stack(x):
        return jnp.concatenate([jnp.where(low, x, 0.0), jnp.where(low, 0.0, x)], axis=0)

    npair = C_A // LANES
    lanes = lambda p: slice(LANES * p, LANES * (p + 1))
    rows = lambda c: slice(CHUNK * c, CHUNK * (c + 1))
    s = [s_ref[p] for p in range(npair)]
    pending = []

    def chain_step(c, rp, yc, pm, qm):
        for p in range(npair):
            y = _mm(rp[p], s[p], _NT) + yc[p]
            y_ref[rows(c), lanes(p)] = y[:CHUNK] + y[CHUNK:]
            s[p] = _mm(s[p], pm[p], _NN, CHAIN_TERMS, CHAIN_TERMS) + qm[p]

    def emit_pending():
        if pending:
            chain_step(*pending.pop(0))

    for c0 in range(0, nchunk, CHUNK_GROUP):
        idx = [(p, c) for c in range(c0, c0 + CHUNK_GROUP) for p in range(npair)]
        ld = lambda ref: [stack(ref[rows(c), lanes(p)]) for p, c in idx]
        at, bt, kt, rt, vs, bh, kh = (ld(ref) for ref in (at_ref, bt_ref, kt_ref, rt_ref, v_ref, bh_ref, kh_ref))
        n = range(len(idx))
        aa = [_mm(jnp.concatenate([at[i], rt[i]], axis=0), jnp.concatenate([bt[i], kt[i]], axis=0), _NT) for i in n]
        a_ab = [jnp.where(strict, aa[i][:PAIR, :PAIR], 0.0) for i in n]
        a_ak = [jnp.where(strict, aa[i][:PAIR, PAIR:], 0.0) for i in n]
        a_rb = [jnp.where(incl, aa[i][PAIR:, :PAIR], 0.0) for i in n]
        a_rk = [jnp.where(incl, aa[i][PAIR:, PAIR:], 0.0) for i in n]
        tinv = _neumann_inverse_many(a_ab, eye, INV_TERMS, emit_pending)
        akv = [_mm(a_ak[i], vs[i]) for i in n]
        gu = [_mm(tinv[i], jnp.concatenate([at[i], akv[i]], axis=1)) for i in n]
        ry = [_mm(a_rb[i], gu[i]) for i in n]
        rkv = [_mm(a_rk[i], vs[i]) for i in n]
        pq = [_mm(gu[i], bh[i], _TN) for i in n]
        vk = [_mm(vs[i], kh[i], _TN) for i in n]
        rp = [rt[i] + ry[i][:, :PAIR] for i in n]
        yc = [ry[i][:, PAIR:] + rkv[i] for i in n]
        pm = [eye * gl[idx[i][1]][:, lanes(idx[i][0])] + pq[i][:PAIR] for i in n]
        qm = [pq[i][PAIR:] + vk[i] for i in n]
        while pending:
            emit_pending()
        for j, c in enumerate(range(c0, c0 + CHUNK_GROUP)):
            grp = slice(npair * j, npair * (j + 1))
            pending.append((c, rp[grp], yc[grp], pm[grp], qm[grp]))
    while pending:
        emit_pending()
    for p in range(npair):
        s_ref[p] = s[p]

    oa_ref[0] = _group_norm_gate(y_ref[...], bonus, g, lnw_ref[...], lnb_ref[...], hb)
    wkv_ref[0] = s_ref[...]


def _tile_maps(bsz, seq, tile):
    tps = seq // tile
    last = bsz * tps - 1
    cur = lambda s: jnp.minimum(s, last)
    prv = lambda s: jnp.maximum(s - 1, 0)
    load = lambda s: (cur(s) // tps, cur(s) % tps, 0)
    emit = lambda s: (prv(s) // tps, prv(s) % tps, 0)
    per_seq3 = lambda s: (prv(s) // tps, 0, 0)
    per_seq4 = lambda s: (prv(s) // tps, 0, 0, 0)
    return tps, last + 2, load, emit, per_seq3, per_seq4


def _rwkv_prompt(x, nmix, wa, mu, w0, w2p, a0, a2p, g2, kkw, ka, rk, lnw, lnb, hb, tri):
    bsz, seq, _ = x.shape
    consts = (nmix, wa, mu, w0, w2p, a0, a2p, g2, kkw, ka, rk, lnw, lnb, hb, tri)
    npair = C_A // LANES
    tps, steps, load, emit, per_seq3, per_seq4 = _tile_maps(bsz, seq, T_TILE)
    tile = lambda: pltpu.VMEM((T_TILE, C_A), F32)
    return pl.pallas_call(
        functools.partial(_rwkv_prompt_kernel, tps),
        grid=(steps,),
        in_specs=[pl.BlockSpec((1, T_TILE, D_MODEL), load)] + [_const_spec(c.shape) for c in consts],
        out_specs=[pl.BlockSpec((1, T_TILE, C_A), emit),
                   pl.BlockSpec((1, npair, LANES, LANES), per_seq4),
                   pl.BlockSpec((1, HALO, A_COLS), per_seq3)],
        out_shape=[jax.ShapeDtypeStruct((bsz, seq, C_A), F32),
                   jax.ShapeDtypeStruct((bsz, npair, LANES, LANES), F32),
                   jax.ShapeDtypeStruct((bsz, HALO, A_COLS), F32)],
        scratch_shapes=[pltpu.VMEM((T_TILE, A_COLS), F32),
                        pltpu.VMEM((T_TILE + HALO, A_COLS), F32), pltpu.VMEM((npair, LANES, LANES), F32)]
                       + [tile() for _ in range(8)],
        compiler_params=pltpu.CompilerParams(dimension_semantics=("arbitrary",), vmem_limit_bytes=VMEM_LIMIT),
        name="rwkv_prompt",
    )(x, *consts)


def _gdn_prep(x0, x1, x2, x3, ba, conv, alog, dtb):
    c = _silu(x0 * conv[0:1] + x1 * conv[1:2] + x2 * conv[2:3] + x3 * conv[3:4])
    qs, ks = [], []
    for h in range(H_B):
        qh = c[:, DK * h:DK * (h + 1)]
        kh = c[:, C_BK + DK * h:C_BK + DK * (h + 1)]
        qs.append(qh * lax.rsqrt(jnp.sum(qh * qh, axis=-1, keepdims=True) + L2_EPS) * (DK ** -0.5))
        ks.append(kh * lax.rsqrt(jnp.sum(kh * kh, axis=-1, keepdims=True) + L2_EPS))
    q = jnp.concatenate(qs, axis=1)
    k = jnp.concatenate(ks, axis=1)
    v = c[:, 2 * C_BK:]
    beta = _sigmoid(ba)
    glog = -jnp.exp(alog) * _softplus(ba + dtb)
    return q, k, v, beta, glog


def _head_norm_gate(o, z, gnorm):
    outs = []
    for h in range(H_B):
        oh = o[:, DV * h:DV * (h + 1)]
        zh = z[:, DV * h:DV * (h + 1)]
        oh = oh * lax.rsqrt(jnp.mean(oh * oh, axis=-1, keepdims=True) + NORM_EPS) * gnorm
        outs.append(oh * _silu(zh))
    return jnp.concatenate(outs, axis=1)


def _gdn_prompt_kernel(tiles_per_seq, x_ref, nmix_ref, wq_ref, wz_ref, wb_ref,
                       conv_ref, alog_ref, dtb_ref, gnorm_ref, tri_ref,
                       ob_ref, gdn_ref, convo_ref,
                       qkvn_ref, zn_ref, ban_ref, ext_ref, s_ref, q_ref, k_ref, v_ref, gc_ref, be_ref, o_ref):
    s_id = pl.program_id(0)

    @pl.when(s_id == 0)
    def _():
        qkvn_ref[...] = jnp.zeros_like(qkvn_ref)
        zn_ref[...] = jnp.zeros_like(zn_ref)
        ban_ref[...] = jnp.zeros_like(ban_ref)

    @pl.when(jnp.maximum(s_id - 1, 0) % tiles_per_seq == 0)
    def _():
        ext_ref[0:HALO, :] = jnp.zeros((HALO, CONV_CH), F32)
        s_ref[...] = jnp.zeros_like(s_ref)

    x3 = qkvn_ref[...]
    z = zn_ref[...]
    ba = ban_ref[...]
    ext_ref[HALO:, :] = x3
    x0 = ext_ref[pl.ds(HALO - 3, G_TILE), :]
    x1 = ext_ref[pl.ds(HALO - 2, G_TILE), :]
    x2 = ext_ref[pl.ds(HALO - 1, G_TILE), :]
    ext_ref[0:HALO, :] = x3[G_TILE - HALO:, :]
    convo_ref[0] = x3[G_TILE - HALO:, :]

    u = _rms(x_ref[0], nmix_ref[...]).astype(BF16)
    qkvn_ref[...] = _dot(u, wq_ref[...])
    zn_ref[...] = _dot(u, wz_ref[...])
    ban_ref[...] = _dot(u, wb_ref[...])

    q, k, v, beta, glog = _gdn_prep(x0, x1, x2, x3, ba, conv_ref[...], alog_ref[...], dtb_ref[...])
    q_ref[...] = q
    k_ref[...] = k
    v_ref[...] = v
    be_ref[...] = beta
    gc_ref[...] = _mm(tri_ref[...], glog, _NN, 1, CUMSUM_TERMS)

    strict, incl, eye = _pair_masks()

    nchunk = G_TILE // CHUNK
    idx = [(pr, c) for pr in range(H_B // 2) for c in range(nchunk)]
    n = range(len(idx))
    rows = lambda c: slice(CHUNK * c, CHUNK * (c + 1))
    cat = lambda ref: [jnp.concatenate([ref[rows(c), DK * h:DK * (h + 1)] for h in (2 * pr, 2 * pr + 1)], axis=0)
                       for pr, c in idx]
    col = lambda ref, off: [jnp.concatenate([ref[rows(c), off + h:off + h + 1] for h in (2 * pr, 2 * pr + 1)], axis=0)
                            for pr, c in idx]
    qs, ks, vs = cat(q_ref), cat(k_ref), cat(v_ref)
    beta_c = col(be_ref, 0)
    gc = col(gc_ref, H_B)
    gt = [jnp.concatenate([jnp.broadcast_to(g_[CHUNK * (i + 1) - 1:CHUNK * (i + 1)], (CHUNK, 1)) for i in range(2)],
                          axis=0) for g_ in gc]
    decay = []
    for g_ in gc:
        gc_full = jnp.broadcast_to(g_, (PAIR, PAIR))
        diff = gc_full - gc_full.T
        decay.append(jnp.exp(diff))
    kb = [ks[i] * beta_c[i] for i in n]
    vb = [vs[i] * beta_c[i] for i in n]
    kq = [_mm(jnp.concatenate([kb[i], qs[i]], axis=0), ks[i], _NT) for i in n]
    lmat = [jnp.where(strict, kq[i][:PAIR] * decay[i], 0.0) for i in n]
    qk = [jnp.where(incl, kq[i][PAIR:] * decay[i], 0.0) for i in n]
    tinv = _neumann_inverse_many([-l for l in lmat], eye, INV_TERMS)
    eg = [jnp.exp(g_) for g_ in gc]
    uw = [_mm(tinv[i], jnp.concatenate([vb[i], kb[i] * eg[i]], axis=1)) for i in n]
    ow = [_mm(qk[i], uw[i]) for i in n]
    oc = [ow[i][:, :DV] for i in n]
    rq = [qs[i] * eg[i] - ow[i][:, DV:] for i in n]
    kd = [ks[i] * jnp.exp(gt[i] - gc[i]) for i in n]
    half = lambda j: slice(CHUNK * j, CHUNK * (j + 1))
    pq = [[_mm(kd[i][half(j)], uw[i][half(j)], _TN) for j in range(2)] for i in n]
    pm = [[eye * jnp.exp(gt[i][CHUNK * j:CHUNK * j + 1]) - pq[i][j][:, DV:] for j in range(2)] for i in n]
    s = [s_ref[h] for h in range(H_B)]
    for c in range(nchunk):
        for h in range(H_B):
            i, j = idx.index((h // 2, c)), h % 2
            o_ref[rows(c), DV * h:DV * (h + 1)] = _mm(rq[i][half(j)], s[h]) + oc[i][half(j)]
            s[h] = _mm(pm[i][j], s[h], _NN, CHAIN_TERMS, CHAIN_TERMS) + pq[i][j][:, :DV]
    for h in range(H_B):
        s_ref[h] = s[h]

    ob_ref[0] = _head_norm_gate(o_ref[...], z, gnorm_ref[...])
    gdn_ref[0] = s_ref[...]


def _gdn_prompt(x, nmix, wq, wz, wb, conv, alog, dtb, gnorm, tri):
    bsz, seq, _ = x.shape
    consts = (nmix, wq, wz, wb, conv, alog, dtb, gnorm, tri)
    tps, steps, load, emit, per_seq3, per_seq4 = _tile_maps(bsz, seq, G_TILE)
    return pl.pallas_call(
        functools.partial(_gdn_prompt_kernel, tps),
        grid=(steps,),
        in_specs=[pl.BlockSpec((1, G_TILE, D_MODEL), load)] + [_const_spec(c.shape) for c in consts],
        out_specs=[pl.BlockSpec((1, G_TILE, C_BV), emit), pl.BlockSpec((1, H_B, DK, DV), per_seq4),
                   pl.BlockSpec((1, HALO, CONV_CH), per_seq3)],
        out_shape=[jax.ShapeDtypeStruct((bsz, seq, C_BV), F32), jax.ShapeDtypeStruct((bsz, H_B, DK, DV), F32),
                   jax.ShapeDtypeStruct((bsz, HALO, CONV_CH), F32)],
        scratch_shapes=[pltpu.VMEM((G_TILE, CONV_CH), F32), pltpu.VMEM((G_TILE, C_BV), F32),
                        pltpu.VMEM((G_TILE, LANES), F32),
                        pltpu.VMEM((G_TILE + HALO, CONV_CH), F32), pltpu.VMEM((H_B, DK, DV), F32),
                        pltpu.VMEM((G_TILE, C_BK), F32), pltpu.VMEM((G_TILE, C_BK), F32),
                        pltpu.VMEM((G_TILE, C_BV), F32), pltpu.VMEM((G_TILE, LANES), F32),
                        pltpu.VMEM((G_TILE, LANES), F32), pltpu.VMEM((G_TILE, C_BV), F32)],
        compiler_params=pltpu.CompilerParams(dimension_semantics=("arbitrary",), vmem_limit_bytes=VMEM_LIMIT),
        name="gdn_prompt",
    )(x, *consts)


def _to_columns(x):
    pad = jnp.zeros((LANES - DEC_TILE, x.shape[1]), F32)
    return jnp.concatenate([x, pad], axis=0).T


def _decode_rwkv_kernel(pa_ref, shift_ref, wkv_ref,
                        mu_ref, w0_ref, w2_ref, a0_ref, a2_ref, g2_ref, kkw_ref, ka_ref, rk_ref, lnw_ref, lnb_ref,
                        hb_ref, oa_ref, wkvo_ref, tr_ref, yt_ref, g_ref, bonus_ref):
    h = pl.program_id(0)

    @pl.when(h == 0)
    def _():
        r, k2, v, logw, a, g, kk, bonus = _rwkv_prep(
            pa_ref[...], shift_ref[...], mu_ref[...], w0_ref[...], w2_ref[...], a0_ref[...], a2_ref[...],
            g2_ref[...], kkw_ref[...], ka_ref[...], rk_ref[...], hb_ref[...])
        for i, x in enumerate((-kk, jnp.exp(logw), kk * a, k2, r, v)):
            tr_ref[i] = x.T
        g_ref[...] = g
        bonus_ref[...] = bonus

    base = pl.multiple_of(h * HEAD_A, HEAD_A)
    hs = pl.ds(base, HEAD_A)
    a_t, w_t, b_t, k_t, r_t = (tr_ref[i, hs, :] for i in range(5))

    def value_row(vi, carry):
        st = wkv_ref[0, vi]
        sa = jnp.sum(st * a_t, axis=0, keepdims=True)
        st = st * w_t + sa * b_t + tr_ref[5, pl.ds(base + vi, 1), :] * k_t
        wkvo_ref[0, vi] = st
        yt_ref[pl.ds(base + vi, 1), :] = jnp.sum(st * r_t, axis=0, keepdims=True)
        return carry

    lax.fori_loop(0, HEAD_A, value_row, 0, unroll=8)

    @pl.when(h == H_A - 1)
    def _():
        oa_ref[...] = _group_norm_gate(yt_ref[...].T, bonus_ref[...], g_ref[...], lnw_ref[...], lnb_ref[...],
                                       hb_ref[...])


def _decode_rwkv(pa, shift, wkv_t, rw_consts):
    n = pa.shape[0]
    full = lambda w: pl.BlockSpec((n, w), lambda h: (0, 0))
    state = pl.BlockSpec((1, HEAD_A, HEAD_A, n), lambda h: (h, 0, 0, 0))
    return pl.pallas_call(
        _decode_rwkv_kernel,
        grid=(H_A,),
        in_specs=[full(A_COLS), full(A_COLS), state] + [_const_spec(c.shape) for c in rw_consts],
        out_specs=[full(C_A), state],
        out_shape=[jax.ShapeDtypeStruct((n, C_A), F32), jax.ShapeDtypeStruct(wkv_t.shape, F32)],
        scratch_shapes=[pltpu.VMEM((6, C_A, n), F32), pltpu.VMEM((C_A, n), F32),
                        pltpu.VMEM((n, C_A), F32), pltpu.VMEM((n, C_A), F32)],
        compiler_params=pltpu.CompilerParams(dimension_semantics=("arbitrary",), vmem_limit_bytes=VMEM_LIMIT),
        name="decode_rwkv",
    )(pa, shift, wkv_t, *rw_consts)


def _decode_gdn_kernel(qkv_ref, z_ref, ba_ref, cst_ref, gdn_ref, conv_ref, alog_ref, dtb_ref, gnorm_ref,
                       ob_ref, gdno_ref):
    cst = cst_ref[...]
    q, k, vv, beta, glog = _gdn_prep(cst[:, :CONV_CH], cst[:, CONV_CH:2 * CONV_CH], cst[:, 2 * CONV_CH:],
                                     qkv_ref[...], ba_ref[...], conv_ref[...], alog_ref[...], dtb_ref[...])
    eg = jnp.exp(glog)
    k_cols = _to_columns(k)
    ls = lambda h: slice(DK * h, DK * (h + 1))

    def split2(x):
        hi = x.astype(BF16)
        return hi, x - hi.astype(F32)

    rows_a, rows_b = [], []
    for h in range(H_B):
        e_h = eg[:, H_B + h:H_B + h + 1]
        (wh, wl), (gh, gl_) = split2(k[:, ls(h)] * (beta[:, h:h + 1] * e_h)), split2(q[:, ls(h)] * e_h)
        rows_a.append(jnp.concatenate([wh.astype(F32), gh.astype(F32), wl, gl_], axis=0).astype(BF16))
        rows_b.append(jnp.concatenate([wh.astype(F32), gh.astype(F32)], axis=0).astype(BF16))
    gunits = [(s_i, h) for s_i in range(DEC_TILE) for h in range(H_B)]
    gn = range(len(gunits))
    be = [beta[s_i:s_i + 1, h:h + 1] for s_i, h in gunits]
    e = [eg[s_i:s_i + 1, H_B + h:H_B + h + 1] for s_i, h in gunits]
    kc = [k_cols[ls(h), s_i:s_i + 1] for s_i, h in gunits]
    gst = [gdn_ref[s_i, h] for s_i, h in gunits]
    parts = [split2(st) for st in gst]
    ra = [_dot(rows_a[h], parts[i][0].astype(BF16)) + jnp.concatenate(
        [_dot(rows_b[h], parts[i][1].astype(BF16)), jnp.zeros((2 * DEC_TILE, DV), F32)], axis=0)
        for i, (s_i, h) in enumerate(gunits)]
    pick = lambda r, j, s_i: r[DEC_TILE * j + s_i:DEC_TILE * j + s_i + 1]
    ws = [pick(ra[i], 0, s_i) + pick(ra[i], 2, s_i) for i, (s_i, h) in enumerate(gunits)]
    qs = [pick(ra[i], 1, s_i) + pick(ra[i], 3, s_i) for i, (s_i, h) in enumerate(gunits)]
    v_new = [be[i] * vv[s_i:s_i + 1, ls(h)] - ws[i] for i, (s_i, h) in enumerate(gunits)]
    qk = [jnp.sum(q[s_i:s_i + 1, ls(h)] * k[s_i:s_i + 1, ls(h)], axis=-1, keepdims=True) for s_i, h in gunits]
    for i, (s_i, h) in enumerate(gunits):
        gdno_ref[s_i, h] = gst[i] * e[i] + kc[i] * v_new[i]
    o_units = [qs[i] + qk[i] * v_new[i] for i in gn]
    o = jnp.concatenate([jnp.concatenate(o_units[H_B * s_i:H_B * (s_i + 1)], axis=1) for s_i in range(DEC_TILE)],
                        axis=0)
    ob_ref[...] = _head_norm_gate(o, z_ref[...], gnorm_ref[...])


def _decode_gdn(qkv, z, ba, cst, gdn, gdn_consts):
    n = qkv.shape[0]
    row = lambda w: pl.BlockSpec((DEC_TILE, w), lambda i: (i, 0))
    state = pl.BlockSpec((DEC_TILE, H_B, DK, DV), lambda i: (i, 0, 0, 0))
    return pl.pallas_call(
        _decode_gdn_kernel,
        grid=(n // DEC_TILE,),
        in_specs=[row(CONV_CH), row(C_BV), row(LANES), row(3 * CONV_CH), state]
                 + [_const_spec(c.shape) for c in gdn_consts],
        out_specs=[row(C_BV), state],
        out_shape=[jax.ShapeDtypeStruct((n, C_BV), F32), jax.ShapeDtypeStruct(gdn.shape, F32)],
        compiler_params=pltpu.CompilerParams(dimension_semantics=("parallel",), vmem_limit_bytes=VMEM_LIMIT),
        name="decode_gdn",
    )(qkv, z, ba, cst, gdn, *gdn_consts)


def _post_kernel(x_ref, oa_ref, ob_ref, pe_ref, nmix_ref, wg_ref, wba_ref, wbb_ref, wout_ref, nffn_ref,
                 wfg_ref, wfu_ref, wfd_ref, nple_ref, wpg_ref, wpp_ref, nfin_ref, y_ref):
    nrow = x_ref.shape[0]
    ngroup = max(1, nrow // POST_GROUP_ROWS)
    groups = [slice(r0, r0 + nrow // ngroup) for r0 in range(0, nrow, nrow // ngroup)]
    x = [x_ref[g, :] for g in groups]
    ma = [_dot(oa_ref[g, :].astype(BF16), wba_ref[...]) for g in groups]
    mb = [_dot(ob_ref[g, :].astype(BF16), wbb_ref[...]) for g in groups]
    u = [_rms(xi, nmix_ref[...]).astype(BF16) for xi in x]
    gates = [_sigmoid(_dot(ui, wg_ref[...])) for ui in u]
    mix = [gi[:, :D_MODEL] * mai + gi[:, D_MODEL:] * mbi for gi, mai, mbi in zip(gates, ma, mb)]
    h = [xi + _dot(mi.astype(BF16), wout_ref[...]) for xi, mi in zip(x, mix)]
    pp = [_dot(pe_ref[g, :].astype(BF16), wpp_ref[...]) for g in groups]
    u2 = [_rms(hi, nffn_ref[...]).astype(BF16) for hi in h]
    fg = [_silu(_dot(ui, wfg_ref[...])) for ui in u2]
    ff = [fgi * _dot(ui, wfu_ref[...]) for fgi, ui in zip(fg, u2)]
    h = [hi + _dot(fi.astype(BF16), wfd_ref[...]) for hi, fi in zip(h, ff)]
    u3 = [_rms(hi, nple_ref[...]).astype(BF16) for hi in h]
    h = [hi + _sigmoid(_dot(ui, wpg_ref[...])) * ppi for hi, ui, ppi in zip(h, u3, pp)]
    for g, hi in zip(groups, h):
        y_ref[g, :] = _rms(hi, nfin_ref[...])


def _post(x, oa, ob, pe, consts):
    n = x.shape[0]
    tm = min(ROW_TILE, n)
    row = lambda w: pl.BlockSpec((tm, w), lambda i: (i, 0))
    return pl.pallas_call(
        _post_kernel,
        grid=(n // tm,),
        in_specs=[row(D_MODEL), row(C_A), row(C_BV), row(PLE_DIM)] + [_const_spec(c.shape) for c in consts],
        out_specs=row(D_MODEL),
        out_shape=jax.ShapeDtypeStruct((n, D_MODEL), F32),
        compiler_params=pltpu.CompilerParams(dimension_semantics=("parallel",), vmem_limit_bytes=VMEM_LIMIT),
        name="post",
    )(x, oa, ob, pe, *consts)


def _chunk_cumsum_matrix(tile):
    i = jnp.arange(tile)
    same = (i[:, None] // CHUNK) == (i[None, :] // CHUNK)
    return (same & (i[None, :] <= i[:, None])).astype(BF16)


def kernel(x_prompt, x_sample, p_prompt, p_sample, state_shift, state_wkv, state_conv, state_gdn, norm_mix, w_in, mu_shift, rw_w0, rw_w2, rw_a0, rw_a2, rw_g2, rw_kk, rw_ka, rw_rk, rw_ln_w, rw_ln_b, gdn_conv, gdn_a_log, gdn_dt_bias, gdn_norm, w_branch_a, w_branch_b, w_out, norm_ffn, w_ffn_gate, w_ffn_up, w_ffn_down, norm_ple, w_ple_gate, w_ple_proj, norm_final):
    bsz, seq, _ = x_prompt.shape
    nd = x_sample.shape[0]
    row = lambda p: p.reshape(1, -1)

    w_in0 = w_in[0]
    b0 = A_COLS
    wa = w_in0[:, :b0].astype(BF16)
    wq = w_in0[:, b0:b0 + CONV_CH].astype(BF16)
    wz = w_in0[:, b0 + CONV_CH:b0 + CONV_CH + C_BV].astype(BF16)
    wb = jnp.pad(w_in0[:, b0 + CONV_CH + C_BV:b0 + CONV_CH + C_BV + 2 * H_B], ((0, 0), (0, LANES - 2 * H_B))).astype(BF16)
    wg = w_in0[:, b0 + CONV_CH + C_BV + 2 * H_B:].astype(BF16)
    w2p = jnp.concatenate([rw_w2[0], jnp.zeros((LORA_A, C_A), F32)], axis=0).astype(BF16)
    a2p = jnp.concatenate([jnp.zeros((LORA_W, C_A), F32), rw_a2[0]], axis=0).astype(BF16)
    ch = jnp.arange(LANES) // HEAD_A
    hb = (ch[:, None] == ch[None, :]).astype(BF16)
    tri = _chunk_cumsum_matrix(T_TILE)
    tri_g = _chunk_cumsum_matrix(G_TILE)
    alog = jnp.pad(gdn_a_log[0], (H_B, LANES - 2 * H_B)).reshape(1, LANES)
    dtb = jnp.pad(gdn_dt_bias[0], (H_B, LANES - 2 * H_B)).reshape(1, LANES)
    rw_consts = (row(mu_shift[0]), row(rw_w0[0]), w2p, row(rw_a0[0]), a2p, rw_g2[0].astype(BF16), row(rw_kk[0]), row(rw_ka[0]),
                 row(rw_rk[0]), row(rw_ln_w[0]), row(rw_ln_b[0]), hb)
    gdn_consts = (gdn_conv[0], alog, dtb, row(gdn_norm[0]))
    post_consts = (row(norm_mix[0]), wg, w_branch_a[0].astype(BF16), w_branch_b[0].astype(BF16),
                   w_out[0].astype(BF16), row(norm_ffn[0]), w_ffn_gate[0].astype(BF16), w_ffn_up[0].astype(BF16),
                   w_ffn_down[0].astype(BF16), row(norm_ple[0]), w_ple_gate[0].astype(BF16),
                   w_ple_proj[0].astype(BF16), row(norm_final))

    xp = x_prompt.reshape(bsz * seq, D_MODEL)
    oa, wkv_pairs, shift_tail = _rwkv_prompt(x_prompt, row(norm_mix[0]), wa, *rw_consts, tri)
    ob, gdn_p, conv_tail = _gdn_prompt(x_prompt, row(norm_mix[0]), wq, wz, wb, *gdn_consts, tri_g)
    y_prompt = _post(xp, oa.reshape(bsz * seq, C_A), ob.reshape(bsz * seq, C_BV),
                     p_prompt[0].reshape(bsz * seq, PLE_DIM), post_consts).reshape(bsz, seq, D_MODEL)
    wkv_p = jnp.stack([wkv_pairs[:, :, :HEAD_A, :HEAD_A], wkv_pairs[:, :, HEAD_A:, HEAD_A:]], axis=2)
    wkv_p = wkv_p.reshape(bsz, H_A, HEAD_A, HEAD_A)
    shift_p = shift_tail[:, HALO - 1:, :]
    conv_p = conv_tail[:, HALO - (CONV_W - 1):, :]

    xs = x_sample.reshape(nd, D_MODEL)
    pa_s, qkv_s, z_s, ba_s = _inproj(xs, row(norm_mix[0]), wa, wq, wz, wb)
    cst = state_conv[0].reshape(nd, (CONV_W - 1) * CONV_CH)
    oa_s, wkv_t = _decode_rwkv(pa_s, state_shift[0].reshape(nd, A_COLS), jnp.transpose(state_wkv[0], (1, 2, 3, 0)),
                               rw_consts)
    wkv_s = jnp.transpose(wkv_t, (3, 0, 1, 2))
    ob_s, gdn_s = _decode_gdn(qkv_s, z_s, ba_s, cst, state_gdn[0], gdn_consts)
    y_sample = _post(xs, oa_s, ob_s, p_sample[0].reshape(nd, PLE_DIM), post_consts).reshape(nd, 1, D_MODEL)
    conv_s = jnp.concatenate([cst[:, CONV_CH:], qkv_s], axis=1).reshape(nd, CONV_W - 1, CONV_CH)

    return (y_prompt, y_sample, shift_p[None], wkv_p[None], conv_p[None], gdn_p[None],
            pa_s.reshape(1, nd, 1, A_COLS), wkv_s[None], conv_s[None], gdn_s[None])
```

```python
import functools

import jax
import jax.numpy as jnp
from jax import lax
from jax.experimental import pallas as pl
from jax.experimental.pallas import tpu as pltpu

F32 = jnp.float32
BF16 = jnp.bfloat16

D_MODEL = 1024
HEAD_A = 64
C_A = 512
H_A = 8
LORA_W = 64
LORA_A = 64
LORA_G = 128
A_COLS = 3 * C_A + LORA_W + LORA_A + LORA_G
DK = 128
DV = 128
H_B = 4
C_BK = 512
C_BV = 512
CONV_W = 4
CONV_CH = 2 * C_BK + C_BV
D_FF = 2816
PLE_DIM = 256
NORM_EPS = 1e-6
GN_EPS = 64e-5
L2_EPS = 1e-6

LANES = 128
CHUNK = 64
PAIR = 2 * CHUNK
T_TILE = 256
PROJ_COLS = 256
ROW_TILE = 512
POST_GROUP_ROWS = 256
DEC_TILE = 8
HALO = 8
VMEM_LIMIT = 56 * 1024 * 1024
SUM_TERMS = 2
CUMSUM_TERMS = 3
INV_TERMS = 1
CHAIN_TERMS = 1


def _dot(a, b):
    return jnp.dot(a, b, preferred_element_type=F32)


def _sigmoid(x):
    return 1.0 / (1.0 + jnp.exp(-x))


def _silu(x):
    return x * _sigmoid(x)


def _softplus(x):
    return jnp.maximum(x, 0.0) + jnp.log(1.0 + jnp.exp(-jnp.abs(x)))


def _rms(x, gain):
    return x * lax.rsqrt(jnp.mean(x * x, axis=-1, keepdims=True) + NORM_EPS) * gain


def _pair_masks():
    ri = lax.broadcasted_iota(jnp.int32, (PAIR, PAIR), 0)
    ci = lax.broadcasted_iota(jnp.int32, (PAIR, PAIR), 1)
    same = (ri < CHUNK) == (ci < CHUNK)
    strict = same & (ci < ri)
    incl = same & (ci <= ri)
    eye = (ri == ci).astype(F32)
    return strict, incl, eye


_NN = (((1,), (0,)), ((), ()))
_NT = (((1,), (1,)), ((), ()))
_TN = (((0,), (0,)), ((), ()))


def _split(x, terms):
    if isinstance(x, (list, tuple)):
        return list(x)
    if x.dtype == BF16:
        return [x]
    parts = []
    for i in range(terms):
        h = x.astype(BF16)
        parts.append(h)
        if i + 1 < terms:
            x = x - h.astype(F32)
    return parts


def _mm(a, b, dims=_NN, na=1, nb=1):
    pa, pb = _split(a, na), _split(b, nb)
    acc = None
    for i, ai in enumerate(pa):
        for j, bj in enumerate(pb):
            if i + j < max(len(pa), len(pb)):
                d = lax.dot_general(ai, bj, dims, preferred_element_type=F32)
                acc = d if acc is None else acc + d
    return acc


def _neumann_inverse_many(lmats, eye, terms):
    ts = [eye + l for l in lmats]
    ps = [_split(l, terms) for l in lmats]
    n = 2
    while n < CHUNK:
        ps = [_split(_mm(p, p), terms) for p in ps]
        ts = [t + _mm(t, p, _NN, terms) for t, p in zip(ts, ps)]
        n *= 2
    return ts


def _inproj_kernel(x_ref, g_ref, wa_ref, wq_ref, wz_ref, wb_ref, pa_ref, qkv_ref, z_ref, ba_ref):
    u = _rms(x_ref[...], g_ref[...]).astype(BF16)
    pa_ref[...] = _dot(u, wa_ref[...])
    qkv_ref[...] = _dot(u, wq_ref[...])
    z_ref[...] = _dot(u, wz_ref[...])
    ba_ref[...] = _dot(u, wb_ref[...])


def _const_spec(shape):
    nd = len(shape)
    return pl.BlockSpec(shape, lambda *_: (0,) * nd, pipeline_mode=pl.Buffered(1))


def _inproj(x, gain, wa, wq, wz, wb):
    n = x.shape[0]
    tm = min(ROW_TILE, n)
    row = lambda w: pl.BlockSpec((tm, w), lambda i: (i, 0))
    return pl.pallas_call(
        _inproj_kernel,
        grid=(n // tm,),
        in_specs=[row(D_MODEL), _const_spec(gain.shape), _const_spec(wa.shape), _const_spec(wq.shape),
                  _const_spec(wz.shape), _const_spec(wb.shape)],
        out_specs=[row(A_COLS), row(CONV_CH), row(C_BV), row(LANES)],
        out_shape=[jax.ShapeDtypeStruct((n, A_COLS), F32), jax.ShapeDtypeStruct((n, CONV_CH), F32),
                   jax.ShapeDtypeStruct((n, C_BV), F32), jax.ShapeDtypeStruct((n, LANES), F32)],
        compiler_params=pltpu.CompilerParams(dimension_semantics=("parallel",), vmem_limit_bytes=VMEM_LIMIT),
        name="inproj",
    )(x, gain, wa, wq, wz, wb)


def _rwkv_prep(pa, prev, mu, w0, w2p, a0, a2p, g2, kkw, ka, rk, hb, between=lambda: None):
    xa = pa + (prev - pa) * mu
    r = xa[:, :C_A]
    k = xa[:, C_A:2 * C_A]
    v = xa[:, 2 * C_A:3 * C_A]
    xwa = xa[:, 3 * C_A:3 * C_A + LORA_W + LORA_A]
    xg = xa[:, 3 * C_A + LORA_W + LORA_A:]
    w_pre = _mm(jnp.tanh(xwa), w2p)
    a_pre = _mm(xwa, a2p)
    g = _mm(_sigmoid(xg), g2)
    kx = k * kkw
    kk_den = _head_sum(kx * kx, hb)
    between()
    w_log = -_softplus(-(w0 + w_pre)) - 0.5
    logw = -jnp.exp(w_log)
    a = _sigmoid(a0 + a_pre)
    kk = kx * lax.rsqrt(kk_den + L2_EPS)
    k2 = k * (1.0 + (a - 1.0) * ka)
    bonus_sum = _head_sum(r * k2 * rk, hb)
    between()
    bonus = bonus_sum * v
    return r, k2, v, logw, a, g, kk, bonus


def _head_sum(x, hb):
    groups = [_mm(x[:, LANES * j:LANES * (j + 1)], hb, _NN, SUM_TERMS) for j in range(x.shape[1] // LANES)]
    return jnp.concatenate(groups, axis=1)


def _group_norm_gate(y, bonus, g, lnw, lnb, hb):
    mean = _head_sum(y, hb) * (1.0 / HEAD_A)
    d = y - mean
    var = _head_sum(d * d, hb) * (1.0 / HEAD_A)
    yn = d * lax.rsqrt(var + GN_EPS) * lnw + lnb
    return (yn + bonus) * g


def _gdn_prep(x0, x1, x2, x3, ba, conv, alog, dtb):
    c = _silu(x0 * conv[0:1] + x1 * conv[1:2] + x2 * conv[2:3] + x3 * conv[3:4])
    qs, ks = [], []
    for h in range(H_B):
        qh = c[:, DK * h:DK * (h + 1)]
        kh = c[:, C_BK + DK * h:C_BK + DK * (h + 1)]
        qs.append(qh * lax.rsqrt(jnp.sum(qh * qh, axis=-1, keepdims=True) + L2_EPS) * (DK ** -0.5))
        ks.append(kh * lax.rsqrt(jnp.sum(kh * kh, axis=-1, keepdims=True) + L2_EPS))
    q = jnp.concatenate(qs, axis=1)
    k = jnp.concatenate(ks, axis=1)
    v = c[:, 2 * C_BK:]
    beta = _sigmoid(ba)
    glog = -jnp.exp(alog) * _softplus(ba + dtb)
    return q, k, v, beta, glog


def _head_norm_gate(o, z, gnorm):
    outs = []
    for h in range(H_B):
        oh = o[:, DV * h:DV * (h + 1)]
        zh = z[:, DV * h:DV * (h + 1)]
        oh = oh * lax.rsqrt(jnp.mean(oh * oh, axis=-1, keepdims=True) + NORM_EPS) * gnorm
        outs.append(oh * _silu(zh))
    return jnp.concatenate(outs, axis=1)


def _tile_maps(bsz, seq, tile):
    tps = seq // tile
    last = bsz * tps - 1
    cur = lambda s: jnp.minimum(s, last)
    prv = lambda s: jnp.maximum(s - 1, 0)
    load = lambda s: (cur(s) // tps, cur(s) % tps, 0)
    emit = lambda s: (prv(s) // tps, prv(s) % tps, 0)
    per_seq3 = lambda s: (prv(s) // tps, 0, 0)
    per_seq4 = lambda s: (prv(s) // tps, 0, 0, 0)
    return tps, last + 2, load, emit, per_seq3, per_seq4


def _gdn_prep_groups(x0, x1, x2, x3, conv):
    def conv_cols(c0, c1):
        return _silu(x0[:, c0:c1] * conv[0:1, c0:c1] + x1[:, c0:c1] * conv[1:2, c0:c1]
                     + x2[:, c0:c1] * conv[2:3, c0:c1] + x3[:, c0:c1] * conv[3:4, c0:c1])

    def unit(xh):
        return xh * lax.rsqrt(jnp.sum(xh * xh, axis=-1, keepdims=True) + L2_EPS)

    for h in range(H_B):
        yield 0, h, unit(conv_cols(DK * h, DK * (h + 1))) * (DK ** -0.5)
        yield 1, h, unit(conv_cols(C_BK + DK * h, C_BK + DK * (h + 1)))
        yield 2, h, conv_cols(2 * C_BK + DV * h, 2 * C_BK + DV * (h + 1))


def _mixers_prompt_kernel(tiles_per_seq, x_ref, nmix_ref, wa_ref, wq_ref, wz_ref, wb_ref,
                          mu_ref, w0_ref, w2_ref, a0_ref, a2_ref, g2_ref, kkw_ref, ka_ref, rk_ref,
                          lnw_ref, lnb_ref, hb_ref, conv_ref, alog_ref, dtb_ref, gnorm_ref, tri_ref,
                          oa_ref, wkv_ref, shift_ref, ob_ref, gdn_ref, convo_ref,
                          pa_ref, exta_ref, sa_ref, at_ref, bt_ref, kt_ref, rt_ref, va_ref, bh_ref, kh_ref, y_ref,
                          qkvn_ref, zn_ref, ban_ref, extb_ref, sb_ref, q_ref, k_ref, vb_ref, gc_ref, be_ref, o_ref):
    s_id = pl.program_id(0)

    @pl.when(s_id == 0)
    def _():
        pa_ref[...] = jnp.zeros_like(pa_ref)
        qkvn_ref[...] = jnp.zeros_like(qkvn_ref)
        zn_ref[...] = jnp.zeros_like(zn_ref)
        ban_ref[...] = jnp.zeros_like(ban_ref)

    @pl.when(jnp.maximum(s_id - 1, 0) % tiles_per_seq == 0)
    def _():
        exta_ref[0:HALO, :] = jnp.zeros((HALO, A_COLS), F32)
        extb_ref[0:HALO, :] = jnp.zeros((HALO, CONV_CH), F32)
        sa_ref[...] = jnp.zeros_like(sa_ref)
        sb_ref[...] = jnp.zeros_like(sb_ref)

    pa = pa_ref[...]
    exta_ref[HALO:, :] = pa
    prev = exta_ref[pl.ds(HALO - 1, T_TILE), :]
    exta_ref[0:HALO, :] = pa[T_TILE - HALO:, :]
    shift_ref[0] = pa[T_TILE - HALO:, :]
    x3 = qkvn_ref[...]
    z = zn_ref[...]
    ba = ban_ref[...]
    extb_ref[HALO:, :] = x3
    x0 = extb_ref[pl.ds(HALO - 3, T_TILE), :]
    x1 = extb_ref[pl.ds(HALO - 2, T_TILE), :]
    x2 = extb_ref[pl.ds(HALO - 1, T_TILE), :]
    extb_ref[0:HALO, :] = x3[T_TILE - HALO:, :]
    convo_ref[0] = x3[T_TILE - HALO:, :]

    u_next = _rms(x_ref[0], nmix_ref[...]).astype(BF16)
    blocks = [(w_ref, dst_ref, c0, min(c0 + PROJ_COLS, w_ref.shape[1]))
              for w_ref, dst_ref in ((wa_ref, pa_ref), (wq_ref, qkvn_ref), (wz_ref, zn_ref), (wb_ref, ban_ref))
              for c0 in range(0, w_ref.shape[1], PROJ_COLS)]

    def project(count):
        for w_ref, dst_ref, c0, c1 in blocks[:count]:
            dst_ref[:, c0:c1] = _dot(u_next, w_ref[:, c0:c1])
        del blocks[:count]

    gdn_groups = _gdn_prep_groups(x0, x1, x2, x3, conv_ref[...])
    gdn_dst = (q_ref, k_ref, vb_ref)

    def gdn_prep(count):
        for _ in range(count):
            item = next(gdn_groups, None)
            if item is not None:
                kind, h, arr = item
                gdn_dst[kind][:, DK * h:DK * (h + 1)] = arr

    def between():
        project(3)
        gdn_prep(3)

    hb = hb_ref[...]
    r, k2, v, logw, a, g, kk, bonus = _rwkv_prep(
        pa, prev, mu_ref[...], w0_ref[...], w2_ref[...], a0_ref[...], a2_ref[...], g2_ref[...],
        kkw_ref[...], ka_ref[...], rk_ref[...], hb, between)

    nchunk = T_TILE // CHUNK
    cs = _mm(tri_ref[...], logw, _NN, 1, CUMSUM_TERMS)
    between()
    ends = [cs[CHUNK * (c + 1) - 1:CHUNK * (c + 1)] for c in range(nchunk)]
    tot = jnp.concatenate([jnp.broadcast_to(e, (CHUNK, C_A)) for e in ends], axis=0)
    dinv = jnp.exp(-cs)
    dend = jnp.exp(tot - cs)
    b_in = kk * a
    at_ref[...] = -kk * jnp.exp(cs - logw)
    bt_ref[...] = b_in * dinv
    kt_ref[...] = k2 * dinv
    between()
    rt_ref[...] = r * jnp.exp(cs)
    va_ref[...] = v
    bh_ref[...] = b_in * dend
    kh_ref[...] = k2 * dend
    gl = [jnp.exp(e) for e in ends]
    project(len(blocks))
    gdn_prep(3 * H_B)
    be_ref[...] = _sigmoid(ba)
    glog = -jnp.exp(alog_ref[...]) * _softplus(ba + dtb_ref[...])
    gc_ref[...] = _mm(tri_ref[...], glog, _NN, 1, CUMSUM_TERMS)

    strict, incl, eye = _pair_masks()
    low = lax.broadcasted_iota(jnp.int32, (CHUNK, LANES), 1) < HEAD_A

    def stack(x):
        return jnp.concatenate([jnp.where(low, x, 0.0), jnp.where(low, 0.0, x)], axis=0)

    npair = C_A // LANES
    lanes = lambda p: slice(LANES * p, LANES * (p + 1))
    rows = lambda c: slice(CHUNK * c, CHUNK * (c + 1))
    half = lambda j: slice(CHUNK * j, CHUNK * (j + 1))
    ia = [(p, c) for p in range(npair) for c in range(nchunk)]
    ib = [(pr, c) for pr in range(H_B // 2) for c in range(nchunk)]
    na, nb = range(len(ia)), range(len(ib))
    ld = lambda ref: [stack(ref[rows(c), lanes(p)]) for p, c in ia]
    at, bt, kt, rt, vs, bh, kh = (ld(ref) for ref in (at_ref, bt_ref, kt_ref, rt_ref, va_ref, bh_ref, kh_ref))
    cat = lambda ref: [jnp.concatenate([ref[rows(c), DK * h:DK * (h + 1)] for h in (2 * pr, 2 * pr + 1)], axis=0)
                       for pr, c in ib]
    col = lambda ref, off: [jnp.concatenate([ref[rows(c), off + h:off + h + 1] for h in (2 * pr, 2 * pr + 1)], axis=0)
                            for pr, c in ib]
    qg_, kg_, vg_ = cat(q_ref), cat(k_ref), cat(vb_ref)
    beta_c = col(be_ref, 0)
    gc = col(gc_ref, H_B)
    gt = [jnp.concatenate([jnp.broadcast_to(g_[CHUNK * (i + 1) - 1:CHUNK * (i + 1)], (CHUNK, 1)) for i in range(2)],
                          axis=0) for g_ in gc]
    decay = []
    for g_ in gc:
        gc_full = jnp.broadcast_to(g_, (PAIR, PAIR))
        decay.append(jnp.exp(gc_full - gc_full.T))
    kb = [kg_[i] * beta_c[i] for i in nb]
    vbb = [vg_[i] * beta_c[i] for i in nb]
    eg = [jnp.exp(g_) for g_ in gc]

    aa = [_mm(jnp.concatenate([at[i], rt[i]], axis=0), jnp.concatenate([bt[i], kt[i]], axis=0), _NT) for i in na]
    kq = [_mm(jnp.concatenate([kb[i], qg_[i]], axis=0), kg_[i], _NT) for i in nb]
    a_ab = [jnp.where(strict, aa[i][:PAIR, :PAIR], 0.0) for i in na]
    a_ak = [jnp.where(strict, aa[i][:PAIR, PAIR:], 0.0) for i in na]
    a_rb = [jnp.where(incl, aa[i][PAIR:, :PAIR], 0.0) for i in na]
    a_rk = [jnp.where(incl, aa[i][PAIR:, PAIR:], 0.0) for i in na]
    lneg = [jnp.where(strict, -(kq[i][:PAIR] * decay[i]), 0.0) for i in nb]
    qk = [jnp.where(incl, kq[i][PAIR:] * decay[i], 0.0) for i in nb]
    tall = _neumann_inverse_many(a_ab + lneg, eye, INV_TERMS)
    tinv, tinvb = tall[:len(ia)], tall[len(ia):]
    akv = [_mm(a_ak[i], vs[i]) for i in na]
    uw = [_mm(tinvb[i], jnp.concatenate([vbb[i], kb[i] * eg[i]], axis=1)) for i in nb]
    gu = [_mm(tinv[i], jnp.concatenate([at[i], akv[i]], axis=1)) for i in na]
    ow = [_mm(qk[i], uw[i]) for i in nb]
    ry = [_mm(a_rb[i], gu[i]) for i in na]
    rkv = [_mm(a_rk[i], vs[i]) for i in na]
    kd = [kg_[i] * jnp.exp(gt[i] - gc[i]) for i in nb]
    pqb = [[_mm(kd[i][half(j)], uw[i][half(j)], _TN) for j in range(2)] for i in nb]
    pq = [_mm(gu[i], bh[i], _TN) for i in na]
    vk = [_mm(vs[i], kh[i], _TN) for i in na]
    rp = [rt[i] + ry[i][:, :PAIR] for i in na]
    yc = [ry[i][:, PAIR:] + rkv[i] for i in na]
    pm = [eye * gl[ia[i][1]][:, lanes(ia[i][0])] + pq[i][:PAIR] for i in na]
    qm = [pq[i][PAIR:] + vk[i] for i in na]
    oc = [ow[i][:, :DV] for i in nb]
    rq = [qg_[i] * eg[i] - ow[i][:, DV:] for i in nb]
    pmb = [[eye * jnp.exp(gt[i][CHUNK * j:CHUNK * j + 1]) - pqb[i][j][:, DV:] for j in range(2)] for i in nb]

    sa = [sa_ref[p] for p in range(npair)]
    sb = [sb_ref[h] for h in range(H_B)]
    for c in range(nchunk):
        for p in range(npair):
            i = ia.index((p, c))
            y = _mm(rp[i], sa[p], _NT) + yc[i]
            y_ref[rows(c), lanes(p)] = y[:CHUNK] + y[CHUNK:]
            sa[p] = _mm(sa[p], pm[i], _NN, CHAIN_TERMS, CHAIN_TERMS) + qm[i]
        for h in range(H_B):
            i, j = ib.index((h // 2, c)), h % 2
            o_ref[rows(c), DV * h:DV * (h + 1)] = _mm(rq[i][half(j)], sb[h]) + oc[i][half(j)]
            sb[h] = _mm(pmb[i][j], sb[h], _NN, CHAIN_TERMS, CHAIN_TERMS) + pqb[i][j][:, :DV]
    for p in range(npair):
        sa_ref[p] = sa[p]
    for h in range(H_B):
        sb_ref[h] = sb[h]

    oa_ref[0] = _group_norm_gate(y_ref[...], bonus, g, lnw_ref[...], lnb_ref[...], hb)
    wkv_ref[0] = sa_ref[...]
    ob_ref[0] = _head_norm_gate(o_ref[...], z, gnorm_ref[...])
    gdn_ref[0] = sb_ref[...]


def _mixers_prompt(x, nmix, wa, wq, wz, wb, rw_consts, gdn_consts, tri):
    bsz, seq, _ = x.shape
    consts = (nmix, wa, wq, wz, wb) + tuple(rw_consts) + tuple(gdn_consts) + (tri,)
    npair = C_A // LANES
    tps, steps, load, emit, per_seq3, per_seq4 = _tile_maps(bsz, seq, T_TILE)
    tile = lambda w: pltpu.VMEM((T_TILE, w), F32)
    return pl.pallas_call(
        functools.partial(_mixers_prompt_kernel, tps),
        grid=(steps,),
        in_specs=[pl.BlockSpec((1, T_TILE, D_MODEL), load)] + [_const_spec(c.shape) for c in consts],
        out_specs=[pl.BlockSpec((1, T_TILE, C_A), emit),
                   pl.BlockSpec((1, npair, LANES, LANES), per_seq4),
                   pl.BlockSpec((1, HALO, A_COLS), per_seq3),
                   pl.BlockSpec((1, T_TILE, C_BV), emit),
                   pl.BlockSpec((1, H_B, DK, DV), per_seq4),
                   pl.BlockSpec((1, HALO, CONV_CH), per_seq3)],
        out_shape=[jax.ShapeDtypeStruct((bsz, seq, C_A), F32),
                   jax.ShapeDtypeStruct((bsz, npair, LANES, LANES), F32),
                   jax.ShapeDtypeStruct((bsz, HALO, A_COLS), F32),
                   jax.ShapeDtypeStruct((bsz, seq, C_BV), F32),
                   jax.ShapeDtypeStruct((bsz, H_B, DK, DV), F32),
                   jax.ShapeDtypeStruct((bsz, HALO, CONV_CH), F32)],
        scratch_shapes=[tile(A_COLS), pltpu.VMEM((T_TILE + HALO, A_COLS), F32), pltpu.VMEM((npair, LANES, LANES), F32)]
                       + [tile(C_A) for _ in range(8)]
                       + [tile(CONV_CH), tile(C_BV), tile(LANES), pltpu.VMEM((T_TILE + HALO, CONV_CH), F32),
                          pltpu.VMEM((H_B, DK, DV), F32), tile(C_BK), tile(C_BK), tile(C_BV), tile(LANES), tile(LANES),
                          tile(C_BV)],
        compiler_params=pltpu.CompilerParams(dimension_semantics=("arbitrary",), vmem_limit_bytes=VMEM_LIMIT),
        name="mixers_prompt",
    )(x, *consts)


def _to_columns(x):
    pad = jnp.zeros((LANES - DEC_TILE, x.shape[1]), F32)
    return jnp.concatenate([x, pad], axis=0).T


def _decode_rwkv_kernel(pa_ref, shift_ref, wkv_ref,
                        mu_ref, w0_ref, w2_ref, a0_ref, a2_ref, g2_ref, kkw_ref, ka_ref, rk_ref, lnw_ref, lnb_ref,
                        hb_ref, oa_ref, wkvo_ref, tr_ref, yt_ref, g_ref, bonus_ref):
    h = pl.program_id(0)

    @pl.when(h == 0)
    def _():
        r, k2, v, logw, a, g, kk, bonus = _rwkv_prep(
            pa_ref[...], shift_ref[...], mu_ref[...], w0_ref[...], w2_ref[...], a0_ref[...], a2_ref[...],
            g2_ref[...], kkw_ref[...], ka_ref[...], rk_ref[...], hb_ref[...])
        for i, x in enumerate((-kk, jnp.exp(logw), kk * a, k2, r, v)):
            tr_ref[i] = x.T
        g_ref[...] = g
        bonus_ref[...] = bonus

    base = pl.multiple_of(h * HEAD_A, HEAD_A)
    hs = pl.ds(base, HEAD_A)
    a_t, w_t, b_t, k_t, r_t = (tr_ref[i, hs, :] for i in range(5))

    def value_row(vi, carry):
        st = wkv_ref[0, vi]
        sa = jnp.sum(st * a_t, axis=0, keepdims=True)
        st = st * w_t + sa * b_t + tr_ref[5, pl.ds(base + vi, 1), :] * k_t
        wkvo_ref[0, vi] = st
        yt_ref[pl.ds(base + vi, 1), :] = jnp.sum(st * r_t, axis=0, keepdims=True)
        return carry

    lax.fori_loop(0, HEAD_A, value_row, 0, unroll=8)

    @pl.when(h == H_A - 1)
    def _():
        oa_ref[...] = _group_norm_gate(yt_ref[...].T, bonus_ref[...], g_ref[...], lnw_ref[...], lnb_ref[...],
                                       hb_ref[...])


def _decode_rwkv(pa, shift, wkv_t, rw_consts):
    n = pa.shape[0]
    full = lambda w: pl.BlockSpec((n, w), lambda h: (0, 0))
    state = pl.BlockSpec((1, HEAD_A, HEAD_A, n), lambda h: (h, 0, 0, 0))
    return pl.pallas_call(
        _decode_rwkv_kernel,
        grid=(H_A,),
        in_specs=[full(A_COLS), full(A_COLS), state] + [_const_spec(c.shape) for c in rw_consts],
        out_specs=[full(C_A), state],
        out_shape=[jax.ShapeDtypeStruct((n, C_A), F32), jax.ShapeDtypeStruct(wkv_t.shape, F32)],
        scratch_shapes=[pltpu.VMEM((6, C_A, n), F32), pltpu.VMEM((C_A, n), F32),
                        pltpu.VMEM((n, C_A), F32), pltpu.VMEM((n, C_A), F32)],
        compiler_params=pltpu.CompilerParams(dimension_semantics=("arbitrary",), vmem_limit_bytes=VMEM_LIMIT),
        name="decode_rwkv",
    )(pa, shift, wkv_t, *rw_consts)


def _decode_gdn_kernel(qkv_ref, z_ref, ba_ref, cst_ref, gdn_ref, conv_ref, alog_ref, dtb_ref, gnorm_ref,
                       ob_ref, gdno_ref):
    cst = cst_ref[...]
    q, k, vv, beta, glog = _gdn_prep(cst[:, :CONV_CH], cst[:, CONV_CH:2 * CONV_CH], cst[:, 2 * CONV_CH:],
                                     qkv_ref[...], ba_ref[...], conv_ref[...], alog_ref[...], dtb_ref[...])
    eg = jnp.exp(glog)
    k_cols = _to_columns(k)
    ls = lambda h: slice(DK * h, DK * (h + 1))

    def split2(x):
        hi = x.astype(BF16)
        return hi, x - hi.astype(F32)

    rows_a, rows_b = [], []
    for h in range(H_B):
        e_h = eg[:, H_B + h:H_B + h + 1]
        (wh, wl), (gh, gl_) = split2(k[:, ls(h)] * (beta[:, h:h + 1] * e_h)), split2(q[:, ls(h)] * e_h)
        rows_a.append(jnp.concatenate([wh.astype(F32), gh.astype(F32), wl, gl_], axis=0).astype(BF16))
        rows_b.append(jnp.concatenate([wh.astype(F32), gh.astype(F32)], axis=0).astype(BF16))
    gunits = [(s_i, h) for s_i in range(DEC_TILE) for h in range(H_B)]
    gn = range(len(gunits))
    be = [beta[s_i:s_i + 1, h:h + 1] for s_i, h in gunits]
    e = [eg[s_i:s_i + 1, H_B + h:H_B + h + 1] for s_i, h in gunits]
    kc = [k_cols[ls(h), s_i:s_i + 1] for s_i, h in gunits]
    gst = [gdn_ref[s_i, h] for s_i, h in gunits]
    parts = [split2(st) for st in gst]
    ra = [_dot(rows_a[h], parts[i][0].astype(BF16)) + jnp.concatenate(
        [_dot(rows_b[h], parts[i][1].astype(BF16)), jnp.zeros((2 * DEC_TILE, DV), F32)], axis=0)
        for i, (s_i, h) in enumerate(gunits)]
    pick = lambda r, j, s_i: r[DEC_TILE * j + s_i:DEC_TILE * j + s_i + 1]
    ws = [pick(ra[i], 0, s_i) + pick(ra[i], 2, s_i) for i, (s_i, h) in enumerate(gunits)]
    qs = [pick(ra[i], 1, s_i) + pick(ra[i], 3, s_i) for i, (s_i, h) in enumerate(gunits)]
    v_new = [be[i] * vv[s_i:s_i + 1, ls(h)] - ws[i] for i, (s_i, h) in enumerate(gunits)]
    qk = [jnp.sum(q[s_i:s_i + 1, ls(h)] * k[s_i:s_i + 1, ls(h)], axis=-1, keepdims=True) for s_i, h in gunits]
    for i, (s_i, h) in enumerate(gunits):
        gdno_ref[s_i, h] = gst[i] * e[i] + kc[i] * v_new[i]
    o_units = [qs[i] + qk[i] * v_new[i] for i in gn]
    o = jnp.concatenate([jnp.concatenate(o_units[H_B * s_i:H_B * (s_i + 1)], axis=1) for s_i in range(DEC_TILE)],
                        axis=0)
    ob_ref[...] = _head_norm_gate(o, z_ref[...], gnorm_ref[...])


def _decode_gdn(qkv, z, ba, cst, gdn, gdn_consts):
    n = qkv.shape[0]
    row = lambda w: pl.BlockSpec((DEC_TILE, w), lambda i: (i, 0))
    state = pl.BlockSpec((DEC_TILE, H_B, DK, DV), lambda i: (i, 0, 0, 0))
    return pl.pallas_call(
        _decode_gdn_kernel,
        grid=(n // DEC_TILE,),
        in_specs=[row(CONV_CH), row(C_BV), row(LANES), row(3 * CONV_CH), state]
                 + [_const_spec(c.shape) for c in gdn_consts],
        out_specs=[row(C_BV), state],
        out_shape=[jax.ShapeDtypeStruct((n, C_BV), F32), jax.ShapeDtypeStruct(gdn.shape, F32)],
        compiler_params=pltpu.CompilerParams(dimension_semantics=("parallel",), vmem_limit_bytes=VMEM_LIMIT),
        name="decode_gdn",
    )(qkv, z, ba, cst, gdn, *gdn_consts)


def _post_kernel(x_ref, oa_ref, ob_ref, pe_ref, nmix_ref, wg_ref, wba_ref, wbb_ref, wout_ref, nffn_ref,
                 wfg_ref, wfu_ref, wfd_ref, nple_ref, wpg_ref, wpp_ref, nfin_ref, y_ref):
    nrow = x_ref.shape[0]
    ngroup = max(1, nrow // POST_GROUP_ROWS)
    groups = [slice(r0, r0 + nrow // ngroup) for r0 in range(0, nrow, nrow // ngroup)]
    x = [x_ref[g, :] for g in groups]
    ma = [_dot(oa_ref[g, :].astype(BF16), wba_ref[...]) for g in groups]
    mb = [_dot(ob_ref[g, :].astype(BF16), wbb_ref[...]) for g in groups]
    u = [_rms(xi, nmix_ref[...]).astype(BF16) for xi in x]
    gates = [_sigmoid(_dot(ui, wg_ref[...])) for ui in u]
    mix = [gi[:, :D_MODEL] * mai + gi[:, D_MODEL:] * mbi for gi, mai, mbi in zip(gates, ma, mb)]
    h = [xi + _dot(mi.astype(BF16), wout_ref[...]) for xi, mi in zip(x, mix)]
    pp = [_dot(pe_ref[g, :].astype(BF16), wpp_ref[...]) for g in groups]
    u2 = [_rms(hi, nffn_ref[...]).astype(BF16) for hi in h]
    fg = [_silu(_dot(ui, wfg_ref[...])) for ui in u2]
    ff = [fgi * _dot(ui, wfu_ref[...]) for fgi, ui in zip(fg, u2)]
    h = [hi + _dot(fi.astype(BF16), wfd_ref[...]) for hi, fi in zip(h, ff)]
    u3 = [_rms(hi, nple_ref[...]).astype(BF16) for hi in h]
    h = [hi + _sigmoid(_dot(ui, wpg_ref[...])) * ppi for hi, ui, ppi in zip(h, u3, pp)]
    for g, hi in zip(groups, h):
        y_ref[g, :] = _rms(hi, nfin_ref[...])


def _post(x, oa, ob, pe, consts):
    n = x.shape[0]
    tm = min(ROW_TILE, n)
    row = lambda w: pl.BlockSpec((tm, w), lambda i: (i, 0))
    return pl.pallas_call(
        _post_kernel,
        grid=(n // tm,),
        in_specs=[row(D_MODEL), row(C_A), row(C_BV), row(PLE_DIM)] + [_const_spec(c.shape) for c in consts],
        out_specs=row(D_MODEL),
        out_shape=jax.ShapeDtypeStruct((n, D_MODEL), F32),
        compiler_params=pltpu.CompilerParams(dimension_semantics=("parallel",), vmem_limit_bytes=VMEM_LIMIT),
        name="post",
    )(x, oa, ob, pe, *consts)


def _chunk_cumsum_matrix(tile):
    i = jnp.arange(tile)
    same = (i[:, None] // CHUNK) == (i[None, :] // CHUNK)
    return (same & (i[None, :] <= i[:, None])).astype(BF16)


def kernel(x_prompt, x_sample, p_prompt, p_sample, state_shift, state_wkv, state_conv, state_gdn, norm_mix, w_in, mu_shift, rw_w0, rw_w2, rw_a0, rw_a2, rw_g2, rw_kk, rw_ka, rw_rk, rw_ln_w, rw_ln_b, gdn_conv, gdn_a_log, gdn_dt_bias, gdn_norm, w_branch_a, w_branch_b, w_out, norm_ffn, w_ffn_gate, w_ffn_up, w_ffn_down, norm_ple, w_ple_gate, w_ple_proj, norm_final):
    bsz, seq, _ = x_prompt.shape
    nd = x_sample.shape[0]
    row = lambda p: p.reshape(1, -1)

    w_in0 = w_in[0]
    b0 = A_COLS
    wa = w_in0[:, :b0].astype(BF16)
    wq = w_in0[:, b0:b0 + CONV_CH].astype(BF16)
    wz = w_in0[:, b0 + CONV_CH:b0 + CONV_CH + C_BV].astype(BF16)
    wb = jnp.pad(w_in0[:, b0 + CONV_CH + C_BV:b0 + CONV_CH + C_BV + 2 * H_B], ((0, 0), (0, LANES - 2 * H_B))).astype(BF16)
    wg = w_in0[:, b0 + CONV_CH + C_BV + 2 * H_B:].astype(BF16)
    w2p = jnp.concatenate([rw_w2[0], jnp.zeros((LORA_A, C_A), F32)], axis=0).astype(BF16)
    a2p = jnp.concatenate([jnp.zeros((LORA_W, C_A), F32), rw_a2[0]], axis=0).astype(BF16)
    ch = jnp.arange(LANES) // HEAD_A
    hb = (ch[:, None] == ch[None, :]).astype(BF16)
    tri = _chunk_cumsum_matrix(T_TILE)
    alog = jnp.pad(gdn_a_log[0], (H_B, LANES - 2 * H_B)).reshape(1, LANES)
    dtb = jnp.pad(gdn_dt_bias[0], (H_B, LANES - 2 * H_B)).reshape(1, LANES)
    rw_consts = (row(mu_shift[0]), row(rw_w0[0]), w2p, row(rw_a0[0]), a2p, rw_g2[0].astype(BF16), row(rw_kk[0]), row(rw_ka[0]),
                 row(rw_rk[0]), row(rw_ln_w[0]), row(rw_ln_b[0]), hb)
    gdn_consts = (gdn_conv[0], alog, dtb, row(gdn_norm[0]))
    post_consts = (row(norm_mix[0]), wg, w_branch_a[0].astype(BF16), w_branch_b[0].astype(BF16),
                   w_out[0].astype(BF16), row(norm_ffn[0]), w_ffn_gate[0].astype(BF16), w_ffn_up[0].astype(BF16),
                   w_ffn_down[0].astype(BF16), row(norm_ple[0]), w_ple_gate[0].astype(BF16),
                   w_ple_proj[0].astype(BF16), row(norm_final))

    xp = x_prompt.reshape(bsz * seq, D_MODEL)
    oa, wkv_pairs, shift_tail, ob, gdn_p, conv_tail = _mixers_prompt(
        x_prompt, row(norm_mix[0]), wa, wq, wz, wb, rw_consts, gdn_consts, tri)
    y_prompt = _post(xp, oa.reshape(bsz * seq, C_A), ob.reshape(bsz * seq, C_BV),
                     p_prompt[0].reshape(bsz * seq, PLE_DIM), post_consts).reshape(bsz, seq, D_MODEL)
    wkv_p = jnp.stack([wkv_pairs[:, :, :HEAD_A, :HEAD_A], wkv_pairs[:, :, HEAD_A:, HEAD_A:]], axis=2)
    wkv_p = wkv_p.reshape(bsz, H_A, HEAD_A, HEAD_A)
    shift_p = shift_tail[:, HALO - 1:, :]
    conv_p = conv_tail[:, HALO - (CONV_W - 1):, :]

    xs = x_sample.reshape(nd, D_MODEL)
    pa_s, qkv_s, z_s, ba_s = _inproj(xs, row(norm_mix[0]), wa, wq, wz, wb)
    cst = state_conv[0].reshape(nd, (CONV_W - 1) * CONV_CH)
    oa_s, wkv_t = _decode_rwkv(pa_s, state_shift[0].reshape(nd, A_COLS), jnp.transpose(state_wkv[0], (1, 2, 3, 0)),
                               rw_consts)
    wkv_s = jnp.transpose(wkv_t, (3, 0, 1, 2))
    ob_s, gdn_s = _decode_gdn(qkv_s, z_s, ba_s, cst, state_gdn[0], gdn_consts)
    y_sample = _post(xs, oa_s, ob_s, p_sample[0].reshape(nd, PLE_DIM), post_consts).reshape(nd, 1, D_MODEL)
    conv_s = jnp.concatenate([cst[:, CONV_CH:], qkv_s], axis=1).reshape(nd, CONV_W - 1, CONV_CH)

    return (y_prompt, y_sample, shift_p[None], wkv_p[None], conv_p[None], gdn_p[None],
            pa_s.reshape(1, nd, 1, A_COLS), wkv_s[None], conv_s[None], gdn_s[None])
```

```python
import functools

import jax
import jax.numpy as jnp
from jax import lax
from jax.experimental import pallas as pl
from jax.experimental.pallas import tpu as pltpu

F32 = jnp.float32
BF16 = jnp.bfloat16

D_MODEL = 1024
HEAD_A = 64
C_A = 512
H_A = 8
LORA_W = 64
LORA_A = 64
LORA_G = 128
A_COLS = 3 * C_A + LORA_W + LORA_A + LORA_G
DK = 128
DV = 128
H_B = 4
C_BK = 512
C_BV = 512
CONV_W = 4
CONV_CH = 2 * C_BK + C_BV
D_FF = 2816
PLE_DIM = 256
NORM_EPS = 1e-6
GN_EPS = 64e-5
L2_EPS = 1e-6

LANES = 128
CHUNK = 64
PAIR = 2 * CHUNK
T_TILE = 256
PROJ_COLS = 256
ROW_TILE = 512
POST_GROUP_ROWS = 256
DEC_TILE = 8
HALO = 8
VMEM_LIMIT = 56 * 1024 * 1024
SUM_TERMS = 1
CUMSUM_TERMS = 2
INV_TERMS = 1
CHAIN_TERMS = 1


def _dot(a, b):
    return jnp.dot(a, b, preferred_element_type=F32)


def _sigmoid(x):
    return 1.0 / (1.0 + jnp.exp(-x))


def _silu(x):
    return x * _sigmoid(x)


def _softplus(x):
    return jnp.maximum(x, 0.0) + jnp.log(1.0 + jnp.exp(-jnp.abs(x)))


def _rms(x, gain):
    return x * lax.rsqrt(jnp.mean(x * x, axis=-1, keepdims=True) + NORM_EPS) * gain


def _pair_masks():
    ri = lax.broadcasted_iota(jnp.int32, (PAIR, PAIR), 0)
    ci = lax.broadcasted_iota(jnp.int32, (PAIR, PAIR), 1)
    same = (ri < CHUNK) == (ci < CHUNK)
    strict = same & (ci < ri)
    incl = same & (ci <= ri)
    eye = (ri == ci).astype(F32)
    return strict, incl, eye


_NN = (((1,), (0,)), ((), ()))
_NT = (((1,), (1,)), ((), ()))
_TN = (((0,), (0,)), ((), ()))


def _split(x, terms):
    if isinstance(x, (list, tuple)):
        return list(x)
    if x.dtype == BF16:
        return [x]
    parts = []
    for i in range(terms):
        h = x.astype(BF16)
        parts.append(h)
        if i + 1 < terms:
            x = x - h.astype(F32)
    return parts


def _mm(a, b, dims=_NN, na=1, nb=1):
    pa, pb = _split(a, na), _split(b, nb)
    acc = None
    for i, ai in enumerate(pa):
        for j, bj in enumerate(pb):
            if i + j < max(len(pa), len(pb)):
                d = lax.dot_general(ai, bj, dims, preferred_element_type=F32)
                acc = d if acc is None else acc + d
    return acc


def _neumann_inverse_many(lmats, eye, terms):
    ts = [eye + l for l in lmats]
    ps = [_split(l, terms) for l in lmats]
    n = 2
    while n < CHUNK:
        ps = [_split(_mm(p, p), terms) for p in ps]
        ts = [t + _mm(t, p, _NN, terms) for t, p in zip(ts, ps)]
        n *= 2
    return ts


def _inproj_kernel(x_ref, g_ref, wa_ref, wq_ref, wz_ref, wb_ref, pa_ref, qkv_ref, z_ref, ba_ref):
    u = _rms(x_ref[...], g_ref[...]).astype(BF16)
    pa_ref[...] = _dot(u, wa_ref[...])
    qkv_ref[...] = _dot(u, wq_ref[...])
    z_ref[...] = _dot(u, wz_ref[...])
    ba_ref[...] = _dot(u, wb_ref[...])


def _const_spec(shape):
    nd = len(shape)
    return pl.BlockSpec(shape, lambda *_: (0,) * nd, pipeline_mode=pl.Buffered(1))


def _inproj(x, gain, wa, wq, wz, wb):
    n = x.shape[0]
    tm = min(ROW_TILE, n)
    row = lambda w: pl.BlockSpec((tm, w), lambda i: (i, 0))
    return pl.pallas_call(
        _inproj_kernel,
        grid=(n // tm,),
        in_specs=[row(D_MODEL), _const_spec(gain.shape), _const_spec(wa.shape), _const_spec(wq.shape),
                  _const_spec(wz.shape), _const_spec(wb.shape)],
        out_specs=[row(A_COLS), row(CONV_CH), row(C_BV), row(LANES)],
        out_shape=[jax.ShapeDtypeStruct((n, A_COLS), F32), jax.ShapeDtypeStruct((n, CONV_CH), F32),
                   jax.ShapeDtypeStruct((n, C_BV), F32), jax.ShapeDtypeStruct((n, LANES), F32)],
        compiler_params=pltpu.CompilerParams(dimension_semantics=("parallel",), vmem_limit_bytes=VMEM_LIMIT),
        name="inproj",
    )(x, gain, wa, wq, wz, wb)


def _rwkv_prep(pa, prev, mu, w0, w2p, a0, a2p, g2, kkw, ka, rk, hb, between=lambda: None):
    xa = pa + (prev - pa) * mu
    r = xa[:, :C_A]
    k = xa[:, C_A:2 * C_A]
    v = xa[:, 2 * C_A:3 * C_A]
    xwa = xa[:, 3 * C_A:3 * C_A + LORA_W + LORA_A]
    xg = xa[:, 3 * C_A + LORA_W + LORA_A:]
    w_pre = _mm(jnp.tanh(xwa), w2p)
    a_pre = _mm(xwa, a2p)
    g = _mm(_sigmoid(xg), g2)
    kx = k * kkw
    kk_den = _head_sum(kx * kx, hb)
    between()
    w_log = -_softplus(-(w0 + w_pre)) - 0.5
    logw = -jnp.exp(w_log)
    a = _sigmoid(a0 + a_pre)
    kk = kx * lax.rsqrt(kk_den + L2_EPS)
    k2 = k * (1.0 + (a - 1.0) * ka)
    bonus_sum = _head_sum(r * k2 * rk, hb)
    between()
    bonus = bonus_sum * v
    return r, k2, v, logw, a, g, kk, bonus


def _head_sum(x, hb):
    groups = [_mm(x[:, LANES * j:LANES * (j + 1)], hb, _NN, SUM_TERMS) for j in range(x.shape[1] // LANES)]
    return jnp.concatenate(groups, axis=1)


def _group_norm_gate(y, bonus, g, lnw, lnb, hb):
    mean = _head_sum(y, hb) * (1.0 / HEAD_A)
    d = y - mean
    var = _head_sum(d * d, hb) * (1.0 / HEAD_A)
    yn = d * lax.rsqrt(var + GN_EPS) * lnw + lnb
    return (yn + bonus) * g


def _gdn_prep(x0, x1, x2, x3, ba, conv, alog, dtb):
    c = _silu(x0 * conv[0:1] + x1 * conv[1:2] + x2 * conv[2:3] + x3 * conv[3:4])
    qs, ks = [], []
    for h in range(H_B):
        qh = c[:, DK * h:DK * (h + 1)]
        kh = c[:, C_BK + DK * h:C_BK + DK * (h + 1)]
        qs.append(qh * lax.rsqrt(jnp.sum(qh * qh, axis=-1, keepdims=True) + L2_EPS) * (DK ** -0.5))
        ks.append(kh * lax.rsqrt(jnp.sum(kh * kh, axis=-1, keepdims=True) + L2_EPS))
    q = jnp.concatenate(qs, axis=1)
    k = jnp.concatenate(ks, axis=1)
    v = c[:, 2 * C_BK:]
    beta = _sigmoid(ba)
    glog = -jnp.exp(alog) * _softplus(ba + dtb)
    return q, k, v, beta, glog


def _head_norm_gate(o, z, gnorm):
    outs = []
    for h in range(H_B):
        oh = o[:, DV * h:DV * (h + 1)]
        zh = z[:, DV * h:DV * (h + 1)]
        oh = oh * lax.rsqrt(jnp.mean(oh * oh, axis=-1, keepdims=True) + NORM_EPS) * gnorm
        outs.append(oh * _silu(zh))
    return jnp.concatenate(outs, axis=1)


def _tile_maps(bsz, seq, tile):
    tps = seq // tile
    last = bsz * tps - 1
    cur = lambda s: jnp.minimum(s, last)
    prv = lambda s: jnp.maximum(s - 1, 0)
    load = lambda s: (cur(s) // tps, cur(s) % tps, 0)
    emit = lambda s: (prv(s) // tps, prv(s) % tps, 0)
    per_seq3 = lambda s: (prv(s) // tps, 0, 0)
    per_seq4 = lambda s: (prv(s) // tps, 0, 0, 0)
    return tps, last + 2, load, emit, per_seq3, per_seq4


def _gdn_prep_groups(x0, x1, x2, x3, conv):
    def conv_cols(c0, c1):
        return _silu(x0[:, c0:c1] * conv[0:1, c0:c1] + x1[:, c0:c1] * conv[1:2, c0:c1]
                     + x2[:, c0:c1] * conv[2:3, c0:c1] + x3[:, c0:c1] * conv[3:4, c0:c1])

    def unit(xh):
        return xh * lax.rsqrt(jnp.sum(xh * xh, axis=-1, keepdims=True) + L2_EPS)

    for h in range(H_B):
        yield 0, h, unit(conv_cols(DK * h, DK * (h + 1))) * (DK ** -0.5)
        yield 1, h, unit(conv_cols(C_BK + DK * h, C_BK + DK * (h + 1)))
        yield 2, h, conv_cols(2 * C_BK + DV * h, 2 * C_BK + DV * (h + 1))


def _mixers_prompt_kernel(tiles_per_seq, x_ref, nmix_ref, wa_ref, wq_ref, wz_ref, wb_ref,
                          mu_ref, w0_ref, w2_ref, a0_ref, a2_ref, g2_ref, kkw_ref, ka_ref, rk_ref,
                          lnw_ref, lnb_ref, hb_ref, conv_ref, alog_ref, dtb_ref, gnorm_ref, tri_ref,
                          oa_ref, wkv_ref, shift_ref, ob_ref, gdn_ref, convo_ref,
                          pa_ref, exta_ref, sa_ref, at_ref, bt_ref, kt_ref, rt_ref, va_ref, bh_ref, kh_ref, y_ref,
                          qkvn_ref, zn_ref, ban_ref, extb_ref, sb_ref, q_ref, k_ref, vb_ref, gc_ref, be_ref, o_ref):
    s_id = pl.program_id(0)

    @pl.when(s_id == 0)
    def _():
        pa_ref[...] = jnp.zeros_like(pa_ref)
        qkvn_ref[...] = jnp.zeros_like(qkvn_ref)
        zn_ref[...] = jnp.zeros_like(zn_ref)
        ban_ref[...] = jnp.zeros_like(ban_ref)

    @pl.when(jnp.maximum(s_id - 1, 0) % tiles_per_seq == 0)
    def _():
        exta_ref[0:HALO, :] = jnp.zeros((HALO, A_COLS), F32)
        extb_ref[0:HALO, :] = jnp.zeros((HALO, CONV_CH), F32)
        sa_ref[...] = jnp.zeros_like(sa_ref)
        sb_ref[...] = jnp.zeros_like(sb_ref)

    pa = pa_ref[...]
    exta_ref[HALO:, :] = pa
    prev = exta_ref[pl.ds(HALO - 1, T_TILE), :]
    exta_ref[0:HALO, :] = pa[T_TILE - HALO:, :]
    shift_ref[0] = pa[T_TILE - HALO:, :]
    x3 = qkvn_ref[...]
    z = zn_ref[...]
    ba = ban_ref[...]
    extb_ref[HALO:, :] = x3
    x0 = extb_ref[pl.ds(HALO - 3, T_TILE), :]
    x1 = extb_ref[pl.ds(HALO - 2, T_TILE), :]
    x2 = extb_ref[pl.ds(HALO - 1, T_TILE), :]
    extb_ref[0:HALO, :] = x3[T_TILE - HALO:, :]
    convo_ref[0] = x3[T_TILE - HALO:, :]

    u_next = _rms(x_ref[0], nmix_ref[...]).astype(BF16)
    blocks = [(w_ref, dst_ref, c0, min(c0 + PROJ_COLS, w_ref.shape[1]))
              for w_ref, dst_ref in ((wa_ref, pa_ref), (wq_ref, qkvn_ref), (wz_ref, zn_ref), (wb_ref, ban_ref))
              for c0 in range(0, w_ref.shape[1], PROJ_COLS)]

    def project(count):
        for w_ref, dst_ref, c0, c1 in blocks[:count]:
            dst_ref[:, c0:c1] = _dot(u_next, w_ref[:, c0:c1])
        del blocks[:count]

    gdn_groups = _gdn_prep_groups(x0, x1, x2, x3, conv_ref[...])
    gdn_dst = (q_ref, k_ref, vb_ref)

    def gdn_prep(count):
        for _ in range(count):
            item = next(gdn_groups, None)
            if item is not None:
                kind, h, arr = item
                gdn_dst[kind][:, DK * h:DK * (h + 1)] = arr

    def between():
        project(3)
        gdn_prep(3)

    hb = hb_ref[...]
    r, k2, v, logw, a, g, kk, bonus = _rwkv_prep(
        pa, prev, mu_ref[...], w0_ref[...], w2_ref[...], a0_ref[...], a2_ref[...], g2_ref[...],
        kkw_ref[...], ka_ref[...], rk_ref[...], hb, between)

    nchunk = T_TILE // CHUNK
    cs = _mm(tri_ref[...], logw, _NN, 1, CUMSUM_TERMS)
    between()
    ends = [cs[CHUNK * (c + 1) - 1:CHUNK * (c + 1)] for c in range(nchunk)]
    tot = jnp.concatenate([jnp.broadcast_to(e, (CHUNK, C_A)) for e in ends], axis=0)
    dinv = jnp.exp(-cs)
    dend = jnp.exp(tot - cs)
    b_in = kk * a
    at_ref[...] = -kk * jnp.exp(cs - logw)
    bt_ref[...] = b_in * dinv
    kt_ref[...] = k2 * dinv
    between()
    rt_ref[...] = r * jnp.exp(cs)
    va_ref[...] = v
    bh_ref[...] = b_in * dend
    kh_ref[...] = k2 * dend
    gl = [jnp.exp(e) for e in ends]
    project(len(blocks))
    gdn_prep(3 * H_B)
    be_ref[...] = _sigmoid(ba)
    glog = -jnp.exp(alog_ref[...]) * _softplus(ba + dtb_ref[...])
    gc_ref[...] = _mm(tri_ref[...], glog, _NN, 1, CUMSUM_TERMS)

    strict, incl, eye = _pair_masks()
    low = lax.broadcasted_iota(jnp.int32, (CHUNK, LANES), 1) < HEAD_A

    def stack(x):
        return jnp.concatenate([jnp.where(low, x, 0.0), jnp.where(low, 0.0, x)], axis=0)

    npair = C_A // LANES
    lanes = lambda p: slice(LANES * p, LANES * (p + 1))
    rows = lambda c: slice(CHUNK * c, CHUNK * (c + 1))
    half = lambda j: slice(CHUNK * j, CHUNK * (j + 1))
    ia = [(p, c) for p in range(npair) for c in range(nchunk)]
    ib = [(pr, c) for pr in range(H_B // 2) for c in range(nchunk)]
    na, nb = range(len(ia)), range(len(ib))
    ld = lambda ref: [stack(ref[rows(c), lanes(p)]) for p, c in ia]
    at, bt, kt, rt, vs, bh, kh = (ld(ref) for ref in (at_ref, bt_ref, kt_ref, rt_ref, va_ref, bh_ref, kh_ref))
    cat = lambda ref: [jnp.concatenate([ref[rows(c), DK * h:DK * (h + 1)] for h in (2 * pr, 2 * pr + 1)], axis=0)
                       for pr, c in ib]
    col = lambda ref, off: [jnp.concatenate([ref[rows(c), off + h:off + h + 1] for h in (2 * pr, 2 * pr + 1)], axis=0)
                            for pr, c in ib]
    qg_, kg_, vg_ = cat(q_ref), cat(k_ref), cat(vb_ref)
    beta_c = col(be_ref, 0)
    gc = col(gc_ref, H_B)
    gt = [jnp.concatenate([jnp.broadcast_to(g_[CHUNK * (i + 1) - 1:CHUNK * (i + 1)], (CHUNK, 1)) for i in range(2)],
                          axis=0) for g_ in gc]
    decay = []
    for g_ in gc:
        gc_full = jnp.broadcast_to(g_, (PAIR, PAIR))
        decay.append(jnp.exp(gc_full - gc_full.T))
    kb = [kg_[i] * beta_c[i] for i in nb]
    vbb = [vg_[i] * beta_c[i] for i in nb]
    eg = [jnp.exp(g_) for g_ in gc]

    aa = [_mm(jnp.concatenate([at[i], rt[i]], axis=0), jnp.concatenate([bt[i], kt[i]], axis=0), _NT) for i in na]
    kq = [_mm(jnp.concatenate([kb[i], qg_[i]], axis=0), kg_[i], _NT) for i in nb]
    a_ab = [jnp.where(strict, aa[i][:PAIR, :PAIR], 0.0) for i in na]
    a_ak = [jnp.where(strict, aa[i][:PAIR, PAIR:], 0.0) for i in na]
    a_rb = [jnp.where(incl, aa[i][PAIR:, :PAIR], 0.0) for i in na]
    a_rk = [jnp.where(incl, aa[i][PAIR:, PAIR:], 0.0) for i in na]
    lneg = [jnp.where(strict, -(kq[i][:PAIR] * decay[i]), 0.0) for i in nb]
    qk = [jnp.where(incl, kq[i][PAIR:] * decay[i], 0.0) for i in nb]
    tall = _neumann_inverse_many(a_ab + lneg, eye, INV_TERMS)
    tinv, tinvb = tall[:len(ia)], tall[len(ia):]
    akv = [_mm(a_ak[i], vs[i]) for i in na]
    uw = [_mm(tinvb[i], jnp.concatenate([vbb[i], kb[i] * eg[i]], axis=1)) for i in nb]
    gu = [_mm(tinv[i], jnp.concatenate([at[i], akv[i]], axis=1)) for i in na]
    ow = [_mm(qk[i], uw[i]) for i in nb]
    ry = [_mm(a_rb[i], gu[i]) for i in na]
    rkv = [_mm(a_rk[i], vs[i]) for i in na]
    kd = [kg_[i] * jnp.exp(gt[i] - gc[i]) for i in nb]
    pqb = [[_mm(kd[i][half(j)], uw[i][half(j)], _TN) for j in range(2)] for i in nb]
    pq = [_mm(gu[i], bh[i], _TN) for i in na]
    vk = [_mm(vs[i], kh[i], _TN) for i in na]
    rp = [rt[i] + ry[i][:, :PAIR] for i in na]
    yc = [ry[i][:, PAIR:] + rkv[i] for i in na]
    pm = [eye * gl[ia[i][1]][:, lanes(ia[i][0])] + pq[i][:PAIR] for i in na]
    qm = [pq[i][PAIR:] + vk[i] for i in na]
    oc = [ow[i][:, :DV] for i in nb]
    rq = [qg_[i] * eg[i] - ow[i][:, DV:] for i in nb]
    pmb = [[eye * jnp.exp(gt[i][CHUNK * j:CHUNK * j + 1]) - pqb[i][j][:, DV:] for j in range(2)] for i in nb]

    sa = [sa_ref[p] for p in range(npair)]
    sb = [sb_ref[h] for h in range(H_B)]
    for c in range(nchunk):
        for p in range(npair):
            i = ia.index((p, c))
            y = _mm(rp[i], sa[p], _NT) + yc[i]
            y_ref[rows(c), lanes(p)] = y[:CHUNK] + y[CHUNK:]
            sa[p] = _mm(sa[p], pm[i], _NN, CHAIN_TERMS, CHAIN_TERMS) + qm[i]
        for h in range(H_B):
            i, j = ib.index((h // 2, c)), h % 2
            o_ref[rows(c), DV * h:DV * (h + 1)] = _mm(rq[i][half(j)], sb[h]) + oc[i][half(j)]
            sb[h] = _mm(pmb[i][j], sb[h], _NN, CHAIN_TERMS, CHAIN_TERMS) + pqb[i][j][:, :DV]
    for p in range(npair):
        sa_ref[p] = sa[p]
    for h in range(H_B):
        sb_ref[h] = sb[h]

    oa_ref[0] = _group_norm_gate(y_ref[...], bonus, g, lnw_ref[...], lnb_ref[...], hb)
    wkv_ref[0] = sa_ref[...]
    ob_ref[0] = _head_norm_gate(o_ref[...], z, gnorm_ref[...])
    gdn_ref[0] = sb_ref[...]


def _mixers_prompt(x, nmix, wa, wq, wz, wb, rw_consts, gdn_consts, tri):
    bsz, seq, _ = x.shape
    consts = (nmix, wa, wq, wz, wb) + tuple(rw_consts) + tuple(gdn_consts) + (tri,)
    npair = C_A // LANES
    tps, steps, load, emit, per_seq3, per_seq4 = _tile_maps(bsz, seq, T_TILE)
    tile = lambda w: pltpu.VMEM((T_TILE, w), F32)
    return pl.pallas_call(
        functools.partial(_mixers_prompt_kernel, tps),
        grid=(steps,),
        in_specs=[pl.BlockSpec((1, T_TILE, D_MODEL), load)] + [_const_spec(c.shape) for c in consts],
        out_specs=[pl.BlockSpec((1, T_TILE, C_A), emit),
                   pl.BlockSpec((1, npair, LANES, LANES), per_seq4),
                   pl.BlockSpec((1, HALO, A_COLS), per_seq3),
                   pl.BlockSpec((1, T_TILE, C_BV), emit),
                   pl.BlockSpec((1, H_B, DK, DV), per_seq4),
                   pl.BlockSpec((1, HALO, CONV_CH), per_seq3)],
        out_shape=[jax.ShapeDtypeStruct((bsz, seq, C_A), F32),
                   jax.ShapeDtypeStruct((bsz, npair, LANES, LANES), F32),
                   jax.ShapeDtypeStruct((bsz, HALO, A_COLS), F32),
                   jax.ShapeDtypeStruct((bsz, seq, C_BV), F32),
                   jax.ShapeDtypeStruct((bsz, H_B, DK, DV), F32),
                   jax.ShapeDtypeStruct((bsz, HALO, CONV_CH), F32)],
        scratch_shapes=[tile(A_COLS), pltpu.VMEM((T_TILE + HALO, A_COLS), F32), pltpu.VMEM((npair, LANES, LANES), F32)]
                       + [tile(C_A) for _ in range(8)]
                       + [tile(CONV_CH), tile(C_BV), tile(LANES), pltpu.VMEM((T_TILE + HALO, CONV_CH), F32),
                          pltpu.VMEM((H_B, DK, DV), F32), tile(C_BK), tile(C_BK), tile(C_BV), tile(LANES), tile(LANES),
                          tile(C_BV)],
        compiler_params=pltpu.CompilerParams(dimension_semantics=("arbitrary",), vmem_limit_bytes=VMEM_LIMIT),
        name="mixers_prompt",
    )(x, *consts)


def _to_columns(x):
    pad = jnp.zeros((LANES - DEC_TILE, x.shape[1]), F32)
    return jnp.concatenate([x, pad], axis=0).T


def _decode_rwkv_kernel(pa_ref, shift_ref, wkv_ref,
                        mu_ref, w0_ref, w2_ref, a0_ref, a2_ref, g2_ref, kkw_ref, ka_ref, rk_ref, lnw_ref, lnb_ref,
                        hb_ref, oa_ref, wkvo_ref, tr_ref, yt_ref, g_ref, bonus_ref):
    h = pl.program_id(0)

    @pl.when(h == 0)
    def _():
        r, k2, v, logw, a, g, kk, bonus = _rwkv_prep(
            pa_ref[...], shift_ref[...], mu_ref[...], w0_ref[...], w2_ref[...], a0_ref[...], a2_ref[...],
            g2_ref[...], kkw_ref[...], ka_ref[...], rk_ref[...], hb_ref[...])
        for i, x in enumerate((-kk, jnp.exp(logw), kk * a, k2, r, v)):
            tr_ref[i] = x.T
        g_ref[...] = g
        bonus_ref[...] = bonus

    base = pl.multiple_of(h * HEAD_A, HEAD_A)
    hs = pl.ds(base, HEAD_A)
    a_t, w_t, b_t, k_t, r_t = (tr_ref[i, hs, :] for i in range(5))

    def value_row(vi, carry):
        st = wkv_ref[0, vi]
        sa = jnp.sum(st * a_t, axis=0, keepdims=True)
        st = st * w_t + sa * b_t + tr_ref[5, pl.ds(base + vi, 1), :] * k_t
        wkvo_ref[0, vi] = st
        yt_ref[pl.ds(base + vi, 1), :] = jnp.sum(st * r_t, axis=0, keepdims=True)
        return carry

    lax.fori_loop(0, HEAD_A, value_row, 0, unroll=8)

    @pl.when(h == H_A - 1)
    def _():
        oa_ref[...] = _group_norm_gate(yt_ref[...].T, bonus_ref[...], g_ref[...], lnw_ref[...], lnb_ref[...],
                                       hb_ref[...])


def _decode_rwkv(pa, shift, wkv_t, rw_consts):
    n = pa.shape[0]
    full = lambda w: pl.BlockSpec((n, w), lambda h: (0, 0))
    state = pl.BlockSpec((1, HEAD_A, HEAD_A, n), lambda h: (h, 0, 0, 0))
    return pl.pallas_call(
        _decode_rwkv_kernel,
        grid=(H_A,),
        in_specs=[full(A_COLS), full(A_COLS), state] + [_const_spec(c.shape) for c in rw_consts],
        out_specs=[full(C_A), state],
        out_shape=[jax.ShapeDtypeStruct((n, C_A), F32), jax.ShapeDtypeStruct(wkv_t.shape, F32)],
        scratch_shapes=[pltpu.VMEM((6, C_A, n), F32), pltpu.VMEM((C_A, n), F32),
                        pltpu.VMEM((n, C_A), F32), pltpu.VMEM((n, C_A), F32)],
        compiler_params=pltpu.CompilerParams(dimension_semantics=("arbitrary",), vmem_limit_bytes=VMEM_LIMIT),
        name="decode_rwkv",
    )(pa, shift, wkv_t, *rw_consts)


def _decode_gdn_kernel(qkv_ref, z_ref, ba_ref, cst_ref, gdn_ref, conv_ref, alog_ref, dtb_ref, gnorm_ref,
                       ob_ref, gdno_ref):
    cst = cst_ref[...]
    q, k, vv, beta, glog = _gdn_prep(cst[:, :CONV_CH], cst[:, CONV_CH:2 * CONV_CH], cst[:, 2 * CONV_CH:],
                                     qkv_ref[...], ba_ref[...], conv_ref[...], alog_ref[...], dtb_ref[...])
    eg = jnp.exp(glog)
    k_cols = _to_columns(k)
    ls = lambda h: slice(DK * h, DK * (h + 1))

    def split2(x):
        hi = x.astype(BF16)
        return hi, x - hi.astype(F32)

    rows_a, rows_b = [], []
    for h in range(H_B):
        e_h = eg[:, H_B + h:H_B + h + 1]
        (wh, wl), (gh, gl_) = split2(k[:, ls(h)] * (beta[:, h:h + 1] * e_h)), split2(q[:, ls(h)] * e_h)
        rows_a.append(jnp.concatenate([wh.astype(F32), gh.astype(F32), wl, gl_], axis=0).astype(BF16))
        rows_b.append(jnp.concatenate([wh.astype(F32), gh.astype(F32)], axis=0).astype(BF16))
    gunits = [(s_i, h) for s_i in range(DEC_TILE) for h in range(H_B)]
    gn = range(len(gunits))
    be = [beta[s_i:s_i + 1, h:h + 1] for s_i, h in gunits]
    e = [eg[s_i:s_i + 1, H_B + h:H_B + h + 1] for s_i, h in gunits]
    kc = [k_cols[ls(h), s_i:s_i + 1] for s_i, h in gunits]
    gst = [gdn_ref[s_i, h] for s_i, h in gunits]
    parts = [split2(st) for st in gst]
    ra = [_dot(rows_a[h], parts[i][0].astype(BF16)) + jnp.concatenate(
        [_dot(rows_b[h], parts[i][1].astype(BF16)), jnp.zeros((2 * DEC_TILE, DV), F32)], axis=0)
        for i, (s_i, h) in enumerate(gunits)]
    pick = lambda r, j, s_i: r[DEC_TILE * j + s_i:DEC_TILE * j + s_i + 1]
    ws = [pick(ra[i], 0, s_i) + pick(ra[i], 2, s_i) for i, (s_i, h) in enumerate(gunits)]
    qs = [pick(ra[i], 1, s_i) + pick(ra[i], 3, s_i) for i, (s_i, h) in enumerate(gunits)]
    v_new = [be[i] * vv[s_i:s_i + 1, ls(h)] - ws[i] for i, (s_i, h) in enumerate(gunits)]
    qk = [jnp.sum(q[s_i:s_i + 1, ls(h)] * k[s_i:s_i + 1, ls(h)], axis=-1, keepdims=True) for s_i, h in gunits]
    for i, (s_i, h) in enumerate(gunits):
        gdno_ref[s_i, h] = gst[i] * e[i] + kc[i] * v_new[i]
    o_units = [qs[i] + qk[i] * v_new[i] for i in gn]
    o = jnp.concatenate([jnp.concatenate(o_units[H_B * s_i:H_B * (s_i + 1)], axis=1) for s_i in range(DEC_TILE)],
                        axis=0)
    ob_ref[...] = _head_norm_gate(o, z_ref[...], gnorm_ref[...])


def _decode_gdn(qkv, z, ba, cst, gdn, gdn_consts):
    n = qkv.shape[0]
    row = lambda w: pl.BlockSpec((DEC_TILE, w), lambda i: (i, 0))
    state = pl.BlockSpec((DEC_TILE, H_B, DK, DV), lambda i: (i, 0, 0, 0))
    return pl.pallas_call(
        _decode_gdn_kernel,
        grid=(n // DEC_TILE,),
        in_specs=[row(CONV_CH), row(C_BV), row(LANES), row(3 * CONV_CH), state]
                 + [_const_spec(c.shape) for c in gdn_consts],
        out_specs=[row(C_BV), state],
        out_shape=[jax.ShapeDtypeStruct((n, C_BV), F32), jax.ShapeDtypeStruct(gdn.shape, F32)],
        compiler_params=pltpu.CompilerParams(dimension_semantics=("parallel",), vmem_limit_bytes=VMEM_LIMIT),
        name="decode_gdn",
    )(qkv, z, ba, cst, gdn, *gdn_consts)


def _post_kernel(x_ref, oa_ref, ob_ref, pe_ref, nmix_ref, wg_ref, wba_ref, wbb_ref, wout_ref, nffn_ref,
                 wfg_ref, wfu_ref, wfd_ref, nple_ref, wpg_ref, wpp_ref, nfin_ref, y_ref):
    nrow = x_ref.shape[0]
    ngroup = max(1, nrow // POST_GROUP_ROWS)
    groups = [slice(r0, r0 + nrow // ngroup) for r0 in range(0, nrow, nrow // ngroup)]
    x = [x_ref[g, :] for g in groups]
    ma = [_dot(oa_ref[g, :].astype(BF16), wba_ref[...]) for g in groups]
    mb = [_dot(ob_ref[g, :].astype(BF16), wbb_ref[...]) for g in groups]
    u = [_rms(xi, nmix_ref[...]).astype(BF16) for xi in x]
    gates = [_sigmoid(_dot(ui, wg_ref[...])) for ui in u]
    mix = [gi[:, :D_MODEL] * mai + gi[:, D_MODEL:] * mbi for gi, mai, mbi in zip(gates, ma, mb)]
    h = [xi + _dot(mi.astype(BF16), wout_ref[...]) for xi, mi in zip(x, mix)]
    pp = [_dot(pe_ref[g, :].astype(BF16), wpp_ref[...]) for g in groups]
    u2 = [_rms(hi, nffn_ref[...]).astype(BF16) for hi in h]
    fg = [_silu(_dot(ui, wfg_ref[...])) for ui in u2]
    ff = [fgi * _dot(ui, wfu_ref[...]) for fgi, ui in zip(fg, u2)]
    h = [hi + _dot(fi.astype(BF16), wfd_ref[...]) for hi, fi in zip(h, ff)]
    u3 = [_rms(hi, nple_ref[...]).astype(BF16) for hi in h]
    h = [hi + _sigmoid(_dot(ui, wpg_ref[...])) * ppi for hi, ui, ppi in zip(h, u3, pp)]
    for g, hi in zip(groups, h):
        y_ref[g, :] = _rms(hi, nfin_ref[...])


def _post(x, oa, ob, pe, consts):
    n = x.shape[0]
    tm = min(ROW_TILE, n)
    row = lambda w: pl.BlockSpec((tm, w), lambda i: (i, 0))
    return pl.pallas_call(
        _post_kernel,
        grid=(n // tm,),
        in_specs=[row(D_MODEL), row(C_A), row(C_BV), row(PLE_DIM)] + [_const_spec(c.shape) for c in consts],
        out_specs=row(D_MODEL),
        out_shape=jax.ShapeDtypeStruct((n, D_MODEL), F32),
        compiler_params=pltpu.CompilerParams(dimension_semantics=("parallel",), vmem_limit_bytes=VMEM_LIMIT),
        name="post",
    )(x, oa, ob, pe, *consts)


def _chunk_cumsum_matrix(tile):
    i = jnp.arange(tile)
    same = (i[:, None] // CHUNK) == (i[None, :] // CHUNK)
    return (same & (i[None, :] <= i[:, None])).astype(BF16)


def kernel(x_prompt, x_sample, p_prompt, p_sample, state_shift, state_wkv, state_conv, state_gdn, norm_mix, w_in, mu_shift, rw_w0, rw_w2, rw_a0, rw_a2, rw_g2, rw_kk, rw_ka, rw_rk, rw_ln_w, rw_ln_b, gdn_conv, gdn_a_log, gdn_dt_bias, gdn_norm, w_branch_a, w_branch_b, w_out, norm_ffn, w_ffn_gate, w_ffn_up, w_ffn_down, norm_ple, w_ple_gate, w_ple_proj, norm_final):
    bsz, seq, _ = x_prompt.shape
    nd = x_sample.shape[0]
    row = lambda p: p.reshape(1, -1)

    w_in0 = w_in[0]
    b0 = A_COLS
    wa = w_in0[:, :b0].astype(BF16)
    wq = w_in0[:, b0:b0 + CONV_CH].astype(BF16)
    wz = w_in0[:, b0 + CONV_CH:b0 + CONV_CH + C_BV].astype(BF16)
    wb = jnp.pad(w_in0[:, b0 + CONV_CH + C_BV:b0 + CONV_CH + C_BV + 2 * H_B], ((0, 0), (0, LANES - 2 * H_B))).astype(BF16)
    wg = w_in0[:, b0 + CONV_CH + C_BV + 2 * H_B:].astype(BF16)
    w2p = jnp.concatenate([rw_w2[0], jnp.zeros((LORA_A, C_A), F32)], axis=0).astype(BF16)
    a2p = jnp.concatenate([jnp.zeros((LORA_W, C_A), F32), rw_a2[0]], axis=0).astype(BF16)
    ch = jnp.arange(LANES) // HEAD_A
    hb = (ch[:, None] == ch[None, :]).astype(BF16)
    tri = _chunk_cumsum_matrix(T_TILE)
    alog = jnp.pad(gdn_a_log[0], (H_B, LANES - 2 * H_B)).reshape(1, LANES)
    dtb = jnp.pad(gdn_dt_bias[0], (H_B, LANES - 2 * H_B)).reshape(1, LANES)
    rw_consts = (row(mu_shift[0]), row(rw_w0[0]), w2p, row(rw_a0[0]), a2p, rw_g2[0].astype(BF16), row(rw_kk[0]), row(rw_ka[0]),
                 row(rw_rk[0]), row(rw_ln_w[0]), row(rw_ln_b[0]), hb)
    gdn_consts = (gdn_conv[0], alog, dtb, row(gdn_norm[0]))
    post_consts = (row(norm_mix[0]), wg, w_branch_a[0].astype(BF16), w_branch_b[0].astype(BF16),
                   w_out[0].astype(BF16), row(norm_ffn[0]), w_ffn_gate[0].astype(BF16), w_ffn_up[0].astype(BF16),
                   w_ffn_down[0].astype(BF16), row(norm_ple[0]), w_ple_gate[0].astype(BF16),
                   w_ple_proj[0].astype(BF16), row(norm_final))

    xp = x_prompt.reshape(bsz * seq, D_MODEL)
    oa, wkv_pairs, shift_tail, ob, gdn_p, conv_tail = _mixers_prompt(
        x_prompt, row(norm_mix[0]), wa, wq, wz, wb, rw_consts, gdn_consts, tri)
    y_prompt = _post(xp, oa.reshape(bsz * seq, C_A), ob.reshape(bsz * seq, C_BV),
                     p_prompt[0].reshape(bsz * seq, PLE_DIM), post_consts).reshape(bsz, seq, D_MODEL)
    wkv_p = jnp.stack([wkv_pairs[:, :, :HEAD_A, :HEAD_A], wkv_pairs[:, :, HEAD_A:, HEAD_A:]], axis=2)
    wkv_p = wkv_p.reshape(bsz, H_A, HEAD_A, HEAD_A)
    shift_p = shift_tail[:, HALO - 1:, :]
    conv_p = conv_tail[:, HALO - (CONV_W - 1):, :]

    xs = x_sample.reshape(nd, D_MODEL)
    pa_s, qkv_s, z_s, ba_s = _inproj(xs, row(norm_mix[0]), wa, wq, wz, wb)
    cst = state_conv[0].reshape(nd, (CONV_W - 1) * CONV_CH)
    oa_s, wkv_t = _decode_rwkv(pa_s, state_shift[0].reshape(nd, A_COLS), jnp.transpose(state_wkv[0], (1, 2, 3, 0)),
                               rw_consts)
    wkv_s = jnp.transpose(wkv_t, (3, 0, 1, 2))
    ob_s, gdn_s = _decode_gdn(qkv_s, z_s, ba_s, cst, state_gdn[0], gdn_consts)
    y_sample = _post(xs, oa_s, ob_s, p_sample[0].reshape(nd, PLE_DIM), post_consts).reshape(nd, 1, D_MODEL)
    conv_s = jnp.concatenate([cst[:, CONV_CH:], qkv_s], axis=1).reshape(nd, CONV_W - 1, CONV_CH)

    return (y_prompt, y_sample, shift_p[None], wkv_p[None], conv_p[None], gdn_p[None],
            pa_s.reshape(1, nd, 1, A_COLS), wkv_s[None], conv_s[None], gdn_s[None])
```

```python
import functools

import jax
import jax.numpy as jnp
from jax import lax
from jax.experimental import pallas as pl
from jax.experimental.pallas import tpu as pltpu

F32 = jnp.float32
BF16 = jnp.bfloat16

D_MODEL = 1024
HEAD_A = 64
C_A = 512
H_A = 8
LORA_W = 64
LORA_A = 64
LORA_G = 128
A_COLS = 3 * C_A + LORA_W + LORA_A + LORA_G
DK = 128
DV = 128
H_B = 4
C_BK = 512
C_BV = 512
CONV_W = 4
CONV_CH = 2 * C_BK + C_BV
D_FF = 2816
PLE_DIM = 256
NORM_EPS = 1e-6
GN_EPS = 64e-5
L2_EPS = 1e-6

LANES = 128
CHUNK = 64
PAIR = 2 * CHUNK
T_TILE = 256
PROJ_COLS = 256
ROW_TILE = 512
POST_GROUP_ROWS = 256
DEC_TILE = 8
HALO = 8
V7X_VMEM_BYTES = 64 * 1024 * 1024
VMEM_LIMIT = V7X_VMEM_BYTES * 7 // 8
SUM_TERMS = 1
CUMSUM_TERMS = 2
INV_TERMS = 1
CHAIN_TERMS = 1


def _dot(a, b):
    return jnp.dot(a, b, preferred_element_type=F32)


def _sigmoid(x):
    return 1.0 / (1.0 + jnp.exp(-x))


def _silu(x):
    return x * _sigmoid(x)


def _softplus(x):
    return jnp.maximum(x, 0.0) + jnp.log(1.0 + jnp.exp(-jnp.abs(x)))


def _rms(x, gain):
    return x * lax.rsqrt(jnp.mean(x * x, axis=-1, keepdims=True) + NORM_EPS) * gain


def _pair_masks():
    ri = lax.broadcasted_iota(jnp.int32, (PAIR, PAIR), 0)
    ci = lax.broadcasted_iota(jnp.int32, (PAIR, PAIR), 1)
    same = (ri < CHUNK) == (ci < CHUNK)
    strict = same & (ci < ri)
    incl = same & (ci <= ri)
    eye = (ri == ci).astype(F32)
    return strict, incl, eye


_NN = (((1,), (0,)), ((), ()))
_NT = (((1,), (1,)), ((), ()))
_TN = (((0,), (0,)), ((), ()))


def _split(x, terms):
    if isinstance(x, (list, tuple)):
        return list(x)
    if x.dtype == BF16:
        return [x]
    parts = []
    for i in range(terms):
        h = x.astype(BF16)
        parts.append(h)
        if i + 1 < terms:
            x = x - h.astype(F32)
    return parts


def _mm(a, b, dims=_NN, na=1, nb=1):
    pa, pb = _split(a, na), _split(b, nb)
    acc = None
    for i, ai in enumerate(pa):
        for j, bj in enumerate(pb):
            if i + j < max(len(pa), len(pb)):
                d = lax.dot_general(ai, bj, dims, preferred_element_type=F32)
                acc = d if acc is None else acc + d
    return acc


def _neumann_inverse_many(lmats, eye, terms):
    ts = [eye + l for l in lmats]
    ps = [_split(l, terms) for l in lmats]
    n = 2
    while n < CHUNK:
        ps = [_split(_mm(p, p), terms) for p in ps]
        ts = [t + _mm(t, p, _NN, terms) for t, p in zip(ts, ps)]
        n *= 2
    return ts


def _inproj_kernel(x_ref, g_ref, wa_ref, wq_ref, wz_ref, wb_ref, pa_ref, qkv_ref, z_ref, ba_ref):
    u = _rms(x_ref[...], g_ref[...]).astype(BF16)
    pa_ref[...] = _dot(u, wa_ref[...])
    qkv_ref[...] = _dot(u, wq_ref[...])
    z_ref[...] = _dot(u, wz_ref[...])
    ba_ref[...] = _dot(u, wb_ref[...])


def _const_spec(shape):
    nd = len(shape)
    return pl.BlockSpec(shape, lambda *_: (0,) * nd, pipeline_mode=pl.Buffered(1))


def _inproj(x, gain, wa, wq, wz, wb):
    n = x.shape[0]
    tm = min(ROW_TILE, n)
    row = lambda w: pl.BlockSpec((tm, w), lambda i: (i, 0))
    return pl.pallas_call(
        _inproj_kernel,
        grid=(n // tm,),
        in_specs=[row(D_MODEL), _const_spec(gain.shape), _const_spec(wa.shape), _const_spec(wq.shape),
                  _const_spec(wz.shape), _const_spec(wb.shape)],
        out_specs=[row(A_COLS), row(CONV_CH), row(C_BV), row(LANES)],
        out_shape=[jax.ShapeDtypeStruct((n, A_COLS), F32), jax.ShapeDtypeStruct((n, CONV_CH), F32),
                   jax.ShapeDtypeStruct((n, C_BV), F32), jax.ShapeDtypeStruct((n, LANES), F32)],
        compiler_params=pltpu.CompilerParams(dimension_semantics=("parallel",), vmem_limit_bytes=VMEM_LIMIT),
        name="inproj",
    )(x, gain, wa, wq, wz, wb)


def _rwkv_prep(pa, prev, mu, w0, w2p, a0, a2p, g2, kkw, ka, rk, hb, between=lambda: None):
    xa = pa + (prev - pa) * mu
    r = xa[:, :C_A]
    k = xa[:, C_A:2 * C_A]
    v = xa[:, 2 * C_A:3 * C_A]
    xwa = xa[:, 3 * C_A:3 * C_A + LORA_W + LORA_A]
    xg = xa[:, 3 * C_A + LORA_W + LORA_A:]
    w_pre = _mm(jnp.tanh(xwa), w2p)
    a_pre = _mm(xwa, a2p)
    g = _mm(_sigmoid(xg), g2)
    kx = k * kkw
    kk_den = _head_sum(kx * kx, hb)
    between()
    w_log = -_softplus(-(w0 + w_pre)) - 0.5
    logw = -jnp.exp(w_log)
    a = _sigmoid(a0 + a_pre)
    kk = kx * lax.rsqrt(kk_den + L2_EPS)
    k2 = k * (1.0 + (a - 1.0) * ka)
    bonus_sum = _head_sum(r * k2 * rk, hb)
    between()
    bonus = bonus_sum * v
    return r, k2, v, logw, a, g, kk, bonus


def _head_sum(x, hb):
    groups = [_mm(x[:, LANES * j:LANES * (j + 1)], hb, _NN, SUM_TERMS) for j in range(x.shape[1] // LANES)]
    return jnp.concatenate(groups, axis=1)


def _group_norm_gate(y, bonus, g, lnw, lnb, hb):
    mean = _head_sum(y, hb) * (1.0 / HEAD_A)
    d = y - mean
    var = _head_sum(d * d, hb) * (1.0 / HEAD_A)
    yn = d * lax.rsqrt(var + GN_EPS) * lnw + lnb
    return (yn + bonus) * g


def _gdn_prep(x0, x1, x2, x3, ba, conv, alog, dtb):
    c = _silu(x0 * conv[0:1] + x1 * conv[1:2] + x2 * conv[2:3] + x3 * conv[3:4])
    qs, ks = [], []
    for h in range(H_B):
        qh = c[:, DK * h:DK * (h + 1)]
        kh = c[:, C_BK + DK * h:C_BK + DK * (h + 1)]
        qs.append(qh * lax.rsqrt(jnp.sum(qh * qh, axis=-1, keepdims=True) + L2_EPS) * (DK ** -0.5))
        ks.append(kh * lax.rsqrt(jnp.sum(kh * kh, axis=-1, keepdims=True) + L2_EPS))
    q = jnp.concatenate(qs, axis=1)
    k = jnp.concatenate(ks, axis=1)
    v = c[:, 2 * C_BK:]
    beta = _sigmoid(ba)
    glog = -jnp.exp(alog) * _softplus(ba + dtb)
    return q, k, v, beta, glog


def _head_norm_gate(o, z, gnorm):
    outs = []
    for h in range(H_B):
        oh = o[:, DV * h:DV * (h + 1)]
        zh = z[:, DV * h:DV * (h + 1)]
        oh = oh * lax.rsqrt(jnp.mean(oh * oh, axis=-1, keepdims=True) + NORM_EPS) * gnorm
        outs.append(oh * _silu(zh))
    return jnp.concatenate(outs, axis=1)


def _tile_maps(bsz, seq, tile):
    tps = seq // tile
    last = bsz * tps - 1
    cur = lambda s: jnp.minimum(s, last)
    prv = lambda s: jnp.maximum(s - 1, 0)
    load = lambda s: (cur(s) // tps, cur(s) % tps, 0)
    emit = lambda s: (prv(s) // tps, prv(s) % tps, 0)
    per_seq3 = lambda s: (prv(s) // tps, 0, 0)
    per_seq4 = lambda s: (prv(s) // tps, 0, 0, 0)
    return tps, last + 2, load, emit, per_seq3, per_seq4


def _gdn_prep_groups(x0, x1, x2, x3, conv):
    def conv_cols(c0, c1):
        return _silu(x0[:, c0:c1] * conv[0:1, c0:c1] + x1[:, c0:c1] * conv[1:2, c0:c1]
                     + x2[:, c0:c1] * conv[2:3, c0:c1] + x3[:, c0:c1] * conv[3:4, c0:c1])

    def unit(xh):
        return xh * lax.rsqrt(jnp.sum(xh * xh, axis=-1, keepdims=True) + L2_EPS)

    for h in range(H_B):
        yield 0, h, unit(conv_cols(DK * h, DK * (h + 1))) * (DK ** -0.5)
        yield 1, h, unit(conv_cols(C_BK + DK * h, C_BK + DK * (h + 1)))
        yield 2, h, conv_cols(2 * C_BK + DV * h, 2 * C_BK + DV * (h + 1))


def _mixers_prompt_kernel(tiles_per_seq, x_ref, nmix_ref, wa_ref, wq_ref, wz_ref, wb_ref,
                          mu_ref, w0_ref, w2_ref, a0_ref, a2_ref, g2_ref, kkw_ref, ka_ref, rk_ref,
                          lnw_ref, lnb_ref, hb_ref, conv_ref, alog_ref, dtb_ref, gnorm_ref, tri_ref,
                          oa_ref, wkv_ref, shift_ref, ob_ref, gdn_ref, convo_ref,
                          pa_ref, exta_ref, sa_ref, at_ref, bt_ref, kt_ref, rt_ref, va_ref, bh_ref, kh_ref, y_ref,
                          qkvn_ref, zn_ref, ban_ref, extb_ref, sb_ref, q_ref, k_ref, vb_ref, gc_ref, be_ref, o_ref):
    s_id = pl.program_id(0)

    @pl.when(s_id == 0)
    def _():
        pa_ref[...] = jnp.zeros_like(pa_ref)
        qkvn_ref[...] = jnp.zeros_like(qkvn_ref)
        zn_ref[...] = jnp.zeros_like(zn_ref)
        ban_ref[...] = jnp.zeros_like(ban_ref)

    @pl.when(jnp.maximum(s_id - 1, 0) % tiles_per_seq == 0)
    def _():
        exta_ref[0:HALO, :] = jnp.zeros((HALO, A_COLS), F32)
        extb_ref[0:HALO, :] = jnp.zeros((HALO, CONV_CH), F32)
        sa_ref[...] = jnp.zeros_like(sa_ref)
        sb_ref[...] = jnp.zeros_like(sb_ref)

    pa = pa_ref[...]
    exta_ref[HALO:, :] = pa
    prev = exta_ref[pl.ds(HALO - 1, T_TILE), :]
    exta_ref[0:HALO, :] = pa[T_TILE - HALO:, :]
    shift_ref[0] = pa[T_TILE - HALO:, :]
    x3 = qkvn_ref[...]
    z = zn_ref[...]
    ba = ban_ref[...]
    extb_ref[HALO:, :] = x3
    x0 = extb_ref[pl.ds(HALO - 3, T_TILE), :]
    x1 = extb_ref[pl.ds(HALO - 2, T_TILE), :]
    x2 = extb_ref[pl.ds(HALO - 1, T_TILE), :]
    extb_ref[0:HALO, :] = x3[T_TILE - HALO:, :]
    convo_ref[0] = x3[T_TILE - HALO:, :]

    u_next = _rms(x_ref[0], nmix_ref[...]).astype(BF16)
    blocks = [(w_ref, dst_ref, c0, min(c0 + PROJ_COLS, w_ref.shape[1]))
              for w_ref, dst_ref in ((wa_ref, pa_ref), (wq_ref, qkvn_ref), (wz_ref, zn_ref), (wb_ref, ban_ref))
              for c0 in range(0, w_ref.shape[1], PROJ_COLS)]

    def project(count):
        for w_ref, dst_ref, c0, c1 in blocks[:count]:
            dst_ref[:, c0:c1] = _dot(u_next, w_ref[:, c0:c1])
        del blocks[:count]

    gdn_groups = _gdn_prep_groups(x0, x1, x2, x3, conv_ref[...])
    gdn_dst = (q_ref, k_ref, vb_ref)

    def gdn_prep(count):
        for _ in range(count):
            item = next(gdn_groups, None)
            if item is not None:
                kind, h, arr = item
                gdn_dst[kind][:, DK * h:DK * (h + 1)] = arr

    def between():
        project(3)
        gdn_prep(3)

    hb = hb_ref[...]
    r, k2, v, logw, a, g, kk, bonus = _rwkv_prep(
        pa, prev, mu_ref[...], w0_ref[...], w2_ref[...], a0_ref[...], a2_ref[...], g2_ref[...],
        kkw_ref[...], ka_ref[...], rk_ref[...], hb, between)

    nchunk = T_TILE // CHUNK
    cs = _mm(tri_ref[...], logw, _NN, 1, CUMSUM_TERMS)
    between()
    ends = [cs[CHUNK * (c + 1) - 1:CHUNK * (c + 1)] for c in range(nchunk)]
    tot = jnp.concatenate([jnp.broadcast_to(e, (CHUNK, C_A)) for e in ends], axis=0)
    dinv = jnp.exp(-cs)
    dend = jnp.exp(tot - cs)
    b_in = kk * a
    at_ref[...] = -kk * jnp.exp(cs - logw)
    bt_ref[...] = b_in * dinv
    kt_ref[...] = k2 * dinv
    between()
    rt_ref[...] = r * jnp.exp(cs)
    va_ref[...] = v
    bh_ref[...] = b_in * dend
    kh_ref[...] = k2 * dend
    gl = [jnp.exp(e) for e in ends]
    project(len(blocks))
    gdn_prep(3 * H_B)
    be_ref[...] = _sigmoid(ba)
    glog = -jnp.exp(alog_ref[...]) * _softplus(ba + dtb_ref[...])
    gc_ref[...] = _mm(tri_ref[...], glog, _NN, 1, CUMSUM_TERMS)

    strict, incl, eye = _pair_masks()
    low = lax.broadcasted_iota(jnp.int32, (CHUNK, LANES), 1) < HEAD_A

    def stack(x):
        return jnp.concatenate([jnp.where(low, x, 0.0), jnp.where(low, 0.0, x)], axis=0)

    npair = C_A // LANES
    lanes = lambda p: slice(LANES * p, LANES * (p + 1))
    rows = lambda c: slice(CHUNK * c, CHUNK * (c + 1))
    half = lambda j: slice(CHUNK * j, CHUNK * (j + 1))
    ia = [(p, c) for p in range(npair) for c in range(nchunk)]
    ib = [(pr, c) for pr in range(H_B // 2) for c in range(nchunk)]
    na, nb = range(len(ia)), range(len(ib))
    ld = lambda ref: [stack(ref[rows(c), lanes(p)]) for p, c in ia]
    at, bt, kt, rt, vs, bh, kh = (ld(ref) for ref in (at_ref, bt_ref, kt_ref, rt_ref, va_ref, bh_ref, kh_ref))
    cat = lambda ref: [jnp.concatenate([ref[rows(c), DK * h:DK * (h + 1)] for h in (2 * pr, 2 * pr + 1)], axis=0)
                       for pr, c in ib]
    col = lambda ref, off: [jnp.concatenate([ref[rows(c), off + h:off + h + 1] for h in (2 * pr, 2 * pr + 1)], axis=0)
                            for pr, c in ib]
    qg_, kg_, vg_ = cat(q_ref), cat(k_ref), cat(vb_ref)
    beta_c = col(be_ref, 0)
    gc = col(gc_ref, H_B)
    gt = [jnp.concatenate([jnp.broadcast_to(g_[CHUNK * (i + 1) - 1:CHUNK * (i + 1)], (CHUNK, 1)) for i in range(2)],
                          axis=0) for g_ in gc]
    decay = []
    for g_ in gc:
        gc_full = jnp.broadcast_to(g_, (PAIR, PAIR))
        decay.append(jnp.exp(gc_full - gc_full.T))
    kb = [kg_[i] * beta_c[i] for i in nb]
    vbb = [vg_[i] * beta_c[i] for i in nb]
    eg = [jnp.exp(g_) for g_ in gc]

    aa = [_mm(jnp.concatenate([at[i], rt[i]], axis=0), jnp.concatenate([bt[i], kt[i]], axis=0), _NT) for i in na]
    kq = [_mm(jnp.concatenate([kb[i], qg_[i]], axis=0), kg_[i], _NT) for i in nb]
    a_ab = [jnp.where(strict, aa[i][:PAIR, :PAIR], 0.0) for i in na]
    a_ak = [jnp.where(strict, aa[i][:PAIR, PAIR:], 0.0) for i in na]
    a_rb = [jnp.where(incl, aa[i][PAIR:, :PAIR], 0.0) for i in na]
    a_rk = [jnp.where(incl, aa[i][PAIR:, PAIR:], 0.0) for i in na]
    lneg = [jnp.where(strict, -(kq[i][:PAIR] * decay[i]), 0.0) for i in nb]
    qk = [jnp.where(incl, kq[i][PAIR:] * decay[i], 0.0) for i in nb]
    tall = _neumann_inverse_many(a_ab + lneg, eye, INV_TERMS)
    tinv, tinvb = tall[:len(ia)], tall[len(ia):]
    akv = [_mm(a_ak[i], vs[i]) for i in na]
    uw = [_mm(tinvb[i], jnp.concatenate([vbb[i], kb[i] * eg[i]], axis=1)) for i in nb]
    gu = [_mm(tinv[i], jnp.concatenate([at[i], akv[i]], axis=1)) for i in na]
    ow = [_mm(qk[i], uw[i]) for i in nb]
    ry = [_mm(a_rb[i], gu[i]) for i in na]
    rkv = [_mm(a_rk[i], vs[i]) for i in na]
    kd = [kg_[i] * jnp.exp(gt[i] - gc[i]) for i in nb]
    pqb = [[_mm(kd[i][half(j)], uw[i][half(j)], _TN) for j in range(2)] for i in nb]
    pq = [_mm(gu[i], bh[i], _TN) for i in na]
    vk = [_mm(vs[i], kh[i], _TN) for i in na]
    rp = [rt[i] + ry[i][:, :PAIR] for i in na]
    yc = [ry[i][:, PAIR:] + rkv[i] for i in na]
    pm = [eye * gl[ia[i][1]][:, lanes(ia[i][0])] + pq[i][:PAIR] for i in na]
    qm = [pq[i][PAIR:] + vk[i] for i in na]
    oc = [ow[i][:, :DV] for i in nb]
    rq = [qg_[i] * eg[i] - ow[i][:, DV:] for i in nb]
    pmb = [[eye * jnp.exp(gt[i][CHUNK * j:CHUNK * j + 1]) - pqb[i][j][:, DV:] for j in range(2)] for i in nb]

    sa = [sa_ref[p] for p in range(npair)]
    sb = [sb_ref[h] for h in range(H_B)]
    for c in range(nchunk):
        for p in range(npair):
            i = ia.index((p, c))
            y = _mm(rp[i], sa[p], _NT) + yc[i]
            y_ref[rows(c), lanes(p)] = y[:CHUNK] + y[CHUNK:]
            sa[p] = _mm(sa[p], pm[i], _NN, CHAIN_TERMS, CHAIN_TERMS) + qm[i]
        for h in range(H_B):
            i, j = ib.index((h // 2, c)), h % 2
            o_ref[rows(c), DV * h:DV * (h + 1)] = _mm(rq[i][half(j)], sb[h]) + oc[i][half(j)]
            sb[h] = _mm(pmb[i][j], sb[h], _NN, CHAIN_TERMS, CHAIN_TERMS) + pqb[i][j][:, :DV]
    for p in range(npair):
        sa_ref[p] = sa[p]
    for h in range(H_B):
        sb_ref[h] = sb[h]

    oa_ref[0] = _group_norm_gate(y_ref[...], bonus, g, lnw_ref[...], lnb_ref[...], hb)
    wkv_ref[0] = sa_ref[...]
    ob_ref[0] = _head_norm_gate(o_ref[...], z, gnorm_ref[...])
    gdn_ref[0] = sb_ref[...]


def _mixers_prompt(x, nmix, wa, wq, wz, wb, rw_consts, gdn_consts, tri):
    bsz, seq, _ = x.shape
    consts = (nmix, wa, wq, wz, wb) + tuple(rw_consts) + tuple(gdn_consts) + (tri,)
    npair = C_A // LANES
    tps, steps, load, emit, per_seq3, per_seq4 = _tile_maps(bsz, seq, T_TILE)
    tile = lambda w: pltpu.VMEM((T_TILE, w), F32)
    return pl.pallas_call(
        functools.partial(_mixers_prompt_kernel, tps),
        grid=(steps,),
        in_specs=[pl.BlockSpec((1, T_TILE, D_MODEL), load)] + [_const_spec(c.shape) for c in consts],
        out_specs=[pl.BlockSpec((1, T_TILE, C_A), emit),
                   pl.BlockSpec((1, npair, LANES, LANES), per_seq4),
                   pl.BlockSpec((1, HALO, A_COLS), per_seq3),
                   pl.BlockSpec((1, T_TILE, C_BV), emit),
                   pl.BlockSpec((1, H_B, DK, DV), per_seq4),
                   pl.BlockSpec((1, HALO, CONV_CH), per_seq3)],
        out_shape=[jax.ShapeDtypeStruct((bsz, seq, C_A), F32),
                   jax.ShapeDtypeStruct((bsz, npair, LANES, LANES), F32),
                   jax.ShapeDtypeStruct((bsz, HALO, A_COLS), F32),
                   jax.ShapeDtypeStruct((bsz, seq, C_BV), F32),
                   jax.ShapeDtypeStruct((bsz, H_B, DK, DV), F32),
                   jax.ShapeDtypeStruct((bsz, HALO, CONV_CH), F32)],
        scratch_shapes=[tile(A_COLS), pltpu.VMEM((T_TILE + HALO, A_COLS), F32), pltpu.VMEM((npair, LANES, LANES), F32)]
                       + [tile(C_A) for _ in range(8)]
                       + [tile(CONV_CH), tile(C_BV), tile(LANES), pltpu.VMEM((T_TILE + HALO, CONV_CH), F32),
                          pltpu.VMEM((H_B, DK, DV), F32), tile(C_BK), tile(C_BK), tile(C_BV), tile(LANES), tile(LANES),
                          tile(C_BV)],
        compiler_params=pltpu.CompilerParams(dimension_semantics=("arbitrary",), vmem_limit_bytes=VMEM_LIMIT),
        name="mixers_prompt",
    )(x, *consts)


def _to_columns(x):
    pad = jnp.zeros((LANES - DEC_TILE, x.shape[1]), F32)
    return jnp.concatenate([x, pad], axis=0).T


def _decode_rwkv_kernel(pa_ref, shift_ref, wkv_ref,
                        mu_ref, w0_ref, w2_ref, a0_ref, a2_ref, g2_ref, kkw_ref, ka_ref, rk_ref, lnw_ref, lnb_ref,
                        hb_ref, oa_ref, wkvo_ref, tr_ref, yt_ref, g_ref, bonus_ref):
    h = pl.program_id(0)

    @pl.when(h == 0)
    def _():
        r, k2, v, logw, a, g, kk, bonus = _rwkv_prep(
            pa_ref[...], shift_ref[...], mu_ref[...], w0_ref[...], w2_ref[...], a0_ref[...], a2_ref[...],
            g2_ref[...], kkw_ref[...], ka_ref[...], rk_ref[...], hb_ref[...])
        for i, x in enumerate((-kk, jnp.exp(logw), kk * a, k2, r, v)):
            tr_ref[i] = x.T
        g_ref[...] = g
        bonus_ref[...] = bonus

    base = pl.multiple_of(h * HEAD_A, HEAD_A)
    hs = pl.ds(base, HEAD_A)
    a_t, w_t, b_t, k_t, r_t = (tr_ref[i, hs, :] for i in range(5))

    def value_row(vi, carry):
        st = wkv_ref[0, vi]
        sa = jnp.sum(st * a_t, axis=0, keepdims=True)
        st = st * w_t + sa * b_t + tr_ref[5, pl.ds(base + vi, 1), :] * k_t
        wkvo_ref[0, vi] = st
        yt_ref[pl.ds(base + vi, 1), :] = jnp.sum(st * r_t, axis=0, keepdims=True)
        return carry

    lax.fori_loop(0, HEAD_A, value_row, 0, unroll=8)

    @pl.when(h == H_A - 1)
    def _():
        oa_ref[...] = _group_norm_gate(yt_ref[...].T, bonus_ref[...], g_ref[...], lnw_ref[...], lnb_ref[...],
                                       hb_ref[...])


def _decode_rwkv(pa, shift, wkv_t, rw_consts):
    n = pa.shape[0]
    full = lambda w: pl.BlockSpec((n, w), lambda h: (0, 0))
    state = pl.BlockSpec((1, HEAD_A, HEAD_A, n), lambda h: (h, 0, 0, 0))
    return pl.pallas_call(
        _decode_rwkv_kernel,
        grid=(H_A,),
        in_specs=[full(A_COLS), full(A_COLS), state] + [_const_spec(c.shape) for c in rw_consts],
        out_specs=[full(C_A), state],
        out_shape=[jax.ShapeDtypeStruct((n, C_A), F32), jax.ShapeDtypeStruct(wkv_t.shape, F32)],
        scratch_shapes=[pltpu.VMEM((6, C_A, n), F32), pltpu.VMEM((C_A, n), F32),
                        pltpu.VMEM((n, C_A), F32), pltpu.VMEM((n, C_A), F32)],
        compiler_params=pltpu.CompilerParams(dimension_semantics=("arbitrary",), vmem_limit_bytes=VMEM_LIMIT),
        name="decode_rwkv",
    )(pa, shift, wkv_t, *rw_consts)


def _decode_gdn_kernel(qkv_ref, z_ref, ba_ref, cst_ref, gdn_ref, conv_ref, alog_ref, dtb_ref, gnorm_ref,
                       ob_ref, gdno_ref):
    cst = cst_ref[...]
    q, k, vv, beta, glog = _gdn_prep(cst[:, :CONV_CH], cst[:, CONV_CH:2 * CONV_CH], cst[:, 2 * CONV_CH:],
                                     qkv_ref[...], ba_ref[...], conv_ref[...], alog_ref[...], dtb_ref[...])
    eg = jnp.exp(glog)
    k_cols = _to_columns(k)
    ls = lambda h: slice(DK * h, DK * (h + 1))

    def split2(x):
        hi = x.astype(BF16)
        return hi, x - hi.astype(F32)

    rows_a, rows_b = [], []
    for h in range(H_B):
        e_h = eg[:, H_B + h:H_B + h + 1]
        (wh, wl), (gh, gl_) = split2(k[:, ls(h)] * (beta[:, h:h + 1] * e_h)), split2(q[:, ls(h)] * e_h)
        rows_a.append(jnp.concatenate([wh.astype(F32), gh.astype(F32), wl, gl_], axis=0).astype(BF16))
        rows_b.append(jnp.concatenate([wh.astype(F32), gh.astype(F32)], axis=0).astype(BF16))
    gunits = [(s_i, h) for s_i in range(DEC_TILE) for h in range(H_B)]
    gn = range(len(gunits))
    be = [beta[s_i:s_i + 1, h:h + 1] for s_i, h in gunits]
    e = [eg[s_i:s_i + 1, H_B + h:H_B + h + 1] for s_i, h in gunits]
    kc = [k_cols[ls(h), s_i:s_i + 1] for s_i, h in gunits]
    gst = [gdn_ref[s_i, h] for s_i, h in gunits]
    parts = [split2(st) for st in gst]
    ra = [_dot(rows_a[h], parts[i][0].astype(BF16)) + jnp.concatenate(
        [_dot(rows_b[h], parts[i][1].astype(BF16)), jnp.zeros((2 * DEC_TILE, DV), F32)], axis=0)
        for i, (s_i, h) in enumerate(gunits)]
    pick = lambda r, j, s_i: r[DEC_TILE * j + s_i:DEC_TILE * j + s_i + 1]
    ws = [pick(ra[i], 0, s_i) + pick(ra[i], 2, s_i) for i, (s_i, h) in enumerate(gunits)]
    qs = [pick(ra[i], 1, s_i) + pick(ra[i], 3, s_i) for i, (s_i, h) in enumerate(gunits)]
    v_new = [be[i] * vv[s_i:s_i + 1, ls(h)] - ws[i] for i, (s_i, h) in enumerate(gunits)]
    qk = [jnp.sum(q[s_i:s_i + 1, ls(h)] * k[s_i:s_i + 1, ls(h)], axis=-1, keepdims=True) for s_i, h in gunits]
    for i, (s_i, h) in enumerate(gunits):
        gdno_ref[s_i, h] = gst[i] * e[i] + kc[i] * v_new[i]
    o_units = [qs[i] + qk[i] * v_new[i] for i in gn]
    o = jnp.concatenate([jnp.concatenate(o_units[H_B * s_i:H_B * (s_i + 1)], axis=1) for s_i in range(DEC_TILE)],
                        axis=0)
    ob_ref[...] = _head_norm_gate(o, z_ref[...], gnorm_ref[...])


def _decode_gdn(qkv, z, ba, cst, gdn, gdn_consts):
    n = qkv.shape[0]
    row = lambda w: pl.BlockSpec((DEC_TILE, w), lambda i: (i, 0))
    state = pl.BlockSpec((DEC_TILE, H_B, DK, DV), lambda i: (i, 0, 0, 0))
    return pl.pallas_call(
        _decode_gdn_kernel,
        grid=(n // DEC_TILE,),
        in_specs=[row(CONV_CH), row(C_BV), row(LANES), row(3 * CONV_CH), state]
                 + [_const_spec(c.shape) for c in gdn_consts],
        out_specs=[row(C_BV), state],
        out_shape=[jax.ShapeDtypeStruct((n, C_BV), F32), jax.ShapeDtypeStruct(gdn.shape, F32)],
        compiler_params=pltpu.CompilerParams(dimension_semantics=("parallel",), vmem_limit_bytes=VMEM_LIMIT),
        name="decode_gdn",
    )(qkv, z, ba, cst, gdn, *gdn_consts)


def _post_kernel(x_ref, oa_ref, ob_ref, pe_ref, nmix_ref, wg_ref, wba_ref, wbb_ref, wout_ref, nffn_ref,
                 wfg_ref, wfu_ref, wfd_ref, nple_ref, wpg_ref, wpp_ref, nfin_ref, y_ref):
    nrow = x_ref.shape[0]
    ngroup = max(1, nrow // POST_GROUP_ROWS)
    groups = [slice(r0, r0 + nrow // ngroup) for r0 in range(0, nrow, nrow // ngroup)]
    x = [x_ref[g, :] for g in groups]
    ma = [_dot(oa_ref[g, :].astype(BF16), wba_ref[...]) for g in groups]
    mb = [_dot(ob_ref[g, :].astype(BF16), wbb_ref[...]) for g in groups]
    u = [_rms(xi, nmix_ref[...]).astype(BF16) for xi in x]
    gates = [_sigmoid(_dot(ui, wg_ref[...])) for ui in u]
    mix = [gi[:, :D_MODEL] * mai + gi[:, D_MODEL:] * mbi for gi, mai, mbi in zip(gates, ma, mb)]
    h = [xi + _dot(mi.astype(BF16), wout_ref[...]) for xi, mi in zip(x, mix)]
    pp = [_dot(pe_ref[g, :].astype(BF16), wpp_ref[...]) for g in groups]
    u2 = [_rms(hi, nffn_ref[...]).astype(BF16) for hi in h]
    fg = [_silu(_dot(ui, wfg_ref[...])) for ui in u2]
    ff = [fgi * _dot(ui, wfu_ref[...]) for fgi, ui in zip(fg, u2)]
    h = [hi + _dot(fi.astype(BF16), wfd_ref[...]) for hi, fi in zip(h, ff)]
    u3 = [_rms(hi, nple_ref[...]).astype(BF16) for hi in h]
    h = [hi + _sigmoid(_dot(ui, wpg_ref[...])) * ppi for hi, ui, ppi in zip(h, u3, pp)]
    for g, hi in zip(groups, h):
        y_ref[g, :] = _rms(hi, nfin_ref[...])


def _post(x, oa, ob, pe, consts):
    n = x.shape[0]
    tm = min(ROW_TILE, n)
    row = lambda w: pl.BlockSpec((tm, w), lambda i: (i, 0))
    return pl.pallas_call(
        _post_kernel,
        grid=(n // tm,),
        in_specs=[row(D_MODEL), row(C_A), row(C_BV), row(PLE_DIM)] + [_const_spec(c.shape) for c in consts],
        out_specs=row(D_MODEL),
        out_shape=jax.ShapeDtypeStruct((n, D_MODEL), F32),
        compiler_params=pltpu.CompilerParams(dimension_semantics=("parallel",), vmem_limit_bytes=VMEM_LIMIT),
        name="post",
    )(x, oa, ob, pe, *consts)


def _chunk_cumsum_matrix(tile):
    i = jnp.arange(tile)
    same = (i[:, None] // CHUNK) == (i[None, :] // CHUNK)
    return (same & (i[None, :] <= i[:, None])).astype(BF16)


def kernel(x_prompt, x_sample, p_prompt, p_sample, state_shift, state_wkv, state_conv, state_gdn, norm_mix, w_in, mu_shift, rw_w0, rw_w2, rw_a0, rw_a2, rw_g2, rw_kk, rw_ka, rw_rk, rw_ln_w, rw_ln_b, gdn_conv, gdn_a_log, gdn_dt_bias, gdn_norm, w_branch_a, w_branch_b, w_out, norm_ffn, w_ffn_gate, w_ffn_up, w_ffn_down, norm_ple, w_ple_gate, w_ple_proj, norm_final):
    bsz, seq, _ = x_prompt.shape
    nd = x_sample.shape[0]
    row = lambda p: p.reshape(1, -1)

    w_in0 = w_in[0]
    b0 = A_COLS
    wa = w_in0[:, :b0].astype(BF16)
    wq = w_in0[:, b0:b0 + CONV_CH].astype(BF16)
    wz = w_in0[:, b0 + CONV_CH:b0 + CONV_CH + C_BV].astype(BF16)
    wb = jnp.pad(w_in0[:, b0 + CONV_CH + C_BV:b0 + CONV_CH + C_BV + 2 * H_B], ((0, 0), (0, LANES - 2 * H_B))).astype(BF16)
    wg = w_in0[:, b0 + CONV_CH + C_BV + 2 * H_B:].astype(BF16)
    w2p = jnp.concatenate([rw_w2[0], jnp.zeros((LORA_A, C_A), F32)], axis=0).astype(BF16)
    a2p = jnp.concatenate([jnp.zeros((LORA_W, C_A), F32), rw_a2[0]], axis=0).astype(BF16)
    ch = jnp.arange(LANES) // HEAD_A
    hb = (ch[:, None] == ch[None, :]).astype(BF16)
    tri = _chunk_cumsum_matrix(T_TILE)
    alog = jnp.pad(gdn_a_log[0], (H_B, LANES - 2 * H_B)).reshape(1, LANES)
    dtb = jnp.pad(gdn_dt_bias[0], (H_B, LANES - 2 * H_B)).reshape(1, LANES)
    rw_consts = (row(mu_shift[0]), row(rw_w0[0]), w2p, row(rw_a0[0]), a2p, rw_g2[0].astype(BF16), row(rw_kk[0]), row(rw_ka[0]),
                 row(rw_rk[0]), row(rw_ln_w[0]), row(rw_ln_b[0]), hb)
    gdn_consts = (gdn_conv[0], alog, dtb, row(gdn_norm[0]))
    post_consts = (row(norm_mix[0]), wg, w_branch_a[0].astype(BF16), w_branch_b[0].astype(BF16),
                   w_out[0].astype(BF16), row(norm_ffn[0]), w_ffn_gate[0].astype(BF16), w_ffn_up[0].astype(BF16),
                   w_ffn_down[0].astype(BF16), row(norm_ple[0]), w_ple_gate[0].astype(BF16),
                   w_ple_proj[0].astype(BF16), row(norm_final))

    xp = x_prompt.reshape(bsz * seq, D_MODEL)
    oa, wkv_pairs, shift_tail, ob, gdn_p, conv_tail = _mixers_prompt(
        x_prompt, row(norm_mix[0]), wa, wq, wz, wb, rw_consts, gdn_consts, tri)
    y_prompt = _post(xp, oa.reshape(bsz * seq, C_A), ob.reshape(bsz * seq, C_BV),
                     p_prompt[0].reshape(bsz * seq, PLE_DIM), post_consts).reshape(bsz, seq, D_MODEL)
    wkv_p = jnp.stack([wkv_pairs[:, :, :HEAD_A, :HEAD_A], wkv_pairs[:, :, HEAD_A:, HEAD_A:]], axis=2)
    wkv_p = wkv_p.reshape(bsz, H_A, HEAD_A, HEAD_A)
    shift_p = shift_tail[:, HALO - 1:, :]
    conv_p = conv_tail[:, HALO - (CONV_W - 1):, :]

    xs = x_sample.reshape(nd, D_MODEL)
    pa_s, qkv_s, z_s, ba_s = _inproj(xs, row(norm_mix[0]), wa, wq, wz, wb)
    cst = state_conv[0].reshape(nd, (CONV_W - 1) * CONV_CH)
    oa_s, wkv_t = _decode_rwkv(pa_s, state_shift[0].reshape(nd, A_COLS), jnp.transpose(state_wkv[0], (1, 2, 3, 0)),
                               rw_consts)
    wkv_s = jnp.transpose(wkv_t, (3, 0, 1, 2))
    ob_s, gdn_s = _decode_gdn(qkv_s, z_s, ba_s, cst, state_gdn[0], gdn_consts)
    y_sample = _post(xs, oa_s, ob_s, p_sample[0].reshape(nd, PLE_DIM), post_consts).reshape(nd, 1, D_MODEL)
    conv_s = jnp.concatenate([cst[:, CONV_CH:], qkv_s], axis=1).reshape(nd, CONV_W - 1, CONV_CH)

    return (y_prompt, y_sample, shift_p[None], wkv_p[None], conv_p[None], gdn_p[None],
            pa_s.reshape(1, nd, 1, A_COLS), wkv_s[None], conv_s[None], gdn_s[None])
```

```python
import functools

import jax
import jax.numpy as jnp
from jax import lax
from jax.experimental import pallas as pl
from jax.experimental.pallas import tpu as pltpu

F32 = jnp.float32
BF16 = jnp.bfloat16

D_MODEL = 1024
HEAD_A = 64
C_A = 512
H_A = 8
LORA_W = 64
LORA_A = 64
LORA_G = 128
A_COLS = 3 * C_A + LORA_W + LORA_A + LORA_G
DK = 128
DV = 128
H_B = 4
C_BK = 512
C_BV = 512
CONV_W = 4
CONV_CH = 2 * C_BK + C_BV
D_FF = 2816
PLE_DIM = 256
NORM_EPS = 1e-6
GN_EPS = 64e-5
L2_EPS = 1e-6

LANES = 128
CHUNK = 64
PAIR = 2 * CHUNK
T_TILE = 256
PROJ_COLS = 256
ROW_TILE = 512
POST_GROUP_ROWS = 256
DEC_TILE = 16
HALO = 8
V7X_VMEM_BYTES = 64 * 1024 * 1024
VMEM_LIMIT = V7X_VMEM_BYTES * 7 // 8
SUM_TERMS = 1
CUMSUM_TERMS = 2
INV_TERMS = 1
CHAIN_TERMS = 1


def _dot(a, b):
    return jnp.dot(a, b, preferred_element_type=F32)


def _sigmoid(x):
    return 1.0 / (1.0 + jnp.exp(-x))


def _silu(x):
    return x * _sigmoid(x)


def _softplus(x):
    return jnp.maximum(x, 0.0) + jnp.log(1.0 + jnp.exp(-jnp.abs(x)))


def _rms(x, gain):
    return x * lax.rsqrt(jnp.mean(x * x, axis=-1, keepdims=True) + NORM_EPS) * gain


def _pair_masks():
    ri = lax.broadcasted_iota(jnp.int32, (PAIR, PAIR), 0)
    ci = lax.broadcasted_iota(jnp.int32, (PAIR, PAIR), 1)
    same = (ri < CHUNK) == (ci < CHUNK)
    strict = same & (ci < ri)
    incl = same & (ci <= ri)
    eye = (ri == ci).astype(F32)
    return strict, incl, eye


_NN = (((1,), (0,)), ((), ()))
_NT = (((1,), (1,)), ((), ()))
_TN = (((0,), (0,)), ((), ()))


def _split(x, terms):
    if isinstance(x, (list, tuple)):
        return list(x)
    if x.dtype == BF16:
        return [x]
    parts = []
    for i in range(terms):
        h = x.astype(BF16)
        parts.append(h)
        if i + 1 < terms:
            x = x - h.astype(F32)
    return parts


def _mm(a, b, dims=_NN, na=1, nb=1):
    pa, pb = _split(a, na), _split(b, nb)
    acc = None
    for i, ai in enumerate(pa):
        for j, bj in enumerate(pb):
            if i + j < max(len(pa), len(pb)):
                d = lax.dot_general(ai, bj, dims, preferred_element_type=F32)
                acc = d if acc is None else acc + d
    return acc


def _neumann_inverse_many(lmats, eye, terms):
    ts = [eye + l for l in lmats]
    ps = [_split(l, terms) for l in lmats]
    n = 2
    while n < CHUNK:
        ps = [_split(_mm(p, p), terms) for p in ps]
        ts = [t + _mm(t, p, _NN, terms) for t, p in zip(ts, ps)]
        n *= 2
    return ts


def _mix_segments(pa_ref, qkv_ref, z_ref, ba_ref):
    return ((0, pa_ref), (A_COLS, qkv_ref), (A_COLS + CONV_CH, z_ref), (A_COLS + CONV_CH + C_BV, ba_ref))


def _inproj_kernel(x_ref, g_ref, w_ref, pa_ref, qkv_ref, z_ref, ba_ref):
    u = _rms(x_ref[...], g_ref[...]).astype(BF16)
    for c0, dst_ref in _mix_segments(pa_ref, qkv_ref, z_ref, ba_ref):
        dst_ref[...] = _dot(u, w_ref[:, c0:c0 + dst_ref.shape[1]])


def _const_spec(shape):
    nd = len(shape)
    return pl.BlockSpec(shape, lambda *_: (0,) * nd, pipeline_mode=pl.Buffered(1))


def _inproj(x, gain, w_mix):
    n = x.shape[0]
    tm = min(ROW_TILE, n)
    row = lambda w: pl.BlockSpec((tm, w), lambda i: (i, 0))
    return pl.pallas_call(
        _inproj_kernel,
        grid=(n // tm,),
        in_specs=[row(D_MODEL), _const_spec(gain.shape), _const_spec(w_mix.shape)],
        out_specs=[row(A_COLS), row(CONV_CH), row(C_BV), row(LANES)],
        out_shape=[jax.ShapeDtypeStruct((n, A_COLS), F32), jax.ShapeDtypeStruct((n, CONV_CH), F32),
                   jax.ShapeDtypeStruct((n, C_BV), F32), jax.ShapeDtypeStruct((n, LANES), F32)],
        compiler_params=pltpu.CompilerParams(dimension_semantics=("parallel",), vmem_limit_bytes=VMEM_LIMIT),
        name="inproj",
    )(x, gain, w_mix)


def _rwkv_prep(pa, prev, mu, w0, w2p, a0, a2p, g2, kkw, ka, rk, hb, between=lambda: None):
    xa = pa + (prev - pa) * mu
    r = xa[:, :C_A]
    k = xa[:, C_A:2 * C_A]
    v = xa[:, 2 * C_A:3 * C_A]
    xwa = xa[:, 3 * C_A:3 * C_A + LORA_W + LORA_A]
    xg = xa[:, 3 * C_A + LORA_W + LORA_A:]
    w_pre = _mm(jnp.tanh(xwa), w2p)
    a_pre = _mm(xwa, a2p)
    g = _mm(_sigmoid(xg), g2)
    kx = k * kkw
    kk_den = _head_sum(kx * kx, hb)
    between()
    w_log = -_softplus(-(w0 + w_pre)) - 0.5
    logw = -jnp.exp(w_log)
    a = _sigmoid(a0 + a_pre)
    kk = kx * lax.rsqrt(kk_den + L2_EPS)
    k2 = k * (1.0 + (a - 1.0) * ka)
    bonus_sum = _head_sum(r * k2 * rk, hb)
    between()
    bonus = bonus_sum * v
    return r, k2, v, logw, a, g, kk, bonus


def _head_sum(x, hb):
    groups = [_mm(x[:, LANES * j:LANES * (j + 1)], hb, _NN, SUM_TERMS) for j in range(x.shape[1] // LANES)]
    return jnp.concatenate(groups, axis=1)


def _group_norm_gate(y, bonus, g, lnw, lnb, hb):
    mean = _head_sum(y, hb) * (1.0 / HEAD_A)
    d = y - mean
    var = _head_sum(d * d, hb) * (1.0 / HEAD_A)
    yn = d * lax.rsqrt(var + GN_EPS) * lnw + lnb
    return (yn + bonus) * g


def _gdn_prep(x0, x1, x2, x3, ba, conv, alog, dtb):
    c = _silu(x0 * conv[0:1] + x1 * conv[1:2] + x2 * conv[2:3] + x3 * conv[3:4])
    qs, ks = [], []
    for h in range(H_B):
        qh = c[:, DK * h:DK * (h + 1)]
        kh = c[:, C_BK + DK * h:C_BK + DK * (h + 1)]
        qs.append(qh * lax.rsqrt(jnp.sum(qh * qh, axis=-1, keepdims=True) + L2_EPS) * (DK ** -0.5))
        ks.append(kh * lax.rsqrt(jnp.sum(kh * kh, axis=-1, keepdims=True) + L2_EPS))
    q = jnp.concatenate(qs, axis=1)
    k = jnp.concatenate(ks, axis=1)
    v = c[:, 2 * C_BK:]
    beta = _sigmoid(ba)
    glog = -jnp.exp(alog) * _softplus(ba + dtb)
    return q, k, v, beta, glog


def _head_norm_gate(o, z, gnorm):
    outs = []
    for h in range(H_B):
        oh = o[:, DV * h:DV * (h + 1)]
        zh = z[:, DV * h:DV * (h + 1)]
        oh = oh * lax.rsqrt(jnp.mean(oh * oh, axis=-1, keepdims=True) + NORM_EPS) * gnorm
        outs.append(oh * _silu(zh))
    return jnp.concatenate(outs, axis=1)


def _tile_maps(bsz, seq, tile):
    tps = seq // tile
    last = bsz * tps - 1
    cur = lambda s: jnp.minimum(s, last)
    prv = lambda s: jnp.maximum(s - 1, 0)
    load = lambda s: (cur(s) // tps, cur(s) % tps, 0)
    emit = lambda s: (prv(s) // tps, prv(s) % tps, 0)
    per_seq3 = lambda s: (prv(s) // tps, 0, 0)
    per_seq4 = lambda s: (prv(s) // tps, 0, 0, 0)
    return tps, last + 2, load, emit, per_seq3, per_seq4


def _gdn_prep_groups(x0, x1, x2, x3, conv):
    def conv_cols(c0, c1):
        return _silu(x0[:, c0:c1] * conv[0:1, c0:c1] + x1[:, c0:c1] * conv[1:2, c0:c1]
                     + x2[:, c0:c1] * conv[2:3, c0:c1] + x3[:, c0:c1] * conv[3:4, c0:c1])

    def unit(xh):
        return xh * lax.rsqrt(jnp.sum(xh * xh, axis=-1, keepdims=True) + L2_EPS)

    for h in range(H_B):
        yield 0, h, unit(conv_cols(DK * h, DK * (h + 1))) * (DK ** -0.5)
        yield 1, h, unit(conv_cols(C_BK + DK * h, C_BK + DK * (h + 1)))
        yield 2, h, conv_cols(2 * C_BK + DV * h, 2 * C_BK + DV * (h + 1))


def _mixers_prompt_kernel(tiles_per_seq, x_ref, nmix_ref, wmix_ref,
                          mu_ref, w0_ref, w2_ref, a0_ref, a2_ref, g2_ref, kkw_ref, ka_ref, rk_ref,
                          lnw_ref, lnb_ref, hb_ref, conv_ref, alog_ref, dtb_ref, gnorm_ref, tri_ref,
                          oa_ref, wkv_ref, shift_ref, ob_ref, gdn_ref, convo_ref,
                          pa_ref, exta_ref, sa_ref, at_ref, bt_ref, kt_ref, rt_ref, va_ref, bh_ref, kh_ref, y_ref,
                          qkvn_ref, zn_ref, ban_ref, extb_ref, sb_ref, q_ref, k_ref, vb_ref, gc_ref, be_ref, o_ref):
    s_id = pl.program_id(0)

    @pl.when(s_id == 0)
    def _():
        pa_ref[...] = jnp.zeros_like(pa_ref)
        qkvn_ref[...] = jnp.zeros_like(qkvn_ref)
        zn_ref[...] = jnp.zeros_like(zn_ref)
        ban_ref[...] = jnp.zeros_like(ban_ref)

    @pl.when(jnp.maximum(s_id - 1, 0) % tiles_per_seq == 0)
    def _():
        exta_ref[0:HALO, :] = jnp.zeros((HALO, A_COLS), F32)
        extb_ref[0:HALO, :] = jnp.zeros((HALO, CONV_CH), F32)
        sa_ref[...] = jnp.zeros_like(sa_ref)
        sb_ref[...] = jnp.zeros_like(sb_ref)

    pa = pa_ref[...]
    exta_ref[HALO:, :] = pa
    prev = exta_ref[pl.ds(HALO - 1, T_TILE), :]
    exta_ref[0:HALO, :] = pa[T_TILE - HALO:, :]
    shift_ref[0] = pa[T_TILE - HALO:, :]
    x3 = qkvn_ref[...]
    z = zn_ref[...]
    ba = ban_ref[...]
    extb_ref[HALO:, :] = x3
    x0 = extb_ref[pl.ds(HALO - 3, T_TILE), :]
    x1 = extb_ref[pl.ds(HALO - 2, T_TILE), :]
    x2 = extb_ref[pl.ds(HALO - 1, T_TILE), :]
    extb_ref[0:HALO, :] = x3[T_TILE - HALO:, :]
    convo_ref[0] = x3[T_TILE - HALO:, :]

    u_next = _rms(x_ref[0], nmix_ref[...]).astype(BF16)
    blocks = [(w0, dst_ref, c0, min(c0 + PROJ_COLS, dst_ref.shape[1]))
              for w0, dst_ref in _mix_segments(pa_ref, qkvn_ref, zn_ref, ban_ref)
              for c0 in range(0, dst_ref.shape[1], PROJ_COLS)]

    def project(count):
        for w0, dst_ref, c0, c1 in blocks[:count]:
            dst_ref[:, c0:c1] = _dot(u_next, wmix_ref[:, w0 + c0:w0 + c1])
        del blocks[:count]

    gdn_groups = _gdn_prep_groups(x0, x1, x2, x3, conv_ref[...])
    gdn_dst = (q_ref, k_ref, vb_ref)

    def gdn_prep(count):
        for _ in range(count):
            item = next(gdn_groups, None)
            if item is not None:
                kind, h, arr = item
                gdn_dst[kind][:, DK * h:DK * (h + 1)] = arr

    def between():
        project(3)
        gdn_prep(3)

    hb = hb_ref[...]
    r, k2, v, logw, a, g, kk, bonus = _rwkv_prep(
        pa, prev, mu_ref[...], w0_ref[...], w2_ref[...], a0_ref[...], a2_ref[...], g2_ref[...],
        kkw_ref[...], ka_ref[...], rk_ref[...], hb, between)

    nchunk = T_TILE // CHUNK
    cs = _mm(tri_ref[...], logw, _NN, 1, CUMSUM_TERMS)
    between()
    ends = [cs[CHUNK * (c + 1) - 1:CHUNK * (c + 1)] for c in range(nchunk)]
    tot = jnp.concatenate([jnp.broadcast_to(e, (CHUNK, C_A)) for e in ends], axis=0)
    dinv = jnp.exp(-cs)
    dend = jnp.exp(tot - cs)
    b_in = kk * a
    at_ref[...] = -kk * jnp.exp(cs - logw)
    bt_ref[...] = b_in * dinv
    kt_ref[...] = k2 * dinv
    between()
    rt_ref[...] = r * jnp.exp(cs)
    va_ref[...] = v
    bh_ref[...] = b_in * dend
    kh_ref[...] = k2 * dend
    gl = [jnp.exp(e) for e in ends]
    project(len(blocks))
    gdn_prep(3 * H_B)
    be_ref[...] = _sigmoid(ba)
    glog = -jnp.exp(alog_ref[...]) * _softplus(ba + dtb_ref[...])
    gc_ref[...] = _mm(tri_ref[...], glog, _NN, 1, CUMSUM_TERMS)

    strict, incl, eye = _pair_masks()
    low = lax.broadcasted_iota(jnp.int32, (CHUNK, LANES), 1) < HEAD_A

    def stack(x):
        return jnp.concatenate([jnp.where(low, x, 0.0), jnp.where(low, 0.0, x)], axis=0)

    npair = C_A // LANES
    lanes = lambda p: slice(LANES * p, LANES * (p + 1))
    rows = lambda c: slice(CHUNK * c, CHUNK * (c + 1))
    half = lambda j: slice(CHUNK * j, CHUNK * (j + 1))
    ia = [(p, c) for p in range(npair) for c in range(nchunk)]
    ib = [(pr, c) for pr in range(H_B // 2) for c in range(nchunk)]
    na, nb = range(len(ia)), range(len(ib))
    ld = lambda ref: [stack(ref[rows(c), lanes(p)]) for p, c in ia]
    at, bt, kt, rt, vs, bh, kh = (ld(ref) for ref in (at_ref, bt_ref, kt_ref, rt_ref, va_ref, bh_ref, kh_ref))
    cat = lambda ref: [jnp.concatenate([ref[rows(c), DK * h:DK * (h + 1)] for h in (2 * pr, 2 * pr + 1)], axis=0)
                       for pr, c in ib]
    col = lambda ref, off: [jnp.concatenate([ref[rows(c), off + h:off + h + 1] for h in (2 * pr, 2 * pr + 1)], axis=0)
                            for pr, c in ib]
    qg_, kg_, vg_ = cat(q_ref), cat(k_ref), cat(vb_ref)
    beta_c = col(be_ref, 0)
    gc = col(gc_ref, H_B)
    gt = [jnp.concatenate([jnp.broadcast_to(g_[CHUNK * (i + 1) - 1:CHUNK * (i + 1)], (CHUNK, 1)) for i in range(2)],
                          axis=0) for g_ in gc]
    decay = []
    for g_ in gc:
        gc_full = jnp.broadcast_to(g_, (PAIR, PAIR))
        decay.append(jnp.exp(gc_full - gc_full.T))
    kb = [kg_[i] * beta_c[i] for i in nb]
    vbb = [vg_[i] * beta_c[i] for i in nb]
    eg = [jnp.exp(g_) for g_ in gc]

    aa = [_mm(jnp.concatenate([at[i], rt[i]], axis=0), jnp.concatenate([bt[i], kt[i]], axis=0), _NT) for i in na]
    kq = [_mm(jnp.concatenate([kb[i], qg_[i]], axis=0), kg_[i], _NT) for i in nb]
    a_ab = [jnp.where(strict, aa[i][:PAIR, :PAIR], 0.0) for i in na]
    a_ak = [jnp.where(strict, aa[i][:PAIR, PAIR:], 0.0) for i in na]
    a_rb = [jnp.where(incl, aa[i][PAIR:, :PAIR], 0.0) for i in na]
    a_rk = [jnp.where(incl, aa[i][PAIR:, PAIR:], 0.0) for i in na]
    lneg = [jnp.where(strict, -(kq[i][:PAIR] * decay[i]), 0.0) for i in nb]
    qk = [jnp.where(incl, kq[i][PAIR:] * decay[i], 0.0) for i in nb]
    tall = _neumann_inverse_many(a_ab + lneg, eye, INV_TERMS)
    tinv, tinvb = tall[:len(ia)], tall[len(ia):]
    akv = [_mm(a_ak[i], vs[i]) for i in na]
    uw = [_mm(tinvb[i], jnp.concatenate([vbb[i], kb[i] * eg[i]], axis=1)) for i in nb]
    gu = [_mm(tinv[i], jnp.concatenate([at[i], akv[i]], axis=1)) for i in na]
    ow = [_mm(qk[i], uw[i]) for i in nb]
    ry = [_mm(a_rb[i], gu[i]) for i in na]
    rkv = [_mm(a_rk[i], vs[i]) for i in na]
    kd = [kg_[i] * jnp.exp(gt[i] - gc[i]) for i in nb]
    pqb = [[_mm(kd[i][half(j)], uw[i][half(j)], _TN) for j in range(2)] for i in nb]
    pq = [_mm(gu[i], bh[i], _TN) for i in na]
    vk = [_mm(vs[i], kh[i], _TN) for i in na]
    rp = [rt[i] + ry[i][:, :PAIR] for i in na]
    yc = [ry[i][:, PAIR:] + rkv[i] for i in na]
    pm = [eye * gl[ia[i][1]][:, lanes(ia[i][0])] + pq[i][:PAIR] for i in na]
    qm = [pq[i][PAIR:] + vk[i] for i in na]
    oc = [ow[i][:, :DV] for i in nb]
    rq = [qg_[i] * eg[i] - ow[i][:, DV:] for i in nb]
    pmb = [[eye * jnp.exp(gt[i][CHUNK * j:CHUNK * j + 1]) - pqb[i][j][:, DV:] for j in range(2)] for i in nb]

    sa = [sa_ref[p] for p in range(npair)]
    sb = [sb_ref[h] for h in range(H_B)]
    for c in range(nchunk):
        for p in range(npair):
            i = ia.index((p, c))
            y = _mm(rp[i], sa[p], _NT) + yc[i]
            y_ref[rows(c), lanes(p)] = y[:CHUNK] + y[CHUNK:]
            sa[p] = _mm(sa[p], pm[i], _NN, CHAIN_TERMS, CHAIN_TERMS) + qm[i]
        for h in range(H_B):
            i, j = ib.index((h // 2, c)), h % 2
            o_ref[rows(c), DV * h:DV * (h + 1)] = _mm(rq[i][half(j)], sb[h]) + oc[i][half(j)]
            sb[h] = _mm(pmb[i][j], sb[h], _NN, CHAIN_TERMS, CHAIN_TERMS) + pqb[i][j][:, :DV]
    for p in range(npair):
        sa_ref[p] = sa[p]
    for h in range(H_B):
        sb_ref[h] = sb[h]

    oa_ref[0] = _group_norm_gate(y_ref[...], bonus, g, lnw_ref[...], lnb_ref[...], hb)
    wkv_ref[0] = sa_ref[...]
    ob_ref[0] = _head_norm_gate(o_ref[...], z, gnorm_ref[...])
    gdn_ref[0] = sb_ref[...]


def _mixers_prompt(x, nmix, w_mix, rw_consts, gdn_consts, tri):
    bsz, seq, _ = x.shape
    consts = (nmix, w_mix) + tuple(rw_consts) + tuple(gdn_consts) + (tri,)
    npair = C_A // LANES
    tps, steps, load, emit, per_seq3, per_seq4 = _tile_maps(bsz, seq, T_TILE)
    tile = lambda w: pltpu.VMEM((T_TILE, w), F32)
    return pl.pallas_call(
        functools.partial(_mixers_prompt_kernel, tps),
        grid=(steps,),
        in_specs=[pl.BlockSpec((1, T_TILE, D_MODEL), load)] + [_const_spec(c.shape) for c in consts],
        out_specs=[pl.BlockSpec((1, T_TILE, C_A), emit),
                   pl.BlockSpec((1, npair, LANES, LANES), per_seq4),
                   pl.BlockSpec((1, HALO, A_COLS), per_seq3),
                   pl.BlockSpec((1, T_TILE, C_BV), emit),
                   pl.BlockSpec((1, H_B, DK, DV), per_seq4),
                   pl.BlockSpec((1, HALO, CONV_CH), per_seq3)],
        out_shape=[jax.ShapeDtypeStruct((bsz, seq, C_A), F32),
                   jax.ShapeDtypeStruct((bsz, npair, LANES, LANES), F32),
                   jax.ShapeDtypeStruct((bsz, HALO, A_COLS), F32),
                   jax.ShapeDtypeStruct((bsz, seq, C_BV), F32),
                   jax.ShapeDtypeStruct((bsz, H_B, DK, DV), F32),
                   jax.ShapeDtypeStruct((bsz, HALO, CONV_CH), F32)],
        scratch_shapes=[tile(A_COLS), pltpu.VMEM((T_TILE + HALO, A_COLS), F32), pltpu.VMEM((npair, LANES, LANES), F32)]
                       + [tile(C_A) for _ in range(8)]
                       + [tile(CONV_CH), tile(C_BV), tile(LANES), pltpu.VMEM((T_TILE + HALO, CONV_CH), F32),
                          pltpu.VMEM((H_B, DK, DV), F32), tile(C_BK), tile(C_BK), tile(C_BV), tile(LANES), tile(LANES),
                          tile(C_BV)],
        compiler_params=pltpu.CompilerParams(dimension_semantics=("arbitrary",), vmem_limit_bytes=VMEM_LIMIT),
        name="mixers_prompt",
    )(x, *consts)


def _to_columns(x):
    pad = jnp.zeros((LANES - DEC_TILE, x.shape[1]), F32)
    return jnp.concatenate([x, pad], axis=0).T


def _decode_rwkv_kernel(pa_ref, shift_ref, wkv_ref,
                        mu_ref, w0_ref, w2_ref, a0_ref, a2_ref, g2_ref, kkw_ref, ka_ref, rk_ref, lnw_ref, lnb_ref,
                        hb_ref, oa_ref, wkvo_ref, tr_ref, yt_ref, g_ref, bonus_ref):
    h = pl.program_id(0)

    @pl.when(h == 0)
    def _():
        r, k2, v, logw, a, g, kk, bonus = _rwkv_prep(
            pa_ref[...], shift_ref[...], mu_ref[...], w0_ref[...], w2_ref[...], a0_ref[...], a2_ref[...],
            g2_ref[...], kkw_ref[...], ka_ref[...], rk_ref[...], hb_ref[...])
        for i, x in enumerate((-kk, jnp.exp(logw), kk * a, k2, r, v)):
            tr_ref[i] = x.T
        g_ref[...] = g
        bonus_ref[...] = bonus

    base = pl.multiple_of(h * HEAD_A, HEAD_A)
    hs = pl.ds(base, HEAD_A)
    a_t, w_t, b_t, k_t, r_t = (tr_ref[i, hs, :] for i in range(5))

    def value_row(vi, carry):
        st = wkv_ref[0, vi]
        sa = jnp.sum(st * a_t, axis=0, keepdims=True)
        st = st * w_t + sa * b_t + tr_ref[5, pl.ds(base + vi, 1), :] * k_t
        wkvo_ref[0, vi] = st
        yt_ref[pl.ds(base + vi, 1), :] = jnp.sum(st * r_t, axis=0, keepdims=True)
        return carry

    lax.fori_loop(0, HEAD_A, value_row, 0, unroll=8)

    @pl.when(h == H_A - 1)
    def _():
        oa_ref[...] = _group_norm_gate(yt_ref[...].T, bonus_ref[...], g_ref[...], lnw_ref[...], lnb_ref[...],
                                       hb_ref[...])


def _decode_rwkv(pa, shift, wkv_t, rw_consts):
    n = pa.shape[0]
    full = lambda w: pl.BlockSpec((n, w), lambda h: (0, 0))
    state = pl.BlockSpec((1, HEAD_A, HEAD_A, n), lambda h: (h, 0, 0, 0))
    return pl.pallas_call(
        _decode_rwkv_kernel,
        grid=(H_A,),
        in_specs=[full(A_COLS), full(A_COLS), state] + [_const_spec(c.shape) for c in rw_consts],
        out_specs=[full(C_A), state],
        out_shape=[jax.ShapeDtypeStruct((n, C_A), F32), jax.ShapeDtypeStruct(wkv_t.shape, F32)],
        scratch_shapes=[pltpu.VMEM((6, C_A, n), F32), pltpu.VMEM((C_A, n), F32),
                        pltpu.VMEM((n, C_A), F32), pltpu.VMEM((n, C_A), F32)],
        compiler_params=pltpu.CompilerParams(dimension_semantics=("arbitrary",), vmem_limit_bytes=VMEM_LIMIT),
        name="decode_rwkv",
    )(pa, shift, wkv_t, *rw_consts)


def _decode_gdn_kernel(qkv_ref, z_ref, ba_ref, cst_ref, gdn_ref, conv_ref, alog_ref, dtb_ref, gnorm_ref,
                       ob_ref, gdno_ref):
    cst = cst_ref[...]
    q, k, vv, beta, glog = _gdn_prep(cst[:, :CONV_CH], cst[:, CONV_CH:2 * CONV_CH], cst[:, 2 * CONV_CH:],
                                     qkv_ref[...], ba_ref[...], conv_ref[...], alog_ref[...], dtb_ref[...])
    eg = jnp.exp(glog)
    k_cols = _to_columns(k)
    ls = lambda h: slice(DK * h, DK * (h + 1))

    def split2(x):
        hi = x.astype(BF16)
        return hi, x - hi.astype(F32)

    rows_a, rows_b = [], []
    for h in range(H_B):
        e_h = eg[:, H_B + h:H_B + h + 1]
        (wh, wl), (gh, gl_) = split2(k[:, ls(h)] * (beta[:, h:h + 1] * e_h)), split2(q[:, ls(h)] * e_h)
        rows_a.append(jnp.concatenate([wh.astype(F32), gh.astype(F32), wl, gl_], axis=0).astype(BF16))
        rows_b.append(jnp.concatenate([wh.astype(F32), gh.astype(F32)], axis=0).astype(BF16))
    gunits = [(s_i, h) for s_i in range(DEC_TILE) for h in range(H_B)]
    gn = range(len(gunits))
    be = [beta[s_i:s_i + 1, h:h + 1] for s_i, h in gunits]
    e = [eg[s_i:s_i + 1, H_B + h:H_B + h + 1] for s_i, h in gunits]
    kc = [k_cols[ls(h), s_i:s_i + 1] for s_i, h in gunits]
    gst = [gdn_ref[s_i, h] for s_i, h in gunits]
    parts = [split2(st) for st in gst]
    ra = [_dot(rows_a[h], parts[i][0].astype(BF16)) + jnp.concatenate(
        [_dot(rows_b[h], parts[i][1].astype(BF16)), jnp.zeros((2 * DEC_TILE, DV), F32)], axis=0)
        for i, (s_i, h) in enumerate(gunits)]
    pick = lambda r, j, s_i: r[DEC_TILE * j + s_i:DEC_TILE * j + s_i + 1]
    ws = [pick(ra[i], 0, s_i) + pick(ra[i], 2, s_i) for i, (s_i, h) in enumerate(gunits)]
    qs = [pick(ra[i], 1, s_i) + pick(ra[i], 3, s_i) for i, (s_i, h) in enumerate(gunits)]
    v_new = [be[i] * vv[s_i:s_i + 1, ls(h)] - ws[i] for i, (s_i, h) in enumerate(gunits)]
    qk = [jnp.sum(q[s_i:s_i + 1, ls(h)] * k[s_i:s_i + 1, ls(h)], axis=-1, keepdims=True) for s_i, h in gunits]
    for i, (s_i, h) in enumerate(gunits):
        gdno_ref[s_i, h] = gst[i] * e[i] + kc[i] * v_new[i]
    o_units = [qs[i] + qk[i] * v_new[i] for i in gn]
    o = jnp.concatenate([jnp.concatenate(o_units[H_B * s_i:H_B * (s_i + 1)], axis=1) for s_i in range(DEC_TILE)],
                        axis=0)
    ob_ref[...] = _head_norm_gate(o, z_ref[...], gnorm_ref[...])


def _decode_gdn(qkv, z, ba, cst, gdn, gdn_consts):
    n = qkv.shape[0]
    row = lambda w: pl.BlockSpec((DEC_TILE, w), lambda i: (i, 0))
    state = pl.BlockSpec((DEC_TILE, H_B, DK, DV), lambda i: (i, 0, 0, 0))
    return pl.pallas_call(
        _decode_gdn_kernel,
        grid=(n // DEC_TILE,),
        in_specs=[row(CONV_CH), row(C_BV), row(LANES), row(3 * CONV_CH), state]
                 + [_const_spec(c.shape) for c in gdn_consts],
        out_specs=[row(C_BV), state],
        out_shape=[jax.ShapeDtypeStruct((n, C_BV), F32), jax.ShapeDtypeStruct(gdn.shape, F32)],
        compiler_params=pltpu.CompilerParams(dimension_semantics=("parallel",), vmem_limit_bytes=VMEM_LIMIT),
        name="decode_gdn",
    )(qkv, z, ba, cst, gdn, *gdn_consts)


def _post_kernel(x_ref, oa_ref, ob_ref, pe_ref, nmix_ref, wg_ref, wba_ref, wbb_ref, wout_ref, nffn_ref,
                 wfg_ref, wfu_ref, wfd_ref, nple_ref, wpg_ref, wpp_ref, nfin_ref, y_ref):
    nrow = x_ref.shape[0]
    ngroup = max(1, nrow // POST_GROUP_ROWS)
    groups = [slice(r0, r0 + nrow // ngroup) for r0 in range(0, nrow, nrow // ngroup)]
    x = [x_ref[g, :] for g in groups]
    ma = [_dot(oa_ref[g, :].astype(BF16), wba_ref[...]) for g in groups]
    mb = [_dot(ob_ref[g, :].astype(BF16), wbb_ref[...]) for g in groups]
    u = [_rms(xi, nmix_ref[...]).astype(BF16) for xi in x]
    gates = [_sigmoid(_dot(ui, wg_ref[...])) for ui in u]
    mix = [gi[:, :D_MODEL] * mai + gi[:, D_MODEL:] * mbi for gi, mai, mbi in zip(gates, ma, mb)]
    h = [xi + _dot(mi.astype(BF16), wout_ref[...]) for xi, mi in zip(x, mix)]
    pp = [_dot(pe_ref[g, :].astype(BF16), wpp_ref[...]) for g in groups]
    u2 = [_rms(hi, nffn_ref[...]).astype(BF16) for hi in h]
    fg = [_silu(_dot(ui, wfg_ref[...])) for ui in u2]
    ff = [fgi * _dot(ui, wfu_ref[...]) for fgi, ui in zip(fg, u2)]
    h = [hi + _dot(fi.astype(BF16), wfd_ref[...]) for hi, fi in zip(h, ff)]
    u3 = [_rms(hi, nple_ref[...]).astype(BF16) for hi in h]
    h = [hi + _sigmoid(_dot(ui, wpg_ref[...])) * ppi for hi, ui, ppi in zip(h, u3, pp)]
    for g, hi in zip(groups, h):
        y_ref[g, :] = _rms(hi, nfin_ref[...])


def _post(x, oa, ob, pe, consts):
    n = x.shape[0]
    tm = min(ROW_TILE, n)
    row = lambda w: pl.BlockSpec((tm, w), lambda i: (i, 0))
    return pl.pallas_call(
        _post_kernel,
        grid=(n // tm,),
        in_specs=[row(D_MODEL), row(C_A), row(C_BV), row(PLE_DIM)] + [_const_spec(c.shape) for c in consts],
        out_specs=row(D_MODEL),
        out_shape=jax.ShapeDtypeStruct((n, D_MODEL), F32),
        compiler_params=pltpu.CompilerParams(dimension_semantics=("parallel",), vmem_limit_bytes=VMEM_LIMIT),
        name="post",
    )(x, oa, ob, pe, *consts)


def _chunk_cumsum_matrix(tile):
    i = jnp.arange(tile)
    same = (i[:, None] // CHUNK) == (i[None, :] // CHUNK)
    return (same & (i[None, :] <= i[:, None])).astype(BF16)


def kernel(x_prompt, x_sample, p_prompt, p_sample, state_shift, state_wkv, state_conv, state_gdn, norm_mix, w_in, mu_shift, rw_w0, rw_w2, rw_a0, rw_a2, rw_g2, rw_kk, rw_ka, rw_rk, rw_ln_w, rw_ln_b, gdn_conv, gdn_a_log, gdn_dt_bias, gdn_norm, w_branch_a, w_branch_b, w_out, norm_ffn, w_ffn_gate, w_ffn_up, w_ffn_down, norm_ple, w_ple_gate, w_ple_proj, norm_final):
    bsz, seq, _ = x_prompt.shape
    nd = x_sample.shape[0]
    row = lambda p: p.reshape(1, -1)

    w_in0 = w_in[0]
    b0 = A_COLS
    w_mix = w_in0[:, :b0 + CONV_CH + C_BV + LANES].astype(BF16)
    wg = w_in0[:, b0 + CONV_CH + C_BV + 2 * H_B:].astype(BF16)
    w2p = jnp.concatenate([rw_w2[0], jnp.zeros((LORA_A, C_A), F32)], axis=0).astype(BF16)
    a2p = jnp.concatenate([jnp.zeros((LORA_W, C_A), F32), rw_a2[0]], axis=0).astype(BF16)
    ch = jnp.arange(LANES) // HEAD_A
    hb = (ch[:, None] == ch[None, :]).astype(BF16)
    tri = _chunk_cumsum_matrix(T_TILE)
    alog = jnp.pad(gdn_a_log[0], (H_B, LANES - 2 * H_B)).reshape(1, LANES)
    dtb = jnp.pad(gdn_dt_bias[0], (H_B, LANES - 2 * H_B)).reshape(1, LANES)
    rw_consts = (row(mu_shift[0]), row(rw_w0[0]), w2p, row(rw_a0[0]), a2p, rw_g2[0].astype(BF16), row(rw_kk[0]), row(rw_ka[0]),
                 row(rw_rk[0]), row(rw_ln_w[0]), row(rw_ln_b[0]), hb)
    gdn_consts = (gdn_conv[0], alog, dtb, row(gdn_norm[0]))
    post_consts = (row(norm_mix[0]), wg, w_branch_a[0].astype(BF16), w_branch_b[0].astype(BF16),
                   w_out[0].astype(BF16), row(norm_ffn[0]), w_ffn_gate[0].astype(BF16), w_ffn_up[0].astype(BF16),
                   w_ffn_down[0].astype(BF16), row(norm_ple[0]), w_ple_gate[0].astype(BF16),
                   w_ple_proj[0].astype(BF16), row(norm_final))

    xp = x_prompt.reshape(bsz * seq, D_MODEL)
    oa, wkv_pairs, shift_tail, ob, gdn_p, conv_tail = _mixers_prompt(
        x_prompt, row(norm_mix[0]), w_mix, rw_consts, gdn_consts, tri)
    y_prompt = _post(xp, oa.reshape(bsz * seq, C_A), ob.reshape(bsz * seq, C_BV),
                     p_prompt[0].reshape(bsz * seq, PLE_DIM), post_consts).reshape(bsz, seq, D_MODEL)
    wkv_p = jnp.stack([wkv_pairs[:, :, :HEAD_A, :HEAD_A], wkv_pairs[:, :, HEAD_A:, HEAD_A:]], axis=2)
    wkv_p = wkv_p.reshape(bsz, H_A, HEAD_A, HEAD_A)
    shift_p = shift_tail[:, HALO - 1:, :]
    conv_p = conv_tail[:, HALO - (CONV_W - 1):, :]

    xs = x_sample.reshape(nd, D_MODEL)
    pa_s, qkv_s, z_s, ba_s = _inproj(xs, row(norm_mix[0]), w_mix)
    cst = state_conv[0].reshape(nd, (CONV_W - 1) * CONV_CH)
    oa_s, wkv_t = _decode_rwkv(pa_s, state_shift[0].reshape(nd, A_COLS), jnp.transpose(state_wkv[0], (1, 2, 3, 0)),
                               rw_consts)
    wkv_s = jnp.transpose(wkv_t, (3, 0, 1, 2))
    ob_s, gdn_s = _decode_gdn(qkv_s, z_s, ba_s, cst, state_gdn[0], gdn_consts)
    y_sample = _post(xs, oa_s, ob_s, p_sample[0].reshape(nd, PLE_DIM), post_consts).reshape(nd, 1, D_MODEL)
    conv_s = jnp.concatenate([cst[:, CONV_CH:], qkv_s], axis=1).reshape(nd, CONV_W - 1, CONV_CH)

    return (y_prompt, y_sample, shift_p[None], wkv_p[None], conv_p[None], gdn_p[None],
            pa_s.reshape(1, nd, 1, A_COLS), wkv_s[None], conv_s[None], gdn_s[None])
```

```python
import functools

import jax
import jax.numpy as jnp
from jax import lax
from jax.experimental import pallas as pl
from jax.experimental.pallas import tpu as pltpu

F32 = jnp.float32
BF16 = jnp.bfloat16

D_MODEL = 1024
HEAD_A = 64
C_A = 512
H_A = 8
LORA_W = 64
LORA_A = 64
LORA_G = 128
A_COLS = 3 * C_A + LORA_W + LORA_A + LORA_G
DK = 128
DV = 128
H_B = 4
C_BK = 512
C_BV = 512
CONV_W = 4
CONV_CH = 2 * C_BK + C_BV
D_FF = 2816
PLE_DIM = 256
NORM_EPS = 1e-6
GN_EPS = 64e-5
L2_EPS = 1e-6

LANES = 128
CHUNK = 64
PAIR = 2 * CHUNK
T_TILE = 256
PROJ_COLS = 256
ROW_TILE = 512
POST_GROUP_ROWS = 256
DEC_TILE = 16
HALO = 8
V7X_VMEM_BYTES = 64 * 1024 * 1024
VMEM_LIMIT = V7X_VMEM_BYTES * 7 // 8
SUM_TERMS = 1
CUMSUM_TERMS = 2
INV_TERMS = 1
CHAIN_TERMS = 1


def _dot(a, b):
    return jnp.dot(a, b, preferred_element_type=F32)


def _sigmoid(x):
    return 1.0 / (1.0 + jnp.exp(-x))


def _silu(x):
    return x * _sigmoid(x)


def _softplus(x):
    return jnp.maximum(x, 0.0) + jnp.log(1.0 + jnp.exp(-jnp.abs(x)))


def _rms(x, gain):
    return x * lax.rsqrt(jnp.mean(x * x, axis=-1, keepdims=True) + NORM_EPS) * gain


def _pair_masks():
    ri = lax.broadcasted_iota(jnp.int32, (PAIR, PAIR), 0)
    ci = lax.broadcasted_iota(jnp.int32, (PAIR, PAIR), 1)
    same = (ri < CHUNK) == (ci < CHUNK)
    strict = same & (ci < ri)
    incl = same & (ci <= ri)
    eye = (ri == ci).astype(F32)
    return strict, incl, eye


_NN = (((1,), (0,)), ((), ()))
_NT = (((1,), (1,)), ((), ()))
_TN = (((0,), (0,)), ((), ()))


def _split(x, terms):
    if isinstance(x, (list, tuple)):
        return list(x)
    if x.dtype == BF16:
        return [x]
    parts = []
    for i in range(terms):
        h = x.astype(BF16)
        parts.append(h)
        if i + 1 < terms:
            x = x - h.astype(F32)
    return parts


def _mm(a, b, dims=_NN, na=1, nb=1):
    pa, pb = _split(a, na), _split(b, nb)
    acc = None
    for i, ai in enumerate(pa):
        for j, bj in enumerate(pb):
            if i + j < max(len(pa), len(pb)):
                d = lax.dot_general(ai, bj, dims, preferred_element_type=F32)
                acc = d if acc is None else acc + d
    return acc


def _neumann_inverse_many(lmats, eye, terms):
    ts = [eye + l for l in lmats]
    ps = [_split(l, terms) for l in lmats]
    n = 2
    while n < CHUNK:
        ps = [_split(_mm(p, p), terms) for p in ps]
        ts = [t + _mm(t, p, _NN, terms) for t, p in zip(ts, ps)]
        n *= 2
    return ts


def _mix_segments(pa_ref, qkv_ref, z_ref, ba_ref):
    return ((0, pa_ref), (A_COLS, qkv_ref), (A_COLS + CONV_CH, z_ref), (A_COLS + CONV_CH + C_BV, ba_ref))


def _inproj_kernel(x_ref, g_ref, w_ref, pa_ref, qkv_ref, z_ref, ba_ref):
    u = _rms(x_ref[...], g_ref[...]).astype(BF16)
    for c0, dst_ref in _mix_segments(pa_ref, qkv_ref, z_ref, ba_ref):
        dst_ref[...] = _dot(u, w_ref[:, c0:c0 + dst_ref.shape[1]])


def _const_spec(shape):
    nd = len(shape)
    return pl.BlockSpec(shape, lambda *_: (0,) * nd, pipeline_mode=pl.Buffered(1))


def _inproj(x, gain, w_mix):
    n = x.shape[0]
    tm = min(ROW_TILE, n)
    row = lambda w: pl.BlockSpec((tm, w), lambda i: (i, 0))
    return pl.pallas_call(
        _inproj_kernel,
        grid=(n // tm,),
        in_specs=[row(D_MODEL), _const_spec(gain.shape), _const_spec(w_mix.shape)],
        out_specs=[row(A_COLS), row(CONV_CH), row(C_BV), row(LANES)],
        out_shape=[jax.ShapeDtypeStruct((n, A_COLS), F32), jax.ShapeDtypeStruct((n, CONV_CH), F32),
                   jax.ShapeDtypeStruct((n, C_BV), F32), jax.ShapeDtypeStruct((n, LANES), F32)],
        compiler_params=pltpu.CompilerParams(dimension_semantics=("parallel",), vmem_limit_bytes=VMEM_LIMIT),
        name="inproj",
    )(x, gain, w_mix)


def _rwkv_prep(pa, prev, mu, w0, w2p, a0, a2p, g2, kkw, ka, rk, hb, between=lambda: None):
    xa = pa + (prev - pa) * mu
    r = xa[:, :C_A]
    k = xa[:, C_A:2 * C_A]
    v = xa[:, 2 * C_A:3 * C_A]
    xwa = xa[:, 3 * C_A:3 * C_A + LORA_W + LORA_A]
    xg = xa[:, 3 * C_A + LORA_W + LORA_A:]
    w_pre = _mm(jnp.tanh(xwa), w2p)
    a_pre = _mm(xwa, a2p)
    g = _mm(_sigmoid(xg), g2)
    kx = k * kkw
    kk_den = _head_sum(kx * kx, hb)
    between()
    w_log = -_softplus(-(w0 + w_pre)) - 0.5
    logw = -jnp.exp(w_log)
    a = _sigmoid(a0 + a_pre)
    kk = kx * lax.rsqrt(kk_den + L2_EPS)
    k2 = k * (1.0 + (a - 1.0) * ka)
    bonus_sum = _head_sum(r * k2 * rk, hb)
    between()
    bonus = bonus_sum * v
    return r, k2, v, logw, a, g, kk, bonus


def _head_sum(x, hb):
    groups = [_mm(x[:, LANES * j:LANES * (j + 1)], hb, _NN, SUM_TERMS) for j in range(x.shape[1] // LANES)]
    return jnp.concatenate(groups, axis=1)


def _group_norm_gate(y, bonus, g, lnw, lnb, hb):
    mean = _head_sum(y, hb) * (1.0 / HEAD_A)
    d = y - mean
    var = _head_sum(d * d, hb) * (1.0 / HEAD_A)
    yn = d * lax.rsqrt(var + GN_EPS) * lnw + lnb
    return (yn + bonus) * g


def _gdn_prep(x0, x1, x2, x3, ba, conv, alog, dtb):
    c = _silu(x0 * conv[0:1] + x1 * conv[1:2] + x2 * conv[2:3] + x3 * conv[3:4])
    qs, ks = [], []
    for h in range(H_B):
        qh = c[:, DK * h:DK * (h + 1)]
        kh = c[:, C_BK + DK * h:C_BK + DK * (h + 1)]
        qs.append(qh * lax.rsqrt(jnp.sum(qh * qh, axis=-1, keepdims=True) + L2_EPS) * (DK ** -0.5))
        ks.append(kh * lax.rsqrt(jnp.sum(kh * kh, axis=-1, keepdims=True) + L2_EPS))
    q = jnp.concatenate(qs, axis=1)
    k = jnp.concatenate(ks, axis=1)
    v = c[:, 2 * C_BK:]
    beta = _sigmoid(ba)
    glog = -jnp.exp(alog) * _softplus(ba + dtb)
    return q, k, v, beta, glog


def _head_norm_gate(o, z, gnorm):
    outs = []
    for h in range(H_B):
        oh = o[:, DV * h:DV * (h + 1)]
        zh = z[:, DV * h:DV * (h + 1)]
        oh = oh * lax.rsqrt(jnp.mean(oh * oh, axis=-1, keepdims=True) + NORM_EPS) * gnorm
        outs.append(oh * _silu(zh))
    return jnp.concatenate(outs, axis=1)


def _tile_maps(bsz, seq, tile):
    tps = seq // tile
    last = bsz * tps - 1
    cur = lambda s: jnp.minimum(s, last)
    prv = lambda s: jnp.maximum(s - 1, 0)
    load = lambda s: (cur(s) // tps, cur(s) % tps, 0)
    emit = lambda s: (prv(s) // tps, prv(s) % tps, 0)
    per_seq3 = lambda s: (prv(s) // tps, 0, 0)
    per_seq4 = lambda s: (prv(s) // tps, 0, 0, 0)
    return tps, last + 2, load, emit, per_seq3, per_seq4


def _gdn_prep_groups(x0, x1, x2, x3, conv):
    def conv_cols(c0, c1):
        return _silu(x0[:, c0:c1] * conv[0:1, c0:c1] + x1[:, c0:c1] * conv[1:2, c0:c1]
                     + x2[:, c0:c1] * conv[2:3, c0:c1] + x3[:, c0:c1] * conv[3:4, c0:c1])

    def unit(xh):
        return xh * lax.rsqrt(jnp.sum(xh * xh, axis=-1, keepdims=True) + L2_EPS)

    for h in range(H_B):
        yield 0, h, unit(conv_cols(DK * h, DK * (h + 1))) * (DK ** -0.5)
        yield 1, h, unit(conv_cols(C_BK + DK * h, C_BK + DK * (h + 1)))
        yield 2, h, conv_cols(2 * C_BK + DV * h, 2 * C_BK + DV * (h + 1))


def _mixers_prompt_kernel(tiles_per_seq, x_ref, nmix_ref, wmix_ref,
                          mu_ref, w0_ref, w2_ref, a0_ref, a2_ref, g2_ref, kkw_ref, ka_ref, rk_ref,
                          lnw_ref, lnb_ref, hb_ref, conv_ref, alog_ref, dtb_ref, gnorm_ref, tri_ref,
                          oa_ref, wkv_ref, shift_ref, ob_ref, gdn_ref, convo_ref,
                          pa_ref, exta_ref, sa_ref, at_ref, bt_ref, kt_ref, rt_ref, va_ref, bh_ref, kh_ref, y_ref,
                          qkvn_ref, zn_ref, ban_ref, extb_ref, sb_ref, q_ref, k_ref, vb_ref, gc_ref, be_ref, o_ref):
    s_id = pl.program_id(0)

    @pl.when(s_id == 0)
    def _():
        pa_ref[...] = jnp.zeros_like(pa_ref)
        qkvn_ref[...] = jnp.zeros_like(qkvn_ref)
        zn_ref[...] = jnp.zeros_like(zn_ref)
        ban_ref[...] = jnp.zeros_like(ban_ref)

    @pl.when(jnp.maximum(s_id - 1, 0) % tiles_per_seq == 0)
    def _():
        exta_ref[0:HALO, :] = jnp.zeros((HALO, A_COLS), F32)
        extb_ref[0:HALO, :] = jnp.zeros((HALO, CONV_CH), F32)
        sa_ref[...] = jnp.zeros_like(sa_ref)
        sb_ref[...] = jnp.zeros_like(sb_ref)

    pa = pa_ref[...]
    exta_ref[HALO:, :] = pa
    prev = exta_ref[pl.ds(HALO - 1, T_TILE), :]
    exta_ref[0:HALO, :] = pa[T_TILE - HALO:, :]
    shift_ref[0] = pa[T_TILE - HALO:, :]
    x3 = qkvn_ref[...]
    z = zn_ref[...]
    ba = ban_ref[...]
    extb_ref[HALO:, :] = x3
    x0 = extb_ref[pl.ds(HALO - 3, T_TILE), :]
    x1 = extb_ref[pl.ds(HALO - 2, T_TILE), :]
    x2 = extb_ref[pl.ds(HALO - 1, T_TILE), :]
    extb_ref[0:HALO, :] = x3[T_TILE - HALO:, :]
    convo_ref[0] = x3[T_TILE - HALO:, :]

    u_next = _rms(x_ref[0], nmix_ref[...]).astype(BF16)
    blocks = [(w0, dst_ref, c0, min(c0 + PROJ_COLS, dst_ref.shape[1]))
              for w0, dst_ref in _mix_segments(pa_ref, qkvn_ref, zn_ref, ban_ref)
              for c0 in range(0, dst_ref.shape[1], PROJ_COLS)]

    def project(count):
        for w0, dst_ref, c0, c1 in blocks[:count]:
            dst_ref[:, c0:c1] = _dot(u_next, wmix_ref[:, w0 + c0:w0 + c1])
        del blocks[:count]

    gdn_groups = _gdn_prep_groups(x0, x1, x2, x3, conv_ref[...])
    gdn_dst = (q_ref, k_ref, vb_ref)

    def gdn_prep(count):
        for _ in range(count):
            item = next(gdn_groups, None)
            if item is not None:
                kind, h, arr = item
                gdn_dst[kind][:, DK * h:DK * (h + 1)] = arr

    def between():
        project(3)
        gdn_prep(3)

    hb = hb_ref[...]
    r, k2, v, logw, a, g, kk, bonus = _rwkv_prep(
        pa, prev, mu_ref[...], w0_ref[...], w2_ref[...], a0_ref[...], a2_ref[...], g2_ref[...],
        kkw_ref[...], ka_ref[...], rk_ref[...], hb, between)

    nchunk = T_TILE // CHUNK
    cs = _mm(tri_ref[...], logw, _NN, 1, CUMSUM_TERMS)
    between()
    ends = [cs[CHUNK * (c + 1) - 1:CHUNK * (c + 1)] for c in range(nchunk)]
    tot = jnp.concatenate([jnp.broadcast_to(e, (CHUNK, C_A)) for e in ends], axis=0)
    dinv = jnp.exp(-cs)
    dend = jnp.exp(tot - cs)
    b_in = kk * a
    at_ref[...] = -kk * jnp.exp(cs - logw)
    bt_ref[...] = b_in * dinv
    kt_ref[...] = k2 * dinv
    between()
    rt_ref[...] = r * jnp.exp(cs)
    va_ref[...] = v
    bh_ref[...] = b_in * dend
    kh_ref[...] = k2 * dend
    gl = [jnp.exp(e) for e in ends]
    project(len(blocks))
    gdn_prep(3 * H_B)
    be_ref[...] = _sigmoid(ba)
    glog = -jnp.exp(alog_ref[...]) * _softplus(ba + dtb_ref[...])
    gc_ref[...] = _mm(tri_ref[...], glog, _NN, 1, CUMSUM_TERMS)

    strict, incl, eye = _pair_masks()
    low = lax.broadcasted_iota(jnp.int32, (CHUNK, LANES), 1) < HEAD_A

    def stack(x):
        return jnp.concatenate([jnp.where(low, x, 0.0), jnp.where(low, 0.0, x)], axis=0)

    npair = C_A // LANES
    lanes = lambda p: slice(LANES * p, LANES * (p + 1))
    rows = lambda c: slice(CHUNK * c, CHUNK * (c + 1))
    half = lambda j: slice(CHUNK * j, CHUNK * (j + 1))
    ia = [(p, c) for p in range(npair) for c in range(nchunk)]
    ib = [(pr, c) for pr in range(H_B // 2) for c in range(nchunk)]
    na, nb = range(len(ia)), range(len(ib))
    ld = lambda ref: [stack(ref[rows(c), lanes(p)]) for p, c in ia]
    at, bt, kt, rt, vs, bh, kh = (ld(ref) for ref in (at_ref, bt_ref, kt_ref, rt_ref, va_ref, bh_ref, kh_ref))
    cat = lambda ref: [jnp.concatenate([ref[rows(c), DK * h:DK * (h + 1)] for h in (2 * pr, 2 * pr + 1)], axis=0)
                       for pr, c in ib]
    col = lambda ref, off: [jnp.concatenate([ref[rows(c), off + h:off + h + 1] for h in (2 * pr, 2 * pr + 1)], axis=0)
                            for pr, c in ib]
    qg_, kg_, vg_ = cat(q_ref), cat(k_ref), cat(vb_ref)
    beta_c = col(be_ref, 0)
    gc = col(gc_ref, H_B)
    gt = [jnp.concatenate([jnp.broadcast_to(g_[CHUNK * (i + 1) - 1:CHUNK * (i + 1)], (CHUNK, 1)) for i in range(2)],
                          axis=0) for g_ in gc]
    decay = []
    for g_ in gc:
        gc_full = jnp.broadcast_to(g_, (PAIR, PAIR))
        decay.append(jnp.exp(gc_full - gc_full.T))
    kb = [kg_[i] * beta_c[i] for i in nb]
    vbb = [vg_[i] * beta_c[i] for i in nb]
    eg = [jnp.exp(g_) for g_ in gc]

    aa = [_mm(jnp.concatenate([at[i], rt[i]], axis=0), jnp.concatenate([bt[i], kt[i]], axis=0), _NT) for i in na]
    kq = [_mm(jnp.concatenate([kb[i], qg_[i]], axis=0), kg_[i], _NT) for i in nb]
    a_ab = [jnp.where(strict, aa[i][:PAIR, :PAIR], 0.0) for i in na]
    a_ak = [jnp.where(strict, aa[i][:PAIR, PAIR:], 0.0) for i in na]
    a_rb = [jnp.where(incl, aa[i][PAIR:, :PAIR], 0.0) for i in na]
    a_rk = [jnp.where(incl, aa[i][PAIR:, PAIR:], 0.0) for i in na]
    lneg = [jnp.where(strict, -(kq[i][:PAIR] * decay[i]), 0.0) for i in nb]
    qk = [jnp.where(incl, kq[i][PAIR:] * decay[i], 0.0) for i in nb]
    tall = _neumann_inverse_many(a_ab + lneg, eye, INV_TERMS)
    tinv, tinvb = tall[:len(ia)], tall[len(ia):]
    akv = [_mm(a_ak[i], vs[i]) for i in na]
    uw = [_mm(tinvb[i], jnp.concatenate([vbb[i], kb[i] * eg[i]], axis=1)) for i in nb]
    gu = [_mm(tinv[i], jnp.concatenate([at[i], akv[i]], axis=1)) for i in na]
    ow = [_mm(qk[i], uw[i]) for i in nb]
    ry = [_mm(a_rb[i], gu[i]) for i in na]
    rkv = [_mm(a_rk[i], vs[i]) for i in na]
    kd = [kg_[i] * jnp.exp(gt[i] - gc[i]) for i in nb]
    pqb = [[_mm(kd[i][half(j)], uw[i][half(j)], _TN) for j in range(2)] for i in nb]
    pq = [_mm(gu[i], bh[i], _TN) for i in na]
    vk = [_mm(vs[i], kh[i], _TN) for i in na]
    rp = [rt[i] + ry[i][:, :PAIR] for i in na]
    yc = [ry[i][:, PAIR:] + rkv[i] for i in na]
    pm = [eye * gl[ia[i][1]][:, lanes(ia[i][0])] + pq[i][:PAIR] for i in na]
    qm = [pq[i][PAIR:] + vk[i] for i in na]
    oc = [ow[i][:, :DV] for i in nb]
    rq = [qg_[i] * eg[i] - ow[i][:, DV:] for i in nb]
    pmb = [[eye * jnp.exp(gt[i][CHUNK * j:CHUNK * j + 1]) - pqb[i][j][:, DV:] for j in range(2)] for i in nb]

    sa = [sa_ref[p] for p in range(npair)]
    sb = [sb_ref[h] for h in range(H_B)]
    for c in range(nchunk):
        for p in range(npair):
            i = ia.index((p, c))
            y = _mm(rp[i], sa[p], _NT) + yc[i]
            y_ref[rows(c), lanes(p)] = y[:CHUNK] + y[CHUNK:]
            sa[p] = _mm(sa[p], pm[i], _NN, CHAIN_TERMS, CHAIN_TERMS) + qm[i]
        for h in range(H_B):
            i, j = ib.index((h // 2, c)), h % 2
            o_ref[rows(c), DV * h:DV * (h + 1)] = _mm(rq[i][half(j)], sb[h]) + oc[i][half(j)]
            sb[h] = _mm(pmb[i][j], sb[h], _NN, CHAIN_TERMS, CHAIN_TERMS) + pqb[i][j][:, :DV]
    for p in range(npair):
        sa_ref[p] = sa[p]
    for h in range(H_B):
        sb_ref[h] = sb[h]

    oa_ref[0] = _group_norm_gate(y_ref[...], bonus, g, lnw_ref[...], lnb_ref[...], hb)
    wkv_ref[0] = sa_ref[...]
    ob_ref[0] = _head_norm_gate(o_ref[...], z, gnorm_ref[...])
    gdn_ref[0] = sb_ref[...]


def _mixers_prompt(x, nmix, w_mix, rw_consts, gdn_consts, tri):
    bsz, seq, _ = x.shape
    consts = (nmix, w_mix) + tuple(rw_consts) + tuple(gdn_consts) + (tri,)
    npair = C_A // LANES
    tps, steps, load, emit, per_seq3, per_seq4 = _tile_maps(bsz, seq, T_TILE)
    tile = lambda w: pltpu.VMEM((T_TILE, w), F32)
    return pl.pallas_call(
        functools.partial(_mixers_prompt_kernel, tps),
        grid=(steps,),
        in_specs=[pl.BlockSpec((1, T_TILE, D_MODEL), load)] + [_const_spec(c.shape) for c in consts],
        out_specs=[pl.BlockSpec((1, T_TILE, C_A), emit),
                   pl.BlockSpec((1, npair, LANES, LANES), per_seq4),
                   pl.BlockSpec((1, HALO, A_COLS), per_seq3),
                   pl.BlockSpec((1, T_TILE, C_BV), emit),
                   pl.BlockSpec((1, H_B, DK, DV), per_seq4),
                   pl.BlockSpec((1, HALO, CONV_CH), per_seq3)],
        out_shape=[jax.ShapeDtypeStruct((bsz, seq, C_A), F32),
                   jax.ShapeDtypeStruct((bsz, npair, LANES, LANES), F32),
                   jax.ShapeDtypeStruct((bsz, HALO, A_COLS), F32),
                   jax.ShapeDtypeStruct((bsz, seq, C_BV), F32),
                   jax.ShapeDtypeStruct((bsz, H_B, DK, DV), F32),
                   jax.ShapeDtypeStruct((bsz, HALO, CONV_CH), F32)],
        scratch_shapes=[tile(A_COLS), pltpu.VMEM((T_TILE + HALO, A_COLS), F32), pltpu.VMEM((npair, LANES, LANES), F32)]
                       + [tile(C_A) for _ in range(8)]
                       + [tile(CONV_CH), tile(C_BV), tile(LANES), pltpu.VMEM((T_TILE + HALO, CONV_CH), F32),
                          pltpu.VMEM((H_B, DK, DV), F32), tile(C_BK), tile(C_BK), tile(C_BV), tile(LANES), tile(LANES),
                          tile(C_BV)],
        compiler_params=pltpu.CompilerParams(dimension_semantics=("arbitrary",), vmem_limit_bytes=VMEM_LIMIT),
        name="mixers_prompt",
    )(x, *consts)


def _to_columns(x):
    pad = jnp.zeros((LANES - DEC_TILE, x.shape[1]), F32)
    return jnp.concatenate([x, pad], axis=0).T


def _decode_rwkv_kernel(pa_ref, shift_ref, wkv_ref,
                        mu_ref, w0_ref, w2_ref, a0_ref, a2_ref, g2_ref, kkw_ref, ka_ref, rk_ref, lnw_ref, lnb_ref,
                        hb_ref, oa_ref, wkvo_ref, tr_ref, yt_ref, g_ref, bonus_ref):
    h = pl.program_id(0)

    @pl.when(h == 0)
    def _():
        r, k2, v, logw, a, g, kk, bonus = _rwkv_prep(
            pa_ref[...], shift_ref[...], mu_ref[...], w0_ref[...], w2_ref[...], a0_ref[...], a2_ref[...],
            g2_ref[...], kkw_ref[...], ka_ref[...], rk_ref[...], hb_ref[...])
        for i, x in enumerate((-kk, jnp.exp(logw), kk * a, k2, r, v)):
            tr_ref[i] = x.T
        g_ref[...] = g
        bonus_ref[...] = bonus

    base = pl.multiple_of(h * HEAD_A, HEAD_A)
    hs = pl.ds(base, HEAD_A)
    a_t, w_t, b_t, k_t, r_t = (tr_ref[i, hs, :] for i in range(5))

    def value_row(vi, carry):
        st = wkv_ref[0, vi]
        sa = jnp.sum(st * a_t, axis=0, keepdims=True)
        st = st * w_t + sa * b_t + tr_ref[5, pl.ds(base + vi, 1), :] * k_t
        wkvo_ref[0, vi] = st
        yt_ref[pl.ds(base + vi, 1), :] = jnp.sum(st * r_t, axis=0, keepdims=True)
        return carry

    lax.fori_loop(0, HEAD_A, value_row, 0, unroll=8)

    @pl.when(h == H_A - 1)
    def _():
        oa_ref[...] = _group_norm_gate(yt_ref[...].T, bonus_ref[...], g_ref[...], lnw_ref[...], lnb_ref[...],
                                       hb_ref[...])


def _decode_rwkv(pa, shift, wkv_t, rw_consts):
    n = pa.shape[0]
    full = lambda w: pl.BlockSpec((n, w), lambda h: (0, 0))
    state = pl.BlockSpec((1, HEAD_A, HEAD_A, n), lambda h: (h, 0, 0, 0))
    return pl.pallas_call(
        _decode_rwkv_kernel,
        grid=(H_A,),
        in_specs=[full(A_COLS), full(A_COLS), state] + [_const_spec(c.shape) for c in rw_consts],
        out_specs=[full(C_A), state],
        out_shape=[jax.ShapeDtypeStruct((n, C_A), F32), jax.ShapeDtypeStruct(wkv_t.shape, F32)],
        scratch_shapes=[pltpu.VMEM((6, C_A, n), F32), pltpu.VMEM((C_A, n), F32),
                        pltpu.VMEM((n, C_A), F32), pltpu.VMEM((n, C_A), F32)],
        compiler_params=pltpu.CompilerParams(dimension_semantics=("arbitrary",), vmem_limit_bytes=VMEM_LIMIT),
        name="decode_rwkv",
    )(pa, shift, wkv_t, *rw_consts)


def _decode_gdn_kernel(qkv_ref, z_ref, ba_ref, cst_ref, gdn_ref, conv_ref, alog_ref, dtb_ref, gnorm_ref,
                       ob_ref, gdno_ref):
    cst = cst_ref[...]
    q, k, vv, beta, glog = _gdn_prep(cst[:, :CONV_CH], cst[:, CONV_CH:2 * CONV_CH], cst[:, 2 * CONV_CH:],
                                     qkv_ref[...], ba_ref[...], conv_ref[...], alog_ref[...], dtb_ref[...])
    eg = jnp.exp(glog)
    k_cols = _to_columns(k)
    ls = lambda h: slice(DK * h, DK * (h + 1))

    def split2(x):
        hi = x.astype(BF16)
        return hi, x - hi.astype(F32)

    rows_a, rows_b = [], []
    for h in range(H_B):
        e_h = eg[:, H_B + h:H_B + h + 1]
        (wh, wl), (gh, gl_) = split2(k[:, ls(h)] * (beta[:, h:h + 1] * e_h)), split2(q[:, ls(h)] * e_h)
        rows_a.append(jnp.concatenate([wh.astype(F32), gh.astype(F32), wl, gl_], axis=0).astype(BF16))
        rows_b.append(jnp.concatenate([wh.astype(F32), gh.astype(F32)], axis=0).astype(BF16))
    gunits = [(s_i, h) for s_i in range(DEC_TILE) for h in range(H_B)]
    gn = range(len(gunits))
    be = [beta[s_i:s_i + 1, h:h + 1] for s_i, h in gunits]
    e = [eg[s_i:s_i + 1, H_B + h:H_B + h + 1] for s_i, h in gunits]
    kc = [k_cols[ls(h), s_i:s_i + 1] for s_i, h in gunits]
    gst = [gdn_ref[s_i, h] for s_i, h in gunits]
    parts = [split2(st) for st in gst]
    ra = [_dot(rows_a[h], parts[i][0].astype(BF16)) + jnp.concatenate(
        [_dot(rows_b[h], parts[i][1].astype(BF16)), jnp.zeros((2 * DEC_TILE, DV), F32)], axis=0)
        for i, (s_i, h) in enumerate(gunits)]
    pick = lambda r, j, s_i: r[DEC_TILE * j + s_i:DEC_TILE * j + s_i + 1]
    ws = [pick(ra[i], 0, s_i) + pick(ra[i], 2, s_i) for i, (s_i, h) in enumerate(gunits)]
    qs = [pick(ra[i], 1, s_i) + pick(ra[i], 3, s_i) for i, (s_i, h) in enumerate(gunits)]
    v_new = [be[i] * vv[s_i:s_i + 1, ls(h)] - ws[i] for i, (s_i, h) in enumerate(gunits)]
    qk = [jnp.sum(q[s_i:s_i + 1, ls(h)] * k[s_i:s_i + 1, ls(h)], axis=-1, keepdims=True) for s_i, h in gunits]
    for i, (s_i, h) in enumerate(gunits):
        gdno_ref[s_i, h] = gst[i] * e[i] + kc[i] * v_new[i]
    o_units = [qs[i] + qk[i] * v_new[i] for i in gn]
    o = jnp.concatenate([jnp.concatenate(o_units[H_B * s_i:H_B * (s_i + 1)], axis=1) for s_i in range(DEC_TILE)],
                        axis=0)
    ob_ref[...] = _head_norm_gate(o, z_ref[...], gnorm_ref[...])


def _decode_gdn(qkv, z, ba, cst, gdn, gdn_consts):
    n = qkv.shape[0]
    row = lambda w: pl.BlockSpec((DEC_TILE, w), lambda i: (i, 0))
    state = pl.BlockSpec((DEC_TILE, H_B, DK, DV), lambda i: (i, 0, 0, 0))
    return pl.pallas_call(
        _decode_gdn_kernel,
        grid=(n // DEC_TILE,),
        in_specs=[row(CONV_CH), row(C_BV), row(LANES), row(3 * CONV_CH), state]
                 + [_const_spec(c.shape) for c in gdn_consts],
        out_specs=[row(C_BV), state],
        out_shape=[jax.ShapeDtypeStruct((n, C_BV), F32), jax.ShapeDtypeStruct(gdn.shape, F32)],
        compiler_params=pltpu.CompilerParams(dimension_semantics=("parallel",), vmem_limit_bytes=VMEM_LIMIT),
        name="decode_gdn",
    )(qkv, z, ba, cst, gdn, *gdn_consts)


def _post_kernel(x_ref, oa_ref, ob_ref, pe_ref, xs_ref, oas_ref, obs_ref, pes_ref, *rest):
    *weights, y_ref, ys_ref = rest
    _post_rows(x_ref, oa_ref, ob_ref, pe_ref, *weights, y_ref)

    @pl.when(pl.program_id(0) == pl.num_programs(0) - 1)
    def _():
        _post_rows(xs_ref, oas_ref, obs_ref, pes_ref, *weights, ys_ref)


def _post_rows(x_ref, oa_ref, ob_ref, pe_ref, nmix_ref, wg_ref, wba_ref, wbb_ref, wout_ref, nffn_ref,
               wfg_ref, wfu_ref, wfd_ref, nple_ref, wpg_ref, wpp_ref, nfin_ref, y_ref):
    nrow = x_ref.shape[0]
    ngroup = max(1, nrow // POST_GROUP_ROWS)
    groups = [slice(r0, r0 + nrow // ngroup) for r0 in range(0, nrow, nrow // ngroup)]
    x = [x_ref[g, :] for g in groups]
    ma = [_dot(oa_ref[g, :].astype(BF16), wba_ref[...]) for g in groups]
    mb = [_dot(ob_ref[g, :].astype(BF16), wbb_ref[...]) for g in groups]
    u = [_rms(xi, nmix_ref[...]).astype(BF16) for xi in x]
    gates = [_sigmoid(_dot(ui, wg_ref[...])) for ui in u]
    mix = [gi[:, :D_MODEL] * mai + gi[:, D_MODEL:] * mbi for gi, mai, mbi in zip(gates, ma, mb)]
    h = [xi + _dot(mi.astype(BF16), wout_ref[...]) for xi, mi in zip(x, mix)]
    pp = [_dot(pe_ref[g, :].astype(BF16), wpp_ref[...]) for g in groups]
    u2 = [_rms(hi, nffn_ref[...]).astype(BF16) for hi in h]
    fg = [_silu(_dot(ui, wfg_ref[...])) for ui in u2]
    ff = [fgi * _dot(ui, wfu_ref[...]) for fgi, ui in zip(fg, u2)]
    h = [hi + _dot(fi.astype(BF16), wfd_ref[...]) for hi, fi in zip(h, ff)]
    u3 = [_rms(hi, nple_ref[...]).astype(BF16) for hi in h]
    h = [hi + _sigmoid(_dot(ui, wpg_ref[...])) * ppi for hi, ui, ppi in zip(h, u3, pp)]
    for g, hi in zip(groups, h):
        y_ref[g, :] = _rms(hi, nfin_ref[...])


def _post(x, oa, ob, pe, xs, oas, obs, pes, consts):
    n, ns = x.shape[0], xs.shape[0]
    row = lambda w: pl.BlockSpec((ROW_TILE, w), lambda i: (i, 0))
    whole = lambda w: pl.BlockSpec((ns, w), lambda i: (0, 0))
    return pl.pallas_call(
        _post_kernel,
        grid=(n // ROW_TILE,),
        in_specs=[row(D_MODEL), row(C_A), row(C_BV), row(PLE_DIM),
                  whole(D_MODEL), whole(C_A), whole(C_BV), whole(PLE_DIM)] + [_const_spec(c.shape) for c in consts],
        out_specs=[row(D_MODEL), whole(D_MODEL)],
        out_shape=[jax.ShapeDtypeStruct((n, D_MODEL), F32), jax.ShapeDtypeStruct((ns, D_MODEL), F32)],
        compiler_params=pltpu.CompilerParams(dimension_semantics=("arbitrary",), vmem_limit_bytes=VMEM_LIMIT),
        name="post",
    )(x, oa, ob, pe, xs, oas, obs, pes, *consts)


def _chunk_cumsum_matrix(tile):
    i = jnp.arange(tile)
    same = (i[:, None] // CHUNK) == (i[None, :] // CHUNK)
    return (same & (i[None, :] <= i[:, None])).astype(BF16)


def kernel(x_prompt, x_sample, p_prompt, p_sample, state_shift, state_wkv, state_conv, state_gdn, norm_mix, w_in, mu_shift, rw_w0, rw_w2, rw_a0, rw_a2, rw_g2, rw_kk, rw_ka, rw_rk, rw_ln_w, rw_ln_b, gdn_conv, gdn_a_log, gdn_dt_bias, gdn_norm, w_branch_a, w_branch_b, w_out, norm_ffn, w_ffn_gate, w_ffn_up, w_ffn_down, norm_ple, w_ple_gate, w_ple_proj, norm_final):
    bsz, seq, _ = x_prompt.shape
    nd = x_sample.shape[0]
    row = lambda p: p.reshape(1, -1)

    w_in0 = w_in[0]
    b0 = A_COLS
    w_mix = w_in0[:, :b0 + CONV_CH + C_BV + LANES].astype(BF16)
    wg = w_in0[:, b0 + CONV_CH + C_BV + 2 * H_B:].astype(BF16)
    w2p = jnp.concatenate([rw_w2[0], jnp.zeros((LORA_A, C_A), F32)], axis=0).astype(BF16)
    a2p = jnp.concatenate([jnp.zeros((LORA_W, C_A), F32), rw_a2[0]], axis=0).astype(BF16)
    ch = jnp.arange(LANES) // HEAD_A
    hb = (ch[:, None] == ch[None, :]).astype(BF16)
    tri = _chunk_cumsum_matrix(T_TILE)
    alog = jnp.pad(gdn_a_log[0], (H_B, LANES - 2 * H_B)).reshape(1, LANES)
    dtb = jnp.pad(gdn_dt_bias[0], (H_B, LANES - 2 * H_B)).reshape(1, LANES)
    rw_consts = (row(mu_shift[0]), row(rw_w0[0]), w2p, row(rw_a0[0]), a2p, rw_g2[0].astype(BF16), row(rw_kk[0]), row(rw_ka[0]),
                 row(rw_rk[0]), row(rw_ln_w[0]), row(rw_ln_b[0]), hb)
    gdn_consts = (gdn_conv[0], alog, dtb, row(gdn_norm[0]))
    post_consts = (row(norm_mix[0]), wg, w_branch_a[0].astype(BF16), w_branch_b[0].astype(BF16),
                   w_out[0].astype(BF16), row(norm_ffn[0]), w_ffn_gate[0].astype(BF16), w_ffn_up[0].astype(BF16),
                   w_ffn_down[0].astype(BF16), row(norm_ple[0]), w_ple_gate[0].astype(BF16),
                   w_ple_proj[0].astype(BF16), row(norm_final))

    xp = x_prompt.reshape(bsz * seq, D_MODEL)
    oa, wkv_pairs, shift_tail, ob, gdn_p, conv_tail = _mixers_prompt(
        x_prompt, row(norm_mix[0]), w_mix, rw_consts, gdn_consts, tri)
    wkv_p = jnp.stack([wkv_pairs[:, :, :HEAD_A, :HEAD_A], wkv_pairs[:, :, HEAD_A:, HEAD_A:]], axis=2)
    wkv_p = wkv_p.reshape(bsz, H_A, HEAD_A, HEAD_A)
    shift_p = shift_tail[:, HALO - 1:, :]
    conv_p = conv_tail[:, HALO - (CONV_W - 1):, :]

    xs = x_sample.reshape(nd, D_MODEL)
    pa_s, qkv_s, z_s, ba_s = _inproj(xs, row(norm_mix[0]), w_mix)
    cst = state_conv[0].reshape(nd, (CONV_W - 1) * CONV_CH)
    oa_s, wkv_t = _decode_rwkv(pa_s, state_shift[0].reshape(nd, A_COLS), jnp.transpose(state_wkv[0], (1, 2, 3, 0)),
                               rw_consts)
    wkv_s = jnp.transpose(wkv_t, (3, 0, 1, 2))
    ob_s, gdn_s = _decode_gdn(qkv_s, z_s, ba_s, cst, state_gdn[0], gdn_consts)
    conv_s = jnp.concatenate([cst[:, CONV_CH:], qkv_s], axis=1).reshape(nd, CONV_W - 1, CONV_CH)

    y_prompt, y_sample = _post(xp, oa.reshape(bsz * seq, C_A), ob.reshape(bsz * seq, C_BV),
                               p_prompt[0].reshape(bsz * seq, PLE_DIM),
                               xs, oa_s, ob_s, p_sample[0].reshape(nd, PLE_DIM), post_consts)
    y_prompt = y_prompt.reshape(bsz, seq, D_MODEL)
    y_sample = y_sample.reshape(nd, 1, D_MODEL)

    return (y_prompt, y_sample, shift_p[None], wkv_p[None], conv_p[None], gdn_p[None],
            pa_s.reshape(1, nd, 1, A_COLS), wkv_s[None], conv_s[None], gdn_s[None])
```

```python
import functools

import jax
import jax.numpy as jnp
from jax import lax
from jax.experimental import pallas as pl
from jax.experimental.pallas import tpu as pltpu

F32 = jnp.float32
BF16 = jnp.bfloat16

D_MODEL = 1024
HEAD_A = 64
C_A = 512
H_A = 8
LORA_W = 64
LORA_A = 64
LORA_G = 128
A_COLS = 3 * C_A + LORA_W + LORA_A + LORA_G
DK = 128
DV = 128
H_B = 4
C_BK = 512
C_BV = 512
CONV_W = 4
CONV_CH = 2 * C_BK + C_BV
D_FF = 2816
PLE_DIM = 256
NORM_EPS = 1e-6
GN_EPS = 64e-5
L2_EPS = 1e-6

LANES = 128
CHUNK = 64
PAIR = 2 * CHUNK
T_TILE = 256
PROJ_COLS = 256
ROW_TILE = 512
POST_GROUP_ROWS = 128
DEC_TILE = 32
HALO = 8
V7X_VMEM_BYTES = 64 * 1024 * 1024
VMEM_LIMIT = V7X_VMEM_BYTES * 7 // 8
SUM_TERMS = 1
CUMSUM_TERMS = 2
INV_TERMS = 1
CHAIN_TERMS = 1


def _dot(a, b):
    return jnp.dot(a, b, preferred_element_type=F32)


def _sigmoid(x):
    return 1.0 / (1.0 + jnp.exp(-x))


def _silu(x):
    return x * _sigmoid(x)


def _softplus(x):
    return jnp.maximum(x, 0.0) + jnp.log(1.0 + jnp.exp(-jnp.abs(x)))


def _rms(x, gain):
    return x * lax.rsqrt(jnp.mean(x * x, axis=-1, keepdims=True) + NORM_EPS) * gain


def _pair_masks():
    ri = lax.broadcasted_iota(jnp.int32, (PAIR, PAIR), 0)
    ci = lax.broadcasted_iota(jnp.int32, (PAIR, PAIR), 1)
    same = (ri < CHUNK) == (ci < CHUNK)
    strict = same & (ci < ri)
    incl = same & (ci <= ri)
    eye = (ri == ci).astype(F32)
    return strict, incl, eye


_NN = (((1,), (0,)), ((), ()))
_NT = (((1,), (1,)), ((), ()))
_TN = (((0,), (0,)), ((), ()))


def _split(x, terms):
    if isinstance(x, (list, tuple)):
        return list(x)
    if x.dtype == BF16:
        return [x]
    parts = []
    for i in range(terms):
        h = x.astype(BF16)
        parts.append(h)
        if i + 1 < terms:
            x = x - h.astype(F32)
    return parts


def _mm(a, b, dims=_NN, na=1, nb=1):
    pa, pb = _split(a, na), _split(b, nb)
    acc = None
    for i, ai in enumerate(pa):
        for j, bj in enumerate(pb):
            if i + j < max(len(pa), len(pb)):
                d = lax.dot_general(ai, bj, dims, preferred_element_type=F32)
                acc = d if acc is None else acc + d
    return acc


def _neumann_inverse_many(lmats, eye, terms):
    ts = [eye + l for l in lmats]
    ps = [_split(l, terms) for l in lmats]
    n = 2
    while n < CHUNK:
        ps = [_split(_mm(p, p), terms) for p in ps]
        ts = [t + _mm(t, p, _NN, terms) for t, p in zip(ts, ps)]
        n *= 2
    return ts


def _mix_segments(pa_ref, qkv_ref, z_ref, ba_ref):
    return ((0, pa_ref), (A_COLS, qkv_ref), (A_COLS + CONV_CH, z_ref), (A_COLS + CONV_CH + C_BV, ba_ref))


def _inproj_kernel(x_ref, g_ref, w_ref, pa_ref, qkv_ref, z_ref, ba_ref):
    u = _rms(x_ref[...], g_ref[...]).astype(BF16)
    for c0, dst_ref in _mix_segments(pa_ref, qkv_ref, z_ref, ba_ref):
        dst_ref[...] = _dot(u, w_ref[:, c0:c0 + dst_ref.shape[1]])


def _const_spec(shape):
    nd = len(shape)
    return pl.BlockSpec(shape, lambda *_: (0,) * nd, pipeline_mode=pl.Buffered(1))


def _inproj(x, gain, w_mix):
    n = x.shape[0]
    tm = min(ROW_TILE, n)
    row = lambda w: pl.BlockSpec((tm, w), lambda i: (i, 0))
    return pl.pallas_call(
        _inproj_kernel,
        grid=(n // tm,),
        in_specs=[row(D_MODEL), _const_spec(gain.shape), _const_spec(w_mix.shape)],
        out_specs=[row(A_COLS), row(CONV_CH), row(C_BV), row(LANES)],
        out_shape=[jax.ShapeDtypeStruct((n, A_COLS), F32), jax.ShapeDtypeStruct((n, CONV_CH), F32),
                   jax.ShapeDtypeStruct((n, C_BV), F32), jax.ShapeDtypeStruct((n, LANES), F32)],
        compiler_params=pltpu.CompilerParams(dimension_semantics=("parallel",), vmem_limit_bytes=VMEM_LIMIT),
        name="inproj",
    )(x, gain, w_mix)


def _rwkv_prep(pa, prev, mu, w0, w2p, a0, a2p, g2, kkw, ka, rk, hb, between=lambda: None):
    xa = pa + (prev - pa) * mu
    r = xa[:, :C_A]
    k = xa[:, C_A:2 * C_A]
    v = xa[:, 2 * C_A:3 * C_A]
    xwa = xa[:, 3 * C_A:3 * C_A + LORA_W + LORA_A]
    xg = xa[:, 3 * C_A + LORA_W + LORA_A:]
    w_pre = _mm(jnp.tanh(xwa), w2p)
    a_pre = _mm(xwa, a2p)
    g = _mm(_sigmoid(xg), g2)
    kx = k * kkw
    kk_den = _head_sum(kx * kx, hb)
    between()
    w_log = -_softplus(-(w0 + w_pre)) - 0.5
    logw = -jnp.exp(w_log)
    a = _sigmoid(a0 + a_pre)
    kk = kx * lax.rsqrt(kk_den + L2_EPS)
    k2 = k * (1.0 + (a - 1.0) * ka)
    bonus_sum = _head_sum(r * k2 * rk, hb)
    between()
    bonus = bonus_sum * v
    return r, k2, v, logw, a, g, kk, bonus


def _head_sum(x, hb):
    groups = [_mm(x[:, LANES * j:LANES * (j + 1)], hb, _NN, SUM_TERMS) for j in range(x.shape[1] // LANES)]
    return jnp.concatenate(groups, axis=1)


def _group_norm_gate(y, bonus, g, lnw, lnb, hb):
    mean = _head_sum(y, hb) * (1.0 / HEAD_A)
    d = y - mean
    var = _head_sum(d * d, hb) * (1.0 / HEAD_A)
    yn = d * lax.rsqrt(var + GN_EPS) * lnw + lnb
    return (yn + bonus) * g


def _gdn_prep(x0, x1, x2, x3, ba, conv, alog, dtb):
    c = _silu(x0 * conv[0:1] + x1 * conv[1:2] + x2 * conv[2:3] + x3 * conv[3:4])
    qs, ks = [], []
    for h in range(H_B):
        qh = c[:, DK * h:DK * (h + 1)]
        kh = c[:, C_BK + DK * h:C_BK + DK * (h + 1)]
        qs.append(qh * lax.rsqrt(jnp.sum(qh * qh, axis=-1, keepdims=True) + L2_EPS) * (DK ** -0.5))
        ks.append(kh * lax.rsqrt(jnp.sum(kh * kh, axis=-1, keepdims=True) + L2_EPS))
    q = jnp.concatenate(qs, axis=1)
    k = jnp.concatenate(ks, axis=1)
    v = c[:, 2 * C_BK:]
    beta = _sigmoid(ba)
    glog = -jnp.exp(alog) * _softplus(ba + dtb)
    return q, k, v, beta, glog


def _head_norm_gate(o, z, gnorm):
    outs = []
    for h in range(H_B):
        oh = o[:, DV * h:DV * (h + 1)]
        zh = z[:, DV * h:DV * (h + 1)]
        oh = oh * lax.rsqrt(jnp.mean(oh * oh, axis=-1, keepdims=True) + NORM_EPS) * gnorm
        outs.append(oh * _silu(zh))
    return jnp.concatenate(outs, axis=1)


def _tile_maps(bsz, seq, tile):
    tps = seq // tile
    last = bsz * tps - 1
    cur = lambda s: jnp.minimum(s, last)
    prv = lambda s: jnp.maximum(s - 1, 0)
    load = lambda s: (cur(s) // tps, cur(s) % tps, 0)
    emit = lambda s: (prv(s) // tps, prv(s) % tps, 0)
    per_seq3 = lambda s: (prv(s) // tps, 0, 0)
    per_seq4 = lambda s: (prv(s) // tps, 0, 0, 0)
    return tps, last + 2, load, emit, per_seq3, per_seq4


def _gdn_prep_groups(x0, x1, x2, x3, conv):
    def conv_cols(c0, c1):
        return _silu(x0[:, c0:c1] * conv[0:1, c0:c1] + x1[:, c0:c1] * conv[1:2, c0:c1]
                     + x2[:, c0:c1] * conv[2:3, c0:c1] + x3[:, c0:c1] * conv[3:4, c0:c1])

    def unit(xh):
        return xh * lax.rsqrt(jnp.sum(xh * xh, axis=-1, keepdims=True) + L2_EPS)

    for h in range(H_B):
        yield 0, h, unit(conv_cols(DK * h, DK * (h + 1))) * (DK ** -0.5)
        yield 1, h, unit(conv_cols(C_BK + DK * h, C_BK + DK * (h + 1)))
        yield 2, h, conv_cols(2 * C_BK + DV * h, 2 * C_BK + DV * (h + 1))


def _mixers_prompt_kernel(tiles_per_seq, x_ref, nmix_ref, wmix_ref,
                          mu_ref, w0_ref, w2_ref, a0_ref, a2_ref, g2_ref, kkw_ref, ka_ref, rk_ref,
                          lnw_ref, lnb_ref, hb_ref, conv_ref, alog_ref, dtb_ref, gnorm_ref, tri_ref,
                          oa_ref, wkv_ref, shift_ref, ob_ref, gdn_ref, convo_ref,
                          pa_ref, exta_ref, sa_ref, at_ref, bt_ref, kt_ref, rt_ref, va_ref, bh_ref, kh_ref, y_ref,
                          qkvn_ref, zn_ref, ban_ref, extb_ref, sb_ref, q_ref, k_ref, vb_ref, gc_ref, be_ref, o_ref):
    s_id = pl.program_id(0)

    @pl.when(s_id == 0)
    def _():
        pa_ref[...] = jnp.zeros_like(pa_ref)
        qkvn_ref[...] = jnp.zeros_like(qkvn_ref)
        zn_ref[...] = jnp.zeros_like(zn_ref)
        ban_ref[...] = jnp.zeros_like(ban_ref)

    @pl.when(jnp.maximum(s_id - 1, 0) % tiles_per_seq == 0)
    def _():
        exta_ref[0:HALO, :] = jnp.zeros((HALO, A_COLS), F32)
        extb_ref[0:HALO, :] = jnp.zeros((HALO, CONV_CH), F32)
        sa_ref[...] = jnp.zeros_like(sa_ref)
        sb_ref[...] = jnp.zeros_like(sb_ref)

    pa = pa_ref[...]
    exta_ref[HALO:, :] = pa
    prev = exta_ref[pl.ds(HALO - 1, T_TILE), :]
    exta_ref[0:HALO, :] = pa[T_TILE - HALO:, :]
    shift_ref[0] = pa[T_TILE - HALO:, :]
    x3 = qkvn_ref[...]
    z = zn_ref[...]
    ba = ban_ref[...]
    extb_ref[HALO:, :] = x3
    x0 = extb_ref[pl.ds(HALO - 3, T_TILE), :]
    x1 = extb_ref[pl.ds(HALO - 2, T_TILE), :]
    x2 = extb_ref[pl.ds(HALO - 1, T_TILE), :]
    extb_ref[0:HALO, :] = x3[T_TILE - HALO:, :]
    convo_ref[0] = x3[T_TILE - HALO:, :]

    u_next = _rms(x_ref[0], nmix_ref[...]).astype(BF16)
    blocks = [(w0, dst_ref, c0, min(c0 + PROJ_COLS, dst_ref.shape[1]))
              for w0, dst_ref in _mix_segments(pa_ref, qkvn_ref, zn_ref, ban_ref)
              for c0 in range(0, dst_ref.shape[1], PROJ_COLS)]

    def project(count):
        for w0, dst_ref, c0, c1 in blocks[:count]:
            dst_ref[:, c0:c1] = _dot(u_next, wmix_ref[:, w0 + c0:w0 + c1])
        del blocks[:count]

    gdn_groups = _gdn_prep_groups(x0, x1, x2, x3, conv_ref[...])
    gdn_dst = (q_ref, k_ref, vb_ref)

    def gdn_prep(count):
        for _ in range(count):
            item = next(gdn_groups, None)
            if item is not None:
                kind, h, arr = item
                gdn_dst[kind][:, DK * h:DK * (h + 1)] = arr

    def between():
        project(3)
        gdn_prep(3)

    hb = hb_ref[...]
    r, k2, v, logw, a, g, kk, bonus = _rwkv_prep(
        pa, prev, mu_ref[...], w0_ref[...], w2_ref[...], a0_ref[...], a2_ref[...], g2_ref[...],
        kkw_ref[...], ka_ref[...], rk_ref[...], hb, between)

    nchunk = T_TILE // CHUNK
    cs = _mm(tri_ref[...], logw, _NN, 1, CUMSUM_TERMS)
    between()
    ends = [cs[CHUNK * (c + 1) - 1:CHUNK * (c + 1)] for c in range(nchunk)]
    tot = jnp.concatenate([jnp.broadcast_to(e, (CHUNK, C_A)) for e in ends], axis=0)
    dinv = jnp.exp(-cs)
    dend = jnp.exp(tot - cs)
    b_in = kk * a
    at_ref[...] = -kk * jnp.exp(cs - logw)
    bt_ref[...] = b_in * dinv
    kt_ref[...] = k2 * dinv
    between()
    rt_ref[...] = r * jnp.exp(cs)
    va_ref[...] = v
    bh_ref[...] = b_in * dend
    kh_ref[...] = k2 * dend
    gl = [jnp.exp(e) for e in ends]
    project(len(blocks))
    gdn_prep(3 * H_B)
    be_ref[...] = _sigmoid(ba)
    glog = -jnp.exp(alog_ref[...]) * _softplus(ba + dtb_ref[...])
    gc_ref[...] = _mm(tri_ref[...], glog, _NN, 1, CUMSUM_TERMS)

    strict, incl, eye = _pair_masks()
    low = lax.broadcasted_iota(jnp.int32, (CHUNK, LANES), 1) < HEAD_A

    def stack(x):
        return jnp.concatenate([jnp.where(low, x, 0.0), jnp.where(low, 0.0, x)], axis=0)

    npair = C_A // LANES
    lanes = lambda p: slice(LANES * p, LANES * (p + 1))
    rows = lambda c: slice(CHUNK * c, CHUNK * (c + 1))
    half = lambda j: slice(CHUNK * j, CHUNK * (j + 1))
    ia = [(p, c) for p in range(npair) for c in range(nchunk)]
    ib = [(pr, c) for pr in range(H_B // 2) for c in range(nchunk)]
    na, nb = range(len(ia)), range(len(ib))
    ld = lambda ref: [stack(ref[rows(c), lanes(p)]) for p, c in ia]
    at, bt, kt, rt, vs, bh, kh = (ld(ref) for ref in (at_ref, bt_ref, kt_ref, rt_ref, va_ref, bh_ref, kh_ref))
    cat = lambda ref: [jnp.concatenate([ref[rows(c), DK * h:DK * (h + 1)] for h in (2 * pr, 2 * pr + 1)], axis=0)
                       for pr, c in ib]
    col = lambda ref, off: [jnp.concatenate([ref[rows(c), off + h:off + h + 1] for h in (2 * pr, 2 * pr + 1)], axis=0)
                            for pr, c in ib]
    qg_, kg_, vg_ = cat(q_ref), cat(k_ref), cat(vb_ref)
    beta_c = col(be_ref, 0)
    gc = col(gc_ref, H_B)
    gt = [jnp.concatenate([jnp.broadcast_to(g_[CHUNK * (i + 1) - 1:CHUNK * (i + 1)], (CHUNK, 1)) for i in range(2)],
                          axis=0) for g_ in gc]
    decay = []
    for g_ in gc:
        gc_full = jnp.broadcast_to(g_, (PAIR, PAIR))
        decay.append(jnp.exp(gc_full - gc_full.T))
    kb = [kg_[i] * beta_c[i] for i in nb]
    vbb = [vg_[i] * beta_c[i] for i in nb]
    eg = [jnp.exp(g_) for g_ in gc]

    aa = [_mm(jnp.concatenate([at[i], rt[i]], axis=0), jnp.concatenate([bt[i], kt[i]], axis=0), _NT) for i in na]
    kq = [_mm(jnp.concatenate([kb[i], qg_[i]], axis=0), kg_[i], _NT) for i in nb]
    a_ab = [jnp.where(strict, aa[i][:PAIR, :PAIR], 0.0) for i in na]
    a_ak = [jnp.where(strict, aa[i][:PAIR, PAIR:], 0.0) for i in na]
    a_rb = [jnp.where(incl, aa[i][PAIR:, :PAIR], 0.0) for i in na]
    a_rk = [jnp.where(incl, aa[i][PAIR:, PAIR:], 0.0) for i in na]
    lneg = [jnp.where(strict, -(kq[i][:PAIR] * decay[i]), 0.0) for i in nb]
    qk = [jnp.where(incl, kq[i][PAIR:] * decay[i], 0.0) for i in nb]
    tall = _neumann_inverse_many(a_ab + lneg, eye, INV_TERMS)
    tinv, tinvb = tall[:len(ia)], tall[len(ia):]
    akv = [_mm(a_ak[i], vs[i]) for i in na]
    uw = [_mm(tinvb[i], jnp.concatenate([vbb[i], kb[i] * eg[i]], axis=1)) for i in nb]
    gu = [_mm(tinv[i], jnp.concatenate([at[i], akv[i]], axis=1)) for i in na]
    ow = [_mm(qk[i], uw[i]) for i in nb]
    ry = [_mm(a_rb[i], gu[i]) for i in na]
    rkv = [_mm(a_rk[i], vs[i]) for i in na]
    kd = [kg_[i] * jnp.exp(gt[i] - gc[i]) for i in nb]
    pqb = [[_mm(kd[i][half(j)], uw[i][half(j)], _TN) for j in range(2)] for i in nb]
    pq = [_mm(gu[i], bh[i], _TN) for i in na]
    vk = [_mm(vs[i], kh[i], _TN) for i in na]
    rp = [rt[i] + ry[i][:, :PAIR] for i in na]
    yc = [ry[i][:, PAIR:] + rkv[i] for i in na]
    pm = [eye * gl[ia[i][1]][:, lanes(ia[i][0])] + pq[i][:PAIR] for i in na]
    qm = [pq[i][PAIR:] + vk[i] for i in na]
    oc = [ow[i][:, :DV] for i in nb]
    rq = [qg_[i] * eg[i] - ow[i][:, DV:] for i in nb]
    pmb = [[eye * jnp.exp(gt[i][CHUNK * j:CHUNK * j + 1]) - pqb[i][j][:, DV:] for j in range(2)] for i in nb]

    sa = [sa_ref[p] for p in range(npair)]
    sb = [sb_ref[h] for h in range(H_B)]
    for c in range(nchunk):
        for p in range(npair):
            i = ia.index((p, c))
            y = _mm(rp[i], sa[p], _NT) + yc[i]
            y_ref[rows(c), lanes(p)] = y[:CHUNK] + y[CHUNK:]
            sa[p] = _mm(sa[p], pm[i], _NN, CHAIN_TERMS, CHAIN_TERMS) + qm[i]
        for h in range(H_B):
            i, j = ib.index((h // 2, c)), h % 2
            o_ref[rows(c), DV * h:DV * (h + 1)] = _mm(rq[i][half(j)], sb[h]) + oc[i][half(j)]
            sb[h] = _mm(pmb[i][j], sb[h], _NN, CHAIN_TERMS, CHAIN_TERMS) + pqb[i][j][:, :DV]
    for p in range(npair):
        sa_ref[p] = sa[p]
    for h in range(H_B):
        sb_ref[h] = sb[h]

    oa_ref[0] = _group_norm_gate(y_ref[...], bonus, g, lnw_ref[...], lnb_ref[...], hb)
    wkv_ref[0] = sa_ref[...]
    ob_ref[0] = _head_norm_gate(o_ref[...], z, gnorm_ref[...])
    gdn_ref[0] = sb_ref[...]


def _mixers_prompt(x, nmix, w_mix, rw_consts, gdn_consts, tri):
    bsz, seq, _ = x.shape
    consts = (nmix, w_mix) + tuple(rw_consts) + tuple(gdn_consts) + (tri,)
    npair = C_A // LANES
    tps, steps, load, emit, per_seq3, per_seq4 = _tile_maps(bsz, seq, T_TILE)
    tile = lambda w: pltpu.VMEM((T_TILE, w), F32)
    return pl.pallas_call(
        functools.partial(_mixers_prompt_kernel, tps),
        grid=(steps,),
        in_specs=[pl.BlockSpec((1, T_TILE, D_MODEL), load)] + [_const_spec(c.shape) for c in consts],
        out_specs=[pl.BlockSpec((1, T_TILE, C_A), emit),
                   pl.BlockSpec((1, npair, LANES, LANES), per_seq4),
                   pl.BlockSpec((1, HALO, A_COLS), per_seq3),
                   pl.BlockSpec((1, T_TILE, C_BV), emit),
                   pl.BlockSpec((1, H_B, DK, DV), per_seq4),
                   pl.BlockSpec((1, HALO, CONV_CH), per_seq3)],
        out_shape=[jax.ShapeDtypeStruct((bsz, seq, C_A), F32),
                   jax.ShapeDtypeStruct((bsz, npair, LANES, LANES), F32),
                   jax.ShapeDtypeStruct((bsz, HALO, A_COLS), F32),
                   jax.ShapeDtypeStruct((bsz, seq, C_BV), F32),
                   jax.ShapeDtypeStruct((bsz, H_B, DK, DV), F32),
                   jax.ShapeDtypeStruct((bsz, HALO, CONV_CH), F32)],
        scratch_shapes=[tile(A_COLS), pltpu.VMEM((T_TILE + HALO, A_COLS), F32), pltpu.VMEM((npair, LANES, LANES), F32)]
                       + [tile(C_A) for _ in range(8)]
                       + [tile(CONV_CH), tile(C_BV), tile(LANES), pltpu.VMEM((T_TILE + HALO, CONV_CH), F32),
                          pltpu.VMEM((H_B, DK, DV), F32), tile(C_BK), tile(C_BK), tile(C_BV), tile(LANES), tile(LANES),
                          tile(C_BV)],
        compiler_params=pltpu.CompilerParams(dimension_semantics=("arbitrary",), vmem_limit_bytes=VMEM_LIMIT),
        name="mixers_prompt",
    )(x, *consts)


def _to_columns(x):
    pad = jnp.zeros((LANES - DEC_TILE, x.shape[1]), F32)
    return jnp.concatenate([x, pad], axis=0).T


def _decode_rwkv_kernel(pa_ref, shift_ref, wkv_ref,
                        mu_ref, w0_ref, w2_ref, a0_ref, a2_ref, g2_ref, kkw_ref, ka_ref, rk_ref, lnw_ref, lnb_ref,
                        hb_ref, oa_ref, wkvo_ref, tr_ref, yt_ref, g_ref, bonus_ref):
    h = pl.program_id(0)

    @pl.when(h == 0)
    def _():
        r, k2, v, logw, a, g, kk, bonus = _rwkv_prep(
            pa_ref[...], shift_ref[...], mu_ref[...], w0_ref[...], w2_ref[...], a0_ref[...], a2_ref[...],
            g2_ref[...], kkw_ref[...], ka_ref[...], rk_ref[...], hb_ref[...])
        for i, x in enumerate((-kk, jnp.exp(logw), kk * a, k2, r, v)):
            tr_ref[i] = x.T
        g_ref[...] = g
        bonus_ref[...] = bonus

    base = pl.multiple_of(h * HEAD_A, HEAD_A)
    hs = pl.ds(base, HEAD_A)
    a_t, w_t, b_t, k_t, r_t = (tr_ref[i, hs, :] for i in range(5))

    def value_row(vi, carry):
        st = wkv_ref[0, vi]
        sa = jnp.sum(st * a_t, axis=0, keepdims=True)
        st = st * w_t + sa * b_t + tr_ref[5, pl.ds(base + vi, 1), :] * k_t
        wkvo_ref[0, vi] = st
        yt_ref[pl.ds(base + vi, 1), :] = jnp.sum(st * r_t, axis=0, keepdims=True)
        return carry

    lax.fori_loop(0, HEAD_A, value_row, 0, unroll=8)

    @pl.when(h == H_A - 1)
    def _():
        oa_ref[...] = _group_norm_gate(yt_ref[...].T, bonus_ref[...], g_ref[...], lnw_ref[...], lnb_ref[...],
                                       hb_ref[...])


def _decode_rwkv(pa, shift, wkv_t, rw_consts):
    n = pa.shape[0]
    full = lambda w: pl.BlockSpec((n, w), lambda h: (0, 0))
    state = pl.BlockSpec((1, HEAD_A, HEAD_A, n), lambda h: (h, 0, 0, 0))
    return pl.pallas_call(
        _decode_rwkv_kernel,
        grid=(H_A,),
        in_specs=[full(A_COLS), full(A_COLS), state] + [_const_spec(c.shape) for c in rw_consts],
        out_specs=[full(C_A), state],
        out_shape=[jax.ShapeDtypeStruct((n, C_A), F32), jax.ShapeDtypeStruct(wkv_t.shape, F32)],
        scratch_shapes=[pltpu.VMEM((6, C_A, n), F32), pltpu.VMEM((C_A, n), F32),
                        pltpu.VMEM((n, C_A), F32), pltpu.VMEM((n, C_A), F32)],
        compiler_params=pltpu.CompilerParams(dimension_semantics=("arbitrary",), vmem_limit_bytes=VMEM_LIMIT),
        name="decode_rwkv",
    )(pa, shift, wkv_t, *rw_consts)


def _decode_gdn_kernel(qkv_ref, z_ref, ba_ref, cst_ref, gdn_ref, conv_ref, alog_ref, dtb_ref, gnorm_ref,
                       ob_ref, gdno_ref):
    cst = cst_ref[...]
    q, k, vv, beta, glog = _gdn_prep(cst[:, :CONV_CH], cst[:, CONV_CH:2 * CONV_CH], cst[:, 2 * CONV_CH:],
                                     qkv_ref[...], ba_ref[...], conv_ref[...], alog_ref[...], dtb_ref[...])
    eg = jnp.exp(glog)
    k_cols = _to_columns(k)
    ls = lambda h: slice(DK * h, DK * (h + 1))

    def split2(x):
        hi = x.astype(BF16)
        return hi, x - hi.astype(F32)

    rows_a, rows_b = [], []
    for h in range(H_B):
        e_h = eg[:, H_B + h:H_B + h + 1]
        (wh, wl), (gh, gl_) = split2(k[:, ls(h)] * (beta[:, h:h + 1] * e_h)), split2(q[:, ls(h)] * e_h)
        rows_a.append(jnp.concatenate([wh.astype(F32), gh.astype(F32), wl, gl_], axis=0).astype(BF16))
        rows_b.append(jnp.concatenate([wh.astype(F32), gh.astype(F32)], axis=0).astype(BF16))
    gunits = [(s_i, h) for s_i in range(DEC_TILE) for h in range(H_B)]
    gn = range(len(gunits))
    be = [beta[s_i:s_i + 1, h:h + 1] for s_i, h in gunits]
    e = [eg[s_i:s_i + 1, H_B + h:H_B + h + 1] for s_i, h in gunits]
    kc = [k_cols[ls(h), s_i:s_i + 1] for s_i, h in gunits]
    gst = [gdn_ref[s_i, h] for s_i, h in gunits]
    parts = [split2(st) for st in gst]
    ra = [_dot(rows_a[h], parts[i][0].astype(BF16)) + jnp.concatenate(
        [_dot(rows_b[h], parts[i][1].astype(BF16)), jnp.zeros((2 * DEC_TILE, DV), F32)], axis=0)
        for i, (s_i, h) in enumerate(gunits)]
    pick = lambda r, j, s_i: r[DEC_TILE * j + s_i:DEC_TILE * j + s_i + 1]
    ws = [pick(ra[i], 0, s_i) + pick(ra[i], 2, s_i) for i, (s_i, h) in enumerate(gunits)]
    qs = [pick(ra[i], 1, s_i) + pick(ra[i], 3, s_i) for i, (s_i, h) in enumerate(gunits)]
    v_new = [be[i] * vv[s_i:s_i + 1, ls(h)] - ws[i] for i, (s_i, h) in enumerate(gunits)]
    qk = [jnp.sum(q[s_i:s_i + 1, ls(h)] * k[s_i:s_i + 1, ls(h)], axis=-1, keepdims=True) for s_i, h in gunits]
    for i, (s_i, h) in enumerate(gunits):
        gdno_ref[s_i, h] = gst[i] * e[i] + kc[i] * v_new[i]
    o_units = [qs[i] + qk[i] * v_new[i] for i in gn]
    o = jnp.concatenate([jnp.concatenate(o_units[H_B * s_i:H_B * (s_i + 1)], axis=1) for s_i in range(DEC_TILE)],
                        axis=0)
    ob_ref[...] = _head_norm_gate(o, z_ref[...], gnorm_ref[...])


def _decode_gdn(qkv, z, ba, cst, gdn, gdn_consts):
    n = qkv.shape[0]
    row = lambda w: pl.BlockSpec((DEC_TILE, w), lambda i: (i, 0))
    state = pl.BlockSpec((DEC_TILE, H_B, DK, DV), lambda i: (i, 0, 0, 0))
    return pl.pallas_call(
        _decode_gdn_kernel,
        grid=(n // DEC_TILE,),
        in_specs=[row(CONV_CH), row(C_BV), row(LANES), row(3 * CONV_CH), state]
                 + [_const_spec(c.shape) for c in gdn_consts],
        out_specs=[row(C_BV), state],
        out_shape=[jax.ShapeDtypeStruct((n, C_BV), F32), jax.ShapeDtypeStruct(gdn.shape, F32)],
        compiler_params=pltpu.CompilerParams(dimension_semantics=("parallel",), vmem_limit_bytes=VMEM_LIMIT),
        name="decode_gdn",
    )(qkv, z, ba, cst, gdn, *gdn_consts)


def _post_kernel(x_ref, oa_ref, ob_ref, pe_ref, nmix_ref, wg_ref, wba_ref, wbb_ref, wout_ref, nffn_ref,
                 wfg_ref, wfu_ref, wfd_ref, nple_ref, wpg_ref, wpp_ref, nfin_ref, y_ref):
    nrow = x_ref.shape[0]
    ngroup = max(1, nrow // POST_GROUP_ROWS)
    groups = [slice(r0, r0 + nrow // ngroup) for r0 in range(0, nrow, nrow // ngroup)]
    x = [x_ref[g, :] for g in groups]
    ma = [_dot(oa_ref[g, :].astype(BF16), wba_ref[...]) for g in groups]
    mb = [_dot(ob_ref[g, :].astype(BF16), wbb_ref[...]) for g in groups]
    u = [_rms(xi, nmix_ref[...]).astype(BF16) for xi in x]
    gates = [_sigmoid(_dot(ui, wg_ref[...])) for ui in u]
    mix = [gi[:, :D_MODEL] * mai + gi[:, D_MODEL:] * mbi for gi, mai, mbi in zip(gates, ma, mb)]
    h = [xi + _dot(mi.astype(BF16), wout_ref[...]) for xi, mi in zip(x, mix)]
    pp = [_dot(pe_ref[g, :].astype(BF16), wpp_ref[...]) for g in groups]
    u2 = [_rms(hi, nffn_ref[...]).astype(BF16) for hi in h]
    fg = [_silu(_dot(ui, wfg_ref[...])) for ui in u2]
    ff = [fgi * _dot(ui, wfu_ref[...]) for fgi, ui in zip(fg, u2)]
    h = [hi + _dot(fi.astype(BF16), wfd_ref[...]) for hi, fi in zip(h, ff)]
    u3 = [_rms(hi, nple_ref[...]).astype(BF16) for hi in h]
    h = [hi + _sigmoid(_dot(ui, wpg_ref[...])) * ppi for hi, ui, ppi in zip(h, u3, pp)]
    for g, hi in zip(groups, h):
        y_ref[g, :] = _rms(hi, nfin_ref[...])


def _post(x, oa, ob, pe, consts):
    n = x.shape[0]
    tm = min(ROW_TILE, n)
    row = lambda w: pl.BlockSpec((tm, w), lambda i: (i, 0))
    return pl.pallas_call(
        _post_kernel,
        grid=(n // tm,),
        in_specs=[row(D_MODEL), row(C_A), row(C_BV), row(PLE_DIM)] + [_const_spec(c.shape) for c in consts],
        out_specs=row(D_MODEL),
        out_shape=jax.ShapeDtypeStruct((n, D_MODEL), F32),
        compiler_params=pltpu.CompilerParams(dimension_semantics=("parallel",), vmem_limit_bytes=VMEM_LIMIT),
        name="post",
    )(x, oa, ob, pe, *consts)


def _chunk_cumsum_matrix(tile):
    i = jnp.arange(tile)
    same = (i[:, None] // CHUNK) == (i[None, :] // CHUNK)
    return (same & (i[None, :] <= i[:, None])).astype(BF16)


def kernel(x_prompt, x_sample, p_prompt, p_sample, state_shift, state_wkv, state_conv, state_gdn, norm_mix, w_in, mu_shift, rw_w0, rw_w2, rw_a0, rw_a2, rw_g2, rw_kk, rw_ka, rw_rk, rw_ln_w, rw_ln_b, gdn_conv, gdn_a_log, gdn_dt_bias, gdn_norm, w_branch_a, w_branch_b, w_out, norm_ffn, w_ffn_gate, w_ffn_up, w_ffn_down, norm_ple, w_ple_gate, w_ple_proj, norm_final):
    bsz, seq, _ = x_prompt.shape
    nd = x_sample.shape[0]
    row = lambda p: p.reshape(1, -1)

    w_in0 = w_in[0]
    b0 = A_COLS
    w_mix = w_in0[:, :b0 + CONV_CH + C_BV + LANES].astype(BF16)
    wg = w_in0[:, b0 + CONV_CH + C_BV + 2 * H_B:].astype(BF16)
    w2p = jnp.concatenate([rw_w2[0], jnp.zeros((LORA_A, C_A), F32)], axis=0).astype(BF16)
    a2p = jnp.concatenate([jnp.zeros((LORA_W, C_A), F32), rw_a2[0]], axis=0).astype(BF16)
    ch = jnp.arange(LANES) // HEAD_A
    hb = (ch[:, None] == ch[None, :]).astype(BF16)
    tri = _chunk_cumsum_matrix(T_TILE)
    alog = jnp.pad(gdn_a_log[0], (H_B, LANES - 2 * H_B)).reshape(1, LANES)
    dtb = jnp.pad(gdn_dt_bias[0], (H_B, LANES - 2 * H_B)).reshape(1, LANES)
    rw_consts = (row(mu_shift[0]), row(rw_w0[0]), w2p, row(rw_a0[0]), a2p, rw_g2[0].astype(BF16), row(rw_kk[0]), row(rw_ka[0]),
                 row(rw_rk[0]), row(rw_ln_w[0]), row(rw_ln_b[0]), hb)
    gdn_consts = (gdn_conv[0], alog, dtb, row(gdn_norm[0]))
    post_consts = (row(norm_mix[0]), wg, w_branch_a[0].astype(BF16), w_branch_b[0].astype(BF16),
                   w_out[0].astype(BF16), row(norm_ffn[0]), w_ffn_gate[0].astype(BF16), w_ffn_up[0].astype(BF16),
                   w_ffn_down[0].astype(BF16), row(norm_ple[0]), w_ple_gate[0].astype(BF16),
                   w_ple_proj[0].astype(BF16), row(norm_final))

    xp = x_prompt.reshape(bsz * seq, D_MODEL)
    oa, wkv_pairs, shift_tail, ob, gdn_p, conv_tail = _mixers_prompt(
        x_prompt, row(norm_mix[0]), w_mix, rw_consts, gdn_consts, tri)
    y_prompt = _post(xp, oa.reshape(bsz * seq, C_A), ob.reshape(bsz * seq, C_BV),
                     p_prompt[0].reshape(bsz * seq, PLE_DIM), post_consts).reshape(bsz, seq, D_MODEL)
    wkv_p = jnp.stack([wkv_pairs[:, :, :HEAD_A, :HEAD_A], wkv_pairs[:, :, HEAD_A:, HEAD_A:]], axis=2)
    wkv_p = wkv_p.reshape(bsz, H_A, HEAD_A, HEAD_A)
    shift_p = shift_tail[:, HALO - 1:, :]
    conv_p = conv_tail[:, HALO - (CONV_W - 1):, :]

    xs = x_sample.reshape(nd, D_MODEL)
    pa_s, qkv_s, z_s, ba_s = _inproj(xs, row(norm_mix[0]), w_mix)
    cst = state_conv[0].reshape(nd, (CONV_W - 1) * CONV_CH)
    oa_s, wkv_t = _decode_rwkv(pa_s, state_shift[0].reshape(nd, A_COLS), jnp.transpose(state_wkv[0], (1, 2, 3, 0)),
                               rw_consts)
    wkv_s = jnp.transpose(wkv_t, (3, 0, 1, 2))
    ob_s, gdn_s = _decode_gdn(qkv_s, z_s, ba_s, cst, state_gdn[0], gdn_consts)
    y_sample = _post(xs, oa_s, ob_s, p_sample[0].reshape(nd, PLE_DIM), post_consts).reshape(nd, 1, D_MODEL)
    conv_s = jnp.concatenate([cst[:, CONV_CH:], qkv_s], axis=1).reshape(nd, CONV_W - 1, CONV_CH)

    return (y_prompt, y_sample, shift_p[None], wkv_p[None], conv_p[None], gdn_p[None],
            pa_s.reshape(1, nd, 1, A_COLS), wkv_s[None], conv_s[None], gdn_s[None])
```

```python
import functools

import jax
import jax.numpy as jnp
from jax import lax
from jax.experimental import pallas as pl
from jax.experimental.pallas import tpu as pltpu

F32 = jnp.float32
BF16 = jnp.bfloat16

D_MODEL = 1024
HEAD_A = 64
C_A = 512
H_A = 8
LORA_W = 64
LORA_A = 64
LORA_G = 128
A_COLS = 3 * C_A + LORA_W + LORA_A + LORA_G
DK = 128
DV = 128
H_B = 4
C_BK = 512
C_BV = 512
CONV_W = 4
CONV_CH = 2 * C_BK + C_BV
D_FF = 2816
PLE_DIM = 256
NORM_EPS = 1e-6
GN_EPS = 64e-5
L2_EPS = 1e-6

LANES = 128
CHUNK = 64
PAIR = 2 * CHUNK
T_TILE = 256
PROJ_COLS = 256
ROW_TILE = 512
POST_GROUP_ROWS = 256
DEC_TILE = 32
HALO = 8
V7X_VMEM_BYTES = 64 * 1024 * 1024
VMEM_LIMIT = V7X_VMEM_BYTES * 7 // 8
SUM_TERMS = 1
CUMSUM_TERMS = 2
INV_TERMS = 1
CHAIN_TERMS = 1


def _dot(a, b):
    return jnp.dot(a, b, preferred_element_type=F32)


def _sigmoid(x):
    return 1.0 / (1.0 + jnp.exp(-x))


def _silu(x):
    return x * _sigmoid(x)


def _softplus(x):
    return jnp.maximum(x, 0.0) + jnp.log(1.0 + jnp.exp(-jnp.abs(x)))


def _rms(x, gain):
    return x * lax.rsqrt(jnp.mean(x * x, axis=-1, keepdims=True) + NORM_EPS) * gain


def _pair_masks():
    ri = lax.broadcasted_iota(jnp.int32, (PAIR, PAIR), 0)
    ci = lax.broadcasted_iota(jnp.int32, (PAIR, PAIR), 1)
    same = (ri < CHUNK) == (ci < CHUNK)
    strict = same & (ci < ri)
    incl = same & (ci <= ri)
    eye = (ri == ci).astype(F32)
    return strict, incl, eye


_NN = (((1,), (0,)), ((), ()))
_NT = (((1,), (1,)), ((), ()))
_TN = (((0,), (0,)), ((), ()))


def _split(x, terms):
    if isinstance(x, (list, tuple)):
        return list(x)
    if x.dtype == BF16:
        return [x]
    parts = []
    for i in range(terms):
        h = x.astype(BF16)
        parts.append(h)
        if i + 1 < terms:
            x = x - h.astype(F32)
    return parts


def _mm(a, b, dims=_NN, na=1, nb=1):
    pa, pb = _split(a, na), _split(b, nb)
    acc = None
    for i, ai in enumerate(pa):
        for j, bj in enumerate(pb):
            if i + j < max(len(pa), len(pb)):
                d = lax.dot_general(ai, bj, dims, preferred_element_type=F32)
                acc = d if acc is None else acc + d
    return acc


def _neumann_inverse_many(lmats, eye, terms):
    ts = [eye + l for l in lmats]
    ps = [_split(l, terms) for l in lmats]
    n = 2
    while n < CHUNK:
        ps = [_split(_mm(p, p), terms) for p in ps]
        ts = [t + _mm(t, p, _NN, terms) for t, p in zip(ts, ps)]
        n *= 2
    return ts


def _mix_segments(pa_ref, qkv_ref, z_ref, ba_ref):
    return ((0, pa_ref), (A_COLS, qkv_ref), (A_COLS + CONV_CH, z_ref), (A_COLS + CONV_CH + C_BV, ba_ref))


def _inproj_kernel(x_ref, g_ref, w_ref, pa_ref, qkv_ref, z_ref, ba_ref):
    u = _rms(x_ref[...], g_ref[...]).astype(BF16)
    for c0, dst_ref in _mix_segments(pa_ref, qkv_ref, z_ref, ba_ref):
        dst_ref[...] = _dot(u, w_ref[:, c0:c0 + dst_ref.shape[1]])


def _const_spec(shape):
    nd = len(shape)
    return pl.BlockSpec(shape, lambda *_: (0,) * nd, pipeline_mode=pl.Buffered(1))


def _inproj(x, gain, w_mix):
    n = x.shape[0]
    tm = min(ROW_TILE, n)
    row = lambda w: pl.BlockSpec((tm, w), lambda i: (i, 0))
    return pl.pallas_call(
        _inproj_kernel,
        grid=(n // tm,),
        in_specs=[row(D_MODEL), _const_spec(gain.shape), _const_spec(w_mix.shape)],
        out_specs=[row(A_COLS), row(CONV_CH), row(C_BV), row(LANES)],
        out_shape=[jax.ShapeDtypeStruct((n, A_COLS), F32), jax.ShapeDtypeStruct((n, CONV_CH), F32),
                   jax.ShapeDtypeStruct((n, C_BV), F32), jax.ShapeDtypeStruct((n, LANES), F32)],
        compiler_params=pltpu.CompilerParams(dimension_semantics=("parallel",), vmem_limit_bytes=VMEM_LIMIT),
        name="inproj",
    )(x, gain, w_mix)


def _rwkv_prep(pa, prev, mu, w0, w2p, a0, a2p, g2, kkw, ka, rk, hb, between=lambda: None):
    xa = pa + (prev - pa) * mu
    r = xa[:, :C_A]
    k = xa[:, C_A:2 * C_A]
    v = xa[:, 2 * C_A:3 * C_A]
    xwa = xa[:, 3 * C_A:3 * C_A + LORA_W + LORA_A]
    xg = xa[:, 3 * C_A + LORA_W + LORA_A:]
    w_pre = _mm(jnp.tanh(xwa), w2p)
    a_pre = _mm(xwa, a2p)
    g = _mm(_sigmoid(xg), g2)
    kx = k * kkw
    kk_den = _head_sum(kx * kx, hb)
    between()
    w_log = -_softplus(-(w0 + w_pre)) - 0.5
    logw = -jnp.exp(w_log)
    a = _sigmoid(a0 + a_pre)
    kk = kx * lax.rsqrt(kk_den + L2_EPS)
    k2 = k * (1.0 + (a - 1.0) * ka)
    bonus_sum = _head_sum(r * k2 * rk, hb)
    between()
    bonus = bonus_sum * v
    return r, k2, v, logw, a, g, kk, bonus


def _head_sum(x, hb):
    groups = [_mm(x[:, LANES * j:LANES * (j + 1)], hb, _NN, SUM_TERMS) for j in range(x.shape[1] // LANES)]
    return jnp.concatenate(groups, axis=1)


def _group_norm_gate(y, bonus, g, lnw, lnb, hb):
    mean = _head_sum(y, hb) * (1.0 / HEAD_A)
    d = y - mean
    var = _head_sum(d * d, hb) * (1.0 / HEAD_A)
    yn = d * lax.rsqrt(var + GN_EPS) * lnw + lnb
    return (yn + bonus) * g


def _gdn_prep(x0, x1, x2, x3, ba, conv, alog, dtb):
    c = _silu(x0 * conv[0:1] + x1 * conv[1:2] + x2 * conv[2:3] + x3 * conv[3:4])
    qs, ks = [], []
    for h in range(H_B):
        qh = c[:, DK * h:DK * (h + 1)]
        kh = c[:, C_BK + DK * h:C_BK + DK * (h + 1)]
        qs.append(qh * lax.rsqrt(jnp.sum(qh * qh, axis=-1, keepdims=True) + L2_EPS) * (DK ** -0.5))
        ks.append(kh * lax.rsqrt(jnp.sum(kh * kh, axis=-1, keepdims=True) + L2_EPS))
    q = jnp.concatenate(qs, axis=1)
    k = jnp.concatenate(ks, axis=1)
    v = c[:, 2 * C_BK:]
    beta = _sigmoid(ba)
    glog = -jnp.exp(alog) * _softplus(ba + dtb)
    return q, k, v, beta, glog


def _head_norm_gate(o, z, gnorm):
    outs = []
    for h in range(H_B):
        oh = o[:, DV * h:DV * (h + 1)]
        zh = z[:, DV * h:DV * (h + 1)]
        oh = oh * lax.rsqrt(jnp.mean(oh * oh, axis=-1, keepdims=True) + NORM_EPS) * gnorm
        outs.append(oh * _silu(zh))
    return jnp.concatenate(outs, axis=1)


def _tile_maps(bsz, seq, tile):
    tps = seq // tile
    last = bsz * tps - 1
    cur = lambda s: jnp.minimum(s, last)
    prv = lambda s: jnp.maximum(s - 1, 0)
    load = lambda s: (cur(s) // tps, cur(s) % tps, 0)
    emit = lambda s: (prv(s) // tps, prv(s) % tps, 0)
    per_seq3 = lambda s: (prv(s) // tps, 0, 0)
    per_seq4 = lambda s: (prv(s) // tps, 0, 0, 0)
    return tps, last + 2, load, emit, per_seq3, per_seq4


def _gdn_prep_groups(x0, x1, x2, x3, conv):
    def conv_cols(c0, c1):
        return _silu(x0[:, c0:c1] * conv[0:1, c0:c1] + x1[:, c0:c1] * conv[1:2, c0:c1]
                     + x2[:, c0:c1] * conv[2:3, c0:c1] + x3[:, c0:c1] * conv[3:4, c0:c1])

    def unit(xh):
        return xh * lax.rsqrt(jnp.sum(xh * xh, axis=-1, keepdims=True) + L2_EPS)

    for h in range(H_B):
        yield 0, h, unit(conv_cols(DK * h, DK * (h + 1))) * (DK ** -0.5)
        yield 1, h, unit(conv_cols(C_BK + DK * h, C_BK + DK * (h + 1)))
        yield 2, h, conv_cols(2 * C_BK + DV * h, 2 * C_BK + DV * (h + 1))


def _mixers_prompt_kernel(tiles_per_seq, x_ref, nmix_ref, wmix_ref,
                          mu_ref, w0_ref, w2_ref, a0_ref, a2_ref, g2_ref, kkw_ref, ka_ref, rk_ref,
                          lnw_ref, lnb_ref, hb_ref, conv_ref, alog_ref, dtb_ref, gnorm_ref, tri_ref,
                          oa_ref, wkv_ref, shift_ref, ob_ref, gdn_ref, convo_ref,
                          pa_ref, exta_ref, sa_ref, at_ref, bt_ref, kt_ref, rt_ref, va_ref, bh_ref, kh_ref, y_ref,
                          qkvn_ref, zn_ref, ban_ref, extb_ref, sb_ref, q_ref, k_ref, vb_ref, gc_ref, be_ref, o_ref):
    s_id = pl.program_id(0)

    @pl.when(s_id == 0)
    def _():
        pa_ref[...] = jnp.zeros_like(pa_ref)
        qkvn_ref[...] = jnp.zeros_like(qkvn_ref)
        zn_ref[...] = jnp.zeros_like(zn_ref)
        ban_ref[...] = jnp.zeros_like(ban_ref)

    @pl.when(jnp.maximum(s_id - 1, 0) % tiles_per_seq == 0)
    def _():
        exta_ref[0:HALO, :] = jnp.zeros((HALO, A_COLS), F32)
        extb_ref[0:HALO, :] = jnp.zeros((HALO, CONV_CH), F32)
        sa_ref[...] = jnp.zeros_like(sa_ref)
        sb_ref[...] = jnp.zeros_like(sb_ref)

    pa = pa_ref[...]
    exta_ref[HALO:, :] = pa
    prev = exta_ref[pl.ds(HALO - 1, T_TILE), :]
    exta_ref[0:HALO, :] = pa[T_TILE - HALO:, :]
    shift_ref[0] = pa[T_TILE - HALO:, :]
    x3 = qkvn_ref[...]
    z = zn_ref[...]
    ba = ban_ref[...]
    extb_ref[HALO:, :] = x3
    x0 = extb_ref[pl.ds(HALO - 3, T_TILE), :]
    x1 = extb_ref[pl.ds(HALO - 2, T_TILE), :]
    x2 = extb_ref[pl.ds(HALO - 1, T_TILE), :]
    extb_ref[0:HALO, :] = x3[T_TILE - HALO:, :]
    convo_ref[0] = x3[T_TILE - HALO:, :]

    u_next = _rms(x_ref[0], nmix_ref[...]).astype(BF16)
    blocks = [(w0, dst_ref, c0, min(c0 + PROJ_COLS, dst_ref.shape[1]))
              for w0, dst_ref in _mix_segments(pa_ref, qkvn_ref, zn_ref, ban_ref)
              for c0 in range(0, dst_ref.shape[1], PROJ_COLS)]

    def project(count):
        for w0, dst_ref, c0, c1 in blocks[:count]:
            dst_ref[:, c0:c1] = _dot(u_next, wmix_ref[:, w0 + c0:w0 + c1])
        del blocks[:count]

    gdn_groups = _gdn_prep_groups(x0, x1, x2, x3, conv_ref[...])
    gdn_dst = (q_ref, k_ref, vb_ref)

    def gdn_prep(count):
        for _ in range(count):
            item = next(gdn_groups, None)
            if item is not None:
                kind, h, arr = item
                gdn_dst[kind][:, DK * h:DK * (h + 1)] = arr

    def between():
        project(3)
        gdn_prep(3)

    hb = hb_ref[...]
    r, k2, v, logw, a, g, kk, bonus = _rwkv_prep(
        pa, prev, mu_ref[...], w0_ref[...], w2_ref[...], a0_ref[...], a2_ref[...], g2_ref[...],
        kkw_ref[...], ka_ref[...], rk_ref[...], hb, between)

    nchunk = T_TILE // CHUNK
    cs = _mm(tri_ref[...], logw, _NN, 1, CUMSUM_TERMS)
    between()
    ends = [cs[CHUNK * (c + 1) - 1:CHUNK * (c + 1)] for c in range(nchunk)]
    tot = jnp.concatenate([jnp.broadcast_to(e, (CHUNK, C_A)) for e in ends], axis=0)
    dinv = jnp.exp(-cs)
    dend = jnp.exp(tot - cs)
    b_in = kk * a
    at_ref[...] = -kk * jnp.exp(cs - logw)
    bt_ref[...] = b_in * dinv
    kt_ref[...] = k2 * dinv
    between()
    rt_ref[...] = r * jnp.exp(cs)
    va_ref[...] = v
    bh_ref[...] = b_in * dend
    kh_ref[...] = k2 * dend
    gl = [jnp.exp(e) for e in ends]
    project(len(blocks))
    gdn_prep(3 * H_B)
    be_ref[...] = _sigmoid(ba)
    glog = -jnp.exp(alog_ref[...]) * _softplus(ba + dtb_ref[...])
    gc_ref[...] = _mm(tri_ref[...], glog, _NN, 1, CUMSUM_TERMS)

    strict, incl, eye = _pair_masks()
    low = lax.broadcasted_iota(jnp.int32, (CHUNK, LANES), 1) < HEAD_A

    def stack(x):
        return jnp.concatenate([jnp.where(low, x, 0.0), jnp.where(low, 0.0, x)], axis=0)

    npair = C_A // LANES
    lanes = lambda p: slice(LANES * p, LANES * (p + 1))
    rows = lambda c: slice(CHUNK * c, CHUNK * (c + 1))
    half = lambda j: slice(CHUNK * j, CHUNK * (j + 1))
    ia = [(p, c) for p in range(npair) for c in range(nchunk)]
    ib = [(pr, c) for pr in range(H_B // 2) for c in range(nchunk)]
    na, nb = range(len(ia)), range(len(ib))
    ld = lambda ref: [stack(ref[rows(c), lanes(p)]) for p, c in ia]
    at, bt, kt, rt, vs, bh, kh = (ld(ref) for ref in (at_ref, bt_ref, kt_ref, rt_ref, va_ref, bh_ref, kh_ref))
    cat = lambda ref: [jnp.concatenate([ref[rows(c), DK * h:DK * (h + 1)] for h in (2 * pr, 2 * pr + 1)], axis=0)
                       for pr, c in ib]
    col = lambda ref, off: [jnp.concatenate([ref[rows(c), off + h:off + h + 1] for h in (2 * pr, 2 * pr + 1)], axis=0)
                            for pr, c in ib]
    qg_, kg_, vg_ = cat(q_ref), cat(k_ref), cat(vb_ref)
    beta_c = col(be_ref, 0)
    gc = col(gc_ref, H_B)
    gt = [jnp.concatenate([jnp.broadcast_to(g_[CHUNK * (i + 1) - 1:CHUNK * (i + 1)], (CHUNK, 1)) for i in range(2)],
                          axis=0) for g_ in gc]
    decay = []
    for g_ in gc:
        gc_full = jnp.broadcast_to(g_, (PAIR, PAIR))
        decay.append(jnp.exp(gc_full - gc_full.T))
    kb = [kg_[i] * beta_c[i] for i in nb]
    vbb = [vg_[i] * beta_c[i] for i in nb]
    eg = [jnp.exp(g_) for g_ in gc]

    aa = [_mm(jnp.concatenate([at[i], rt[i]], axis=0), jnp.concatenate([bt[i], kt[i]], axis=0), _NT) for i in na]
    kq = [_mm(jnp.concatenate([kb[i], qg_[i]], axis=0), kg_[i], _NT) for i in nb]
    a_ab = [jnp.where(strict, aa[i][:PAIR, :PAIR], 0.0) for i in na]
    a_ak = [jnp.where(strict, aa[i][:PAIR, PAIR:], 0.0) for i in na]
    a_rb = [jnp.where(incl, aa[i][PAIR:, :PAIR], 0.0) for i in na]
    a_rk = [jnp.where(incl, aa[i][PAIR:, PAIR:], 0.0) for i in na]
    lneg = [jnp.where(strict, -(kq[i][:PAIR] * decay[i]), 0.0) for i in nb]
    qk = [jnp.where(incl, kq[i][PAIR:] * decay[i], 0.0) for i in nb]
    tall = _neumann_inverse_many(a_ab + lneg, eye, INV_TERMS)
    tinv, tinvb = tall[:len(ia)], tall[len(ia):]
    akv = [_mm(a_ak[i], vs[i]) for i in na]
    uw = [_mm(tinvb[i], jnp.concatenate([vbb[i], kb[i] * eg[i]], axis=1)) for i in nb]
    gu = [_mm(tinv[i], jnp.concatenate([at[i], akv[i]], axis=1)) for i in na]
    ow = [_mm(qk[i], uw[i]) for i in nb]
    ry = [_mm(a_rb[i], gu[i]) for i in na]
    rkv = [_mm(a_rk[i], vs[i]) for i in na]
    kd = [kg_[i] * jnp.exp(gt[i] - gc[i]) for i in nb]
    pqb = [[_mm(kd[i][half(j)], uw[i][half(j)], _TN) for j in range(2)] for i in nb]
    pq = [_mm(gu[i], bh[i], _TN) for i in na]
    vk = [_mm(vs[i], kh[i], _TN) for i in na]
    rp = [rt[i] + ry[i][:, :PAIR] for i in na]
    yc = [ry[i][:, PAIR:] + rkv[i] for i in na]
    pm = [eye * gl[ia[i][1]][:, lanes(ia[i][0])] + pq[i][:PAIR] for i in na]
    qm = [pq[i][PAIR:] + vk[i] for i in na]
    oc = [ow[i][:, :DV] for i in nb]
    rq = [qg_[i] * eg[i] - ow[i][:, DV:] for i in nb]
    pmb = [[eye * jnp.exp(gt[i][CHUNK * j:CHUNK * j + 1]) - pqb[i][j][:, DV:] for j in range(2)] for i in nb]

    sa = [sa_ref[p] for p in range(npair)]
    sb = [sb_ref[h] for h in range(H_B)]
    for c in range(nchunk):
        for p in range(npair):
            i = ia.index((p, c))
            y = _mm(rp[i], sa[p], _NT) + yc[i]
            y_ref[rows(c), lanes(p)] = y[:CHUNK] + y[CHUNK:]
            sa[p] = _mm(sa[p], pm[i], _NN, CHAIN_TERMS, CHAIN_TERMS) + qm[i]
        for h in range(H_B):
            i, j = ib.index((h // 2, c)), h % 2
            o_ref[rows(c), DV * h:DV * (h + 1)] = _mm(rq[i][half(j)], sb[h]) + oc[i][half(j)]
            sb[h] = _mm(pmb[i][j], sb[h], _NN, CHAIN_TERMS, CHAIN_TERMS) + pqb[i][j][:, :DV]
    for p in range(npair):
        sa_ref[p] = sa[p]
    for h in range(H_B):
        sb_ref[h] = sb[h]

    oa_ref[0] = _group_norm_gate(y_ref[...], bonus, g, lnw_ref[...], lnb_ref[...], hb)
    wkv_ref[0] = sa_ref[...]
    ob_ref[0] = _head_norm_gate(o_ref[...], z, gnorm_ref[...])
    gdn_ref[0] = sb_ref[...]


def _mixers_prompt(x, nmix, w_mix, rw_consts, gdn_consts, tri):
    bsz, seq, _ = x.shape
    consts = (nmix, w_mix) + tuple(rw_consts) + tuple(gdn_consts) + (tri,)
    npair = C_A // LANES
    tps, steps, load, emit, per_seq3, per_seq4 = _tile_maps(bsz, seq, T_TILE)
    tile = lambda w: pltpu.VMEM((T_TILE, w), F32)
    return pl.pallas_call(
        functools.partial(_mixers_prompt_kernel, tps),
        grid=(steps,),
        in_specs=[pl.BlockSpec((1, T_TILE, D_MODEL), load)] + [_const_spec(c.shape) for c in consts],
        out_specs=[pl.BlockSpec((1, T_TILE, C_A), emit),
                   pl.BlockSpec((1, npair, LANES, LANES), per_seq4),
                   pl.BlockSpec((1, HALO, A_COLS), per_seq3),
                   pl.BlockSpec((1, T_TILE, C_BV), emit),
                   pl.BlockSpec((1, H_B, DK, DV), per_seq4),
                   pl.BlockSpec((1, HALO, CONV_CH), per_seq3)],
        out_shape=[jax.ShapeDtypeStruct((bsz, seq, C_A), F32),
                   jax.ShapeDtypeStruct((bsz, npair, LANES, LANES), F32),
                   jax.ShapeDtypeStruct((bsz, HALO, A_COLS), F32),
                   jax.ShapeDtypeStruct((bsz, seq, C_BV), F32),
                   jax.ShapeDtypeStruct((bsz, H_B, DK, DV), F32),
                   jax.ShapeDtypeStruct((bsz, HALO, CONV_CH), F32)],
        scratch_shapes=[tile(A_COLS), pltpu.VMEM((T_TILE + HALO, A_COLS), F32), pltpu.VMEM((npair, LANES, LANES), F32)]
                       + [tile(C_A) for _ in range(8)]
                       + [tile(CONV_CH), tile(C_BV), tile(LANES), pltpu.VMEM((T_TILE + HALO, CONV_CH), F32),
                          pltpu.VMEM((H_B, DK, DV), F32), tile(C_BK), tile(C_BK), tile(C_BV), tile(LANES), tile(LANES),
                          tile(C_BV)],
        compiler_params=pltpu.CompilerParams(dimension_semantics=("arbitrary",), vmem_limit_bytes=VMEM_LIMIT),
        name="mixers_prompt",
    )(x, *consts)


def _to_columns(x):
    pad = jnp.zeros((LANES - DEC_TILE, x.shape[1]), F32)
    return jnp.concatenate([x, pad], axis=0).T


def _decode_rwkv_kernel(pa_ref, shift_ref, wkv_ref,
                        mu_ref, w0_ref, w2_ref, a0_ref, a2_ref, g2_ref, kkw_ref, ka_ref, rk_ref, lnw_ref, lnb_ref,
                        hb_ref, oa_ref, wkvo_ref, tr_ref, yt_ref, g_ref, bonus_ref):
    h = pl.program_id(0)

    @pl.when(h == 0)
    def _():
        r, k2, v, logw, a, g, kk, bonus = _rwkv_prep(
            pa_ref[...], shift_ref[...], mu_ref[...], w0_ref[...], w2_ref[...], a0_ref[...], a2_ref[...],
            g2_ref[...], kkw_ref[...], ka_ref[...], rk_ref[...], hb_ref[...])
        for i, x in enumerate((-kk, jnp.exp(logw), kk * a, k2, r, v)):
            tr_ref[i] = x.T
        g_ref[...] = g
        bonus_ref[...] = bonus

    base = pl.multiple_of(h * HEAD_A, HEAD_A)
    hs = pl.ds(base, HEAD_A)
    a_t, w_t, b_t, k_t, r_t = (tr_ref[i, hs, :] for i in range(5))

    def value_row(vi, carry):
        st = wkv_ref[0, vi]
        sa = jnp.sum(st * a_t, axis=0, keepdims=True)
        st = st * w_t + sa * b_t + tr_ref[5, pl.ds(base + vi, 1), :] * k_t
        wkvo_ref[0, vi] = st
        yt_ref[pl.ds(base + vi, 1), :] = jnp.sum(st * r_t, axis=0, keepdims=True)
        return carry

    lax.fori_loop(0, HEAD_A, value_row, 0, unroll=8)

    @pl.when(h == H_A - 1)
    def _():
        oa_ref[...] = _group_norm_gate(yt_ref[...].T, bonus_ref[...], g_ref[...], lnw_ref[...], lnb_ref[...],
                                       hb_ref[...])


def _decode_rwkv(pa, shift, wkv_t, rw_consts):
    n = pa.shape[0]
    full = lambda w: pl.BlockSpec((n, w), lambda h: (0, 0))
    state = pl.BlockSpec((1, HEAD_A, HEAD_A, n), lambda h: (h, 0, 0, 0))
    return pl.pallas_call(
        _decode_rwkv_kernel,
        grid=(H_A,),
        in_specs=[full(A_COLS), full(A_COLS), state] + [_const_spec(c.shape) for c in rw_consts],
        out_specs=[full(C_A), state],
        out_shape=[jax.ShapeDtypeStruct((n, C_A), F32), jax.ShapeDtypeStruct(wkv_t.shape, F32)],
        scratch_shapes=[pltpu.VMEM((6, C_A, n), F32), pltpu.VMEM((C_A, n), F32),
                        pltpu.VMEM((n, C_A), F32), pltpu.VMEM((n, C_A), F32)],
        compiler_params=pltpu.CompilerParams(dimension_semantics=("arbitrary",), vmem_limit_bytes=VMEM_LIMIT),
        name="decode_rwkv",
    )(pa, shift, wkv_t, *rw_consts)


def _decode_gdn_kernel(qkv_ref, z_ref, ba_ref, cst_ref, gdn_ref, conv_ref, alog_ref, dtb_ref, gnorm_ref,
                       ob_ref, gdno_ref):
    cst = cst_ref[...]
    q, k, vv, beta, glog = _gdn_prep(cst[:, :CONV_CH], cst[:, CONV_CH:2 * CONV_CH], cst[:, 2 * CONV_CH:],
                                     qkv_ref[...], ba_ref[...], conv_ref[...], alog_ref[...], dtb_ref[...])
    eg = jnp.exp(glog)
    k_cols = _to_columns(k)
    ls = lambda h: slice(DK * h, DK * (h + 1))

    def split2(x):
        hi = x.astype(BF16)
        return hi, x - hi.astype(F32)

    rows_a, rows_b = [], []
    for h in range(H_B):
        e_h = eg[:, H_B + h:H_B + h + 1]
        (wh, wl), (gh, gl_) = split2(k[:, ls(h)] * (beta[:, h:h + 1] * e_h)), split2(q[:, ls(h)] * e_h)
        rows_a.append(jnp.concatenate([wh.astype(F32), gh.astype(F32), wl, gl_], axis=0).astype(BF16))
        rows_b.append(jnp.concatenate([wh.astype(F32), gh.astype(F32)], axis=0).astype(BF16))
    gunits = [(s_i, h) for s_i in range(DEC_TILE) for h in range(H_B)]
    gn = range(len(gunits))
    be = [beta[s_i:s_i + 1, h:h + 1] for s_i, h in gunits]
    e = [eg[s_i:s_i + 1, H_B + h:H_B + h + 1] for s_i, h in gunits]
    kc = [k_cols[ls(h), s_i:s_i + 1] for s_i, h in gunits]
    gst = [gdn_ref[s_i, h] for s_i, h in gunits]
    parts = [split2(st) for st in gst]
    ra = [_dot(rows_a[h], parts[i][0].astype(BF16)) + jnp.concatenate(
        [_dot(rows_b[h], parts[i][1].astype(BF16)), jnp.zeros((2 * DEC_TILE, DV), F32)], axis=0)
        for i, (s_i, h) in enumerate(gunits)]
    pick = lambda r, j, s_i: r[DEC_TILE * j + s_i:DEC_TILE * j + s_i + 1]
    ws = [pick(ra[i], 0, s_i) + pick(ra[i], 2, s_i) for i, (s_i, h) in enumerate(gunits)]
    qs = [pick(ra[i], 1, s_i) + pick(ra[i], 3, s_i) for i, (s_i, h) in enumerate(gunits)]
    v_new = [be[i] * vv[s_i:s_i + 1, ls(h)] - ws[i] for i, (s_i, h) in enumerate(gunits)]
    qk = [jnp.sum(q[s_i:s_i + 1, ls(h)] * k[s_i:s_i + 1, ls(h)], axis=-1, keepdims=True) for s_i, h in gunits]
    for i, (s_i, h) in enumerate(gunits):
        gdno_ref[s_i, h] = gst[i] * e[i] + kc[i] * v_new[i]
    o_units = [qs[i] + qk[i] * v_new[i] for i in gn]
    o = jnp.concatenate([jnp.concatenate(o_units[H_B * s_i:H_B * (s_i + 1)], axis=1) for s_i in range(DEC_TILE)],
                        axis=0)
    ob_ref[...] = _head_norm_gate(o, z_ref[...], gnorm_ref[...])


def _decode_gdn(qkv, z, ba, cst, gdn, gdn_consts):
    n = qkv.shape[0]
    row = lambda w: pl.BlockSpec((DEC_TILE, w), lambda i: (i, 0))
    state = pl.BlockSpec((DEC_TILE, H_B, DK, DV), lambda i: (i, 0, 0, 0))
    return pl.pallas_call(
        _decode_gdn_kernel,
        grid=(n // DEC_TILE,),
        in_specs=[row(CONV_CH), row(C_BV), row(LANES), row(3 * CONV_CH), state]
                 + [_const_spec(c.shape) for c in gdn_consts],
        out_specs=[row(C_BV), state],
        out_shape=[jax.ShapeDtypeStruct((n, C_BV), F32), jax.ShapeDtypeStruct(gdn.shape, F32)],
        compiler_params=pltpu.CompilerParams(dimension_semantics=("parallel",), vmem_limit_bytes=VMEM_LIMIT),
        name="decode_gdn",
    )(qkv, z, ba, cst, gdn, *gdn_consts)


def _post_kernel(x_ref, oa_ref, ob_ref, pe_ref, nmix_ref, wg_ref, wba_ref, wbb_ref, wout_ref, nffn_ref,
                 wfg_ref, wfu_ref, wfd_ref, nple_ref, wpg_ref, wpp_ref, nfin_ref, y_ref):
    nrow = x_ref.shape[0]
    ngroup = max(1, nrow // POST_GROUP_ROWS)
    groups = [slice(r0, r0 + nrow // ngroup) for r0 in range(0, nrow, nrow // ngroup)]
    x = [x_ref[g, :] for g in groups]
    ma = [_dot(oa_ref[g, :].astype(BF16), wba_ref[...]) for g in groups]
    mb = [_dot(ob_ref[g, :].astype(BF16), wbb_ref[...]) for g in groups]
    u = [_rms(xi, nmix_ref[...]).astype(BF16) for xi in x]
    gates = [_sigmoid(_dot(ui, wg_ref[...])) for ui in u]
    mix = [gi[:, :D_MODEL] * mai + gi[:, D_MODEL:] * mbi for gi, mai, mbi in zip(gates, ma, mb)]
    h = [xi + _dot(mi.astype(BF16), wout_ref[...]) for xi, mi in zip(x, mix)]
    pp = [_dot(pe_ref[g, :].astype(BF16), wpp_ref[...]) for g in groups]
    u2 = [_rms(hi, nffn_ref[...]).astype(BF16) for hi in h]
    fg = [_silu(_dot(ui, wfg_ref[...])) for ui in u2]
    ff = [fgi * _dot(ui, wfu_ref[...]) for fgi, ui in zip(fg, u2)]
    h = [hi + _dot(fi.astype(BF16), wfd_ref[...]) for hi, fi in zip(h, ff)]
    u3 = [_rms(hi, nple_ref[...]).astype(BF16) for hi in h]
    h = [hi + _sigmoid(_dot(ui, wpg_ref[...])) * ppi for hi, ui, ppi in zip(h, u3, pp)]
    for g, hi in zip(groups, h):
        y_ref[g, :] = _rms(hi, nfin_ref[...])


def _post(x, oa, ob, pe, consts):
    n = x.shape[0]
    tm = min(ROW_TILE, n)
    row = lambda w: pl.BlockSpec((tm, w), lambda i: (i, 0))
    return pl.pallas_call(
        _post_kernel,
        grid=(n // tm,),
        in_specs=[row(D_MODEL), row(C_A), row(C_BV), row(PLE_DIM)] + [_const_spec(c.shape) for c in consts],
        out_specs=row(D_MODEL),
        out_shape=jax.ShapeDtypeStruct((n, D_MODEL), F32),
        compiler_params=pltpu.CompilerParams(dimension_semantics=("parallel",), vmem_limit_bytes=VMEM_LIMIT),
        name="post",
    )(x, oa, ob, pe, *consts)


def _chunk_cumsum_matrix(tile):
    i = jnp.arange(tile)
    same = (i[:, None] // CHUNK) == (i[None, :] // CHUNK)
    return (same & (i[None, :] <= i[:, None])).astype(BF16)


def kernel(x_prompt, x_sample, p_prompt, p_sample, state_shift, state_wkv, state_conv, state_gdn, norm_mix, w_in, mu_shift, rw_w0, rw_w2, rw_a0, rw_a2, rw_g2, rw_kk, rw_ka, rw_rk, rw_ln_w, rw_ln_b, gdn_conv, gdn_a_log, gdn_dt_bias, gdn_norm, w_branch_a, w_branch_b, w_out, norm_ffn, w_ffn_gate, w_ffn_up, w_ffn_down, norm_ple, w_ple_gate, w_ple_proj, norm_final):
    bsz, seq, _ = x_prompt.shape
    nd = x_sample.shape[0]
    row = lambda p: p.reshape(1, -1)

    w_in0 = w_in[0]
    b0 = A_COLS
    w_mix = w_in0[:, :b0 + CONV_CH + C_BV + LANES].astype(BF16)
    wg = w_in0[:, b0 + CONV_CH + C_BV + 2 * H_B:].astype(BF16)
    w2p = jnp.concatenate([rw_w2[0], jnp.zeros((LORA_A, C_A), F32)], axis=0).astype(BF16)
    a2p = jnp.concatenate([jnp.zeros((LORA_W, C_A), F32), rw_a2[0]], axis=0).astype(BF16)
    ch = jnp.arange(LANES) // HEAD_A
    hb = (ch[:, None] == ch[None, :]).astype(BF16)
    tri = _chunk_cumsum_matrix(T_TILE)
    alog = jnp.pad(gdn_a_log[0], (H_B, LANES - 2 * H_B)).reshape(1, LANES)
    dtb = jnp.pad(gdn_dt_bias[0], (H_B, LANES - 2 * H_B)).reshape(1, LANES)
    rw_consts = (row(mu_shift[0]), row(rw_w0[0]), w2p, row(rw_a0[0]), a2p, rw_g2[0].astype(BF16), row(rw_kk[0]), row(rw_ka[0]),
                 row(rw_rk[0]), row(rw_ln_w[0]), row(rw_ln_b[0]), hb)
    gdn_consts = (gdn_conv[0], alog, dtb, row(gdn_norm[0]))
    post_consts = (row(norm_mix[0]), wg, w_branch_a[0].astype(BF16), w_branch_b[0].astype(BF16),
                   w_out[0].astype(BF16), row(norm_ffn[0]), w_ffn_gate[0].astype(BF16), w_ffn_up[0].astype(BF16),
                   w_ffn_down[0].astype(BF16), row(norm_ple[0]), w_ple_gate[0].astype(BF16),
                   w_ple_proj[0].astype(BF16), row(norm_final))

    xp = x_prompt.reshape(bsz * seq, D_MODEL)
    oa, wkv_pairs, shift_tail, ob, gdn_p, conv_tail = _mixers_prompt(
        x_prompt, row(norm_mix[0]), w_mix, rw_consts, gdn_consts, tri)
    y_prompt = _post(xp, oa.reshape(bsz * seq, C_A), ob.reshape(bsz * seq, C_BV),
                     p_prompt[0].reshape(bsz * seq, PLE_DIM), post_consts).reshape(bsz, seq, D_MODEL)
    wkv_p = jnp.stack([wkv_pairs[:, :, :HEAD_A, :HEAD_A], wkv_pairs[:, :, HEAD_A:, HEAD_A:]], axis=2)
    wkv_p = wkv_p.reshape(bsz, H_A, HEAD_A, HEAD_A)
    shift_p = shift_tail[:, HALO - 1:, :]
    conv_p = conv_tail[:, HALO - (CONV_W - 1):, :]

    xs = x_sample.reshape(nd, D_MODEL)
    pa_s, qkv_s, z_s, ba_s = _inproj(xs, row(norm_mix[0]), w_mix)
    cst = state_conv[0].reshape(nd, (CONV_W - 1) * CONV_CH)
    oa_s, wkv_t = _decode_rwkv(pa_s, state_shift[0].reshape(nd, A_COLS), jnp.transpose(state_wkv[0], (1, 2, 3, 0)),
                               rw_consts)
    wkv_s = jnp.transpose(wkv_t, (3, 0, 1, 2))
    ob_s, gdn_s = _decode_gdn(qkv_s, z_s, ba_s, cst, state_gdn[0], gdn_consts)
    y_sample = _post(xs, oa_s, ob_s, p_sample[0].reshape(nd, PLE_DIM), post_consts).reshape(nd, 1, D_MODEL)
    conv_s = jnp.concatenate([cst[:, CONV_CH:], qkv_s], axis=1).reshape(nd, CONV_W - 1, CONV_CH)

    return (y_prompt, y_sample, shift_p[None], wkv_p[None], conv_p[None], gdn_p[None],
            pa_s.reshape(1, nd, 1, A_COLS), wkv_s[None], conv_s[None], gdn_s[None])
```

```python
import functools

import jax
import jax.numpy as jnp
from jax import lax
from jax.experimental import pallas as pl
from jax.experimental.pallas import tpu as pltpu

F32 = jnp.float32
BF16 = jnp.bfloat16

D_MODEL = 1024
HEAD_A = 64
C_A = 512
H_A = 8
LORA_W = 64
LORA_A = 64
LORA_G = 128
A_COLS = 3 * C_A + LORA_W + LORA_A + LORA_G
DK = 128
DV = 128
H_B = 4
C_BK = 512
C_BV = 512
CONV_W = 4
CONV_CH = 2 * C_BK + C_BV
D_FF = 2816
PLE_DIM = 256
NORM_EPS = 1e-6
GN_EPS = 64e-5
L2_EPS = 1e-6

LANES = 128
CHUNK = 64
PAIR = 2 * CHUNK
T_TILE = 256
PROJ_COLS = 256
ROW_TILE = 512
POST_GROUP_ROWS = 256
DEC_TILE = 32
HALO = 8
V7X_VMEM_BYTES = 64 * 1024 * 1024
VMEM_LIMIT = V7X_VMEM_BYTES * 7 // 8
SUM_TERMS = 1
CUMSUM_TERMS = 2
INV_TERMS = 1
CHAIN_TERMS = 1


def _dot(a, b):
    return jnp.dot(a, b, preferred_element_type=F32)


def _sigmoid(x):
    return 1.0 / (1.0 + jnp.exp(-x))


def _silu(x):
    return x * _sigmoid(x)


def _softplus(x):
    return jnp.maximum(x, 0.0) + jnp.log(1.0 + jnp.exp(-jnp.abs(x)))


def _rms(x, gain):
    return x * lax.rsqrt(jnp.mean(x * x, axis=-1, keepdims=True) + NORM_EPS) * gain


def _pair_masks():
    ri = lax.broadcasted_iota(jnp.int32, (PAIR, PAIR), 0)
    ci = lax.broadcasted_iota(jnp.int32, (PAIR, PAIR), 1)
    same = (ri < CHUNK) == (ci < CHUNK)
    strict = same & (ci < ri)
    incl = same & (ci <= ri)
    eye = (ri == ci).astype(F32)
    return strict, incl, eye


_NN = (((1,), (0,)), ((), ()))
_NT = (((1,), (1,)), ((), ()))
_TN = (((0,), (0,)), ((), ()))


def _split(x, terms):
    if isinstance(x, (list, tuple)):
        return list(x)
    if x.dtype == BF16:
        return [x]
    parts = []
    for i in range(terms):
        h = x.astype(BF16)
        parts.append(h)
        if i + 1 < terms:
            x = x - h.astype(F32)
    return parts


def _mm(a, b, dims=_NN, na=1, nb=1):
    pa, pb = _split(a, na), _split(b, nb)
    acc = None
    for i, ai in enumerate(pa):
        for j, bj in enumerate(pb):
            if i + j < max(len(pa), len(pb)):
                d = lax.dot_general(ai, bj, dims, preferred_element_type=F32)
                acc = d if acc is None else acc + d
    return acc


def _neumann_inverse_many(lmats, eye, terms):
    ts = [eye + l for l in lmats]
    ps = [_split(l, terms) for l in lmats]
    n = 2
    while n < CHUNK:
        ps = [_split(_mm(p, p), terms) for p in ps]
        ts = [t + _mm(t, p, _NN, terms) for t, p in zip(ts, ps)]
        n *= 2
    return ts


def _mix_segments(pa_ref, qkv_ref, z_ref, ba_ref):
    return ((0, pa_ref), (A_COLS, qkv_ref), (A_COLS + CONV_CH, z_ref), (A_COLS + CONV_CH + C_BV, ba_ref))


def _inproj_kernel(x_ref, g_ref, w_ref, pa_ref, qkv_ref, z_ref, ba_ref):
    u = _rms(x_ref[...], g_ref[...]).astype(BF16)
    for c0, dst_ref in _mix_segments(pa_ref, qkv_ref, z_ref, ba_ref):
        dst_ref[...] = _dot(u, w_ref[:, c0:c0 + dst_ref.shape[1]])


def _const_spec(shape):
    nd = len(shape)
    return pl.BlockSpec(shape, lambda *_: (0,) * nd, pipeline_mode=pl.Buffered(1))


def _inproj(x, gain, w_mix):
    n = x.shape[0]
    tm = min(ROW_TILE, n)
    row = lambda w: pl.BlockSpec((tm, w), lambda i: (i, 0))
    return pl.pallas_call(
        _inproj_kernel,
        grid=(n // tm,),
        in_specs=[row(D_MODEL), _const_spec(gain.shape), _const_spec(w_mix.shape)],
        out_specs=[row(A_COLS), row(CONV_CH), row(C_BV), row(LANES)],
        out_shape=[jax.ShapeDtypeStruct((n, A_COLS), F32), jax.ShapeDtypeStruct((n, CONV_CH), F32),
                   jax.ShapeDtypeStruct((n, C_BV), F32), jax.ShapeDtypeStruct((n, LANES), F32)],
        compiler_params=pltpu.CompilerParams(dimension_semantics=("parallel",), vmem_limit_bytes=VMEM_LIMIT),
        name="inproj",
    )(x, gain, w_mix)


def _rwkv_prep(pa, prev, mu, w0, w2p, a0, a2p, g2, kkw, ka, rk, hb, between=lambda: None):
    xa = pa + (prev - pa) * mu
    r = xa[:, :C_A]
    k = xa[:, C_A:2 * C_A]
    v = xa[:, 2 * C_A:3 * C_A]
    xwa = xa[:, 3 * C_A:3 * C_A + LORA_W + LORA_A]
    xg = xa[:, 3 * C_A + LORA_W + LORA_A:]
    w_pre = _mm(jnp.tanh(xwa), w2p)
    a_pre = _mm(xwa, a2p)
    g = _mm(_sigmoid(xg), g2)
    kx = k * kkw
    kk_den = _head_sum(kx * kx, hb)
    between()
    w_log = -_softplus(-(w0 + w_pre)) - 0.5
    logw = -jnp.exp(w_log)
    a = _sigmoid(a0 + a_pre)
    kk = kx * lax.rsqrt(kk_den + L2_EPS)
    k2 = k * (1.0 + (a - 1.0) * ka)
    bonus_sum = _head_sum(r * k2 * rk, hb)
    between()
    bonus = bonus_sum * v
    return r, k2, v, logw, a, g, kk, bonus


def _head_sum(x, hb):
    groups = [_mm(x[:, LANES * j:LANES * (j + 1)], hb, _NN, SUM_TERMS) for j in range(x.shape[1] // LANES)]
    return jnp.concatenate(groups, axis=1)


def _group_norm_gate(y, bonus, g, lnw, lnb, hb):
    mean = _head_sum(y, hb) * (1.0 / HEAD_A)
    d = y - mean
    var = _head_sum(d * d, hb) * (1.0 / HEAD_A)
    yn = d * lax.rsqrt(var + GN_EPS) * lnw + lnb
    return (yn + bonus) * g


def _gdn_prep(x0, x1, x2, x3, ba, conv, alog, dtb):
    c = _silu(x0 * conv[0:1] + x1 * conv[1:2] + x2 * conv[2:3] + x3 * conv[3:4])
    qs, ks = [], []
    for h in range(H_B):
        qh = c[:, DK * h:DK * (h + 1)]
        kh = c[:, C_BK + DK * h:C_BK + DK * (h + 1)]
        qs.append(qh * lax.rsqrt(jnp.sum(qh * qh, axis=-1, keepdims=True) + L2_EPS) * (DK ** -0.5))
        ks.append(kh * lax.rsqrt(jnp.sum(kh * kh, axis=-1, keepdims=True) + L2_EPS))
    q = jnp.concatenate(qs, axis=1)
    k = jnp.concatenate(ks, axis=1)
    v = c[:, 2 * C_BK:]
    beta = _sigmoid(ba)
    glog = -jnp.exp(alog) * _softplus(ba + dtb)
    return q, k, v, beta, glog


def _head_norm_gate(o, z, gnorm):
    outs = []
    for h in range(H_B):
        oh = o[:, DV * h:DV * (h + 1)]
        zh = z[:, DV * h:DV * (h + 1)]
        oh = oh * lax.rsqrt(jnp.mean(oh * oh, axis=-1, keepdims=True) + NORM_EPS) * gnorm
        outs.append(oh * _silu(zh))
    return jnp.concatenate(outs, axis=1)


def _tile_maps(bsz, seq, tile):
    tps = seq // tile
    last = bsz * tps - 1
    cur = lambda s: jnp.minimum(s, last)
    prv = lambda s: jnp.maximum(s - 1, 0)
    load = lambda s: (cur(s) // tps, cur(s) % tps, 0)
    emit = lambda s: (prv(s) // tps, prv(s) % tps, 0)
    per_seq3 = lambda s: (prv(s) // tps, 0, 0)
    per_seq4 = lambda s: (prv(s) // tps, 0, 0, 0)
    return tps, last + 2, load, emit, per_seq3, per_seq4


def _gdn_prep_groups(x0, x1, x2, x3, conv):
    def conv_cols(c0, c1):
        return _silu(x0[:, c0:c1] * conv[0:1, c0:c1] + x1[:, c0:c1] * conv[1:2, c0:c1]
                     + x2[:, c0:c1] * conv[2:3, c0:c1] + x3[:, c0:c1] * conv[3:4, c0:c1])

    def unit(xh):
        return xh * lax.rsqrt(jnp.sum(xh * xh, axis=-1, keepdims=True) + L2_EPS)

    for h in range(H_B):
        yield 0, h, unit(conv_cols(DK * h, DK * (h + 1))) * (DK ** -0.5)
        yield 1, h, unit(conv_cols(C_BK + DK * h, C_BK + DK * (h + 1)))
        yield 2, h, conv_cols(2 * C_BK + DV * h, 2 * C_BK + DV * (h + 1))


def _mixers_prompt_kernel(tiles_per_seq, x_ref, nmix_ref, wmix_ref,
                          mu_ref, w0_ref, w2_ref, a0_ref, a2_ref, g2_ref, kkw_ref, ka_ref, rk_ref,
                          lnw_ref, lnb_ref, hb_ref, conv_ref, alog_ref, dtb_ref, gnorm_ref, tri_ref,
                          oa_ref, wkv_ref, shift_ref, ob_ref, gdn_ref, convo_ref,
                          pa_ref, exta_ref, sa_ref, at_ref, bt_ref, kt_ref, rt_ref, va_ref, bh_ref, kh_ref, y_ref,
                          qkvn_ref, zn_ref, ban_ref, extb_ref, sb_ref, q_ref, k_ref, vb_ref, gc_ref, be_ref, o_ref):
    s_id = pl.program_id(0)

    @pl.when(s_id == 0)
    def _():
        pa_ref[...] = jnp.zeros_like(pa_ref)
        qkvn_ref[...] = jnp.zeros_like(qkvn_ref)
        zn_ref[...] = jnp.zeros_like(zn_ref)
        ban_ref[...] = jnp.zeros_like(ban_ref)

    @pl.when(jnp.maximum(s_id - 1, 0) % tiles_per_seq == 0)
    def _():
        exta_ref[0:HALO, :] = jnp.zeros((HALO, A_COLS), F32)
        extb_ref[0:HALO, :] = jnp.zeros((HALO, CONV_CH), F32)
        sa_ref[...] = jnp.zeros_like(sa_ref)
        sb_ref[...] = jnp.zeros_like(sb_ref)

    pa = pa_ref[...]
    exta_ref[HALO:, :] = pa
    prev = exta_ref[pl.ds(HALO - 1, T_TILE), :]
    exta_ref[0:HALO, :] = pa[T_TILE - HALO:, :]
    shift_ref[0] = pa[T_TILE - HALO:, :]
    x3 = qkvn_ref[...]
    z = zn_ref[...]
    ba = ban_ref[...]
    extb_ref[HALO:, :] = x3
    x0 = extb_ref[pl.ds(HALO - 3, T_TILE), :]
    x1 = extb_ref[pl.ds(HALO - 2, T_TILE), :]
    x2 = extb_ref[pl.ds(HALO - 1, T_TILE), :]
    extb_ref[0:HALO, :] = x3[T_TILE - HALO:, :]
    convo_ref[0] = x3[T_TILE - HALO:, :]

    u_next = _rms(x_ref[0], nmix_ref[...]).astype(BF16)
    blocks = [(w0, dst_ref, c0, min(c0 + PROJ_COLS, dst_ref.shape[1]))
              for w0, dst_ref in _mix_segments(pa_ref, qkvn_ref, zn_ref, ban_ref)
              for c0 in range(0, dst_ref.shape[1], PROJ_COLS)]

    def project(count):
        for w0, dst_ref, c0, c1 in blocks[:count]:
            dst_ref[:, c0:c1] = _dot(u_next, wmix_ref[:, w0 + c0:w0 + c1])
        del blocks[:count]

    gdn_groups = _gdn_prep_groups(x0, x1, x2, x3, conv_ref[...])
    gdn_dst = (q_ref, k_ref, vb_ref)

    def gdn_prep(count):
        for _ in range(count):
            item = next(gdn_groups, None)
            if item is not None:
                kind, h, arr = item
                gdn_dst[kind][:, DK * h:DK * (h + 1)] = arr

    def between():
        project(3)
        gdn_prep(3)

    hb = hb_ref[...]
    r, k2, v, logw, a, g, kk, bonus = _rwkv_prep(
        pa, prev, mu_ref[...], w0_ref[...], w2_ref[...], a0_ref[...], a2_ref[...], g2_ref[...],
        kkw_ref[...], ka_ref[...], rk_ref[...], hb, between)

    nchunk = T_TILE // CHUNK
    cs = _mm(tri_ref[...], logw, _NN, 1, CUMSUM_TERMS)
    between()
    ends = [cs[CHUNK * (c + 1) - 1:CHUNK * (c + 1)] for c in range(nchunk)]
    tot = jnp.concatenate([jnp.broadcast_to(e, (CHUNK, C_A)) for e in ends], axis=0)
    dinv = jnp.exp(-cs)
    dend = jnp.exp(tot - cs)
    b_in = kk * a
    at_ref[...] = -kk * jnp.exp(cs - logw)
    bt_ref[...] = b_in * dinv
    kt_ref[...] = k2 * dinv
    between()
    rt_ref[...] = r * jnp.exp(cs)
    va_ref[...] = v
    bh_ref[...] = b_in * dend
    kh_ref[...] = k2 * dend
    gl = [jnp.exp(e) for e in ends]
    project(len(blocks))
    gdn_prep(3 * H_B)
    be_ref[...] = _sigmoid(ba)
    glog = -jnp.exp(alog_ref[...]) * _softplus(ba + dtb_ref[...])
    gc_ref[...] = _mm(tri_ref[...], glog, _NN, 1, CUMSUM_TERMS)

    strict, incl, eye = _pair_masks()
    low = lax.broadcasted_iota(jnp.int32, (CHUNK, LANES), 1) < HEAD_A

    def stack(x):
        return jnp.concatenate([jnp.where(low, x, 0.0), jnp.where(low, 0.0, x)], axis=0)

    npair = C_A // LANES
    lanes = lambda p: slice(LANES * p, LANES * (p + 1))
    rows = lambda c: slice(CHUNK * c, CHUNK * (c + 1))
    half = lambda j: slice(CHUNK * j, CHUNK * (j + 1))
    ia = [(p, c) for p in range(npair) for c in range(nchunk)]
    ib = [(pr, c) for pr in range(H_B // 2) for c in range(nchunk)]
    na, nb = range(len(ia)), range(len(ib))
    ld = lambda ref: [stack(ref[rows(c), lanes(p)]) for p, c in ia]
    at, bt, kt, rt, vs, bh, kh = (ld(ref) for ref in (at_ref, bt_ref, kt_ref, rt_ref, va_ref, bh_ref, kh_ref))
    cat = lambda ref: [jnp.concatenate([ref[rows(c), DK * h:DK * (h + 1)] for h in (2 * pr, 2 * pr + 1)], axis=0)
                       for pr, c in ib]
    col = lambda ref, off: [jnp.concatenate([ref[rows(c), off + h:off + h + 1] for h in (2 * pr, 2 * pr + 1)], axis=0)
                            for pr, c in ib]
    qg_, kg_, vg_ = cat(q_ref), cat(k_ref), cat(vb_ref)
    beta_c = col(be_ref, 0)
    gc = col(gc_ref, H_B)
    gt = [jnp.concatenate([jnp.broadcast_to(g_[CHUNK * (i + 1) - 1:CHUNK * (i + 1)], (CHUNK, 1)) for i in range(2)],
                          axis=0) for g_ in gc]
    decay = []
    for g_ in gc:
        gc_full = jnp.broadcast_to(g_, (PAIR, PAIR))
        decay.append(jnp.exp(gc_full - gc_full.T))
    kb = [kg_[i] * beta_c[i] for i in nb]
    vbb = [vg_[i] * beta_c[i] for i in nb]
    eg = [jnp.exp(g_) for g_ in gc]

    aa = [_mm(jnp.concatenate([at[i], rt[i]], axis=0), jnp.concatenate([bt[i], kt[i]], axis=0), _NT) for i in na]
    kq = [_mm(jnp.concatenate([kb[i], qg_[i]], axis=0), kg_[i], _NT) for i in nb]
    a_ab = [jnp.where(strict, aa[i][:PAIR, :PAIR], 0.0) for i in na]
    a_ak = [jnp.where(strict, aa[i][:PAIR, PAIR:], 0.0) for i in na]
    a_rb = [jnp.where(incl, aa[i][PAIR:, :PAIR], 0.0) for i in na]
    a_rk = [jnp.where(incl, aa[i][PAIR:, PAIR:], 0.0) for i in na]
    lneg = [jnp.where(strict, -(kq[i][:PAIR] * decay[i]), 0.0) for i in nb]
    qk = [jnp.where(incl, kq[i][PAIR:] * decay[i], 0.0) for i in nb]
    tall = _neumann_inverse_many(a_ab + lneg, eye, INV_TERMS)
    tinv, tinvb = tall[:len(ia)], tall[len(ia):]
    akv = [_mm(a_ak[i], vs[i]) for i in na]
    uw = [_mm(tinvb[i], jnp.concatenate([vbb[i], kb[i] * eg[i]], axis=1)) for i in nb]
    gu = [_mm(tinv[i], jnp.concatenate([at[i], akv[i]], axis=1)) for i in na]
    ow = [_mm(qk[i], uw[i]) for i in nb]
    ry = [_mm(a_rb[i], gu[i]) for i in na]
    rkv = [_mm(a_rk[i], vs[i]) for i in na]
    kd = [kg_[i] * jnp.exp(gt[i] - gc[i]) for i in nb]
    pqb = [[_mm(kd[i][half(j)], uw[i][half(j)], _TN) for j in range(2)] for i in nb]
    pq = [_mm(gu[i], bh[i], _TN) for i in na]
    vk = [_mm(vs[i], kh[i], _TN) for i in na]
    rp = [rt[i] + ry[i][:, :PAIR] for i in na]
    yc = [ry[i][:, PAIR:] + rkv[i] for i in na]
    pm = [eye * gl[ia[i][1]][:, lanes(ia[i][0])] + pq[i][:PAIR] for i in na]
    qm = [pq[i][PAIR:] + vk[i] for i in na]
    oc = [ow[i][:, :DV] for i in nb]
    rq = [qg_[i] * eg[i] - ow[i][:, DV:] for i in nb]
    pmb = [[eye * jnp.exp(gt[i][CHUNK * j:CHUNK * j + 1]) - pqb[i][j][:, DV:] for j in range(2)] for i in nb]

    sa = [sa_ref[p] for p in range(npair)]
    sb = [sb_ref[h] for h in range(H_B)]
    for c in range(nchunk):
        for p in range(npair):
            i = ia.index((p, c))
            y = _mm(rp[i], sa[p], _NT) + yc[i]
            y_ref[rows(c), lanes(p)] = y[:CHUNK] + y[CHUNK:]
            sa[p] = _mm(sa[p], pm[i], _NN, CHAIN_TERMS, CHAIN_TERMS) + qm[i]
        for h in range(H_B):
            i, j = ib.index((h // 2, c)), h % 2
            o_ref[rows(c), DV * h:DV * (h + 1)] = _mm(rq[i][half(j)], sb[h]) + oc[i][half(j)]
            sb[h] = _mm(pmb[i][j], sb[h], _NN, CHAIN_TERMS, CHAIN_TERMS) + pqb[i][j][:, :DV]
    for p in range(npair):
        sa_ref[p] = sa[p]
    for h in range(H_B):
        sb_ref[h] = sb[h]

    oa_ref[0] = _group_norm_gate(y_ref[...], bonus, g, lnw_ref[...], lnb_ref[...], hb).astype(BF16)
    wkv_ref[0] = sa_ref[...]
    ob_ref[0] = _head_norm_gate(o_ref[...], z, gnorm_ref[...]).astype(BF16)
    gdn_ref[0] = sb_ref[...]


def _mixers_prompt(x, nmix, w_mix, rw_consts, gdn_consts, tri):
    bsz, seq, _ = x.shape
    consts = (nmix, w_mix) + tuple(rw_consts) + tuple(gdn_consts) + (tri,)
    npair = C_A // LANES
    tps, steps, load, emit, per_seq3, per_seq4 = _tile_maps(bsz, seq, T_TILE)
    tile = lambda w: pltpu.VMEM((T_TILE, w), F32)
    return pl.pallas_call(
        functools.partial(_mixers_prompt_kernel, tps),
        grid=(steps,),
        in_specs=[pl.BlockSpec((1, T_TILE, D_MODEL), load)] + [_const_spec(c.shape) for c in consts],
        out_specs=[pl.BlockSpec((1, T_TILE, C_A), emit),
                   pl.BlockSpec((1, npair, LANES, LANES), per_seq4),
                   pl.BlockSpec((1, HALO, A_COLS), per_seq3),
                   pl.BlockSpec((1, T_TILE, C_BV), emit),
                   pl.BlockSpec((1, H_B, DK, DV), per_seq4),
                   pl.BlockSpec((1, HALO, CONV_CH), per_seq3)],
        out_shape=[jax.ShapeDtypeStruct((bsz, seq, C_A), BF16),
                   jax.ShapeDtypeStruct((bsz, npair, LANES, LANES), F32),
                   jax.ShapeDtypeStruct((bsz, HALO, A_COLS), F32),
                   jax.ShapeDtypeStruct((bsz, seq, C_BV), BF16),
                   jax.ShapeDtypeStruct((bsz, H_B, DK, DV), F32),
                   jax.ShapeDtypeStruct((bsz, HALO, CONV_CH), F32)],
        scratch_shapes=[tile(A_COLS), pltpu.VMEM((T_TILE + HALO, A_COLS), F32), pltpu.VMEM((npair, LANES, LANES), F32)]
                       + [tile(C_A) for _ in range(8)]
                       + [tile(CONV_CH), tile(C_BV), tile(LANES), pltpu.VMEM((T_TILE + HALO, CONV_CH), F32),
                          pltpu.VMEM((H_B, DK, DV), F32), tile(C_BK), tile(C_BK), tile(C_BV), tile(LANES), tile(LANES),
                          tile(C_BV)],
        compiler_params=pltpu.CompilerParams(dimension_semantics=("arbitrary",), vmem_limit_bytes=VMEM_LIMIT),
        name="mixers_prompt",
    )(x, *consts)


def _to_columns(x):
    pad = jnp.zeros((LANES - DEC_TILE, x.shape[1]), F32)
    return jnp.concatenate([x, pad], axis=0).T


def _decode_rwkv_kernel(pa_ref, shift_ref, wkv_ref,
                        mu_ref, w0_ref, w2_ref, a0_ref, a2_ref, g2_ref, kkw_ref, ka_ref, rk_ref, lnw_ref, lnb_ref,
                        hb_ref, oa_ref, wkvo_ref, tr_ref, yt_ref, g_ref, bonus_ref):
    h = pl.program_id(0)

    @pl.when(h == 0)
    def _():
        r, k2, v, logw, a, g, kk, bonus = _rwkv_prep(
            pa_ref[...], shift_ref[...], mu_ref[...], w0_ref[...], w2_ref[...], a0_ref[...], a2_ref[...],
            g2_ref[...], kkw_ref[...], ka_ref[...], rk_ref[...], hb_ref[...])
        for i, x in enumerate((-kk, jnp.exp(logw), kk * a, k2, r, v)):
            tr_ref[i] = x.T
        g_ref[...] = g
        bonus_ref[...] = bonus

    base = pl.multiple_of(h * HEAD_A, HEAD_A)
    hs = pl.ds(base, HEAD_A)
    a_t, w_t, b_t, k_t, r_t = (tr_ref[i, hs, :] for i in range(5))

    def value_row(vi, carry):
        st = wkv_ref[0, vi]
        sa = jnp.sum(st * a_t, axis=0, keepdims=True)
        st = st * w_t + sa * b_t + tr_ref[5, pl.ds(base + vi, 1), :] * k_t
        wkvo_ref[0, vi] = st
        yt_ref[pl.ds(base + vi, 1), :] = jnp.sum(st * r_t, axis=0, keepdims=True)
        return carry

    lax.fori_loop(0, HEAD_A, value_row, 0, unroll=8)

    @pl.when(h == H_A - 1)
    def _():
        oa_ref[...] = _group_norm_gate(yt_ref[...].T, bonus_ref[...], g_ref[...], lnw_ref[...], lnb_ref[...],
                                       hb_ref[...])


def _decode_rwkv(pa, shift, wkv_t, rw_consts):
    n = pa.shape[0]
    full = lambda w: pl.BlockSpec((n, w), lambda h: (0, 0))
    state = pl.BlockSpec((1, HEAD_A, HEAD_A, n), lambda h: (h, 0, 0, 0))
    return pl.pallas_call(
        _decode_rwkv_kernel,
        grid=(H_A,),
        in_specs=[full(A_COLS), full(A_COLS), state] + [_const_spec(c.shape) for c in rw_consts],
        out_specs=[full(C_A), state],
        out_shape=[jax.ShapeDtypeStruct((n, C_A), F32), jax.ShapeDtypeStruct(wkv_t.shape, F32)],
        scratch_shapes=[pltpu.VMEM((6, C_A, n), F32), pltpu.VMEM((C_A, n), F32),
                        pltpu.VMEM((n, C_A), F32), pltpu.VMEM((n, C_A), F32)],
        compiler_params=pltpu.CompilerParams(dimension_semantics=("arbitrary",), vmem_limit_bytes=VMEM_LIMIT),
        name="decode_rwkv",
    )(pa, shift, wkv_t, *rw_consts)


def _decode_gdn_kernel(qkv_ref, z_ref, ba_ref, cst_ref, gdn_ref, conv_ref, alog_ref, dtb_ref, gnorm_ref,
                       ob_ref, gdno_ref):
    cst = cst_ref[...]
    q, k, vv, beta, glog = _gdn_prep(cst[:, :CONV_CH], cst[:, CONV_CH:2 * CONV_CH], cst[:, 2 * CONV_CH:],
                                     qkv_ref[...], ba_ref[...], conv_ref[...], alog_ref[...], dtb_ref[...])
    eg = jnp.exp(glog)
    k_cols = _to_columns(k)
    ls = lambda h: slice(DK * h, DK * (h + 1))

    def split2(x):
        hi = x.astype(BF16)
        return hi, x - hi.astype(F32)

    rows_a, rows_b = [], []
    for h in range(H_B):
        e_h = eg[:, H_B + h:H_B + h + 1]
        (wh, wl), (gh, gl_) = split2(k[:, ls(h)] * (beta[:, h:h + 1] * e_h)), split2(q[:, ls(h)] * e_h)
        rows_a.append(jnp.concatenate([wh.astype(F32), gh.astype(F32), wl, gl_], axis=0).astype(BF16))
        rows_b.append(jnp.concatenate([wh.astype(F32), gh.astype(F32)], axis=0).astype(BF16))
    gunits = [(s_i, h) for s_i in range(DEC_TILE) for h in range(H_B)]
    gn = range(len(gunits))
    be = [beta[s_i:s_i + 1, h:h + 1] for s_i, h in gunits]
    e = [eg[s_i:s_i + 1, H_B + h:H_B + h + 1] for s_i, h in gunits]
    kc = [k_cols[ls(h), s_i:s_i + 1] for s_i, h in gunits]
    gst = [gdn_ref[s_i, h] for s_i, h in gunits]
    parts = [split2(st) for st in gst]
    ra = [_dot(rows_a[h], parts[i][0].astype(BF16)) + jnp.concatenate(
        [_dot(rows_b[h], parts[i][1].astype(BF16)), jnp.zeros((2 * DEC_TILE, DV), F32)], axis=0)
        for i, (s_i, h) in enumerate(gunits)]
    pick = lambda r, j, s_i: r[DEC_TILE * j + s_i:DEC_TILE * j + s_i + 1]
    ws = [pick(ra[i], 0, s_i) + pick(ra[i], 2, s_i) for i, (s_i, h) in enumerate(gunits)]
    qs = [pick(ra[i], 1, s_i) + pick(ra[i], 3, s_i) for i, (s_i, h) in enumerate(gunits)]
    v_new = [be[i] * vv[s_i:s_i + 1, ls(h)] - ws[i] for i, (s_i, h) in enumerate(gunits)]
    qk = [jnp.sum(q[s_i:s_i + 1, ls(h)] * k[s_i:s_i + 1, ls(h)], axis=-1, keepdims=True) for s_i, h in gunits]
    for i, (s_i, h) in enumerate(gunits):
        gdno_ref[s_i, h] = gst[i] * e[i] + kc[i] * v_new[i]
    o_units = [qs[i] + qk[i] * v_new[i] for i in gn]
    o = jnp.concatenate([jnp.concatenate(o_units[H_B * s_i:H_B * (s_i + 1)], axis=1) for s_i in range(DEC_TILE)],
                        axis=0)
    ob_ref[...] = _head_norm_gate(o, z_ref[...], gnorm_ref[...])


def _decode_gdn(qkv, z, ba, cst, gdn, gdn_consts):
    n = qkv.shape[0]
    row = lambda w: pl.BlockSpec((DEC_TILE, w), lambda i: (i, 0))
    state = pl.BlockSpec((DEC_TILE, H_B, DK, DV), lambda i: (i, 0, 0, 0))
    return pl.pallas_call(
        _decode_gdn_kernel,
        grid=(n // DEC_TILE,),
        in_specs=[row(CONV_CH), row(C_BV), row(LANES), row(3 * CONV_CH), state]
                 + [_const_spec(c.shape) for c in gdn_consts],
        out_specs=[row(C_BV), state],
        out_shape=[jax.ShapeDtypeStruct((n, C_BV), F32), jax.ShapeDtypeStruct(gdn.shape, F32)],
        compiler_params=pltpu.CompilerParams(dimension_semantics=("parallel",), vmem_limit_bytes=VMEM_LIMIT),
        name="decode_gdn",
    )(qkv, z, ba, cst, gdn, *gdn_consts)


def _post_kernel(x_ref, oa_ref, ob_ref, pe_ref, nmix_ref, wg_ref, wba_ref, wbb_ref, wout_ref, nffn_ref,
                 wfg_ref, wfu_ref, wfd_ref, nple_ref, wpg_ref, wpp_ref, nfin_ref, y_ref):
    nrow = x_ref.shape[0]
    ngroup = max(1, nrow // POST_GROUP_ROWS)
    groups = [slice(r0, r0 + nrow // ngroup) for r0 in range(0, nrow, nrow // ngroup)]
    x = [x_ref[g, :] for g in groups]
    ma = [_dot(oa_ref[g, :].astype(BF16), wba_ref[...]) for g in groups]
    mb = [_dot(ob_ref[g, :].astype(BF16), wbb_ref[...]) for g in groups]
    u = [_rms(xi, nmix_ref[...]).astype(BF16) for xi in x]
    gates = [_sigmoid(_dot(ui, wg_ref[...])) for ui in u]
    mix = [gi[:, :D_MODEL] * mai + gi[:, D_MODEL:] * mbi for gi, mai, mbi in zip(gates, ma, mb)]
    h = [xi + _dot(mi.astype(BF16), wout_ref[...]) for xi, mi in zip(x, mix)]
    pp = [_dot(pe_ref[g, :].astype(BF16), wpp_ref[...]) for g in groups]
    u2 = [_rms(hi, nffn_ref[...]).astype(BF16) for hi in h]
    fg = [_silu(_dot(ui, wfg_ref[...])) for ui in u2]
    ff = [fgi * _dot(ui, wfu_ref[...]) for fgi, ui in zip(fg, u2)]
    h = [hi + _dot(fi.astype(BF16), wfd_ref[...]) for hi, fi in zip(h, ff)]
    u3 = [_rms(hi, nple_ref[...]).astype(BF16) for hi in h]
    h = [hi + _sigmoid(_dot(ui, wpg_ref[...])) * ppi for hi, ui, ppi in zip(h, u3, pp)]
    for g, hi in zip(groups, h):
        y_ref[g, :] = _rms(hi, nfin_ref[...])


def _post(x, oa, ob, pe, consts):
    n = x.shape[0]
    tm = min(ROW_TILE, n)
    row = lambda w: pl.BlockSpec((tm, w), lambda i: (i, 0))
    return pl.pallas_call(
        _post_kernel,
        grid=(n // tm,),
        in_specs=[row(D_MODEL), row(C_A), row(C_BV), row(PLE_DIM)] + [_const_spec(c.shape) for c in consts],
        out_specs=row(D_MODEL),
        out_shape=jax.ShapeDtypeStruct((n, D_MODEL), F32),
        compiler_params=pltpu.CompilerParams(dimension_semantics=("parallel",), vmem_limit_bytes=VMEM_LIMIT),
        name="post",
    )(x, oa, ob, pe, *consts)


def _chunk_cumsum_matrix(tile):
    i = jnp.arange(tile)
    same = (i[:, None] // CHUNK) == (i[None, :] // CHUNK)
    return (same & (i[None, :] <= i[:, None])).astype(BF16)


def kernel(x_prompt, x_sample, p_prompt, p_sample, state_shift, state_wkv, state_conv, state_gdn, norm_mix, w_in, mu_shift, rw_w0, rw_w2, rw_a0, rw_a2, rw_g2, rw_kk, rw_ka, rw_rk, rw_ln_w, rw_ln_b, gdn_conv, gdn_a_log, gdn_dt_bias, gdn_norm, w_branch_a, w_branch_b, w_out, norm_ffn, w_ffn_gate, w_ffn_up, w_ffn_down, norm_ple, w_ple_gate, w_ple_proj, norm_final):
    bsz, seq, _ = x_prompt.shape
    nd = x_sample.shape[0]
    row = lambda p: p.reshape(1, -1)

    w_in0 = w_in[0]
    b0 = A_COLS
    w_mix = w_in0[:, :b0 + CONV_CH + C_BV + LANES].astype(BF16)
    wg = w_in0[:, b0 + CONV_CH + C_BV + 2 * H_B:].astype(BF16)
    w2p = jnp.concatenate([rw_w2[0], jnp.zeros((LORA_A, C_A), F32)], axis=0).astype(BF16)
    a2p = jnp.concatenate([jnp.zeros((LORA_W, C_A), F32), rw_a2[0]], axis=0).astype(BF16)
    ch = jnp.arange(LANES) // HEAD_A
    hb = (ch[:, None] == ch[None, :]).astype(BF16)
    tri = _chunk_cumsum_matrix(T_TILE)
    alog = jnp.pad(gdn_a_log[0], (H_B, LANES - 2 * H_B)).reshape(1, LANES)
    dtb = jnp.pad(gdn_dt_bias[0], (H_B, LANES - 2 * H_B)).reshape(1, LANES)
    rw_consts = (row(mu_shift[0]), row(rw_w0[0]), w2p, row(rw_a0[0]), a2p, rw_g2[0].astype(BF16), row(rw_kk[0]), row(rw_ka[0]),
                 row(rw_rk[0]), row(rw_ln_w[0]), row(rw_ln_b[0]), hb)
    gdn_consts = (gdn_conv[0], alog, dtb, row(gdn_norm[0]))
    post_consts = (row(norm_mix[0]), wg, w_branch_a[0].astype(BF16), w_branch_b[0].astype(BF16),
                   w_out[0].astype(BF16), row(norm_ffn[0]), w_ffn_gate[0].astype(BF16), w_ffn_up[0].astype(BF16),
                   w_ffn_down[0].astype(BF16), row(norm_ple[0]), w_ple_gate[0].astype(BF16),
                   w_ple_proj[0].astype(BF16), row(norm_final))

    xp = x_prompt.reshape(bsz * seq, D_MODEL)
    oa, wkv_pairs, shift_tail, ob, gdn_p, conv_tail = _mixers_prompt(
        x_prompt, row(norm_mix[0]), w_mix, rw_consts, gdn_consts, tri)
    y_prompt = _post(xp, oa.reshape(bsz * seq, C_A), ob.reshape(bsz * seq, C_BV),
                     p_prompt[0].reshape(bsz * seq, PLE_DIM), post_consts).reshape(bsz, seq, D_MODEL)
    wkv_p = jnp.stack([wkv_pairs[:, :, :HEAD_A, :HEAD_A], wkv_pairs[:, :, HEAD_A:, HEAD_A:]], axis=2)
    wkv_p = wkv_p.reshape(bsz, H_A, HEAD_A, HEAD_A)
    shift_p = shift_tail[:, HALO - 1:, :]
    conv_p = conv_tail[:, HALO - (CONV_W - 1):, :]

    xs = x_sample.reshape(nd, D_MODEL)
    pa_s, qkv_s, z_s, ba_s = _inproj(xs, row(norm_mix[0]), w_mix)
    cst = state_conv[0].reshape(nd, (CONV_W - 1) * CONV_CH)
    oa_s, wkv_t = _decode_rwkv(pa_s, state_shift[0].reshape(nd, A_COLS), jnp.transpose(state_wkv[0], (1, 2, 3, 0)),
                               rw_consts)
    wkv_s = jnp.transpose(wkv_t, (3, 0, 1, 2))
    ob_s, gdn_s = _decode_gdn(qkv_s, z_s, ba_s, cst, state_gdn[0], gdn_consts)
    y_sample = _post(xs, oa_s, ob_s, p_sample[0].reshape(nd, PLE_DIM), post_consts).reshape(nd, 1, D_MODEL)
    conv_s = jnp.concatenate([cst[:, CONV_CH:], qkv_s], axis=1).reshape(nd, CONV_W - 1, CONV_CH)

    return (y_prompt, y_sample, shift_p[None], wkv_p[None], conv_p[None], gdn_p[None],
            pa_s.reshape(1, nd, 1, A_COLS), wkv_s[None], conv_s[None], gdn_s[None])
```
